```python
import math
import jax, jax.numpy as jnp
from jax import lax
import numpy as np

D_MODEL = 1024
BATCH = 8
SEQ = 16384
DEPTH = 4

D_MIX = D_MODEL
D_POOL = D_MIX // 2
D_SGU = D_MIX // 2
POOL_WINDOWS = (2, 4, 8, 16)
N_POOL_GROUPS = len(POOL_WINDOWS)
POOL_GROUP_DIM = D_POOL // N_POOL_GROUPS
CHUNK = 128
SGU_HEADS = 4
SGU_HEAD_DIM = D_SGU // SGU_HEADS
D_IN = D_POOL + 2 * D_SGU
D_FF = 2816
CONV_WIDTH = 3
N_MOD = 6
DEEPNORM_ALPHA = (2.0 * DEPTH) ** 0.25
DEEPNORM_BETA = (8.0 * DEPTH) ** -0.25
LN_EPS = 1e-5

kernel_name = "hybrid_pool_sgu_convffn_deepnorm_adaln"


def _layernorm(x, g, b):
    xf = x.astype(jnp.float32)
    mu = jnp.mean(xf, axis=-1, keepdims=True)
    var = jnp.mean(jnp.square(xf - mu), axis=-1, keepdims=True)
    y = (xf - mu) * lax.rsqrt(var + LN_EPS)
    return (y * g.astype(jnp.float32) + b.astype(jnp.float32)).astype(x.dtype)


def _modulate(x, shift, scale):
    return x * (1.0 + scale[:, None, :]) + shift[:, None, :]


def _pool_mixer(a, pool_w, pool_scale):
    B, S, _ = a.shape
    ag = a.reshape(B, S, N_POOL_GROUPS, POOL_GROUP_DIM).astype(jnp.float32)
    cs = jnp.cumsum(ag, axis=1)
    t = jnp.arange(S)
    pooled = []
    for g, w in enumerate(POOL_WINDOWS):
        csg = cs[:, :, g]
        prev = jnp.pad(csg, ((0, 0), (w, 0), (0, 0)))[:, :S]
        cnt = jnp.minimum(t + 1, w).astype(jnp.float32)[None, :, None]
        pooled.append((csg - prev) / cnt)
    pooled = (jnp.stack(pooled, axis=2) - ag).astype(a.dtype)
    mixed = jnp.einsum('bsgc,gcd->bsgd', pooled, pool_w)
    return mixed.reshape(B, S, D_POOL) * pool_scale


def _sgu_mixer(u, v, ln_g, ln_b, sgu_w, sgu_b):
    B, S, _ = v.shape
    u = jax.nn.gelu(u)
    v = _layernorm(jax.nn.gelu(v), ln_g, ln_b)
    vc = v.reshape(B, S // CHUNK, CHUNK, SGU_HEADS, SGU_HEAD_DIM)
    mask = jnp.tril(jnp.ones((CHUNK, CHUNK), dtype=bool))
    w = jnp.where(mask[None], sgu_w, jnp.zeros((), sgu_w.dtype))
    z = jnp.einsum('hts,bcshd->bcthd', w, vc) + jnp.transpose(sgu_b)[None, None, :, :, None]
    return u * z.reshape(B, S, D_SGU)


def _causal_dwconv(h, w, b):
    S = h.shape[1]
    hp = jnp.pad(h, ((0, 0), (CONV_WIDTH - 1, 0), (0, 0)))
    y = b
    for k in range(CONV_WIDTH):
        y = y + hp[:, k:k + S] * w[k]
    return y


def _fwd_setup_inputs(seed: int = 0) -> dict:
    key = jax.random.key(seed)
    ks = jax.random.split(key, 24)
    nrm = lambda k, shp: jax.random.normal(k, shp, dtype=jnp.float32)
    L, D = DEPTH, D_MODEL
    ada_b = jnp.concatenate([
        0.02 * nrm(ks[3], (L, 2 * D)),
        1.0 + 0.02 * nrm(ks[4], (L, D)),
        0.02 * nrm(ks[5], (L, 2 * D)),
        1.0 + 0.02 * nrm(ks[6], (L, D)),
    ], axis=-1)
    return {
        "x": nrm(ks[0], (BATCH, SEQ, D)),
        "c": nrm(ks[1], (BATCH, D)),
        "ada_w": 0.1 * D ** -0.5 * nrm(ks[2], (L, D, N_MOD * D)),
        "ada_b": ada_b,
        "w_in": D ** -0.5 * nrm(ks[7], (L, D, D_IN)),
        "pool_w": POOL_GROUP_DIM ** -0.5 * nrm(ks[8], (L, N_POOL_GROUPS, POOL_GROUP_DIM, POOL_GROUP_DIM)),
        "pool_scale": 1.0 + 0.02 * nrm(ks[9], (L, D_POOL)),
        "sgu_ln_g": 1.0 + 0.02 * nrm(ks[10], (L, D_SGU)),
        "sgu_ln_b": 0.02 * nrm(ks[11], (L, D_SGU)),
        "sgu_w": CHUNK ** -0.5 * nrm(ks[12], (L, SGU_HEADS, CHUNK, CHUNK)),
        "sgu_b": 1.0 + 0.02 * nrm(ks[13], (L, SGU_HEADS, CHUNK)),
        "w_out": DEEPNORM_BETA * D_MIX ** -0.5 * nrm(ks[14], (L, D_MIX, D)),
        "ln1_g": 1.0 + 0.02 * nrm(ks[15], (L, D)),
        "ln1_b": 0.02 * nrm(ks[16], (L, D)),
        "w_up": D ** -0.5 * nrm(ks[17], (L, D, 2 * D_FF)),
        "conv_w": 0.5 * nrm(ks[18], (L, CONV_WIDTH, D_FF)),
        "conv_b": 0.02 * nrm(ks[19], (L, D_FF)),
        "w_down": DEEPNORM_BETA * D_FF ** -0.5 * nrm(ks[20], (L, D_FF, D)),
        "ln2_g": 1.0 + 0.02 * nrm(ks[21], (L, D)),
        "ln2_b": 0.02 * nrm(ks[22], (L, D)),
    }


def _fwd_reference(x, c, ada_w, ada_b, w_in, pool_w, pool_scale, sgu_ln_g, sgu_ln_b, sgu_w, sgu_b,
              w_out, ln1_g, ln1_b, w_up, conv_w, conv_b, w_down, ln2_g, ln2_b):
    c_act = jax.nn.silu(c)
    for l in range(DEPTH):
        mod = c_act @ ada_w[l] + ada_b[l]
        shift1, scale1, gate1, shift2, scale2, gate2 = jnp.split(mod, N_MOD, axis=-1)

        h = _modulate(x, shift1, scale1)
        proj = jnp.einsum('bsd,de->bse', h, w_in[l])
        a = proj[..., :D_POOL]
        u = proj[..., D_POOL:D_POOL + D_SGU]
        v = proj[..., D_POOL + D_SGU:]
        y_a = _pool_mixer(a, pool_w[l], pool_scale[l])
        y_b = _sgu_mixer(u, v, sgu_ln_g[l], sgu_ln_b[l], sgu_w[l], sgu_b[l])
        mix = jnp.concatenate([y_a, y_b], axis=-1)
        f = jnp.einsum('bse,ed->bsd', mix, w_out[l])
        x = _layernorm(DEEPNORM_ALPHA * x + gate1[:, None, :] * f, ln1_g[l], ln1_b[l])

        h = _modulate(x, shift2, scale2)
        up = jnp.einsum('bsd,df->bsf', h, w_up[l])
        g, val = up[..., :D_FF], up[..., D_FF:]
        g = _causal_dwconv(g, conv_w[l], conv_b[l])
        f = jnp.einsum('bsf,fd->bsd', jax.nn.gelu(g) * val, w_down[l])
        x = _layernorm(DEEPNORM_ALPHA * x + gate2[:, None, :] * f, ln2_g[l], ln2_b[l])
    return x


import jax as _jax
import jax.numpy as _jnp

TWIN_FORMAT = 'train_step'
FWD_PARAMS = ['x', 'c', 'ada_w', 'ada_b', 'w_in', 'pool_w', 'pool_scale', 'sgu_ln_g', 'sgu_ln_b', 'sgu_w', 'sgu_b', 'w_out', 'ln1_g', 'ln1_b', 'w_up', 'conv_w', 'conv_b', 'w_down', 'ln2_g', 'ln2_b']
TWIN_WEIGHTS = ['ada_w', 'ada_b', 'w_in', 'pool_w', 'pool_scale', 'sgu_ln_g', 'sgu_ln_b', 'sgu_w', 'sgu_b', 'w_out', 'ln1_g', 'ln1_b', 'w_up', 'conv_w', 'conv_b', 'w_down', 'ln2_g', 'ln2_b']
TWIN_DIFF_INPUT = 'x'
TWIN_INPUTS = ['x', 'c', 'ada_w', 'ada_b', 'w_in', 'pool_w', 'pool_scale', 'sgu_ln_g', 'sgu_ln_b', 'sgu_w', 'sgu_b', 'w_out', 'ln1_g', 'ln1_b', 'w_up', 'conv_w', 'conv_b', 'w_down', 'ln2_g', 'ln2_b', 'loss_target', 'm_ada_w', 'm_ada_b', 'm_w_in', 'm_pool_w', 'm_pool_scale', 'm_sgu_ln_g', 'm_sgu_ln_b', 'm_sgu_w', 'm_sgu_b', 'm_w_out', 'm_ln1_g', 'm_ln1_b', 'm_w_up', 'm_conv_w', 'm_conv_b', 'm_w_down', 'm_ln2_g', 'm_ln2_b', 'v_ada_w', 'v_ada_b', 'v_w_in', 'v_pool_w', 'v_pool_scale', 'v_sgu_ln_g', 'v_sgu_ln_b', 'v_sgu_w', 'v_sgu_b', 'v_w_out', 'v_ln1_g', 'v_ln1_b', 'v_w_up', 'v_conv_w', 'v_conv_b', 'v_w_down', 'v_ln2_g', 'v_ln2_b']
TWIN_OUTPUTS = ['loss', 'grad_x', 'grad_ada_w', 'grad_ada_b', 'grad_w_in', 'grad_pool_w', 'grad_pool_scale', 'grad_sgu_ln_g', 'grad_sgu_ln_b', 'grad_sgu_w', 'grad_sgu_b', 'grad_w_out', 'grad_ln1_g', 'grad_ln1_b', 'grad_w_up', 'grad_conv_w', 'grad_conv_b', 'grad_w_down', 'grad_ln2_g', 'grad_ln2_b', 'delta_ada_w', 'delta_ada_b', 'delta_w_in', 'delta_pool_w', 'delta_pool_scale', 'delta_sgu_ln_g', 'delta_sgu_ln_b', 'delta_sgu_w', 'delta_sgu_b', 'delta_w_out', 'delta_ln1_g', 'delta_ln1_b', 'delta_w_up', 'delta_conv_w', 'delta_conv_b', 'delta_w_down', 'delta_ln2_g', 'delta_ln2_b', 'new_m_ada_w', 'new_m_ada_b', 'new_m_w_in', 'new_m_pool_w', 'new_m_pool_scale', 'new_m_sgu_ln_g', 'new_m_sgu_ln_b', 'new_m_sgu_w', 'new_m_sgu_b', 'new_m_w_out', 'new_m_ln1_g', 'new_m_ln1_b', 'new_m_w_up', 'new_m_conv_w', 'new_m_conv_b', 'new_m_w_down', 'new_m_ln2_g', 'new_m_ln2_b', 'new_v_ada_w', 'new_v_ada_b', 'new_v_w_in', 'new_v_pool_w', 'new_v_pool_scale', 'new_v_sgu_ln_g', 'new_v_sgu_ln_b', 'new_v_sgu_w', 'new_v_sgu_b', 'new_v_w_out', 'new_v_ln1_g', 'new_v_ln1_b', 'new_v_w_up', 'new_v_conv_w', 'new_v_conv_b', 'new_v_w_down', 'new_v_ln2_g', 'new_v_ln2_b']
TWIN_LEAF_KINDS = {'loss': 'loss', 'grad_x': 'grad_x', 'grad_ada_w': 'grad_w', 'grad_ada_b': 'grad_w', 'grad_w_in': 'grad_w', 'grad_pool_w': 'grad_w', 'grad_pool_scale': 'grad_w', 'grad_sgu_ln_g': 'grad_w', 'grad_sgu_ln_b': 'grad_w', 'grad_sgu_w': 'grad_w', 'grad_sgu_b': 'grad_w', 'grad_w_out': 'grad_w', 'grad_ln1_g': 'grad_w', 'grad_ln1_b': 'grad_w', 'grad_w_up': 'grad_w', 'grad_conv_w': 'grad_w', 'grad_conv_b': 'grad_w', 'grad_w_down': 'grad_w', 'grad_ln2_g': 'grad_w', 'grad_ln2_b': 'grad_w', 'delta_ada_w': 'delta_w', 'delta_ada_b': 'delta_w', 'delta_w_in': 'delta_w', 'delta_pool_w': 'delta_w', 'delta_pool_scale': 'delta_w', 'delta_sgu_ln_g': 'delta_w', 'delta_sgu_ln_b': 'delta_w', 'delta_sgu_w': 'delta_w', 'delta_sgu_b': 'delta_w', 'delta_w_out': 'delta_w', 'delta_ln1_g': 'delta_w', 'delta_ln1_b': 'delta_w', 'delta_w_up': 'delta_w', 'delta_conv_w': 'delta_w', 'delta_conv_b': 'delta_w', 'delta_w_down': 'delta_w', 'delta_ln2_g': 'delta_w', 'delta_ln2_b': 'delta_w', 'new_m_ada_w': 'new_m', 'new_m_ada_b': 'new_m', 'new_m_w_in': 'new_m', 'new_m_pool_w': 'new_m', 'new_m_pool_scale': 'new_m', 'new_m_sgu_ln_g': 'new_m', 'new_m_sgu_ln_b': 'new_m', 'new_m_sgu_w': 'new_m', 'new_m_sgu_b': 'new_m', 'new_m_w_out': 'new_m', 'new_m_ln1_g': 'new_m', 'new_m_ln1_b': 'new_m', 'new_m_w_up': 'new_m', 'new_m_conv_w': 'new_m', 'new_m_conv_b': 'new_m', 'new_m_w_down': 'new_m', 'new_m_ln2_g': 'new_m', 'new_m_ln2_b': 'new_m', 'new_v_ada_w': 'new_v', 'new_v_ada_b': 'new_v', 'new_v_w_in': 'new_v', 'new_v_pool_w': 'new_v', 'new_v_pool_scale': 'new_v', 'new_v_sgu_ln_g': 'new_v', 'new_v_sgu_ln_b': 'new_v', 'new_v_sgu_w': 'new_v', 'new_v_sgu_b': 'new_v', 'new_v_w_out': 'new_v', 'new_v_ln1_g': 'new_v', 'new_v_ln1_b': 'new_v', 'new_v_w_up': 'new_v', 'new_v_conv_w': 'new_v', 'new_v_conv_b': 'new_v', 'new_v_w_down': 'new_v', 'new_v_ln2_g': 'new_v', 'new_v_ln2_b': 'new_v'}


def _forward(args):
    return _fwd_reference(*[args[k] for k in FWD_PARAMS])


def _output_shape():
    def fwd():
        inp = _fwd_setup_inputs(0)
        return _fwd_reference(*[inp[k] for k in FWD_PARAMS])
    out = _jax.eval_shape(fwd)
    return out.shape, out.dtype

N_MICROBATCH = 1
ADAM_LR = 0.001
ADAM_B1 = 0.9
ADAM_B2 = 0.999
ADAM_EPS = 1e-08
ADAM_WD = 0.01
ADAM_STEP = 10
PER_EXAMPLE_BATCH_AXIS = {'x': 0, 'c': 0, 'loss_target': 0}
SHARED_INPUTS = []
_WEIGHT_DTYPES = {'ada_w': _jnp.float32, 'ada_b': _jnp.float32, 'w_in': _jnp.float32, 'pool_w': _jnp.float32, 'pool_scale': _jnp.float32, 'sgu_ln_g': _jnp.float32, 'sgu_ln_b': _jnp.float32, 'sgu_w': _jnp.float32, 'sgu_b': _jnp.float32, 'w_out': _jnp.float32, 'ln1_g': _jnp.float32, 'ln1_b': _jnp.float32, 'w_up': _jnp.float32, 'conv_w': _jnp.float32, 'conv_b': _jnp.float32, 'w_down': _jnp.float32, 'ln2_g': _jnp.float32, 'ln2_b': _jnp.float32}
MOMENT_SCALE = {'ada_w': 4.588933e-02, 'ada_b': 9.525845e-02, 'w_in': 6.948864e-02, 'pool_w': 7.960709e-02, 'pool_scale': 8.769357e-02, 'sgu_ln_g': 4.125197e-02, 'sgu_ln_b': 4.220570e-02, 'sgu_w': 4.222407e-02, 'sgu_b': 6.032977e-02, 'w_out': 2.190289e-01, 'ln1_g': 3.774713e+00, 'ln1_b': 1.651807e+00, 'w_up': 3.130574e-02, 'conv_w': 3.692376e-02, 'conv_b': 3.577298e-02, 'w_down': 1.218686e-01, 'ln2_g': 6.438749e+01, 'ln2_b': 6.618291e+00}


def _to_microbatches(a, axis):
    t = _jnp.moveaxis(a, axis, 0)
    t = t.reshape((N_MICROBATCH, t.shape[0] // N_MICROBATCH) + t.shape[1:])
    return _jnp.moveaxis(t, 1, axis + 1)


def setup_inputs(seed: int = 0) -> dict:
    inp = _fwd_setup_inputs(seed)
    key = _jax.random.fold_in(_jax.random.key(seed), 7919)
    shape, _ = _output_shape()
    out = dict(inp)
    out["loss_target"] = _jax.random.normal(_jax.random.fold_in(key, 0), shape, _jnp.float32)
    for i, name in enumerate(TWIN_WEIGHTS):
        w = inp[name].astype(_jnp.float32)
        if MOMENT_SCALE is None:
            s = _jnp.sqrt(_jnp.mean(_jnp.square(w)) + 1e-30)
        else:
            s = MOMENT_SCALE[name]
        km, kv = _jax.random.split(_jax.random.fold_in(key, i + 1))
        out[name] = w
        out["m_" + name] = s * _jax.random.normal(km, w.shape, _jnp.float32)
        out["v_" + name] = (s * s) * _jax.random.uniform(kv, w.shape, _jnp.float32, 0.5, 1.5)
    if N_MICROBATCH > 1:
        for name, axis in PER_EXAMPLE_BATCH_AXIS.items():
            out[name] = _to_microbatches(out[name], axis)
    return {'x': out['x'], 'c': out['c'], 'ada_w': out['ada_w'], 'ada_b': out['ada_b'], 'w_in': out['w_in'], 'pool_w': out['pool_w'], 'pool_scale': out['pool_scale'], 'sgu_ln_g': out['sgu_ln_g'], 'sgu_ln_b': out['sgu_ln_b'], 'sgu_w': out['sgu_w'], 'sgu_b': out['sgu_b'], 'w_out': out['w_out'], 'ln1_g': out['ln1_g'], 'ln1_b': out['ln1_b'], 'w_up': out['w_up'], 'conv_w': out['conv_w'], 'conv_b': out['conv_b'], 'w_down': out['w_down'], 'ln2_g': out['ln2_g'], 'ln2_b': out['ln2_b'], 'loss_target': out['loss_target'], 'm_ada_w': out['m_ada_w'], 'm_ada_b': out['m_ada_b'], 'm_w_in': out['m_w_in'], 'm_pool_w': out['m_pool_w'], 'm_pool_scale': out['m_pool_scale'], 'm_sgu_ln_g': out['m_sgu_ln_g'], 'm_sgu_ln_b': out['m_sgu_ln_b'], 'm_sgu_w': out['m_sgu_w'], 'm_sgu_b': out['m_sgu_b'], 'm_w_out': out['m_w_out'], 'm_ln1_g': out['m_ln1_g'], 'm_ln1_b': out['m_ln1_b'], 'm_w_up': out['m_w_up'], 'm_conv_w': out['m_conv_w'], 'm_conv_b': out['m_conv_b'], 'm_w_down': out['m_w_down'], 'm_ln2_g': out['m_ln2_g'], 'm_ln2_b': out['m_ln2_b'], 'v_ada_w': out['v_ada_w'], 'v_ada_b': out['v_ada_b'], 'v_w_in': out['v_w_in'], 'v_pool_w': out['v_pool_w'], 'v_pool_scale': out['v_pool_scale'], 'v_sgu_ln_g': out['v_sgu_ln_g'], 'v_sgu_ln_b': out['v_sgu_ln_b'], 'v_sgu_w': out['v_sgu_w'], 'v_sgu_b': out['v_sgu_b'], 'v_w_out': out['v_w_out'], 'v_ln1_g': out['v_ln1_g'], 'v_ln1_b': out['v_ln1_b'], 'v_w_up': out['v_w_up'], 'v_conv_w': out['v_conv_w'], 'v_conv_b': out['v_conv_b'], 'v_w_down': out['v_w_down'], 'v_ln2_g': out['v_ln2_g'], 'v_ln2_b': out['v_ln2_b']}


def _loss(weights, diff, rest, loss_target):
    with _jax.named_scope("forward"):
        args = {**rest, TWIN_DIFF_INPUT: diff, **{k: w.astype(_WEIGHT_DTYPES[k]) for k, w in weights.items()}}
        y = _forward(args)
    with _jax.named_scope("loss_head"):
        err = _jnp.square(y.astype(_jnp.float32) - loss_target)
        return 0.5 * _jnp.sum(_jnp.mean(err, axis=-1)) if err.ndim else 0.5 * err


def _adamw(w, g, m, v):
    m = ADAM_B1 * m + (1.0 - ADAM_B1) * g
    v = ADAM_B2 * v + (1.0 - ADAM_B2) * _jnp.square(g)
    m_hat = m / (1.0 - ADAM_B1 ** ADAM_STEP)
    v_hat = v / (1.0 - ADAM_B2 ** ADAM_STEP)
    delta = -ADAM_LR * (m_hat / (_jnp.sqrt(v_hat) + ADAM_EPS) + ADAM_WD * w)
    return delta, m, v


def reference(x, c, ada_w, ada_b, w_in, pool_w, pool_scale, sgu_ln_g, sgu_ln_b, sgu_w, sgu_b, w_out, ln1_g, ln1_b, w_up, conv_w, conv_b, w_down, ln2_g, ln2_b, loss_target, m_ada_w, m_ada_b, m_w_in, m_pool_w, m_pool_scale, m_sgu_ln_g, m_sgu_ln_b, m_sgu_w, m_sgu_b, m_w_out, m_ln1_g, m_ln1_b, m_w_up, m_conv_w, m_conv_b, m_w_down, m_ln2_g, m_ln2_b, v_ada_w, v_ada_b, v_w_in, v_pool_w, v_pool_scale, v_sgu_ln_g, v_sgu_ln_b, v_sgu_w, v_sgu_b, v_w_out, v_ln1_g, v_ln1_b, v_w_up, v_conv_w, v_conv_b, v_w_down, v_ln2_g, v_ln2_b):
    given = dict(x=x, c=c, ada_w=ada_w, ada_b=ada_b, w_in=w_in, pool_w=pool_w, pool_scale=pool_scale, sgu_ln_g=sgu_ln_g, sgu_ln_b=sgu_ln_b, sgu_w=sgu_w, sgu_b=sgu_b, w_out=w_out, ln1_g=ln1_g, ln1_b=ln1_b, w_up=w_up, conv_w=conv_w, conv_b=conv_b, w_down=w_down, ln2_g=ln2_g, ln2_b=ln2_b, loss_target=loss_target, m_ada_w=m_ada_w, m_ada_b=m_ada_b, m_w_in=m_w_in, m_pool_w=m_pool_w, m_pool_scale=m_pool_scale, m_sgu_ln_g=m_sgu_ln_g, m_sgu_ln_b=m_sgu_ln_b, m_sgu_w=m_sgu_w, m_sgu_b=m_sgu_b, m_w_out=m_w_out, m_ln1_g=m_ln1_g, m_ln1_b=m_ln1_b, m_w_up=m_w_up, m_conv_w=m_conv_w, m_conv_b=m_conv_b, m_w_down=m_w_down, m_ln2_g=m_ln2_g, m_ln2_b=m_ln2_b, v_ada_w=v_ada_w, v_ada_b=v_ada_b, v_w_in=v_w_in, v_pool_w=v_pool_w, v_pool_scale=v_pool_scale, v_sgu_ln_g=v_sgu_ln_g, v_sgu_ln_b=v_sgu_ln_b, v_sgu_w=v_sgu_w, v_sgu_b=v_sgu_b, v_w_out=v_w_out, v_ln1_g=v_ln1_g, v_ln1_b=v_ln1_b, v_w_up=v_w_up, v_conv_w=v_conv_w, v_conv_b=v_conv_b, v_w_down=v_w_down, v_ln2_g=v_ln2_g, v_ln2_b=v_ln2_b)
    weights = {n: given[n] for n in TWIN_WEIGHTS}
    shared = {n: given[n] for n in SHARED_INPUTS}
    per_example = {n: given[n] for n in ['x', 'c']}
    grad_fn = _jax.value_and_grad(_loss, argnums=(0, 1))

    def one_microbatch(ex, loss_target):
        ex = dict(ex)
        diff = ex.pop(TWIN_DIFF_INPUT)
        return grad_fn(weights, diff, {**shared, **ex}, loss_target)

    if N_MICROBATCH == 1:
        loss, (grad_w, grad_x) = one_microbatch(per_example, given["loss_target"])
    else:
        def body(carry, xs):
            loss_sum, grad_sum = carry
            l_k, (gw_k, gx_k) = one_microbatch(xs[0], xs[1])
            with _jax.named_scope("update"):
                return (loss_sum + l_k, _jax.tree.map(_jnp.add, grad_sum, gw_k)), gx_k

        init = (_jnp.zeros((), _jnp.float32), _jax.tree.map(_jnp.zeros_like, weights))
        (loss, grad_w), grad_x = _jax.lax.scan(body, init, (per_example, given["loss_target"]))
    with _jax.named_scope("update"):
        delta_w, new_m, new_v = {}, {}, {}
        for n in TWIN_WEIGHTS:
            delta_w[n], new_m[n], new_v[n] = _adamw(weights[n], grad_w[n], given["m_" + n], given["v_" + n])
    return (loss, grad_x, *[grad_w[n] for n in TWIN_WEIGHTS], *[delta_w[n] for n in TWIN_WEIGHTS],
            *[new_m[n] for n in TWIN_WEIGHTS], *[new_v[n] for n in TWIN_WEIGHTS])
```

```python
import functools

import jax
import jax.numpy as jnp
from jax import lax
from jax.experimental import pallas as pl
from jax.experimental.pallas import tpu as pltpu

F32 = jnp.float32
BF16 = jnp.bfloat16

NDEV = 8
D = 1024
DP = 512
DS = 512
DIN = DP + 2 * DS
FF = 2816
FH = FF // 2
NG = 4
GW = 128
WINDOWS = (2, 4, 8, 16)
AHALO = 16
GHALO = 8
LN_EPS = 1e-5
ADAM_LR, ADAM_B1, ADAM_B2, ADAM_EPS, ADAM_WD, ADAM_STEP = 0.001, 0.9, 0.999, 1e-08, 0.01, 10
TSF = 512
TSB = 256
_K0 = 0.7978845608028654
_K1 = 0.044715
MESH_ID = pl.DeviceIdType.MESH


def _mm(a, b):
    return jnp.dot(a, b, preferred_element_type=F32)


def _mm_nt(a, b):
    return lax.dot_general(a, b, (((1,), (1,)), ((), ())), preferred_element_type=F32)


def _mm_tn(a, b):
    return lax.dot_general(a, b, (((0,), (0,)), ((), ())), preferred_element_type=F32)


def _rowmean(x):
    return jnp.mean(x, axis=-1, keepdims=True)


def _ln_stats(x):
    mu = _rowmean(x)
    xc = x - mu
    rstd = lax.rsqrt(_rowmean(xc * xc) + LN_EPS)
    return xc * rstd, rstd


def _ln_bwd(dy, gamma, xhat, rstd):
    dxh = dy * gamma
    return rstd * (dxh - _rowmean(dxh) - xhat * _rowmean(dxh * xhat))


def _gelu_t(x):
    t = jnp.tanh(_K0 * (x + _K1 * (x * x * x)))
    return 0.5 * x * (1.0 + t), t


def _dgelu(x, t):
    return 0.5 * (1.0 + t) + 0.5 * x * (1.0 - t * t) * (_K0 * (1.0 + 3.0 * _K1 * (x * x)))


def _colsum8(x):
    t, n = x.shape
    return jnp.sum(x.reshape(t // 8, 8, n), axis=0)


def _tril_mask():
    r = lax.broadcasted_iota(jnp.int32, (GW, GW), 0)
    c = lax.broadcasted_iota(jnp.int32, (GW, GW), 1)
    return c <= r


def _full(shape):
    n = len(shape)
    return pl.BlockSpec(shape, lambda *_: (0,) * n)


def _resident(tail, lead):
    n = len(tail)
    return pl.BlockSpec((None,) * len(lead) + tuple(tail), lambda *_: tuple(lead) + (0,) * n,
                        pipeline_mode=pl.Buffered(1))


def _layer_vec(rows, width, l):
    return pl.BlockSpec((None, rows, width), lambda *_: (l, 0, 0))


_VMEM_WHOLE = pl.BlockSpec(memory_space=pltpu.VMEM)
_ARB = pltpu.CompilerParams(dimension_semantics=("arbitrary",))


def _a_forward(x, modv_ref, winT_ref, pw_ref, ps_ref, sln_ref, sw_ref, sbf_ref, abuf, zbuf, tile, ts):
    sh1 = modv_ref[0:1, :]
    sc1 = modv_ref[1:2, :]
    hb = (x * (1.0 + sc1) + sh1).astype(BF16)
    proj = _mm_nt(hb, winT_ref[...])
    a = proj[:, 0:DP]
    u = proj[:, DP:DP + DS]
    v = proj[:, DP + DS:]
    abuf[AHALO:AHALO + ts, :] = a
    tglob = tile * ts + lax.broadcasted_iota(jnp.int32, (ts, 1), 0)
    pooled_b, mixed, inv_cnt, pwb = [], [], [], []
    for g, w in enumerate(WINDOWS):
        cs = slice(g * GW, (g + 1) * GW)
        a_g = abuf[AHALO:AHALO + ts, cs]
        s = a_g
        for k in range(1, w):
            s = s + abuf[AHALO - k:AHALO - k + ts, cs]
        inv = 1.0 / jnp.minimum(tglob + 1, w).astype(F32)
        pg = (s * inv - a_g).astype(BF16)
        wg = pw_ref[g].astype(BF16)
        pooled_b.append(pg)
        inv_cnt.append(inv)
        pwb.append(wg)
        mixed.append(_mm(pg, wg))
    mixed = jnp.concatenate(mixed, axis=1)
    ya = mixed * ps_ref[...]
    ug, tu = _gelu_t(u)
    vg, tv = _gelu_t(v)
    vhat, rstdv = _ln_stats(vg)
    vnb = (vhat * sln_ref[0:1, :] + sln_ref[1:2, :]).astype(BF16)
    tri = _tril_mask()
    wt = [jnp.where(tri, sw_ref[h], 0.0).astype(BF16) for h in range(NG)]
    for c in range(ts // GW):
        rs = slice(c * GW, (c + 1) * GW)
        for h in range(NG):
            cs = slice(h * GW, (h + 1) * GW)
            zbuf[rs, cs] = _mm(wt[h], vnb[rs, cs]) + sbf_ref[h]
    z = zbuf[...]
    yb = ug * z
    return dict(hb=hb, sc1=sc1, u=u, v=v, tu=tu, tv=tv, ug=ug, z=z, vhat=vhat, rstdv=rstdv, vnb=vnb,
                wt=wt, pooled_b=pooled_b, pwb=pwb, inv_cnt=inv_cnt, mixed=mixed, ya=ya, yb=yb)


def _fa_call(l, first, alpha, xin, modv, lnp, winT, wout, pool_w, pool_scale, sgu_ln, sgu_w, sgu_bf):
    s = xin.shape[0]
    nt = s // TSF
    rr = TSF // TSB

    def body(xin_ref, modv_ref, lnp_ref, winT_ref, wout_ref, pw_ref, ps_ref, sln_ref, sw_ref, sbf_ref,
             r1_ref, f1_ref, ah_ref, abuf, zbuf, mixbuf):
        i = pl.program_id(0)

        @pl.when(i == 0)
        def _():
            abuf[0:AHALO, :] = jnp.zeros((AHALO, DP), F32)

        x = xin_ref[...]
        if not first:
            xhat, _ = _ln_stats(x)
            x = xhat * lnp_ref[0:1, :] + lnp_ref[1:2, :]
        fw = _a_forward(x, modv_ref, winT_ref, pw_ref, ps_ref, sln_ref, sw_ref, sbf_ref, abuf, zbuf, i, TSF)
        mixbuf[:, 0:DP] = fw["ya"].astype(BF16)
        mixbuf[:, DP:] = fw["yb"].astype(BF16)
        f = _mm(mixbuf[...], wout_ref[...])
        r1_ref[...] = alpha * x + modv_ref[2:3, :] * f
        f1_ref[...] = f.astype(BF16)
        for r in range(rr):
            ah_ref[r] = abuf[(r + 1) * TSB:(r + 1) * TSB + AHALO, :]
        abuf[0:AHALO, :] = abuf[TSF:TSF + AHALO, :]

    return pl.pallas_call(
        body, name=f"fa{l}", grid=(nt,),
        in_specs=[pl.BlockSpec((TSF, D), lambda i: (i, 0)), _layer_vec(8, D, l), _layer_vec(2, D, max(l - 1, 0)),
                  _resident((DIN, D), (l,)), _resident((D, D), (l,)),
                  pl.BlockSpec((None, NG, GW, GW), lambda i: (l, 0, 0, 0)), _layer_vec(1, DP, l),
                  _layer_vec(2, DS, l), pl.BlockSpec((None, NG, GW, GW), lambda i: (l, 0, 0, 0)),
                  pl.BlockSpec((None, NG, GW, GW), lambda i: (l, 0, 0, 0))],
        out_specs=[pl.BlockSpec((TSF, D), lambda i: (i, 0)), pl.BlockSpec((TSF, D), lambda i: (i, 0)),
                   pl.BlockSpec((rr, AHALO, DP), lambda i: (i, 0, 0))],
        out_shape=[jax.ShapeDtypeStruct((s, D), F32), jax.ShapeDtypeStruct((s, D), BF16),
                   jax.ShapeDtypeStruct((s // TSB, AHALO, DP), F32)],
        scratch_shapes=[pltpu.VMEM((TSF + AHALO, DP), F32), pltpu.VMEM((TSF, DS), F32),
                        pltpu.VMEM((TSF, D), BF16)],
        compiler_params=_ARB,
    )(xin, modv, lnp, winT, wout, pool_w, pool_scale, sgu_ln, sgu_w, sgu_bf)


def _ba_call(l, first, alpha, xin, r1, dx1, f1, ahalo, modv, lnp, ln1, winT, wout, pool_w, pool_scale, sgu_ln,
             sgu_w, sgu_bf):
    s = xin.shape[0]
    nt = s // TSB
    ts = TSB

    def body(xin_ref, r1_ref, dx1_ref, f1_ref, ah_ref, modv_ref, lnp_ref, ln1_ref, winT_ref, wout_ref, pw_ref,
             ps_ref, sln_ref, sw_ref, sbf_ref,
             dx_ref, dwin_ref, dwout_ref, dpw_ref, dsw_ref, dsb_ref, va_ref, va5_ref,
             abuf, qbuf, zbuf, dvnbuf, mixbuf, dpbuf, dbacc, vacc, vacc5):
        i = pl.program_id(0)
        j = nt - 1 - i

        @pl.when(i == 0)
        def _():
            dwin_ref[...] = jnp.zeros_like(dwin_ref)
            dwout_ref[...] = jnp.zeros_like(dwout_ref)
            dpw_ref[...] = jnp.zeros_like(dpw_ref)
            dsw_ref[...] = jnp.zeros_like(dsw_ref)
            dbacc[...] = jnp.zeros_like(dbacc)
            vacc[...] = jnp.zeros_like(vacc)
            vacc5[...] = jnp.zeros_like(vacc5)
            qbuf[ts:ts + AHALO, :] = jnp.zeros((AHALO, DP), F32)

        x = xin_ref[...]
        if not first:
            xhat, _ = _ln_stats(x)
            x = xhat * lnp_ref[0:1, :] + lnp_ref[1:2, :]
        abuf[0:AHALO, :] = jnp.where(j > 0, ah_ref[0], 0.0)
        fw = _a_forward(x, modv_ref, winT_ref, pw_ref, ps_ref, sln_ref, sw_ref, sbf_ref, abuf, zbuf, j, ts)
        mixbuf[:, 0:DP] = fw["ya"].astype(BF16)
        mixbuf[:, DP:] = fw["yb"].astype(BF16)

        xhat1, rstd1 = _ln_stats(r1_ref[...])
        dy = dx1_ref[...]
        vacc[0] += _colsum8(dy * xhat1)
        vacc[1] += _colsum8(dy)
        dr1 = _ln_bwd(dy, ln1_ref[0:1, :], xhat1, rstd1)
        vacc[2] += _colsum8(dr1 * f1_ref[...].astype(F32))
        dfb = (dr1 * modv_ref[2:3, :]).astype(BF16)
        dwout_ref[...] += _mm_tn(mixbuf[...], dfb)
        dmix = _mm_nt(dfb, wout_ref[...])
        dya = dmix[:, 0:DP]
        dyb = dmix[:, DP:]

        vacc5[0] += _colsum8(dya * fw["mixed"])
        dmixed = (dya * ps_ref[...]).astype(BF16)
        dpooled = []
        for g in range(NG):
            cs = slice(g * GW, (g + 1) * GW)
            dpw_ref[g] += _mm_tn(fw["pooled_b"][g], dmixed[:, cs])
            dpg = _mm_nt(dmixed[:, cs], fw["pwb"][g])
            dpooled.append(dpg)
            qbuf[0:ts, cs] = dpg * fw["inv_cnt"][g]
        for g, w in enumerate(WINDOWS):
            cs = slice(g * GW, (g + 1) * GW)
            sq = qbuf[0:ts, cs]
            for k in range(1, w):
                sq = sq + qbuf[k:k + ts, cs]
            dpbuf[:, cs] = (sq - dpooled[g]).astype(BF16)
        qbuf[ts:ts + AHALO, :] = qbuf[0:AHALO, :]

        dug = dyb * fw["z"]
        dz = dyb * fw["ug"]
        dzb = dz.astype(BF16)
        for c in range(ts // GW):
            rs = slice(c * GW, (c + 1) * GW)
            for h in range(NG):
                cs = slice(h * GW, (h + 1) * GW)
                dvnbuf[rs, cs] = _mm_tn(fw["wt"][h], dzb[rs, cs])
                dsw_ref[h] += _mm_nt(dzb[rs, cs], fw["vnb"][rs, cs])
            dbacc[...] += dz[rs, :]
        dvn = dvnbuf[...]
        vacc5[1] += _colsum8(dvn * fw["vhat"])
        vacc5[2] += _colsum8(dvn)
        dvg = _ln_bwd(dvn, sln_ref[0:1, :], fw["vhat"], fw["rstdv"])
        dpbuf[:, DP:DP + DS] = (dug * _dgelu(fw["u"], fw["tu"])).astype(BF16)
        dpbuf[:, DP + DS:] = (dvg * _dgelu(fw["v"], fw["tv"])).astype(BF16)

        dpb = dpbuf[...]
        dwin_ref[...] += _mm_tn(dpb, fw["hb"])
        dh = _mm(dpb, winT_ref[...])
        dx_ref[...] = dh * (1.0 + fw["sc1"]) + alpha * dr1
        vacc[3] += _colsum8(dh)
        vacc[4] += _colsum8(dh * x)

        @pl.when(i == nt - 1)
        def _():
            tri = _tril_mask()
            for h in range(NG):
                dsw_ref[h] = jnp.where(tri, dsw_ref[h], 0.0)
                sb = jnp.sum(dbacc[:, h * GW:(h + 1) * GW], axis=1, keepdims=True)
                dsb_ref[h] = jnp.broadcast_to(sb, (GW, GW))
            for n in range(5):
                va_ref[n:n + 1, :] = jnp.sum(vacc[n], axis=0, keepdims=True)
            for n in range(3):
                va5_ref[n:n + 1, :] = jnp.sum(vacc5[n], axis=0, keepdims=True)

    rev = lambda i: (nt - 1 - i, 0)
    small = pl.BlockSpec((None, NG, GW, GW), lambda i: (l, 0, 0, 0))
    return pl.pallas_call(
        body, name=f"ba{l}", grid=(nt,),
        in_specs=[pl.BlockSpec((ts, D), rev), pl.BlockSpec((ts, D), rev), pl.BlockSpec((ts, D), rev),
                  pl.BlockSpec((ts, D), rev),
                  pl.BlockSpec((1, AHALO, DP), lambda i: (jnp.maximum(nt - 2 - i, 0), 0, 0)),
                  _layer_vec(8, D, l), _layer_vec(2, D, max(l - 1, 0)), _layer_vec(2, D, l),
                  _resident((DIN, D), (l,)), _resident((D, D), (l,)),
                  small, _layer_vec(1, DP, l), _layer_vec(2, DS, l), small, small],
        out_specs=[pl.BlockSpec((ts, D), rev), _VMEM_WHOLE, _VMEM_WHOLE, _VMEM_WHOLE, _VMEM_WHOLE, _VMEM_WHOLE,
                   _VMEM_WHOLE, _VMEM_WHOLE],
        out_shape=[jax.ShapeDtypeStruct((s, D), F32), jax.ShapeDtypeStruct((DIN, D), F32),
                   jax.ShapeDtypeStruct((D, D), F32), jax.ShapeDtypeStruct((NG, GW, GW), F32),
                   jax.ShapeDtypeStruct((NG, GW, GW), F32), jax.ShapeDtypeStruct((NG, GW, GW), F32),
                   jax.ShapeDtypeStruct((5, D), F32), jax.ShapeDtypeStruct((3, DP), F32)],
        scratch_shapes=[pltpu.VMEM((ts + AHALO, DP), F32), pltpu.VMEM((ts + AHALO, DP), F32),
                        pltpu.VMEM((ts, DS), F32), pltpu.VMEM((ts, DS), F32), pltpu.VMEM((ts, D), BF16),
                        pltpu.VMEM((ts, DIN), BF16), pltpu.VMEM((GW, DS), F32), pltpu.VMEM((5, 8, D), F32),
                        pltpu.VMEM((3, 8, DP), F32)],
        compiler_params=_ARB,
    )(xin, r1, dx1, f1, ahalo, modv, lnp, ln1, winT, wout, pool_w, pool_scale, sgu_ln, sgu_w, sgu_bf)


def _conv_gate(gbuf_at, g, cw, cb, ts):
    return cb + cw[0:1, :] * gbuf_at(0) + cw[1:2, :] * gbuf_at(1) + cw[2:3, :] * g


def _fb_call(l, alpha, r1, modv, ln1, wup4, wd2, conv_w, conv_b):
    s = r1.shape[0]
    nt = s // TSF
    rr = TSF // TSB

    def body(r1_ref, modv_ref, ln1_ref, wup_ref, wd_ref, cw_ref, cb_ref, r2_ref, f2_ref, gh_ref, gbuf):
        i = pl.program_id(0)

        @pl.when(i == 0)
        def _():
            gbuf[:, 0:GHALO, :] = jnp.zeros((2, GHALO, FH), F32)

        xhat1, _ = _ln_stats(r1_ref[...])
        x1 = xhat1 * ln1_ref[0:1, :] + ln1_ref[1:2, :]
        h2b = (x1 * (1.0 + modv_ref[4:5, :]) + modv_ref[3:4, :]).astype(BF16)
        f2 = jnp.zeros((TSF, D), F32)
        for hf in range(2):
            cs = slice(hf * FH, (hf + 1) * FH)
            g = _mm_nt(h2b, wup_ref[hf])
            val = _mm_nt(h2b, wup_ref[2 + hf])
            gbuf[hf, GHALO:GHALO + TSF, :] = g
            gc = _conv_gate(lambda k: gbuf[hf, GHALO - 2 + k:GHALO - 2 + k + TSF, :], g, cw_ref[:, cs],
                            cb_ref[:, cs], TSF)
            ge, _ = _gelu_t(gc)
            f2 = f2 + _mm((ge * val).astype(BF16), wd_ref[hf])
            for r in range(rr):
                gh_ref[r, :, cs] = gbuf[hf, (r + 1) * TSB:(r + 1) * TSB + GHALO, :]
            gbuf[hf, 0:GHALO, :] = gbuf[hf, TSF:TSF + GHALO, :]
        r2_ref[...] = alpha * x1 + modv_ref[5:6, :] * f2
        f2_ref[...] = f2.astype(BF16)

    return pl.pallas_call(
        body, name=f"fb{l}", grid=(nt,),
        in_specs=[pl.BlockSpec((TSF, D), lambda i: (i, 0)), _layer_vec(8, D, l), _layer_vec(2, D, l),
                  _resident((4, FH, D), (l,)), _resident((2, FH, D), (l,)),
                  _layer_vec(3, FF, l), _layer_vec(1, FF, l)],
        out_specs=[pl.BlockSpec((TSF, D), lambda i: (i, 0)), pl.BlockSpec((TSF, D), lambda i: (i, 0)),
                   pl.BlockSpec((rr, GHALO, FF), lambda i: (i, 0, 0))],
        out_shape=[jax.ShapeDtypeStruct((s, D), F32), jax.ShapeDtypeStruct((s, D), BF16),
                   jax.ShapeDtypeStruct((s // TSB, GHALO, FF), F32)],
        scratch_shapes=[pltpu.VMEM((2, TSF + GHALO, FH), F32)],
        compiler_params=_ARB,
    )(r1, modv, ln1, wup4, wd2, conv_w, conv_b)


def _bb_call(l, hf, last, alpha, r1, r2, dx2, f2, ghalo, dh2_in, modv, ln1, ln2, wup4, wd2, conv_w, conv_b):
    s = r1.shape[0]
    nt = s // TSB
    ts = TSB
    nvec = 3 if hf == 0 else 2
    with_loss = last and hf == 0

    def body(*refs):
        it = iter(refs)
        r1_ref, r2_ref, dx2_ref, f2_ref, gh_ref = next(it), next(it), next(it), next(it), next(it)
        dh2_ref = next(it) if hf == 1 else None
        modv_ref, ln1_ref, ln2_ref, wg_ref, wv_ref, wd_ref, cw_ref, cb_ref = (next(it) for _ in range(8))
        out_ref, dwg_ref, dwv_ref, dwd_ref, vb_ref, cacc_ref = (next(it) for _ in range(6))
        loss_ref = next(it) if with_loss else None
        gbuf, dgcbuf, vacc, cacc = next(it), next(it), next(it), next(it)
        lacc = next(it) if with_loss else None

        i = pl.program_id(0)
        j = nt - 1 - i

        @pl.when(i == 0)
        def _():
            dwg_ref[...] = jnp.zeros_like(dwg_ref)
            dwv_ref[...] = jnp.zeros_like(dwv_ref)
            dwd_ref[...] = jnp.zeros_like(dwd_ref)
            vacc[...] = jnp.zeros_like(vacc)
            cacc[...] = jnp.zeros_like(cacc)
            dgcbuf[ts:ts + GHALO, :] = jnp.zeros((GHALO, FH), F32)
            if with_loss:
                lacc[...] = jnp.zeros_like(lacc)

        xhat1, _ = _ln_stats(r1_ref[...])
        x1 = xhat1 * ln1_ref[0:1, :] + ln1_ref[1:2, :]
        sc2 = modv_ref[4:5, :]
        h2b = (x1 * (1.0 + sc2) + modv_ref[3:4, :]).astype(BF16)
        g = _mm_nt(h2b, wg_ref[...])
        val = _mm_nt(h2b, wv_ref[...])
        gbuf[0:GHALO, :] = jnp.where(j > 0, gh_ref[0], 0.0)
        gbuf[GHALO:GHALO + ts, :] = g
        cw = cw_ref[...]
        gc = _conv_gate(lambda k: gbuf[GHALO - 2 + k:GHALO - 2 + k + ts, :], g, cw, cb_ref[...], ts)
        ge, tg = _gelu_t(gc)
        mb = (ge * val).astype(BF16)

        xhat2, rstd2 = _ln_stats(r2_ref[...])
        if last:
            diff = xhat2 * ln2_ref[0:1, :] + ln2_ref[1:2, :] - dx2_ref[...]
            dy = diff * (1.0 / D)
            if with_loss:
                lacc[...] += _colsum8(diff * diff)
        else:
            dy = dx2_ref[...]
        dr2 = _ln_bwd(dy, ln2_ref[0:1, :], xhat2, rstd2)
        if hf == 0:
            vacc[0] += _colsum8(dy * xhat2)
            vacc[1] += _colsum8(dy)
            vacc[2] += _colsum8(dr2 * f2_ref[...].astype(F32))
        df2b = (dr2 * modv_ref[5:6, :]).astype(BF16)

        dm = _mm_nt(df2b, wd_ref[...])
        dwd_ref[...] += _mm_tn(mb, df2b)
        dval = dm * ge
        dgc = dm * val * _dgelu(gc, tg)
        cacc[0] += _colsum8(dgc)
        for k in range(3):
            cacc[1 + k] += _colsum8(dgc * gbuf[GHALO - 2 + k:GHALO - 2 + k + ts, :])
        dgcbuf[0:ts, :] = dgc
        dg = cw[2:3, :] * dgc + cw[1:2, :] * dgcbuf[1:1 + ts, :] + cw[0:1, :] * dgcbuf[2:2 + ts, :]
        dgcbuf[ts:ts + GHALO, :] = dgcbuf[0:GHALO, :]

        dgb = dg.astype(BF16)
        dvb = dval.astype(BF16)
        dwg_ref[...] += _mm_tn(dgb, h2b)
        dwv_ref[...] += _mm_tn(dvb, h2b)
        dh2 = _mm(dgb, wg_ref[...]) + _mm(dvb, wv_ref[...])
        if hf == 0:
            out_ref[...] = dh2
        else:
            dh2 = dh2 + dh2_ref[...]
            out_ref[...] = dh2 * (1.0 + sc2) + alpha * dr2
            vacc[0] += _colsum8(dh2)
            vacc[1] += _colsum8(dh2 * x1)

        @pl.when(i == nt - 1)
        def _():
            for n in range(nvec):
                vb_ref[n:n + 1, :] = jnp.sum(vacc[n], axis=0, keepdims=True)
            for n in range(4):
                cacc_ref[n:n + 1, :] = jnp.sum(cacc[n], axis=0, keepdims=True)
            if with_loss:
                loss_ref[...] = lacc[...]

    rev = lambda i: (nt - 1 - i, 0)
    tile = pl.BlockSpec((ts, D), rev)
    in_specs = [tile, tile, tile, tile,
                pl.BlockSpec((1, GHALO, FH), lambda i: (jnp.maximum(nt - 2 - i, 0), 0, hf))]
    args = [r1, r2, dx2, f2, ghalo]
    if hf == 1:
        in_specs.append(tile)
        args.append(dh2_in)
    in_specs += [_layer_vec(8, D, l), _layer_vec(2, D, l), _layer_vec(2, D, l),
                 _resident((FH, D), (l, hf)), _resident((FH, D), (l, 2 + hf)), _resident((FH, D), (l, hf)),
                 pl.BlockSpec((None, 3, FH), lambda i: (l, 0, hf)), pl.BlockSpec((None, 1, FH), lambda i: (l, 0, hf))]
    args += [modv, ln1, ln2, wup4, wup4, wd2, conv_w, conv_b]
    out_specs = [tile, _VMEM_WHOLE, _VMEM_WHOLE, _VMEM_WHOLE, _VMEM_WHOLE, _VMEM_WHOLE]
    out_shape = [jax.ShapeDtypeStruct((s, D), F32), jax.ShapeDtypeStruct((FH, D), F32),
                 jax.ShapeDtypeStruct((FH, D), F32), jax.ShapeDtypeStruct((FH, D), F32),
                 jax.ShapeDtypeStruct((nvec, D), F32), jax.ShapeDtypeStruct((4, FH), F32)]
    scratch = [pltpu.VMEM((ts + GHALO, FH), F32), pltpu.VMEM((ts + GHALO, FH), F32),
               pltpu.VMEM((nvec, 8, D), F32), pltpu.VMEM((4, 8, FH), F32)]
    if with_loss:
        out_specs.append(_VMEM_WHOLE)
        out_shape.append(jax.ShapeDtypeStruct((8, D), F32))
        scratch.append(pltpu.VMEM((8, D), F32))
    return pl.pallas_call(
        body, name=f"bb{l}_{hf}", grid=(nt,), in_specs=in_specs, out_specs=out_specs, out_shape=out_shape,
        scratch_shapes=scratch, compiler_params=_ARB,
    )(*args)


def _silu(c):
    return c * (1.0 / (1.0 + jnp.exp(-c)))


def _ada_fwd_call(c_all, ada_w, ada_b_my):
    nl, _, wcols = ada_w.shape

    def body(c_ref, w_ref, b_ref, o_ref):
        ca = _silu(c_ref[...])
        o_ref[...] = jnp.dot(ca, w_ref[...], preferred_element_type=F32,
                             precision=lax.Precision.HIGHEST) + b_ref[...]

    return pl.pallas_call(
        body, name="ada_fwd", grid=(nl,),
        in_specs=[_full((NDEV, D)), pl.BlockSpec((None, D, wcols), lambda l: (l, 0, 0)),
                  pl.BlockSpec((None, 1, wcols), lambda l: (l, 0, 0))],
        out_specs=pl.BlockSpec((None, NDEV, wcols), lambda l: (l, 0, 0)),
        out_shape=jax.ShapeDtypeStruct((nl, NDEV, wcols), F32),
        compiler_params=_ARB,
    )(c_all, ada_w, ada_b_my)


def _adam_update(w, g, m, v):
    m2 = ADAM_B1 * m + (1.0 - ADAM_B1) * g
    v2 = ADAM_B2 * v + (1.0 - ADAM_B2) * (g * g)
    m_hat = m2 / (1.0 - ADAM_B1 ** ADAM_STEP)
    v_hat = v2 / (1.0 - ADAM_B2 ** ADAM_STEP)
    delta = -ADAM_LR * (m_hat / (jnp.sqrt(v_hat) + ADAM_EPS) + ADAM_WD * w)
    return delta, m2, v2


def _ada_bwd_call(c_t, dmod_my, w, m, v):
    nl, _, wcols = w.shape
    rb = 256

    def body(ct_ref, dm_ref, w_ref, m_ref, v_ref, g_ref, d_ref, m2_ref, v2_ref):
        ca_t = _silu(ct_ref[...])
        dm = dm_ref[...]
        g = ca_t[:, 0:1] * dm[0:1, :]
        for b in range(1, NDEV):
            g = g + ca_t[:, b:b + 1] * dm[b:b + 1, :]
        delta, m2, v2 = _adam_update(w_ref[...], g, m_ref[...], v_ref[...])
        g_ref[...] = g
        d_ref[...] = delta
        m2_ref[...] = m2
        v2_ref[...] = v2

    blk = pl.BlockSpec((None, rb, wcols), lambda l, i: (l, i, 0))
    shp = jax.ShapeDtypeStruct(w.shape, F32)
    return pl.pallas_call(
        body, name="ada_bwd", grid=(nl, D // rb),
        in_specs=[pl.BlockSpec((rb, NDEV), lambda l, i: (i, 0)),
                  pl.BlockSpec((None, NDEV, wcols), lambda l, i: (l, 0, 0)), blk, blk, blk],
        out_specs=[blk, blk, blk, blk], out_shape=[shp, shp, shp, shp],
        compiler_params=pltpu.CompilerParams(dimension_semantics=("arbitrary", "arbitrary")),
    )(c_t, dmod_my, w, m, v)


def _row_block(r, row_bytes):
    budget = 6 * 1024 * 1024
    best = None
    for rb in range(8, min(r, 512) + 1, 8):
        if r % rb == 0 and rb * row_bytes <= budget:
            best = rb
    return best if best is not None else r


def _sum_parts_call(parts, name):
    nl, npart, r, c = parts.shape
    rb = _row_block(r, (npart + 1) * c * 4)

    def body(p_ref, g_ref):
        g = p_ref[0]
        for k in range(1, npart):
            g = g + p_ref[k]
        g_ref[...] = g

    return pl.pallas_call(
        body, name=name, grid=(nl, r // rb),
        in_specs=[pl.BlockSpec((None, npart, rb, c), lambda l, i: (l, 0, i, 0))],
        out_specs=pl.BlockSpec((None, rb, c), lambda l, i: (l, i, 0)),
        out_shape=jax.ShapeDtypeStruct((nl, r, c), F32),
        compiler_params=pltpu.CompilerParams(dimension_semantics=("arbitrary", "arbitrary")),
    )(parts)


def _adamw_call(parts, w, m, v, name):
    nl, npart, r, c = parts.shape
    rb = _row_block(r, (npart + 7) * c * 4)

    def body(p_ref, w_ref, m_ref, v_ref, g_ref, d_ref, m2_ref, v2_ref):
        g = p_ref[0]
        for k in range(1, npart):
            g = g + p_ref[k]
        delta, m2, v2 = _adam_update(w_ref[...], g, m_ref[...], v_ref[...])
        g_ref[...] = g
        d_ref[...] = delta
        m2_ref[...] = m2
        v2_ref[...] = v2

    blk = pl.BlockSpec((None, rb, c), lambda l, i: (l, i, 0))
    shp = jax.ShapeDtypeStruct((nl, r, c), F32)
    return pl.pallas_call(
        body, name=name, grid=(nl, r // rb),
        in_specs=[pl.BlockSpec((None, npart, rb, c), lambda l, i: (l, 0, i, 0)), blk, blk, blk],
        out_specs=[blk, blk, blk, blk], out_shape=[shp, shp, shp, shp],
        compiler_params=pltpu.CompilerParams(dimension_semantics=("arbitrary", "arbitrary")),
    )(parts, w, m, v)


def _me():
    return 4 * lax.axis_index("x") + 2 * lax.axis_index("y") + lax.axis_index("c")


def _coords(p):
    return (p >> 2, (p >> 1) & 1, p & 1)


_HBM = pl.BlockSpec(memory_space=pl.ANY)


def _all_gather_call(arrs, name):
    na = len(arrs)

    def body(*refs):
        ins, outs = refs[:na], refs[na:2 * na]
        send, recv, lsem = refs[2 * na:]
        me = _me()

        def remote(a, src_dev, dst_dev):
            n = ins[a].shape[0]
            return pltpu.make_async_remote_copy(
                src_ref=ins[a], dst_ref=outs[a].at[pl.ds(0, n), src_dev], send_sem=send.at[a, dst_dev],
                recv_sem=recv.at[a, src_dev], device_id=_coords(dst_dev) if isinstance(dst_dev, int) else dst_dev,
                device_id_type=MESH_ID)

        local = [pltpu.make_async_copy(ins[a], outs[a].at[pl.ds(0, ins[a].shape[0]), me], lsem.at[a])
                 for a in range(na)]
        for cp in local:
            cp.start()
        for p in range(NDEV):
            @pl.when(me != p)
            def _():
                for a in range(na):
                    remote(a, me, p).start()
        for p in range(NDEV):
            @pl.when(me != p)
            def _():
                for a in range(na):
                    remote(a, p, p).wait_recv()
                    remote(a, me, p).wait_send()
        for cp in local:
            cp.wait()

    return pl.pallas_call(
        body, name=name,
        in_specs=[_HBM] * na, out_specs=[_HBM] * na,
        out_shape=[jax.ShapeDtypeStruct((a.shape[0], NDEV) + a.shape[1:], a.dtype) for a in arrs],
        scratch_shapes=[pltpu.SemaphoreType.DMA((na, NDEV)), pltpu.SemaphoreType.DMA((na, NDEV)),
                        pltpu.SemaphoreType.DMA((na,))],
    )(*arrs)


def _exchange_call(l, groups, bufs, name):
    flat = [arr for grp in groups for arr, _ in grp]
    nin = len(flat)
    nk = len(groups)
    route = []
    base = 0
    for grp in groups:
        table = {}
        for n, (arr, first) in enumerate(grp):
            for q in range(arr.shape[0]):
                table[first + q] = (base + n, q)
        base += len(grp)
        route.append([table[p] for p in range(NDEV)])

    def body(*refs):
        ins = refs[:nin]
        outs = refs[nin + nk:nin + 2 * nk]
        send, recv, lsem = refs[nin + 2 * nk:]
        me = _me()

        def remote(k, src_dev, dst_dev):
            a, q = route[k][dst_dev]
            return pltpu.make_async_remote_copy(
                src_ref=ins[a].at[q], dst_ref=outs[k].at[l, src_dev], send_sem=send.at[k, dst_dev],
                recv_sem=recv.at[k, src_dev], device_id=_coords(dst_dev), device_id_type=MESH_ID)

        for p in range(NDEV):
            @pl.when(me == p)
            def _():
                for k in range(nk):
                    a, q = route[k][p]
                    pltpu.make_async_copy(ins[a].at[q], outs[k].at[l, p], lsem.at[k]).start()

            @pl.when(me != p)
            def _():
                for k in range(nk):
                    remote(k, me, p).start()
        for p in range(NDEV):
            @pl.when(me == p)
            def _():
                for k in range(nk):
                    a, q = route[k][p]
                    pltpu.make_async_copy(ins[a].at[q], outs[k].at[l, p], lsem.at[k]).wait()

            @pl.when(me != p)
            def _():
                for k in range(nk):
                    remote(k, p, p).wait_recv()
                    remote(k, me, p).wait_send()

    return pl.pallas_call(
        body, name=name,
        in_specs=[_HBM] * (nin + nk), out_specs=[_HBM] * nk,
        out_shape=[jax.ShapeDtypeStruct(b.shape, b.dtype) for b in bufs],
        input_output_aliases={nin + k: k for k in range(nk)},
        scratch_shapes=[pltpu.SemaphoreType.DMA((nk, NDEV)), pltpu.SemaphoreType.DMA((nk, NDEV)),
                        pltpu.SemaphoreType.DMA((nk,))],
    )(*flat, *bufs)


def kernel(x, c, ada_w, ada_b, w_in, pool_w, pool_scale, sgu_ln_g, sgu_ln_b, sgu_w, sgu_b, w_out, ln1_g, ln1_b, w_up, conv_w, conv_b, w_down, ln2_g, ln2_b, loss_target, m_ada_w, m_ada_b, m_w_in, m_pool_w, m_pool_scale, m_sgu_ln_g, m_sgu_ln_b, m_sgu_w, m_sgu_b, m_w_out, m_ln1_g, m_ln1_b, m_w_up, m_conv_w, m_conv_b, m_w_down, m_ln2_g, m_ln2_b, v_ada_w, v_ada_b, v_w_in, v_pool_w, v_pool_scale, v_sgu_ln_g, v_sgu_ln_b, v_sgu_w, v_sgu_b, v_w_out, v_ln1_g, v_ln1_b, v_w_up, v_conv_w, v_conv_b, v_w_down, v_ln2_g, v_ln2_b):
    nl = ada_w.shape[0]
    s = x.shape[1]
    alpha = (2.0 * nl) ** 0.25
    me = _me()
    x2 = x[0]
    tgt = loss_target[0]
    acols = ada_w.shape[2]
    icols = w_in.shape[2]
    ucols = w_up.shape[2]
    orows = w_out.shape[1]
    drows = w_down.shape[1]
    ccols = conv_w.shape[2]

    (c_g,) = _all_gather_call([jnp.broadcast_to(c[None], (1, 8, D))], "gather_c")
    c_all = c_g[0, :, 0, :]
    ada_b_my = lax.dynamic_slice(ada_b, (0, me * acols), (nl, acols))[:, None, :]
    mod_blk = _ada_fwd_call(c_all, ada_w, ada_b_my)
    (mod_g,) = _all_gather_call([mod_blk], "gather_mod")
    mod_me = lax.dynamic_index_in_dim(mod_g, me, axis=2, keepdims=False)
    modv = mod_me.reshape(nl, 6, D)
    modv = jnp.concatenate([modv, jnp.zeros((nl, 2, D), F32)], axis=1)

    winT_sh = jnp.swapaxes(w_in, 1, 2).astype(BF16)
    wupT_sh = jnp.swapaxes(w_up, 1, 2).astype(BF16)
    winT_g, wupT_g, wout_g, wd_g, cw_g = _all_gather_call(
        [winT_sh, wupT_sh, w_out.astype(BF16), w_down.astype(BF16), conv_w], "gather_w")
    winT = winT_g.reshape(nl, DIN, D)
    wup4 = wupT_g.reshape(nl, 4, FH, D)
    wout = wout_g.reshape(nl, D, D)
    wd2 = wd_g.reshape(nl, 2, FH, D)
    cw_full = jnp.swapaxes(cw_g, 1, 2).reshape(nl, 3, FF)
    cb_full = conv_b[:, None, :]

    ln1 = jnp.stack([ln1_g, ln1_b], axis=1)
    ln2 = jnp.stack([ln2_g, ln2_b], axis=1)
    sln = jnp.stack([sgu_ln_g, sgu_ln_b], axis=1)
    ps3 = pool_scale[:, None, :]
    sbf = jnp.broadcast_to(sgu_b[..., None], sgu_b.shape + (GW,))

    r1s, r2s, f1s, f2s, ahs, ghs = [], [], [], [], [], []
    xin = x2
    for l in range(nl):
        r1, f1, ah = _fa_call(l, l == 0, alpha, xin, modv, ln2, winT, wout, pool_w, ps3, sln, sgu_w, sbf)
        r2, f2, gh = _fb_call(l, alpha, r1, modv, ln1, wup4, wd2, cw_full, cb_full)
        r1s.append(r1); r2s.append(r2); f1s.append(f1); f2s.append(f2); ahs.append(ah); ghs.append(gh)
        xin = r2

    bufs = [lax.empty((nl, NDEV, icols, D), F32), lax.empty((nl, NDEV, orows, D), F32),
            lax.empty((nl, NDEV, ucols, D), F32), lax.empty((nl, NDEV, drows, D), F32)]
    dx = tgt
    loss_acc = None
    va_l, va5_l, vb0_l, vb1_l, cacc_l, dpw_l, dsw_l, dsb_l = ([None] * nl for _ in range(8))
    for l in reversed(range(nl)):
        last = l == nl - 1
        res0 = _bb_call(l, 0, last, alpha, r1s[l], r2s[l], dx, f2s[l], ghs[l], None, modv, ln1, ln2, wup4, wd2,
                        cw_full, cb_full)
        dh2p, dwg0, dwv0, dwd0, vb0, cacc0 = res0[:6]
        if last:
            loss_acc = res0[6]
        dx1, dwg1, dwv1, dwd1, vb1, cacc1 = _bb_call(l, 1, last, alpha, r1s[l], r2s[l], dx, f2s[l], ghs[l], dh2p,
                                                     modv, ln1, ln2, wup4, wd2, cw_full, cb_full)
        xin = x2 if l == 0 else r2s[l - 1]
        dx, dwin, dwout, dpw, dsw, dsb, va, va5 = _ba_call(
            l, l == 0, alpha, xin, r1s[l], dx1, f1s[l], ahs[l], modv, ln2, ln1, winT, wout, pool_w, ps3, sln, sgu_w,
            sbf)
        groups = [
            [(dwin.reshape(NDEV, icols, D), 0)],
            [(dwout.reshape(NDEV, orows, D), 0)],
            [(dwg0.reshape(2, ucols, D), 0), (dwg1.reshape(2, ucols, D), 2), (dwv0.reshape(2, ucols, D), 4),
             (dwv1.reshape(2, ucols, D), 6)],
            [(dwd0.reshape(4, drows, D), 0), (dwd1.reshape(4, drows, D), 4)],
        ]
        bufs = _exchange_call(l, groups, bufs, f"exchange{l}")
        va_l[l], va5_l[l], vb0_l[l], vb1_l[l] = va, va5, vb0, vb1
        cacc_l[l] = jnp.concatenate([cacc0, cacc1], axis=1)
        dpw_l[l], dsw_l[l], dsb_l[l] = dpw, dsw, dsb[:, :, 0]
    grad_x = dx[None]
    loss = lax.psum((0.5 / D) * jnp.sum(loss_acc), ("x", "y", "c"))

    va = jnp.stack(va_l)
    va5 = jnp.stack(va5_l)
    vb0 = jnp.stack(vb0_l)
    vb1 = jnp.stack(vb1_l)
    cacc = jnp.stack(cacc_l)
    dmod = jnp.stack([va[:, 3], va[:, 4], va[:, 2], vb1[:, 0], vb1[:, 1], vb0[:, 2]], axis=1)
    pieces = [
        ("ada_b", dmod.reshape(nl, 6 * D)),
        ("pool_w", jnp.stack(dpw_l)), ("pool_scale", va5[:, 0]), ("sgu_ln_g", va5[:, 1]), ("sgu_ln_b", va5[:, 2]),
        ("sgu_w", jnp.stack(dsw_l)), ("sgu_b", jnp.stack(dsb_l)),
        ("ln1_g", va[:, 0]), ("ln1_b", va[:, 1]),
        ("conv_w", cacc[:, 1:4]), ("conv_b", cacc[:, 0]),
        ("ln2_g", vb0[:, 0]), ("ln2_b", vb0[:, 1]),
    ]
    flat = jnp.concatenate([p.reshape(-1) for _, p in pieces])
    npad = (-flat.shape[0]) % 1024
    flat = jnp.concatenate([flat, jnp.zeros((npad,), F32)]).reshape(1, -1, 1024)
    (small_g,) = _all_gather_call([flat], "gather_small")
    small_g = small_g.reshape(NDEV, -1)
    gathered = {}
    off = 0
    for name, p in pieces:
        n = p.size
        gathered[name] = small_g[:, off:off + n].reshape((NDEV,) + p.shape)
        off += n

    weights = dict(ada_b=ada_b, pool_w=pool_w, pool_scale=pool_scale, sgu_ln_g=sgu_ln_g, sgu_ln_b=sgu_ln_b,
                   sgu_w=sgu_w, sgu_b=sgu_b, ln1_g=ln1_g, ln1_b=ln1_b, conv_w=conv_w, conv_b=conv_b, ln2_g=ln2_g,
                   ln2_b=ln2_b)
    m_of = dict(ada_b=m_ada_b, pool_w=m_pool_w, pool_scale=m_pool_scale, sgu_ln_g=m_sgu_ln_g, sgu_ln_b=m_sgu_ln_b,
                sgu_w=m_sgu_w, sgu_b=m_sgu_b, ln1_g=m_ln1_g, ln1_b=m_ln1_b, conv_w=m_conv_w, conv_b=m_conv_b,
                ln2_g=m_ln2_g, ln2_b=m_ln2_b)
    v_of = dict(ada_b=v_ada_b, pool_w=v_pool_w, pool_scale=v_pool_scale, sgu_ln_g=v_sgu_ln_g, sgu_ln_b=v_sgu_ln_b,
                sgu_w=v_sgu_w, sgu_b=v_sgu_b, ln1_g=v_ln1_g, ln1_b=v_ln1_b, conv_w=v_conv_w, conv_b=v_conv_b,
                ln2_g=v_ln2_g, ln2_b=v_ln2_b)
    res = {}
    for name in weights:
        w = weights[name]
        parts = gathered[name]
        if name == "conv_w":
            parts = lax.dynamic_slice_in_dim(parts, me * ccols, ccols, axis=3)
        cdim = w.shape[-1]
        w2 = w.reshape(1, -1, cdim)
        outs = _adamw_call(parts.reshape(1, NDEV, -1, cdim), w2, m_of[name].reshape(w2.shape),
                           v_of[name].reshape(w2.shape), f"adamw_{name}")
        res[name] = tuple(o.reshape(w.shape) for o in outs)

    dmod_all = gathered["ada_b"].reshape(NDEV, nl, 6 * D)
    dmod_my = jnp.swapaxes(lax.dynamic_slice_in_dim(dmod_all, me * acols, acols, axis=2), 0, 1)
    res["ada_w"] = _ada_bwd_call(jnp.swapaxes(c_all, 0, 1), dmod_my, ada_w, m_ada_w, v_ada_w)

    g_winT = _sum_parts_call(bufs[0], "sum_w_in")
    g_wupT = _sum_parts_call(bufs[2], "sum_w_up")
    g_win = jnp.swapaxes(g_winT, 1, 2)
    g_wup = jnp.swapaxes(g_wupT, 1, 2)
    res["w_in"] = _adamw_call(g_win[:, None], w_in, m_w_in, v_w_in, "adamw_w_in")
    res["w_up"] = _adamw_call(g_wup[:, None], w_up, m_w_up, v_w_up, "adamw_w_up")
    res["w_out"] = _adamw_call(bufs[1], w_out, m_w_out, v_w_out, "adamw_w_out")
    res["w_down"] = _adamw_call(bufs[3], w_down, m_w_down, v_w_down, "adamw_w_down")

    order = ["ada_w", "ada_b", "w_in", "pool_w", "pool_scale", "sgu_ln_g", "sgu_ln_b", "sgu_w", "sgu_b", "w_out",
             "ln1_g", "ln1_b", "w_up", "conv_w", "conv_b", "w_down", "ln2_g", "ln2_b"]
    out = [loss, grad_x]
    for k in range(4):
        out += [res[n][k] for n in order]
    return tuple(out)
```

```python
import jax
import jax.numpy as jnp
from jax import lax
from jax.experimental import pallas as pl
from jax.experimental.pallas import tpu as pltpu

F32 = jnp.float32
BF16 = jnp.bfloat16

NDEV = 8
D = 1024
DP = 512
DS = 512
DIN = DP + 2 * DS
FF = 2816
FH = FF // 2
NG = 4
GW = 128
WINDOWS = (2, 4, 8, 16)
AHALO = 16
GHALO = 8
LN_EPS = 1e-5
ADAM_LR, ADAM_B1, ADAM_B2, ADAM_EPS, ADAM_WD, ADAM_STEP = 0.001, 0.9, 0.999, 1e-08, 0.01, 10
TSF = 512
TSB = 256
_K0 = 0.7978845608028654
_K1 = 0.044715
MESH_ID = pl.DeviceIdType.MESH


def _mm(a, b):
    return jnp.dot(a, b, preferred_element_type=F32)


def _mm_nt(a, b):
    return lax.dot_general(a, b, (((1,), (1,)), ((), ())), preferred_element_type=F32)


def _mm_tn(a, b):
    return lax.dot_general(a, b, (((0,), (0,)), ((), ())), preferred_element_type=F32)


def _rowmean(x):
    return jnp.mean(x, axis=-1, keepdims=True)


def _ln_stats(x):
    mu = _rowmean(x)
    xc = x - mu
    rstd = lax.rsqrt(_rowmean(xc * xc) + LN_EPS)
    return xc * rstd, rstd


def _ln_bwd(dy, gamma, xhat, rstd):
    dxh = dy * gamma
    return rstd * (dxh - _rowmean(dxh) - xhat * _rowmean(dxh * xhat))


def _gelu_t(x):
    t = jnp.tanh(_K0 * (x + _K1 * (x * x * x)))
    return 0.5 * x * (1.0 + t), t


def _dgelu(x, t):
    return 0.5 * (1.0 + t) + 0.5 * x * (1.0 - t * t) * (_K0 * (1.0 + 3.0 * _K1 * (x * x)))


def _colsum8(x):
    t, n = x.shape
    return jnp.sum(x.reshape(t // 8, 8, n), axis=0)


def _tril_mask():
    r = lax.broadcasted_iota(jnp.int32, (GW, GW), 0)
    c = lax.broadcasted_iota(jnp.int32, (GW, GW), 1)
    return c <= r


def _full(shape):
    n = len(shape)
    return pl.BlockSpec(shape, lambda *_: (0,) * n)


def _resident(tail, lead=()):
    n = len(tail)
    return pl.BlockSpec((None,) * len(lead) + tuple(tail), lambda *_: tuple(lead) + (0,) * n,
                        pipeline_mode=pl.Buffered(1))


def _layer_vec(rows, width, l):
    return pl.BlockSpec((None, rows, width), lambda *_: (l, 0, 0))


_VMEM_WHOLE = pl.BlockSpec(memory_space=pltpu.VMEM)
_HBM = pl.BlockSpec(memory_space=pl.ANY)
_ARB = pltpu.CompilerParams(dimension_semantics=("arbitrary",))


def _me():
    return 4 * lax.axis_index("x") + 2 * lax.axis_index("y") + lax.axis_index("c")


def _coords(p):
    return (p >> 2, (p >> 1) & 1, p & 1)


class _Payload:
    def __init__(self):
        self.srcs, self.new, self.alias, self.transfers = [], [], [], []

    def _src(self, arr):
        self.srcs.append(arr)
        return len(self.srcs) - 1

    def _alias(self, buf):
        self.alias.append(buf)
        return len(self.alias) - 1

    def gather(self, arr, chunk=None):
        pos = self._src(arr)
        blk = arr.shape if chunk is None else arr.shape[1:]
        self.new.append(jax.ShapeDtypeStruct((NDEV,) + tuple(blk), arr.dtype))
        self.transfers.append(([(pos, chunk)] * NDEV, ("new", len(self.new) - 1), ()))
        return len(self.new) - 1

    def gather_into(self, arr, buf, lead):
        pos = self._src(arr)
        self.transfers.append(([(pos, None)] * NDEV, ("alias", self._alias(buf)), tuple(lead)))

    def exchange_into(self, parts, buf, lead):
        route = {}
        for arr, first in parts:
            pos = self._src(arr)
            for q in range(arr.shape[0]):
                route[first + q] = (pos, q)
        self.transfers.append(([route[p] for p in range(NDEV)], ("alias", self._alias(buf)), tuple(lead)))

    def _ends(self, t, io, dst_dev, src_dev):
        srcs, new_out, alias_out = io
        route, (kind, k), lead = self.transfers[t]
        pos, q = route[dst_dev]
        src = srcs[pos] if q is None else srcs[pos].at[q]
        buf = new_out[k] if kind == "new" else alias_out[k]
        return src, buf.at[lead + (src_dev,)]

    def _remote(self, t, io, sems, src_dev, dst_dev):
        src, dst = self._ends(t, io, dst_dev, src_dev)
        return pltpu.make_async_remote_copy(
            src_ref=src, dst_ref=dst, send_sem=sems[0].at[t, dst_dev], recv_sem=sems[1].at[t, src_dev],
            device_id=_coords(dst_dev), device_id_type=MESH_ID)

    def _local(self, t, io, sems, p):
        src, dst = self._ends(t, io, p, p)
        return pltpu.make_async_copy(src, dst, sems[2].at[t])

    def start(self, io, sems):
        me = _me()
        for p in range(NDEV):
            @pl.when(me == p)
            def _():
                for t in range(len(self.transfers)):
                    self._local(t, io, sems, p).start()

            @pl.when(me != p)
            def _():
                for t in range(len(self.transfers)):
                    self._remote(t, io, sems, me, p).start()

    def wait(self, io, sems):
        me = _me()
        for p in range(NDEV):
            @pl.when(me == p)
            def _():
                for t in range(len(self.transfers)):
                    self._local(t, io, sems, p).wait()

            @pl.when(me != p)
            def _():
                for t in range(len(self.transfers)):
                    self._remote(t, io, sems, p, p).wait_recv()
                    self._remote(t, io, sems, me, p).wait_send()


def _pcall(body, name, nsteps, in_specs, out_specs, out_shape, scratch, args, pay=None):
    n_in, n_out, n_scr = len(args), len(out_shape), len(scratch)
    if pay is None:
        res = pl.pallas_call(body, name=name, grid=(nsteps,), in_specs=list(in_specs), out_specs=list(out_specs),
                             out_shape=list(out_shape), scratch_shapes=list(scratch), compiler_params=_ARB)(*args)
        return list(res), [], []
    ns, nn, na, nt = len(pay.srcs), len(pay.new), len(pay.alias), len(pay.transfers)

    def full(*refs):
        cin = refs[:n_in]
        srcs = refs[n_in:n_in + ns]
        o0 = n_in + ns + na
        cout = refs[o0:o0 + n_out]
        new_out = refs[o0 + n_out:o0 + n_out + nn]
        alias_out = refs[o0 + n_out + nn:o0 + n_out + nn + na]
        s0 = o0 + n_out + nn + na
        cscr = refs[s0:s0 + n_scr]
        sems = refs[s0 + n_scr:]
        io = (srcs, new_out, alias_out)
        i = pl.program_id(0)

        @pl.when(i == 0)
        def _():
            pay.start(io, sems)

        body(*cin, *cout, *cscr)

        @pl.when(i == nsteps - 1)
        def _():
            pay.wait(io, sems)

    res = pl.pallas_call(
        full, name=name, grid=(nsteps,),
        in_specs=list(in_specs) + [_HBM] * (ns + na), out_specs=list(out_specs) + [_HBM] * (nn + na),
        out_shape=list(out_shape) + pay.new + [jax.ShapeDtypeStruct(b.shape, b.dtype) for b in pay.alias],
        input_output_aliases={n_in + ns + k: n_out + nn + k for k in range(na)},
        scratch_shapes=list(scratch) + [pltpu.SemaphoreType.DMA((nt, NDEV)), pltpu.SemaphoreType.DMA((nt, NDEV)),
                                        pltpu.SemaphoreType.DMA((nt,))],
        compiler_params=_ARB,
    )(*args, *pay.srcs, *pay.alias)
    return list(res[:n_out]), list(res[n_out:n_out + nn]), list(res[n_out + nn:])


def _comm_call(pay, name):
    def body():
        pass

    _, new, alias = _pcall(body, name, 1, [], [], [], [], [], pay)
    return new, alias


def _a_forward(x, modv_ref, winT_ref, pw_ref, ps_ref, sln_ref, sw_ref, sbf_ref, abuf, zbuf, tile, ts):
    sh1 = modv_ref[0:1, :]
    sc1 = modv_ref[1:2, :]
    hb = (x * (1.0 + sc1) + sh1).astype(BF16)
    proj = _mm_nt(hb, winT_ref[...])
    a = proj[:, 0:DP]
    u = proj[:, DP:DP + DS]
    v = proj[:, DP + DS:]
    abuf[AHALO:AHALO + ts, :] = a
    tglob = tile * ts + lax.broadcasted_iota(jnp.int32, (ts, 1), 0)
    pooled_b, mixed, inv_cnt, pwb = [], [], [], []
    for g, w in enumerate(WINDOWS):
        cs = slice(g * GW, (g + 1) * GW)
        a_g = abuf[AHALO:AHALO + ts, cs]
        s = a_g
        for k in range(1, w):
            s = s + abuf[AHALO - k:AHALO - k + ts, cs]
        inv = 1.0 / jnp.minimum(tglob + 1, w).astype(F32)
        pg = (s * inv - a_g).astype(BF16)
        wg = pw_ref[g].astype(BF16)
        pooled_b.append(pg)
        inv_cnt.append(inv)
        pwb.append(wg)
        mixed.append(_mm(pg, wg))
    mixed = jnp.concatenate(mixed, axis=1)
    ya = mixed * ps_ref[...]
    ug, tu = _gelu_t(u)
    vg, tv = _gelu_t(v)
    vhat, rstdv = _ln_stats(vg)
    vnb = (vhat * sln_ref[0:1, :] + sln_ref[1:2, :]).astype(BF16)
    tri = _tril_mask()
    wt = [jnp.where(tri, sw_ref[h], 0.0).astype(BF16) for h in range(NG)]
    for c in range(ts // GW):
        rs = slice(c * GW, (c + 1) * GW)
        for h in range(NG):
            cs = slice(h * GW, (h + 1) * GW)
            zbuf[rs, cs] = _mm(wt[h], vnb[rs, cs]) + sbf_ref[h]
    z = zbuf[...]
    yb = ug * z
    return dict(hb=hb, sc1=sc1, u=u, v=v, tu=tu, tv=tv, ug=ug, z=z, vhat=vhat, rstdv=rstdv, vnb=vnb,
                wt=wt, pooled_b=pooled_b, pwb=pwb, inv_cnt=inv_cnt, mixed=mixed, ya=ya, yb=yb)


def _small_specs(l):
    grp = pl.BlockSpec((None, NG, GW, GW), lambda i: (l, 0, 0, 0))
    return [grp, _layer_vec(1, DP, l), _layer_vec(2, DS, l), grp, grp]


def _fa_call(l, first, alpha, xin, modv, lnp, winT, wout, small, pay=None):
    s = xin.shape[0]
    nt = s // TSF
    rr = TSF // TSB

    def body(xin_ref, modv_ref, lnp_ref, winT_ref, wout_ref, pw_ref, ps_ref, sln_ref, sw_ref, sbf_ref,
             r1_ref, f1_ref, ah_ref, abuf, zbuf, mixbuf):
        i = pl.program_id(0)

        @pl.when(i == 0)
        def _():
            abuf[0:AHALO, :] = jnp.zeros((AHALO, DP), F32)

        x = xin_ref[...]
        if not first:
            xhat, _ = _ln_stats(x)
            x = xhat * lnp_ref[0:1, :] + lnp_ref[1:2, :]
        fw = _a_forward(x, modv_ref, winT_ref, pw_ref, ps_ref, sln_ref, sw_ref, sbf_ref, abuf, zbuf, i, TSF)
        mixbuf[:, 0:DP] = fw["ya"].astype(BF16)
        mixbuf[:, DP:] = fw["yb"].astype(BF16)
        f = _mm(mixbuf[...], wout_ref[...])
        r1_ref[...] = alpha * x + modv_ref[2:3, :] * f
        f1_ref[...] = f.astype(BF16)
        for r in range(rr):
            ah_ref[r] = abuf[(r + 1) * TSB:(r + 1) * TSB + AHALO, :]
        abuf[0:AHALO, :] = abuf[TSF:TSF + AHALO, :]

    tile = pl.BlockSpec((TSF, D), lambda i: (i, 0))
    return _pcall(
        body, f"fa{l}", nt,
        in_specs=[tile, _layer_vec(8, D, l), _layer_vec(2, D, max(l - 1, 0)), _resident((DIN, D)),
                  _resident((D, D))] + _small_specs(l),
        out_specs=[tile, tile, pl.BlockSpec((rr, AHALO, DP), lambda i: (i, 0, 0))],
        out_shape=[jax.ShapeDtypeStruct((s, D), F32), jax.ShapeDtypeStruct((s, D), BF16),
                   jax.ShapeDtypeStruct((s // TSB, AHALO, DP), F32)],
        scratch=[pltpu.VMEM((TSF + AHALO, DP), F32), pltpu.VMEM((TSF, DS), F32), pltpu.VMEM((TSF, D), BF16)],
        args=[xin, modv, lnp, winT, wout, *small], pay=pay)


def _ba_call(l, first, alpha, xin, r1, dx1, f1, ahalo, modv, lnp, ln1, winT, wout, small, pay=None):
    s = xin.shape[0]
    nt = s // TSB
    ts = TSB

    def body(xin_ref, r1_ref, dx1_ref, f1_ref, ah_ref, modv_ref, lnp_ref, ln1_ref, winT_ref, wout_ref, pw_ref,
             ps_ref, sln_ref, sw_ref, sbf_ref,
             dx_ref, dwin_ref, dwout_ref, dpw_ref, dsw_ref, dsb_ref, va_ref, va5_ref,
             abuf, qbuf, zbuf, dvnbuf, mixbuf, dpbuf, dbacc, vacc, vacc5, dwin_acc, dwout_acc):
        i = pl.program_id(0)
        j = nt - 1 - i

        @pl.when(i == 0)
        def _():
            dwin_acc[...] = jnp.zeros_like(dwin_acc)
            dwout_acc[...] = jnp.zeros_like(dwout_acc)
            dpw_ref[...] = jnp.zeros_like(dpw_ref)
            dsw_ref[...] = jnp.zeros_like(dsw_ref)
            dbacc[...] = jnp.zeros_like(dbacc)
            vacc[...] = jnp.zeros_like(vacc)
            vacc5[...] = jnp.zeros_like(vacc5)
            qbuf[ts:ts + AHALO, :] = jnp.zeros((AHALO, DP), F32)

        x = xin_ref[...]
        if not first:
            xhat, _ = _ln_stats(x)
            x = xhat * lnp_ref[0:1, :] + lnp_ref[1:2, :]
        abuf[0:AHALO, :] = jnp.where(j > 0, ah_ref[0], 0.0)
        fw = _a_forward(x, modv_ref, winT_ref, pw_ref, ps_ref, sln_ref, sw_ref, sbf_ref, abuf, zbuf, j, ts)
        mixbuf[:, 0:DP] = fw["ya"].astype(BF16)
        mixbuf[:, DP:] = fw["yb"].astype(BF16)

        xhat1, rstd1 = _ln_stats(r1_ref[...])
        dy = dx1_ref[...]
        vacc[0] += _colsum8(dy * xhat1)
        vacc[1] += _colsum8(dy)
        dr1 = _ln_bwd(dy, ln1_ref[0:1, :], xhat1, rstd1)
        vacc[2] += _colsum8(dr1 * f1_ref[...].astype(F32))
        dfb = (dr1 * modv_ref[2:3, :]).astype(BF16)
        dwout_acc[...] += _mm_tn(mixbuf[...], dfb)
        dmix = _mm_nt(dfb, wout_ref[...])
        dya = dmix[:, 0:DP]
        dyb = dmix[:, DP:]

        vacc5[0] += _colsum8(dya * fw["mixed"])
        dmixed = (dya * ps_ref[...]).astype(BF16)
        dpooled = []
        for g in range(NG):
            cs = slice(g * GW, (g + 1) * GW)
            dpw_ref[g] += _mm_tn(fw["pooled_b"][g], dmixed[:, cs])
            dpg = _mm_nt(dmixed[:, cs], fw["pwb"][g])
            dpooled.append(dpg)
            qbuf[0:ts, cs] = dpg * fw["inv_cnt"][g]
        for g, w in enumerate(WINDOWS):
            cs = slice(g * GW, (g + 1) * GW)
            sq = qbuf[0:ts, cs]
            for k in range(1, w):
                sq = sq + qbuf[k:k + ts, cs]
            dpbuf[:, cs] = (sq - dpooled[g]).astype(BF16)
        qbuf[ts:ts + AHALO, :] = qbuf[0:AHALO, :]

        dug = dyb * fw["z"]
        dz = dyb * fw["ug"]
        dzb = dz.astype(BF16)
        for c in range(ts // GW):
            rs = slice(c * GW, (c + 1) * GW)
            for h in range(NG):
                cs = slice(h * GW, (h + 1) * GW)
                dvnbuf[rs, cs] = _mm_tn(fw["wt"][h], dzb[rs, cs])
                dsw_ref[h] += _mm_nt(dzb[rs, cs], fw["vnb"][rs, cs])
            dbacc[...] += dz[rs, :]
        dvn = dvnbuf[...]
        vacc5[1] += _colsum8(dvn * fw["vhat"])
        vacc5[2] += _colsum8(dvn)
        dvg = _ln_bwd(dvn, sln_ref[0:1, :], fw["vhat"], fw["rstdv"])
        dpbuf[:, DP:DP + DS] = (dug * _dgelu(fw["u"], fw["tu"])).astype(BF16)
        dpbuf[:, DP + DS:] = (dvg * _dgelu(fw["v"], fw["tv"])).astype(BF16)

        dpb = dpbuf[...]
        dwin_acc[...] += _mm_tn(dpb, fw["hb"])
        dh = _mm(dpb, winT_ref[...])
        dx_ref[...] = dh * (1.0 + fw["sc1"]) + alpha * dr1
        vacc[3] += _colsum8(dh)
        vacc[4] += _colsum8(dh * x)

        @pl.when(i == nt - 1)
        def _():
            dwin_ref[...] = dwin_acc[...].astype(BF16)
            dwout_ref[...] = dwout_acc[...].astype(BF16)
            tri = _tril_mask()
            for h in range(NG):
                dsw_ref[h] = jnp.where(tri, dsw_ref[h], 0.0)
                sb = jnp.sum(dbacc[:, h * GW:(h + 1) * GW], axis=1, keepdims=True)
                dsb_ref[h] = jnp.broadcast_to(sb, (GW, GW))
            for n in range(5):
                va_ref[n:n + 1, :] = jnp.sum(vacc[n], axis=0, keepdims=True)
            for n in range(3):
                va5_ref[n:n + 1, :] = jnp.sum(vacc5[n], axis=0, keepdims=True)

    rev = lambda i: (nt - 1 - i, 0)
    tile = pl.BlockSpec((ts, D), rev)
    return _pcall(
        body, f"ba{l}", nt,
        in_specs=[tile, tile, tile, tile,
                  pl.BlockSpec((1, AHALO, DP), lambda i: (jnp.maximum(nt - 2 - i, 0), 0, 0)),
                  _layer_vec(8, D, l), _layer_vec(2, D, max(l - 1, 0)), _layer_vec(2, D, l),
                  _resident((DIN, D)), _resident((D, D))] + _small_specs(l),
        out_specs=[tile] + [_VMEM_WHOLE] * 7,
        out_shape=[jax.ShapeDtypeStruct((s, D), F32), jax.ShapeDtypeStruct((DIN, D), BF16),
                   jax.ShapeDtypeStruct((D, D), BF16), jax.ShapeDtypeStruct((NG, GW, GW), F32),
                   jax.ShapeDtypeStruct((NG, GW, GW), F32), jax.ShapeDtypeStruct((NG, GW, GW), F32),
                   jax.ShapeDtypeStruct((5, D), F32), jax.ShapeDtypeStruct((3, DP), F32)],
        scratch=[pltpu.VMEM((ts + AHALO, DP), F32), pltpu.VMEM((ts + AHALO, DP), F32),
                 pltpu.VMEM((ts, DS), F32), pltpu.VMEM((ts, DS), F32), pltpu.VMEM((ts, D), BF16),
                 pltpu.VMEM((ts, DIN), BF16), pltpu.VMEM((GW, DS), F32), pltpu.VMEM((5, 8, D), F32),
                 pltpu.VMEM((3, 8, DP), F32), pltpu.VMEM((DIN, D), F32), pltpu.VMEM((D, D), F32)],
        args=[xin, r1, dx1, f1, ahalo, modv, lnp, ln1, winT, wout, *small], pay=pay)


def _conv_gate(gbuf_at, g, cw, cb):
    return cb + cw[0:1, :] * gbuf_at(0) + cw[1:2, :] * gbuf_at(1) + cw[2:3, :] * g


def _fb_call(l, alpha, r1, modv, ln1, wup4, wd2, conv_w, conv_b, pay=None):
    s = r1.shape[0]
    nt = s // TSF
    rr = TSF // TSB

    def body(r1_ref, modv_ref, ln1_ref, wup_ref, wd_ref, cw_ref, cb_ref, r2_ref, f2_ref, gh_ref, gbuf):
        i = pl.program_id(0)

        @pl.when(i == 0)
        def _():
            gbuf[:, 0:GHALO, :] = jnp.zeros((2, GHALO, FH), F32)

        xhat1, _ = _ln_stats(r1_ref[...])
        x1 = xhat1 * ln1_ref[0:1, :] + ln1_ref[1:2, :]
        h2b = (x1 * (1.0 + modv_ref[4:5, :]) + modv_ref[3:4, :]).astype(BF16)
        f2 = jnp.zeros((TSF, D), F32)
        for hf in range(2):
            cs = slice(hf * FH, (hf + 1) * FH)
            g = _mm_nt(h2b, wup_ref[hf])
            val = _mm_nt(h2b, wup_ref[2 + hf])
            gbuf[hf, GHALO:GHALO + TSF, :] = g
            gc = _conv_gate(lambda k: gbuf[hf, GHALO - 2 + k:GHALO - 2 + k + TSF, :], g, cw_ref[:, cs],
                            cb_ref[:, cs])
            ge, _ = _gelu_t(gc)
            f2 = f2 + _mm((ge * val).astype(BF16), wd_ref[hf])
            for r in range(rr):
                gh_ref[r, :, cs] = gbuf[hf, (r + 1) * TSB:(r + 1) * TSB + GHALO, :]
            gbuf[hf, 0:GHALO, :] = gbuf[hf, TSF:TSF + GHALO, :]
        r2_ref[...] = alpha * x1 + modv_ref[5:6, :] * f2
        f2_ref[...] = f2.astype(BF16)

    tile = pl.BlockSpec((TSF, D), lambda i: (i, 0))
    return _pcall(
        body, f"fb{l}", nt,
        in_specs=[tile, _layer_vec(8, D, l), _layer_vec(2, D, l), _resident((4, FH, D)), _resident((2, FH, D)),
                  _layer_vec(3, FF, l), _layer_vec(1, FF, l)],
        out_specs=[tile, tile, pl.BlockSpec((rr, GHALO, FF), lambda i: (i, 0, 0))],
        out_shape=[jax.ShapeDtypeStruct((s, D), F32), jax.ShapeDtypeStruct((s, D), BF16),
                   jax.ShapeDtypeStruct((s // TSB, GHALO, FF), F32)],
        scratch=[pltpu.VMEM((2, TSF + GHALO, FH), F32)],
        args=[r1, modv, ln1, wup4, wd2, conv_w, conv_b], pay=pay)


def _bb_call(l, hf, last, alpha, r1, r2, dx2, f2, ghalo, dh2_in, modv, ln1, ln2, wup4, wd2, conv_w, conv_b,
             pay=None):
    s = r1.shape[0]
    nt = s // TSB
    ts = TSB
    nvec = 3 if hf == 0 else 2
    with_loss = last and hf == 0

    def body(*refs):
        it = iter(refs)
        r1_ref, r2_ref, dx2_ref, f2_ref, gh_ref = next(it), next(it), next(it), next(it), next(it)
        dh2_ref = next(it) if hf == 1 else None
        modv_ref, ln1_ref, ln2_ref, wg_ref, wv_ref, wd_ref, cw_ref, cb_ref = (next(it) for _ in range(8))
        out_ref, dwg_ref, dwv_ref, dwd_ref, vb_ref, cacc_ref = (next(it) for _ in range(6))
        loss_ref = next(it) if with_loss else None
        gbuf, dgcbuf, vacc, cacc, dwg_acc, dwv_acc, dwd_acc = (next(it) for _ in range(7))
        lacc = next(it) if with_loss else None

        i = pl.program_id(0)
        j = nt - 1 - i

        @pl.when(i == 0)
        def _():
            dwg_acc[...] = jnp.zeros_like(dwg_acc)
            dwv_acc[...] = jnp.zeros_like(dwv_acc)
            dwd_acc[...] = jnp.zeros_like(dwd_acc)
            vacc[...] = jnp.zeros_like(vacc)
            cacc[...] = jnp.zeros_like(cacc)
            dgcbuf[ts:ts + GHALO, :] = jnp.zeros((GHALO, FH), F32)
            if with_loss:
                lacc[...] = jnp.zeros_like(lacc)

        xhat1, _ = _ln_stats(r1_ref[...])
        x1 = xhat1 * ln1_ref[0:1, :] + ln1_ref[1:2, :]
        sc2 = modv_ref[4:5, :]
        h2b = (x1 * (1.0 + sc2) + modv_ref[3:4, :]).astype(BF16)
        g = _mm_nt(h2b, wg_ref[...])
        val = _mm_nt(h2b, wv_ref[...])
        gbuf[0:GHALO, :] = jnp.where(j > 0, gh_ref[0], 0.0)
        gbuf[GHALO:GHALO + ts, :] = g
        cw = cw_ref[...]
        gc = _conv_gate(lambda k: gbuf[GHALO - 2 + k:GHALO - 2 + k + ts, :], g, cw, cb_ref[...])
        ge, tg = _gelu_t(gc)
        mb = (ge * val).astype(BF16)

        xhat2, rstd2 = _ln_stats(r2_ref[...])
        if last:
            diff = xhat2 * ln2_ref[0:1, :] + ln2_ref[1:2, :] - dx2_ref[...]
            dy = diff * (1.0 / D)
            if with_loss:
                lacc[...] += _colsum8(diff * diff)
        else:
            dy = dx2_ref[...]
        dr2 = _ln_bwd(dy, ln2_ref[0:1, :], xhat2, rstd2)
        if hf == 0:
            vacc[0] += _colsum8(dy * xhat2)
            vacc[1] += _colsum8(dy)
            vacc[2] += _colsum8(dr2 * f2_ref[...].astype(F32))
        df2b = (dr2 * modv_ref[5:6, :]).astype(BF16)

        dm = _mm_nt(df2b, wd_ref[...])
        dwd_acc[...] += _mm_tn(mb, df2b)
        dval = dm * ge
        dgc = dm * val * _dgelu(gc, tg)
        cacc[0] += _colsum8(dgc)
        for k in range(3):
            cacc[1 + k] += _colsum8(dgc * gbuf[GHALO - 2 + k:GHALO - 2 + k + ts, :])
        dgcbuf[0:ts, :] = dgc
        dg = cw[2:3, :] * dgc + cw[1:2, :] * dgcbuf[1:1 + ts, :] + cw[0:1, :] * dgcbuf[2:2 + ts, :]
        dgcbuf[ts:ts + GHALO, :] = dgcbuf[0:GHALO, :]

        dgb = dg.astype(BF16)
        dvb = dval.astype(BF16)
        dwg_acc[...] += _mm_tn(dgb, h2b)
        dwv_acc[...] += _mm_tn(dvb, h2b)
        dh2 = _mm(dgb, wg_ref[...]) + _mm(dvb, wv_ref[...])
        if hf == 0:
            out_ref[...] = dh2
        else:
            dh2 = dh2 + dh2_ref[...]
            out_ref[...] = dh2 * (1.0 + sc2) + alpha * dr2
            vacc[0] += _colsum8(dh2)
            vacc[1] += _colsum8(dh2 * x1)

        @pl.when(i == nt - 1)
        def _():
            dwg_ref[...] = dwg_acc[...].astype(BF16)
            dwv_ref[...] = dwv_acc[...].astype(BF16)
            dwd_ref[...] = dwd_acc[...].astype(BF16)
            for n in range(nvec):
                vb_ref[n:n + 1, :] = jnp.sum(vacc[n], axis=0, keepdims=True)
            for n in range(4):
                cacc_ref[n:n + 1, :] = jnp.sum(cacc[n], axis=0, keepdims=True)
            if with_loss:
                loss_ref[...] = lacc[...]

    rev = lambda i: (nt - 1 - i, 0)
    tile = pl.BlockSpec((ts, D), rev)
    in_specs = [tile, tile, tile, tile,
                pl.BlockSpec((1, GHALO, FH), lambda i: (jnp.maximum(nt - 2 - i, 0), 0, hf))]
    args = [r1, r2, dx2, f2, ghalo]
    if hf == 1:
        in_specs.append(tile)
        args.append(dh2_in)
    in_specs += [_layer_vec(8, D, l), _layer_vec(2, D, l), _layer_vec(2, D, l),
                 _resident((FH, D), (hf,)), _resident((FH, D), (2 + hf,)), _resident((FH, D), (hf,)),
                 pl.BlockSpec((None, 3, FH), lambda i: (l, 0, hf)), pl.BlockSpec((None, 1, FH), lambda i: (l, 0, hf))]
    args += [modv, ln1, ln2, wup4, wup4, wd2, conv_w, conv_b]
    out_specs = [tile] + [_VMEM_WHOLE] * 5
    out_shape = [jax.ShapeDtypeStruct((s, D), F32), jax.ShapeDtypeStruct((FH, D), BF16),
                 jax.ShapeDtypeStruct((FH, D), BF16), jax.ShapeDtypeStruct((FH, D), BF16),
                 jax.ShapeDtypeStruct((nvec, D), F32), jax.ShapeDtypeStruct((4, FH), F32)]
    scratch = [pltpu.VMEM((ts + GHALO, FH), F32), pltpu.VMEM((ts + GHALO, FH), F32),
               pltpu.VMEM((nvec, 8, D), F32), pltpu.VMEM((4, 8, FH), F32),
               pltpu.VMEM((FH, D), F32), pltpu.VMEM((FH, D), F32), pltpu.VMEM((FH, D), F32)]
    if with_loss:
        out_specs.append(_VMEM_WHOLE)
        out_shape.append(jax.ShapeDtypeStruct((8, D), F32))
        scratch.append(pltpu.VMEM((8, D), F32))
    return _pcall(body, f"bb{l}_{hf}", nt, in_specs, out_specs, out_shape, scratch, args, pay=pay)


def _silu(c):
    return c * (1.0 / (1.0 + jnp.exp(-c)))


def _ada_fwd_call(c_all, ada_w, ada_b_my):
    nl, _, wcols = ada_w.shape

    def body(c_ref, w_ref, b_ref, o_ref):
        ca = _silu(c_ref[...])
        o_ref[...] = jnp.dot(ca, w_ref[...], preferred_element_type=F32,
                             precision=lax.Precision.HIGHEST) + b_ref[...]

    return pl.pallas_call(
        body, name="ada_fwd", grid=(nl,),
        in_specs=[_full((NDEV, D)), pl.BlockSpec((None, D, wcols), lambda l: (l, 0, 0)),
                  pl.BlockSpec((None, 1, wcols), lambda l: (l, 0, 0))],
        out_specs=pl.BlockSpec((None, NDEV, wcols), lambda l: (l, 0, 0)),
        out_shape=jax.ShapeDtypeStruct((nl, NDEV, wcols), F32),
        compiler_params=_ARB,
    )(c_all, ada_w, ada_b_my)


def _adam_update(w, g, m, v):
    m2 = ADAM_B1 * m + (1.0 - ADAM_B1) * g
    v2 = ADAM_B2 * v + (1.0 - ADAM_B2) * (g * g)
    m_hat = m2 / (1.0 - ADAM_B1 ** ADAM_STEP)
    v_hat = v2 / (1.0 - ADAM_B2 ** ADAM_STEP)
    delta = -ADAM_LR * (m_hat / (jnp.sqrt(v_hat) + ADAM_EPS) + ADAM_WD * w)
    return delta, m2, v2


def _ada_bwd_call(c_t, dmod_my, w, m, v):
    nl, _, wcols = w.shape
    rb = 256

    def body(ct_ref, dm_ref, w_ref, m_ref, v_ref, g_ref, d_ref, m2_ref, v2_ref):
        ca_t = _silu(ct_ref[...])
        dm = dm_ref[...]
        g = ca_t[:, 0:1] * dm[0:1, :]
        for b in range(1, NDEV):
            g = g + ca_t[:, b:b + 1] * dm[b:b + 1, :]
        delta, m2, v2 = _adam_update(w_ref[...], g, m_ref[...], v_ref[...])
        g_ref[...] = g
        d_ref[...] = delta
        m2_ref[...] = m2
        v2_ref[...] = v2

    blk = pl.BlockSpec((None, rb, wcols), lambda l, i: (l, i, 0))
    shp = jax.ShapeDtypeStruct(w.shape, F32)
    return pl.pallas_call(
        body, name="ada_bwd", grid=(nl, D // rb),
        in_specs=[pl.BlockSpec((rb, NDEV), lambda l, i: (i, 0)),
                  pl.BlockSpec((None, NDEV, wcols), lambda l, i: (l, 0, 0)), blk, blk, blk],
        out_specs=[blk, blk, blk, blk], out_shape=[shp, shp, shp, shp],
        compiler_params=pltpu.CompilerParams(dimension_semantics=("arbitrary", "arbitrary")),
    )(c_t, dmod_my, w, m, v)


def _row_block(r, row_bytes):
    budget = 6 * 1024 * 1024
    best = None
    for rb in range(16, min(r, 512) + 1, 16):
        if r % rb == 0 and rb * row_bytes <= budget:
            best = rb
    return best if best is not None else r


def _sum_parts_call(parts, name):
    nl, npart, r, c = parts.shape
    rb = _row_block(r, (npart + 1) * c * 4)

    def body(p_ref, g_ref):
        g = p_ref[0].astype(F32)
        for k in range(1, npart):
            g = g + p_ref[k].astype(F32)
        g_ref[...] = g

    return pl.pallas_call(
        body, name=name, grid=(nl, r // rb),
        in_specs=[pl.BlockSpec((None, npart, rb, c), lambda l, i: (l, 0, i, 0))],
        out_specs=pl.BlockSpec((None, rb, c), lambda l, i: (l, i, 0)),
        out_shape=jax.ShapeDtypeStruct((nl, r, c), F32),
        compiler_params=pltpu.CompilerParams(dimension_semantics=("arbitrary", "arbitrary")),
    )(parts)


def _adamw_call(parts, w, m, v, name):
    nl, npart, r, c = parts.shape
    rb = _row_block(r, (npart + 7) * c * 4)

    def body(p_ref, w_ref, m_ref, v_ref, g_ref, d_ref, m2_ref, v2_ref):
        g = p_ref[0].astype(F32)
        for k in range(1, npart):
            g = g + p_ref[k].astype(F32)
        delta, m2, v2 = _adam_update(w_ref[...], g, m_ref[...], v_ref[...])
        g_ref[...] = g
        d_ref[...] = delta
        m2_ref[...] = m2
        v2_ref[...] = v2

    blk = pl.BlockSpec((None, rb, c), lambda l, i: (l, i, 0))
    shp = jax.ShapeDtypeStruct((nl, r, c), F32)
    return pl.pallas_call(
        body, name=name, grid=(nl, r // rb),
        in_specs=[pl.BlockSpec((None, npart, rb, c), lambda l, i: (l, 0, i, 0)), blk, blk, blk],
        out_specs=[blk, blk, blk, blk], out_shape=[shp, shp, shp, shp],
        compiler_params=pltpu.CompilerParams(dimension_semantics=("arbitrary", "arbitrary")),
    )(parts, w, m, v)


_SMALL_ORDER = ("ada_b", "pool_w", "pool_scale", "sgu_ln_g", "sgu_ln_b", "sgu_w", "sgu_b", "ln1_g", "ln1_b",
                "conv_w", "conv_b", "ln2_g", "ln2_b")


def kernel(x, c, ada_w, ada_b, w_in, pool_w, pool_scale, sgu_ln_g, sgu_ln_b, sgu_w, sgu_b, w_out, ln1_g, ln1_b, w_up, conv_w, conv_b, w_down, ln2_g, ln2_b, loss_target, m_ada_w, m_ada_b, m_w_in, m_pool_w, m_pool_scale, m_sgu_ln_g, m_sgu_ln_b, m_sgu_w, m_sgu_b, m_w_out, m_ln1_g, m_ln1_b, m_w_up, m_conv_w, m_conv_b, m_w_down, m_ln2_g, m_ln2_b, v_ada_w, v_ada_b, v_w_in, v_pool_w, v_pool_scale, v_sgu_ln_g, v_sgu_ln_b, v_sgu_w, v_sgu_b, v_w_out, v_ln1_g, v_ln1_b, v_w_up, v_conv_w, v_conv_b, v_w_down, v_ln2_g, v_ln2_b):
    nl = ada_w.shape[0]
    alpha = (2.0 * nl) ** 0.25
    me = _me()
    x2 = x[0]
    tgt = loss_target[0]
    acols = ada_w.shape[2]
    icols = w_in.shape[2]
    ucols = w_up.shape[2]
    orows = w_out.shape[1]
    drows = w_down.shape[1]
    ccols = conv_w.shape[2]

    winT_sh = jnp.swapaxes(w_in, 1, 2).astype(BF16)
    wupT_sh = jnp.swapaxes(w_up, 1, 2).astype(BF16)
    wout_sh = w_out.astype(BF16)
    wd_sh = w_down.astype(BF16)

    pay = _Payload()
    pay.gather(jnp.broadcast_to(c, (8, D)))
    pay.gather(winT_sh, 0)
    pay.gather(wout_sh, 0)
    pay.gather(conv_w)
    (c_g, winT_g, wout_g, cw_g), _ = _comm_call(pay, "gather_first")
    c_all = c_g[:, 0, :]
    winT = winT_g.reshape(DIN, D)
    wout = wout_g.reshape(D, D)
    cw_full = jnp.transpose(cw_g, (1, 2, 0, 3)).reshape(nl, 3, FF)
    cb_full = conv_b[:, None, :]

    ada_b_my = lax.dynamic_slice(ada_b, (0, me * acols), (nl, acols))[:, None, :]
    mod_blk = _ada_fwd_call(c_all, ada_w, ada_b_my)
    pay = _Payload()
    pay.gather(mod_blk)
    (mod_g,), _ = _comm_call(pay, "gather_mod")
    mod_me = lax.dynamic_index_in_dim(mod_g, me, axis=2, keepdims=False)
    modv = jnp.swapaxes(mod_me, 0, 1).reshape(nl, 6, D)
    modv = jnp.concatenate([modv, jnp.zeros((nl, 2, D), F32)], axis=1)

    ln1 = jnp.stack([ln1_g, ln1_b], axis=1)
    ln2 = jnp.stack([ln2_g, ln2_b], axis=1)
    sln = jnp.stack([sgu_ln_g, sgu_ln_b], axis=1)
    sbf = jnp.broadcast_to(sgu_b[..., None], sgu_b.shape + (GW,))
    small = (pool_w, pool_scale[:, None, :], sln, sgu_w, sbf)

    r1s, r2s, f1s, f2s, ahs, ghs, wins, wouts, wups, wds = ([None] * nl for _ in range(10))
    wins[0], wouts[0] = winT, wout
    xin = x2
    for l in range(nl):
        pay = None
        if l == 0:
            pay = _Payload()
            pay.gather(wupT_sh, 0)
            pay.gather(wd_sh, 0)
        (r1, f1, ah), new, _ = _fa_call(l, l == 0, alpha, xin, modv, ln2, wins[l], wouts[l], small, pay)
        if l == 0:
            wups[0], wds[0] = new[0].reshape(4, FH, D), new[1].reshape(2, FH, D)
        pay = None
        if l + 1 < nl:
            pay = _Payload()
            pay.gather(winT_sh, l + 1)
            pay.gather(wout_sh, l + 1)
            pay.gather(wupT_sh, l + 1)
            pay.gather(wd_sh, l + 1)
        (r2, f2, gh), new, _ = _fb_call(l, alpha, r1, modv, ln1, wups[l], wds[l], cw_full, cb_full, pay)
        if l + 1 < nl:
            wins[l + 1], wouts[l + 1] = new[0].reshape(DIN, D), new[1].reshape(D, D)
            wups[l + 1], wds[l + 1] = new[2].reshape(4, FH, D), new[3].reshape(2, FH, D)
        r1s[l], r2s[l], f1s[l], f2s[l], ahs[l], ghs[l] = r1, r2, f1, f2, ah, gh
        xin = r2

    nsmall = 6 * D + 2 * NG * GW * GW + 3 * DP + NG * GW + 4 * D + 4 * FF
    srows = nsmall // 1024
    buf_in = lax.empty((nl, NDEV, icols, D), BF16)
    buf_out = lax.empty((nl, NDEV, orows, D), BF16)
    buf_up = lax.empty((nl, NDEV, ucols, D), BF16)
    buf_down = lax.empty((nl, NDEV, drows, D), BF16)
    buf_small = lax.empty((nl, NDEV, srows, 1024), F32)

    def a_side_payload(l, dwin, dwout, flat):
        p = _Payload()
        p.exchange_into([(dwin.reshape(NDEV, icols, D), 0)], buf_in, (l,))
        p.exchange_into([(dwout.reshape(NDEV, orows, D), 0)], buf_out, (l,))
        p.gather_into(flat, buf_small, (l,))
        return p

    dx = tgt
    loss_acc = None
    pending = None
    for l in reversed(range(nl)):
        last = l == nl - 1
        pay = None if pending is None else a_side_payload(*pending)
        res0, _, al = _bb_call(l, 0, last, alpha, r1s[l], r2s[l], dx, f2s[l], ghs[l], None, modv, ln1, ln2,
                               wups[l], wds[l], cw_full, cb_full, pay)
        if pending is not None:
            buf_in, buf_out, buf_small = al
        dh2p, dwg0, dwv0, dwd0, vb0, cacc0 = res0[:6]
        if last:
            loss_acc = res0[6]
        (dx1, dwg1, dwv1, dwd1, vb1, cacc1), _, _ = _bb_call(
            l, 1, last, alpha, r1s[l], r2s[l], dx, f2s[l], ghs[l], dh2p, modv, ln1, ln2, wups[l], wds[l],
            cw_full, cb_full)
        pay = _Payload()
        pay.exchange_into([(dwg0.reshape(2, ucols, D), 0), (dwg1.reshape(2, ucols, D), 2),
                           (dwv0.reshape(2, ucols, D), 4), (dwv1.reshape(2, ucols, D), 6)], buf_up, (l,))
        pay.exchange_into([(dwd0.reshape(4, drows, D), 0), (dwd1.reshape(4, drows, D), 4)], buf_down, (l,))
        xin = x2 if l == 0 else r2s[l - 1]
        (dx, dwin, dwout, dpw, dsw, dsb, va, va5), _, (buf_up, buf_down) = _ba_call(
            l, l == 0, alpha, xin, r1s[l], dx1, f1s[l], ahs[l], modv, ln2, ln1, wins[l], wouts[l], small, pay)
        cacc = jnp.concatenate([cacc0, cacc1], axis=1)
        piece = dict(ada_b=jnp.stack([va[3], va[4], va[2], vb1[0], vb1[1], vb0[2]]), pool_w=dpw,
                     pool_scale=va5[0], sgu_ln_g=va5[1], sgu_ln_b=va5[2], sgu_w=dsw, sgu_b=dsb[:, :, 0],
                     ln1_g=va[0], ln1_b=va[1], conv_w=cacc[1:4], conv_b=cacc[0], ln2_g=vb0[0], ln2_b=vb0[1])
        flat = jnp.concatenate([piece[n].reshape(-1) for n in _SMALL_ORDER]).reshape(srows, 1024)
        pending = (l, dwin, dwout, flat)
    _, (buf_in, buf_out, buf_small) = _comm_call(a_side_payload(*pending), "exchange_last")
    grad_x = dx[None]
    loss = lax.psum((0.5 / D) * jnp.sum(loss_acc), ("x", "y", "c"))

    weights = dict(ada_b=ada_b, pool_w=pool_w, pool_scale=pool_scale, sgu_ln_g=sgu_ln_g, sgu_ln_b=sgu_ln_b,
                   sgu_w=sgu_w, sgu_b=sgu_b, ln1_g=ln1_g, ln1_b=ln1_b, conv_w=conv_w, conv_b=conv_b, ln2_g=ln2_g,
                   ln2_b=ln2_b)
    m_of = dict(ada_b=m_ada_b, pool_w=m_pool_w, pool_scale=m_pool_scale, sgu_ln_g=m_sgu_ln_g, sgu_ln_b=m_sgu_ln_b,
                sgu_w=m_sgu_w, sgu_b=m_sgu_b, ln1_g=m_ln1_g, ln1_b=m_ln1_b, conv_w=m_conv_w, conv_b=m_conv_b,
                ln2_g=m_ln2_g, ln2_b=m_ln2_b)
    v_of = dict(ada_b=v_ada_b, pool_w=v_pool_w, pool_scale=v_pool_scale, sgu_ln_g=v_sgu_ln_g, sgu_ln_b=v_sgu_ln_b,
                sgu_w=v_sgu_w, sgu_b=v_sgu_b, ln1_g=v_ln1_g, ln1_b=v_ln1_b, conv_w=v_conv_w, conv_b=v_conv_b,
                ln2_g=v_ln2_g, ln2_b=v_ln2_b)
    small_all = jnp.swapaxes(buf_small.reshape(nl, NDEV, nsmall), 0, 1)
    res = {}
    off = 0
    dmod_all = None
    for name in _SMALL_ORDER:
        w = weights[name]
        per_layer = (FF * 3 if name == "conv_w" else w[0].size)
        parts = small_all[:, :, off:off + per_layer]
        off += per_layer
        if name == "ada_b":
            dmod_all = parts
        if name == "conv_w":
            parts = lax.dynamic_slice_in_dim(parts.reshape(NDEV, nl, 3, FF), me * ccols, ccols, axis=3)
        cdim = w.shape[-1]
        w2 = w.reshape(1, -1, cdim)
        outs = _adamw_call(parts.reshape(1, NDEV, -1, cdim), w2, m_of[name].reshape(w2.shape),
                           v_of[name].reshape(w2.shape), f"adamw_{name}")
        res[name] = tuple(o.reshape(w.shape) for o in outs)

    dmod_my = jnp.swapaxes(lax.dynamic_slice_in_dim(dmod_all, me * acols, acols, axis=2), 0, 1)
    res["ada_w"] = _ada_bwd_call(jnp.swapaxes(c_all, 0, 1), dmod_my, ada_w, m_ada_w, v_ada_w)

    g_win = jnp.swapaxes(_sum_parts_call(buf_in, "sum_w_in"), 1, 2)
    g_wup = jnp.swapaxes(_sum_parts_call(buf_up, "sum_w_up"), 1, 2)
    res["w_in"] = _adamw_call(g_win[:, None], w_in, m_w_in, v_w_in, "adamw_w_in")
    res["w_up"] = _adamw_call(g_wup[:, None], w_up, m_w_up, v_w_up, "adamw_w_up")
    res["w_out"] = _adamw_call(buf_out, w_out, m_w_out, v_w_out, "adamw_w_out")
    res["w_down"] = _adamw_call(buf_down, w_down, m_w_down, v_w_down, "adamw_w_down")

    order = ["ada_w", "ada_b", "w_in", "pool_w", "pool_scale", "sgu_ln_g", "sgu_ln_b", "sgu_w", "sgu_b", "w_out",
             "ln1_g", "ln1_b", "w_up", "conv_w", "conv_b", "w_down", "ln2_g", "ln2_b"]
    out = [loss, grad_x]
    for k in range(4):
        out += [res[n][k] for n in order]
    return tuple(out)
```

```python
import jax
import jax.numpy as jnp
from jax import lax
from jax.experimental import pallas as pl
from jax.experimental.pallas import tpu as pltpu

F32 = jnp.float32
BF16 = jnp.bfloat16

NDEV = 8
D = 1024
DP = 512
DS = 512
DIN = DP + 2 * DS
FF = 2816
FH = FF // 2
NG = 4
GW = 128
WINDOWS = (2, 4, 8, 16)
AHALO = 16
GHALO = 8
LN_EPS = 1e-5
ADAM_LR, ADAM_B1, ADAM_B2, ADAM_EPS, ADAM_WD, ADAM_STEP = 0.001, 0.9, 0.999, 1e-08, 0.01, 10
TSF = 512
TSB = 256
_K0 = 0.7978845608028654
_K1 = 0.044715
MESH_ID = pl.DeviceIdType.MESH


def _mm(a, b):
    return jnp.dot(a, b, preferred_element_type=F32)


def _mm_nt(a, b):
    return lax.dot_general(a, b, (((1,), (1,)), ((), ())), preferred_element_type=F32)


def _mm_tn(a, b):
    return lax.dot_general(a, b, (((0,), (0,)), ((), ())), preferred_element_type=F32)


def _rowmean(x):
    return jnp.mean(x, axis=-1, keepdims=True)


def _ln_stats(x):
    mu = _rowmean(x)
    xc = x - mu
    rstd = lax.rsqrt(_rowmean(xc * xc) + LN_EPS)
    return xc * rstd, rstd


def _ln_bwd(dy, gamma, xhat, rstd):
    dxh = dy * gamma
    return rstd * (dxh - _rowmean(dxh) - xhat * _rowmean(dxh * xhat))


def _gelu_t(x):
    t = jnp.tanh(_K0 * (x + _K1 * (x * x * x)))
    return 0.5 * x * (1.0 + t), t


def _dgelu(x, t):
    return 0.5 * (1.0 + t) + 0.5 * x * (1.0 - t * t) * (_K0 * (1.0 + 3.0 * _K1 * (x * x)))


def _colsum8(x):
    t, n = x.shape
    return jnp.sum(x.reshape(t // 8, 8, n), axis=0)


def _tril_mask():
    r = lax.broadcasted_iota(jnp.int32, (GW, GW), 0)
    c = lax.broadcasted_iota(jnp.int32, (GW, GW), 1)
    return c <= r


def _full(shape):
    n = len(shape)
    return pl.BlockSpec(shape, lambda *_: (0,) * n)


def _resident(tail, lead=()):
    n = len(tail)
    return pl.BlockSpec((None,) * len(lead) + tuple(tail), lambda *_: tuple(lead) + (0,) * n,
                        pipeline_mode=pl.Buffered(1))


def _layer_vec(rows, width, l):
    return pl.BlockSpec((None, rows, width), lambda *_: (l, 0, 0))


_VMEM_WHOLE = pl.BlockSpec(memory_space=pltpu.VMEM)
_HBM = pl.BlockSpec(memory_space=pl.ANY)
_ARB = pltpu.CompilerParams(dimension_semantics=("arbitrary",))


def _me():
    return 4 * lax.axis_index("x") + 2 * lax.axis_index("y") + lax.axis_index("c")


def _coords(p):
    return (p >> 2, (p >> 1) & 1, p & 1)


class _Payload:
    def __init__(self):
        self.srcs, self.new, self.alias, self.transfers = [], [], [], []

    def _src(self, arr):
        self.srcs.append(arr)
        return len(self.srcs) - 1

    def _alias(self, buf):
        self.alias.append(buf)
        return len(self.alias) - 1

    def gather(self, arr, chunk=None):
        pos = self._src(arr)
        blk = arr.shape if chunk is None else arr.shape[1:]
        self.new.append(jax.ShapeDtypeStruct((NDEV,) + tuple(blk), arr.dtype))
        self.transfers.append(([(pos, chunk)] * NDEV, ("new", len(self.new) - 1), ()))
        return len(self.new) - 1

    def gather_into(self, arr, buf, lead):
        pos = self._src(arr)
        self.transfers.append(([(pos, None)] * NDEV, ("alias", self._alias(buf)), tuple(lead)))

    def exchange_into(self, parts, buf, lead):
        route = {}
        for arr, first in parts:
            pos = self._src(arr)
            for q in range(arr.shape[0]):
                route[first + q] = (pos, q)
        self.transfers.append(([route[p] for p in range(NDEV)], ("alias", self._alias(buf)), tuple(lead)))

    def _ends(self, t, io, dst_dev, src_dev):
        srcs, new_out, alias_out = io
        route, (kind, k), lead = self.transfers[t]
        pos, q = route[dst_dev]
        src = srcs[pos] if q is None else srcs[pos].at[q]
        buf = new_out[k] if kind == "new" else alias_out[k]
        return src, buf.at[lead + (src_dev,)]

    def _remote(self, t, io, sems, src_dev, dst_dev):
        src, dst = self._ends(t, io, dst_dev, src_dev)
        return pltpu.make_async_remote_copy(
            src_ref=src, dst_ref=dst, send_sem=sems[0].at[t, dst_dev], recv_sem=sems[1].at[t, src_dev],
            device_id=_coords(dst_dev), device_id_type=MESH_ID)

    def _local(self, t, io, sems, p):
        src, dst = self._ends(t, io, p, p)
        return pltpu.make_async_copy(src, dst, sems[2].at[t])

    def start(self, io, sems):
        me = _me()
        for p in range(NDEV):
            @pl.when(me == p)
            def _():
                for t in range(len(self.transfers)):
                    self._local(t, io, sems, p).start()

            @pl.when(me != p)
            def _():
                for t in range(len(self.transfers)):
                    self._remote(t, io, sems, me, p).start()

    def wait(self, io, sems):
        me = _me()
        for p in range(NDEV):
            @pl.when(me == p)
            def _():
                for t in range(len(self.transfers)):
                    self._local(t, io, sems, p).wait()

            @pl.when(me != p)
            def _():
                for t in range(len(self.transfers)):
                    self._remote(t, io, sems, p, p).wait_recv()
                    self._remote(t, io, sems, me, p).wait_send()


def _pcall(body, name, nsteps, in_specs, out_specs, out_shape, scratch, args, pay=None):
    n_in, n_out, n_scr = len(args), len(out_shape), len(scratch)
    if pay is None:
        res = pl.pallas_call(body, name=name, grid=(nsteps,), in_specs=list(in_specs), out_specs=list(out_specs),
                             out_shape=list(out_shape), scratch_shapes=list(scratch), compiler_params=_ARB)(*args)
        return list(res), [], []
    ns, nn, na, nt = len(pay.srcs), len(pay.new), len(pay.alias), len(pay.transfers)

    def full(*refs):
        cin = refs[:n_in]
        srcs = refs[n_in:n_in + ns]
        o0 = n_in + ns + na
        cout = refs[o0:o0 + n_out]
        new_out = refs[o0 + n_out:o0 + n_out + nn]
        alias_out = refs[o0 + n_out + nn:o0 + n_out + nn + na]
        s0 = o0 + n_out + nn + na
        cscr = refs[s0:s0 + n_scr]
        sems = refs[s0 + n_scr:]
        io = (srcs, new_out, alias_out)
        i = pl.program_id(0)

        @pl.when(i == 0)
        def _():
            pay.start(io, sems)

        body(*cin, *cout, *cscr)

        @pl.when(i == nsteps - 1)
        def _():
            pay.wait(io, sems)

    res = pl.pallas_call(
        full, name=name, grid=(nsteps,),
        in_specs=list(in_specs) + [_HBM] * (ns + na), out_specs=list(out_specs) + [_HBM] * (nn + na),
        out_shape=list(out_shape) + pay.new + [jax.ShapeDtypeStruct(b.shape, b.dtype) for b in pay.alias],
        input_output_aliases={n_in + ns + k: n_out + nn + k for k in range(na)},
        scratch_shapes=list(scratch) + [pltpu.SemaphoreType.DMA((nt, NDEV)), pltpu.SemaphoreType.DMA((nt, NDEV)),
                                        pltpu.SemaphoreType.DMA((nt,))],
        compiler_params=_ARB,
    )(*args, *pay.srcs, *pay.alias)
    return list(res[:n_out]), list(res[n_out:n_out + nn]), list(res[n_out + nn:])


def _comm_call(pay, name):
    def body():
        pass

    _, new, alias = _pcall(body, name, 1, [], [], [], [], [], pay)
    return new, alias


def _a_forward(x, modv_ref, winT_ref, pw_ref, ps_ref, sln_ref, sw_ref, sbf_ref, abuf, zbuf, tile, ts):
    sh1 = modv_ref[0:1, :]
    sc1 = modv_ref[1:2, :]
    hb = (x * (1.0 + sc1) + sh1).astype(BF16)
    proj = _mm_nt(hb, winT_ref[...])
    a = proj[:, 0:DP]
    u = proj[:, DP:DP + DS]
    v = proj[:, DP + DS:]
    abuf[AHALO:AHALO + ts, :] = a
    tglob = tile * ts + lax.broadcasted_iota(jnp.int32, (ts, 1), 0)
    pooled_b, mixed, inv_cnt, pwb = [], [], [], []
    for g, w in enumerate(WINDOWS):
        cs = slice(g * GW, (g + 1) * GW)
        a_g = abuf[AHALO:AHALO + ts, cs]
        s = a_g
        for k in range(1, w):
            s = s + abuf[AHALO - k:AHALO - k + ts, cs]
        inv = 1.0 / jnp.minimum(tglob + 1, w).astype(F32)
        pg = (s * inv - a_g).astype(BF16)
        wg = pw_ref[g].astype(BF16)
        pooled_b.append(pg)
        inv_cnt.append(inv)
        pwb.append(wg)
        mixed.append(_mm(pg, wg))
    mixed = jnp.concatenate(mixed, axis=1)
    ya = mixed * ps_ref[...]
    ug, tu = _gelu_t(u)
    vg, tv = _gelu_t(v)
    vhat, rstdv = _ln_stats(vg)
    vnb = (vhat * sln_ref[0:1, :] + sln_ref[1:2, :]).astype(BF16)
    tri = _tril_mask()
    wt = [jnp.where(tri, sw_ref[h], 0.0).astype(BF16) for h in range(NG)]
    for c in range(ts // GW):
        rs = slice(c * GW, (c + 1) * GW)
        for h in range(NG):
            cs = slice(h * GW, (h + 1) * GW)
            zbuf[rs, cs] = _mm(wt[h], vnb[rs, cs]) + sbf_ref[h]
    z = zbuf[...]
    yb = ug * z
    return dict(hb=hb, sc1=sc1, u=u, v=v, tu=tu, tv=tv, ug=ug, z=z, vhat=vhat, rstdv=rstdv, vnb=vnb,
                wt=wt, pooled_b=pooled_b, pwb=pwb, inv_cnt=inv_cnt, mixed=mixed, ya=ya, yb=yb)


def _small_specs(l):
    grp = pl.BlockSpec((None, NG, GW, GW), lambda i: (l, 0, 0, 0))
    return [grp, _layer_vec(1, DP, l), _layer_vec(2, DS, l), grp, grp]


def _fa_call(l, first, alpha, xin, modv, lnp, winT, wout, small, pay=None):
    s = xin.shape[0]
    nt = s // TSF
    rr = TSF // TSB

    def body(xin_ref, modv_ref, lnp_ref, winT_ref, wout_ref, pw_ref, ps_ref, sln_ref, sw_ref, sbf_ref,
             r1_ref, f1_ref, ah_ref, abuf, zbuf, mixbuf):
        i = pl.program_id(0)

        @pl.when(i == 0)
        def _():
            abuf[0:AHALO, :] = jnp.zeros((AHALO, DP), F32)

        x = xin_ref[...]
        if not first:
            xhat, _ = _ln_stats(x)
            x = xhat * lnp_ref[0:1, :] + lnp_ref[1:2, :]
        fw = _a_forward(x, modv_ref, winT_ref, pw_ref, ps_ref, sln_ref, sw_ref, sbf_ref, abuf, zbuf, i, TSF)
        mixbuf[:, 0:DP] = fw["ya"].astype(BF16)
        mixbuf[:, DP:] = fw["yb"].astype(BF16)
        f = _mm(mixbuf[...], wout_ref[...])
        r1_ref[...] = alpha * x + modv_ref[2:3, :] * f
        f1_ref[...] = f.astype(BF16)
        for r in range(rr):
            ah_ref[r] = abuf[(r + 1) * TSB:(r + 1) * TSB + AHALO, :]
        abuf[0:AHALO, :] = abuf[TSF:TSF + AHALO, :]

    tile = pl.BlockSpec((TSF, D), lambda i: (i, 0))
    return _pcall(
        body, f"fa{l}", nt,
        in_specs=[tile, _layer_vec(8, D, l), _layer_vec(2, D, max(l - 1, 0)), _resident((DIN, D)),
                  _resident((D, D))] + _small_specs(l),
        out_specs=[tile, tile, pl.BlockSpec((rr, AHALO, DP), lambda i: (i, 0, 0))],
        out_shape=[jax.ShapeDtypeStruct((s, D), F32), jax.ShapeDtypeStruct((s, D), BF16),
                   jax.ShapeDtypeStruct((s // TSB, AHALO, DP), F32)],
        scratch=[pltpu.VMEM((TSF + AHALO, DP), F32), pltpu.VMEM((TSF, DS), F32), pltpu.VMEM((TSF, D), BF16)],
        args=[xin, modv, lnp, winT, wout, *small], pay=pay)


def _ba_call(l, first, alpha, xin, r1, dx1, f1, ahalo, modv, lnp, ln1, winT, wout, small, pay=None):
    s = xin.shape[0]
    nt = s // TSB
    ts = TSB

    def body(xin_ref, r1_ref, dx1_ref, f1_ref, ah_ref, modv_ref, lnp_ref, ln1_ref, winT_ref, wout_ref, pw_ref,
             ps_ref, sln_ref, sw_ref, sbf_ref,
             dx_ref, dwin_ref, dwout_ref, dpw_ref, dsw_ref, dsb_ref, va_ref, va5_ref,
             abuf, qbuf, zbuf, dvnbuf, mixbuf, dpbuf, dbacc, vacc, vacc5, dwin_acc, dwout_acc):
        i = pl.program_id(0)
        j = nt - 1 - i

        @pl.when(i == 0)
        def _():
            dwin_acc[...] = jnp.zeros_like(dwin_acc)
            dwout_acc[...] = jnp.zeros_like(dwout_acc)
            dpw_ref[...] = jnp.zeros_like(dpw_ref)
            dsw_ref[...] = jnp.zeros_like(dsw_ref)
            dbacc[...] = jnp.zeros_like(dbacc)
            vacc[...] = jnp.zeros_like(vacc)
            vacc5[...] = jnp.zeros_like(vacc5)
            qbuf[ts:ts + AHALO, :] = jnp.zeros((AHALO, DP), F32)

        x = xin_ref[...]
        if not first:
            xhat, _ = _ln_stats(x)
            x = xhat * lnp_ref[0:1, :] + lnp_ref[1:2, :]
        abuf[0:AHALO, :] = jnp.where(j > 0, ah_ref[0], 0.0)
        fw = _a_forward(x, modv_ref, winT_ref, pw_ref, ps_ref, sln_ref, sw_ref, sbf_ref, abuf, zbuf, j, ts)
        mixbuf[:, 0:DP] = fw["ya"].astype(BF16)
        mixbuf[:, DP:] = fw["yb"].astype(BF16)

        xhat1, rstd1 = _ln_stats(r1_ref[...])
        dy = dx1_ref[...]
        vacc[0] += _colsum8(dy * xhat1)
        vacc[1] += _colsum8(dy)
        dr1 = _ln_bwd(dy, ln1_ref[0:1, :], xhat1, rstd1)
        vacc[2] += _colsum8(dr1 * f1_ref[...].astype(F32))
        dfb = (dr1 * modv_ref[2:3, :]).astype(BF16)
        dwout_acc[...] += _mm_tn(mixbuf[...], dfb)
        dmix = _mm_nt(dfb, wout_ref[...])
        dya = dmix[:, 0:DP]
        dyb = dmix[:, DP:]

        vacc5[0] += _colsum8(dya * fw["mixed"])
        dmixed = (dya * ps_ref[...]).astype(BF16)
        dpooled = []
        for g in range(NG):
            cs = slice(g * GW, (g + 1) * GW)
            dpw_ref[g] += _mm_tn(fw["pooled_b"][g], dmixed[:, cs])
            dpg = _mm_nt(dmixed[:, cs], fw["pwb"][g])
            dpooled.append(dpg)
            qbuf[0:ts, cs] = dpg * fw["inv_cnt"][g]
        for g, w in enumerate(WINDOWS):
            cs = slice(g * GW, (g + 1) * GW)
            sq = qbuf[0:ts, cs]
            for k in range(1, w):
                sq = sq + qbuf[k:k + ts, cs]
            dpbuf[:, cs] = (sq - dpooled[g]).astype(BF16)
        qbuf[ts:ts + AHALO, :] = qbuf[0:AHALO, :]

        dug = dyb * fw["z"]
        dz = dyb * fw["ug"]
        dzb = dz.astype(BF16)
        for c in range(ts // GW):
            rs = slice(c * GW, (c + 1) * GW)
            for h in range(NG):
                cs = slice(h * GW, (h + 1) * GW)
                dvnbuf[rs, cs] = _mm_tn(fw["wt"][h], dzb[rs, cs])
                dsw_ref[h] += _mm_nt(dzb[rs, cs], fw["vnb"][rs, cs])
            dbacc[...] += dz[rs, :]
        dvn = dvnbuf[...]
        vacc5[1] += _colsum8(dvn * fw["vhat"])
        vacc5[2] += _colsum8(dvn)
        dvg = _ln_bwd(dvn, sln_ref[0:1, :], fw["vhat"], fw["rstdv"])
        dpbuf[:, DP:DP + DS] = (dug * _dgelu(fw["u"], fw["tu"])).astype(BF16)
        dpbuf[:, DP + DS:] = (dvg * _dgelu(fw["v"], fw["tv"])).astype(BF16)

        dpb = dpbuf[...]
        dwin_acc[...] += _mm_tn(dpb, fw["hb"])
        dh = _mm(dpb, winT_ref[...])
        dx_ref[...] = dh * (1.0 + fw["sc1"]) + alpha * dr1
        vacc[3] += _colsum8(dh)
        vacc[4] += _colsum8(dh * x)

        @pl.when(i == nt - 1)
        def _():
            dwin_ref[...] = dwin_acc[...].astype(BF16)
            dwout_ref[...] = dwout_acc[...].astype(BF16)
            tri = _tril_mask()
            for h in range(NG):
                dsw_ref[h] = jnp.where(tri, dsw_ref[h], 0.0)
                sb = jnp.sum(dbacc[:, h * GW:(h + 1) * GW], axis=1, keepdims=True)
                dsb_ref[h] = jnp.broadcast_to(sb, (GW, GW))
            for n in range(5):
                va_ref[n:n + 1, :] = jnp.sum(vacc[n], axis=0, keepdims=True)
            for n in range(3):
                va5_ref[n:n + 1, :] = jnp.sum(vacc5[n], axis=0, keepdims=True)

    rev = lambda i: (nt - 1 - i, 0)
    tile = pl.BlockSpec((ts, D), rev)
    return _pcall(
        body, f"ba{l}", nt,
        in_specs=[tile, tile, tile, tile,
                  pl.BlockSpec((1, AHALO, DP), lambda i: (jnp.maximum(nt - 2 - i, 0), 0, 0)),
                  _layer_vec(8, D, l), _layer_vec(2, D, max(l - 1, 0)), _layer_vec(2, D, l),
                  _resident((DIN, D)), _resident((D, D))] + _small_specs(l),
        out_specs=[tile] + [_VMEM_WHOLE] * 7,
        out_shape=[jax.ShapeDtypeStruct((s, D), F32), jax.ShapeDtypeStruct((DIN, D), BF16),
                   jax.ShapeDtypeStruct((D, D), BF16), jax.ShapeDtypeStruct((NG, GW, GW), F32),
                   jax.ShapeDtypeStruct((NG, GW, GW), F32), jax.ShapeDtypeStruct((NG, GW, GW), F32),
                   jax.ShapeDtypeStruct((5, D), F32), jax.ShapeDtypeStruct((3, DP), F32)],
        scratch=[pltpu.VMEM((ts + AHALO, DP), F32), pltpu.VMEM((ts + AHALO, DP), F32),
                 pltpu.VMEM((ts, DS), F32), pltpu.VMEM((ts, DS), F32), pltpu.VMEM((ts, D), BF16),
                 pltpu.VMEM((ts, DIN), BF16), pltpu.VMEM((GW, DS), F32), pltpu.VMEM((5, 8, D), F32),
                 pltpu.VMEM((3, 8, DP), F32), pltpu.VMEM((DIN, D), F32), pltpu.VMEM((D, D), F32)],
        args=[xin, r1, dx1, f1, ahalo, modv, lnp, ln1, winT, wout, *small], pay=pay)


def _cast_call(arrs, name):
    r, c = arrs[0].shape
    rb = _row_block(r, 6 * c * len(arrs))

    def body(*refs):
        for src, dst in zip(refs[:len(arrs)], refs[len(arrs):]):
            dst[...] = src[...].astype(BF16)

    blk = pl.BlockSpec((rb, c), lambda i: (i, 0))
    return pl.pallas_call(
        body, name=name, grid=(r // rb,), in_specs=[blk] * len(arrs), out_specs=[blk] * len(arrs),
        out_shape=[jax.ShapeDtypeStruct((r, c), BF16)] * len(arrs), compiler_params=_ARB)(*arrs)


def _rows_before(halo, x):
    ext = jnp.concatenate([halo, x], axis=0)
    return pltpu.roll(ext, 1, 0)[GHALO:, :], pltpu.roll(ext, 2, 0)[GHALO:, :]


def _rows_after(x, halo):
    ts = x.shape[0]
    ext = jnp.concatenate([x, halo], axis=0)
    n = ts + GHALO
    return pltpu.roll(ext, n - 1, 0)[0:ts, :], pltpu.roll(ext, n - 2, 0)[0:ts, :]


def _fb_call(l, alpha, r1, modv, ln1, wup4, wd2, conv_w, conv_b, pay=None):
    s = r1.shape[0]
    nt = s // TSF

    def body(r1_ref, modv_ref, ln1_ref, wup_ref, wd_ref, cw_ref, cb_ref, r2_ref, f2_ref, h2_ref, gs_ref, vs_ref,
             gbuf):
        i = pl.program_id(0)

        @pl.when(i == 0)
        def _():
            gbuf[...] = jnp.zeros_like(gbuf)

        xhat1, _ = _ln_stats(r1_ref[...])
        x1 = xhat1 * ln1_ref[0:1, :] + ln1_ref[1:2, :]
        h2b = (x1 * (1.0 + modv_ref[4:5, :]) + modv_ref[3:4, :]).astype(BF16)
        f2 = jnp.zeros((TSF, D), F32)
        for hf in range(2):
            cs = slice(hf * FH, (hf + 1) * FH)
            g = _mm_nt(h2b, wup_ref[hf])
            val = _mm_nt(h2b, wup_ref[2 + hf])
            gm1, gm2 = _rows_before(gbuf[hf], g)
            cw = cw_ref[:, cs]
            gc = cb_ref[:, cs] + cw[0:1, :] * gm2 + cw[1:2, :] * gm1 + cw[2:3, :] * g
            ge, _ = _gelu_t(gc)
            f2 = f2 + _mm((ge * val).astype(BF16), wd_ref[hf])
            gs_ref[:, cs] = g.astype(BF16)
            vs_ref[:, cs] = val.astype(BF16)
            gbuf[hf] = g[TSF - GHALO:, :]
        r2_ref[...] = alpha * x1 + modv_ref[5:6, :] * f2
        f2_ref[...] = f2.astype(BF16)
        h2_ref[...] = h2b

    tile = pl.BlockSpec((TSF, D), lambda i: (i, 0))
    wide = pl.BlockSpec((TSF, FF), lambda i: (i, 0))
    return _pcall(
        body, f"fb{l}", nt,
        in_specs=[tile, _layer_vec(8, D, l), _layer_vec(2, D, l), _resident((4, FH, D)), _resident((2, FH, D)),
                  _layer_vec(3, FF, l), _layer_vec(1, FF, l)],
        out_specs=[tile, tile, tile, wide, wide],
        out_shape=[jax.ShapeDtypeStruct((s, D), F32), jax.ShapeDtypeStruct((s, D), BF16),
                   jax.ShapeDtypeStruct((s, D), BF16), jax.ShapeDtypeStruct((s, FF), BF16),
                   jax.ShapeDtypeStruct((s, FF), BF16)],
        scratch=[pltpu.VMEM((2, GHALO, FH), F32)],
        args=[r1, modv, ln1, wup4, wd2, conv_w, conv_b], pay=pay)


def _ff_backward(h2b, df2b, gs_ref, vs_ref, gh_ref, first_tile, dgc_next, cw, cb, wg_ref, wv_ref, wd_ref,
                 dwg_ref, dwv_ref, dwd_ref, cacc):
    g = gs_ref[...].astype(F32)
    val = vs_ref[...].astype(F32)
    halo = gh_ref[...].astype(F32)[GHALO:, :]
    gm1, gm2 = _rows_before(jnp.where(first_tile, 0.0, halo), g)
    gc = cb + cw[0:1, :] * gm2 + cw[1:2, :] * gm1 + cw[2:3, :] * g
    ge, tg = _gelu_t(gc)
    mb = (ge * val).astype(BF16)
    dm = _mm_nt(df2b, wd_ref[...])
    dwd_ref[...] += _mm_tn(mb, df2b)
    dval = dm * ge
    dgc = dm * val * _dgelu(gc, tg)
    cacc[0] += _colsum8(dgc)
    cacc[1] += _colsum8(dgc * gm2)
    cacc[2] += _colsum8(dgc * gm1)
    cacc[3] += _colsum8(dgc * g)
    dgp1, dgp2 = _rows_after(dgc, dgc_next[...])
    dg = cw[2:3, :] * dgc + cw[1:2, :] * dgp1 + cw[0:1, :] * dgp2
    dgc_next[...] = dgc[0:GHALO, :]
    dgb = dg.astype(BF16)
    dvb = dval.astype(BF16)
    dwg_ref[...] += _mm_tn(dgb, h2b)
    dwv_ref[...] += _mm_tn(dvb, h2b)
    return _mm(dgb, wg_ref[...]) + _mm(dvb, wv_ref[...])


def _bb_specs(l, hf, nt, ts):
    wide = pl.BlockSpec((ts, FH), lambda i: (nt - 1 - i, hf))
    halo = pl.BlockSpec((2 * GHALO, FH), lambda i: (jnp.maximum((nt - 1 - i) * (ts // (2 * GHALO)) - 1, 0), hf))
    ins = [wide, wide, halo, _resident((FH, D), (hf,)), _resident((FH, D), (2 + hf,)), _resident((FH, D), (hf,)),
           pl.BlockSpec((None, 3, FH), lambda i: (l, 0, hf)), pl.BlockSpec((None, 1, FH), lambda i: (l, 0, hf))]
    acc_shapes = [jax.ShapeDtypeStruct((FH, D), F32)] * 3
    return ins, acc_shapes


def _bb0_call(l, last, h2, r2, dx2, f2, gs, vs, modv, ln2, wup4, wd2, conv_w, conv_b, pay=None):
    s = r2.shape[0]
    nt = s // TSB
    ts = TSB

    def body(*refs):
        it = iter(refs)
        h2_ref, r2_ref, dx2_ref, f2_ref, gs_ref, vs_ref, gh_ref = (next(it) for _ in range(7))
        wg_ref, wv_ref, wd_ref, cw_ref, cb_ref, modv_ref, ln2_ref = (next(it) for _ in range(7))
        dh2_ref, df2_ref, dr2_ref, dwg_ref, dwv_ref, dwd_ref, vb_ref, cacc_ref = (next(it) for _ in range(8))
        loss_ref = next(it) if last else None
        vacc, cacc, dgc_next = next(it), next(it), next(it)
        lacc = next(it) if last else None
        i = pl.program_id(0)

        @pl.when(i == 0)
        def _():
            dwg_ref[...] = jnp.zeros_like(dwg_ref)
            dwv_ref[...] = jnp.zeros_like(dwv_ref)
            dwd_ref[...] = jnp.zeros_like(dwd_ref)
            vacc[...] = jnp.zeros_like(vacc)
            cacc[...] = jnp.zeros_like(cacc)
            dgc_next[...] = jnp.zeros_like(dgc_next)
            if last:
                lacc[...] = jnp.zeros_like(lacc)

        xhat2, rstd2 = _ln_stats(r2_ref[...])
        if last:
            diff = xhat2 * ln2_ref[0:1, :] + ln2_ref[1:2, :] - dx2_ref[...]
            dy = diff * (1.0 / D)
            lacc[...] += _colsum8(diff * diff)
        else:
            dy = dx2_ref[...]
        dr2 = _ln_bwd(dy, ln2_ref[0:1, :], xhat2, rstd2)
        vacc[0] += _colsum8(dy * xhat2)
        vacc[1] += _colsum8(dy)
        vacc[2] += _colsum8(dr2 * f2_ref[...].astype(F32))
        df2b = (dr2 * modv_ref[5:6, :]).astype(BF16)
        dr2_ref[...] = dr2
        df2_ref[...] = df2b
        dh2_ref[...] = _ff_backward(h2_ref[...], df2b, gs_ref, vs_ref, gh_ref, i == nt - 1, dgc_next, cw_ref[...],
                                    cb_ref[...], wg_ref, wv_ref, wd_ref, dwg_ref, dwv_ref, dwd_ref, cacc)

        @pl.when(i == nt - 1)
        def _():
            for n in range(3):
                vb_ref[n:n + 1, :] = jnp.sum(vacc[n], axis=0, keepdims=True)
            for n in range(4):
                cacc_ref[n:n + 1, :] = jnp.sum(cacc[n], axis=0, keepdims=True)
            if last:
                loss_ref[...] = lacc[...]

    tile = pl.BlockSpec((ts, D), lambda i: (nt - 1 - i, 0))
    ff_ins, acc_shapes = _bb_specs(l, 0, nt, ts)
    out_specs = [tile, tile, tile] + [_VMEM_WHOLE] * 5
    out_shape = [jax.ShapeDtypeStruct((s, D), F32), jax.ShapeDtypeStruct((s, D), BF16),
                 jax.ShapeDtypeStruct((s, D), F32)] + acc_shapes + [jax.ShapeDtypeStruct((3, D), F32),
                                                                    jax.ShapeDtypeStruct((4, FH), F32)]
    scratch = [pltpu.VMEM((3, 8, D), F32), pltpu.VMEM((4, 8, FH), F32), pltpu.VMEM((GHALO, FH), F32)]
    if last:
        out_specs.append(_VMEM_WHOLE)
        out_shape.append(jax.ShapeDtypeStruct((8, D), F32))
        scratch.append(pltpu.VMEM((8, D), F32))
    return _pcall(body, f"bb{l}_0", nt, [tile, tile, tile, tile] + ff_ins + [_layer_vec(8, D, l), _layer_vec(2, D, l)],
                  out_specs, out_shape, scratch,
                  [h2, r2, dx2, f2, gs, vs, gs, wup4, wup4, wd2, conv_w, conv_b, modv, ln2], pay=pay)


def _bb1_call(l, alpha, h2, r1, df2, dr2, dh2_in, gs, vs, modv, ln1, wup4, wd2, conv_w, conv_b, pay=None):
    s = r1.shape[0]
    nt = s // TSB
    ts = TSB

    def body(h2_ref, r1_ref, df2_ref, dr2_ref, dh2_ref, gs_ref, vs_ref, gh_ref, wg_ref, wv_ref, wd_ref, cw_ref,
             cb_ref, modv_ref, ln1_ref, dx1_ref, dwg_ref, dwv_ref, dwd_ref, vb_ref, cacc_ref, vacc, cacc, dgc_next):
        i = pl.program_id(0)

        @pl.when(i == 0)
        def _():
            dwg_ref[...] = jnp.zeros_like(dwg_ref)
            dwv_ref[...] = jnp.zeros_like(dwv_ref)
            dwd_ref[...] = jnp.zeros_like(dwd_ref)
            vacc[...] = jnp.zeros_like(vacc)
            cacc[...] = jnp.zeros_like(cacc)
            dgc_next[...] = jnp.zeros_like(dgc_next)

        dh2 = dh2_ref[...] + _ff_backward(h2_ref[...], df2_ref[...], gs_ref, vs_ref, gh_ref, i == nt - 1, dgc_next,
                                          cw_ref[...], cb_ref[...], wg_ref, wv_ref, wd_ref, dwg_ref, dwv_ref,
                                          dwd_ref, cacc)
        xhat1, _ = _ln_stats(r1_ref[...])
        x1 = xhat1 * ln1_ref[0:1, :] + ln1_ref[1:2, :]
        dx1_ref[...] = dh2 * (1.0 + modv_ref[4:5, :]) + alpha * dr2_ref[...]
        vacc[0] += _colsum8(dh2)
        vacc[1] += _colsum8(dh2 * x1)

        @pl.when(i == nt - 1)
        def _():
            for n in range(2):
                vb_ref[n:n + 1, :] = jnp.sum(vacc[n], axis=0, keepdims=True)
            for n in range(4):
                cacc_ref[n:n + 1, :] = jnp.sum(cacc[n], axis=0, keepdims=True)

    tile = pl.BlockSpec((ts, D), lambda i: (nt - 1 - i, 0))
    ff_ins, acc_shapes = _bb_specs(l, 1, nt, ts)
    out_shape = [jax.ShapeDtypeStruct((s, D), F32)] + acc_shapes + [jax.ShapeDtypeStruct((2, D), F32),
                                                                   jax.ShapeDtypeStruct((4, FH), F32)]
    scratch = [pltpu.VMEM((2, 8, D), F32), pltpu.VMEM((4, 8, FH), F32), pltpu.VMEM((GHALO, FH), F32)]
    return _pcall(body, f"bb{l}_1", nt, [tile] * 5 + ff_ins + [_layer_vec(8, D, l), _layer_vec(2, D, l)],
                  [tile] + [_VMEM_WHOLE] * 5, out_shape, scratch,
                  [h2, r1, df2, dr2, dh2_in, gs, vs, gs, wup4, wup4, wd2, conv_w, conv_b, modv, ln1], pay=pay)


def _silu(c):
    return c * (1.0 / (1.0 + jnp.exp(-c)))


def _ada_fwd_call(c_all, ada_w, ada_b_my):
    nl, _, wcols = ada_w.shape

    def body(c_ref, w_ref, b_ref, o_ref):
        ca = _silu(c_ref[...])
        o_ref[...] = jnp.dot(ca, w_ref[...], preferred_element_type=F32,
                             precision=lax.Precision.HIGHEST) + b_ref[...]

    return pl.pallas_call(
        body, name="ada_fwd", grid=(nl,),
        in_specs=[_full((NDEV, D)), pl.BlockSpec((None, D, wcols), lambda l: (l, 0, 0)),
                  pl.BlockSpec((None, 1, wcols), lambda l: (l, 0, 0))],
        out_specs=pl.BlockSpec((None, NDEV, wcols), lambda l: (l, 0, 0)),
        out_shape=jax.ShapeDtypeStruct((nl, NDEV, wcols), F32),
        compiler_params=_ARB,
    )(c_all, ada_w, ada_b_my)


def _adam_update(w, g, m, v):
    m2 = ADAM_B1 * m + (1.0 - ADAM_B1) * g
    v2 = ADAM_B2 * v + (1.0 - ADAM_B2) * (g * g)
    m_hat = m2 / (1.0 - ADAM_B1 ** ADAM_STEP)
    v_hat = v2 / (1.0 - ADAM_B2 ** ADAM_STEP)
    delta = -ADAM_LR * (m_hat / (jnp.sqrt(v_hat) + ADAM_EPS) + ADAM_WD * w)
    return delta, m2, v2


def _ada_bwd_call(c_t, dmod_my, w, m, v):
    nl, _, wcols = w.shape
    rb = 256

    def body(ct_ref, dm_ref, w_ref, m_ref, v_ref, g_ref, d_ref, m2_ref, v2_ref):
        ca_t = _silu(ct_ref[...])
        dm = dm_ref[...]
        g = ca_t[:, 0:1] * dm[0:1, :]
        for b in range(1, NDEV):
            g = g + ca_t[:, b:b + 1] * dm[b:b + 1, :]
        delta, m2, v2 = _adam_update(w_ref[...], g, m_ref[...], v_ref[...])
        g_ref[...] = g
        d_ref[...] = delta
        m2_ref[...] = m2
        v2_ref[...] = v2

    blk = pl.BlockSpec((None, rb, wcols), lambda l, i: (l, i, 0))
    shp = jax.ShapeDtypeStruct(w.shape, F32)
    return pl.pallas_call(
        body, name="ada_bwd", grid=(nl, D // rb),
        in_specs=[pl.BlockSpec((rb, NDEV), lambda l, i: (i, 0)),
                  pl.BlockSpec((None, NDEV, wcols), lambda l, i: (l, 0, 0)), blk, blk, blk],
        out_specs=[blk, blk, blk, blk], out_shape=[shp, shp, shp, shp],
        compiler_params=pltpu.CompilerParams(dimension_semantics=("arbitrary", "arbitrary")),
    )(c_t, dmod_my, w, m, v)


def _row_block(r, row_bytes):
    budget = 6 * 1024 * 1024
    best = None
    for rb in range(16, min(r, 512) + 1, 16):
        if r % rb == 0 and rb * row_bytes <= budget:
            best = rb
    return best if best is not None else r


def _sum_parts_call(parts, name):
    nl, npart, r, c = parts.shape
    rb = _row_block(r, (npart + 1) * c * 4)

    def body(p_ref, g_ref):
        g = p_ref[0].astype(F32)
        for k in range(1, npart):
            g = g + p_ref[k].astype(F32)
        g_ref[...] = g

    return pl.pallas_call(
        body, name=name, grid=(nl, r // rb),
        in_specs=[pl.BlockSpec((None, npart, rb, c), lambda l, i: (l, 0, i, 0))],
        out_specs=pl.BlockSpec((None, rb, c), lambda l, i: (l, i, 0)),
        out_shape=jax.ShapeDtypeStruct((nl, r, c), F32),
        compiler_params=pltpu.CompilerParams(dimension_semantics=("arbitrary", "arbitrary")),
    )(parts)


def _adamw_call(parts, w, m, v, name):
    nl, npart, r, c = parts.shape
    rb = _row_block(r, (npart + 7) * c * 4)

    def body(p_ref, w_ref, m_ref, v_ref, g_ref, d_ref, m2_ref, v2_ref):
        g = p_ref[0].astype(F32)
        for k in range(1, npart):
            g = g + p_ref[k].astype(F32)
        delta, m2, v2 = _adam_update(w_ref[...], g, m_ref[...], v_ref[...])
        g_ref[...] = g
        d_ref[...] = delta
        m2_ref[...] = m2
        v2_ref[...] = v2

    blk = pl.BlockSpec((None, rb, c), lambda l, i: (l, i, 0))
    shp = jax.ShapeDtypeStruct((nl, r, c), F32)
    return pl.pallas_call(
        body, name=name, grid=(nl, r // rb),
        in_specs=[pl.BlockSpec((None, npart, rb, c), lambda l, i: (l, 0, i, 0)), blk, blk, blk],
        out_specs=[blk, blk, blk, blk], out_shape=[shp, shp, shp, shp],
        compiler_params=pltpu.CompilerParams(dimension_semantics=("arbitrary", "arbitrary")),
    )(parts, w, m, v)


_SMALL_ORDER = ("ada_b", "pool_w", "pool_scale", "sgu_ln_g", "sgu_ln_b", "sgu_w", "sgu_b", "ln1_g", "ln1_b",
                "conv_w", "conv_b", "ln2_g", "ln2_b")


def kernel(x, c, ada_w, ada_b, w_in, pool_w, pool_scale, sgu_ln_g, sgu_ln_b, sgu_w, sgu_b, w_out, ln1_g, ln1_b, w_up, conv_w, conv_b, w_down, ln2_g, ln2_b, loss_target, m_ada_w, m_ada_b, m_w_in, m_pool_w, m_pool_scale, m_sgu_ln_g, m_sgu_ln_b, m_sgu_w, m_sgu_b, m_w_out, m_ln1_g, m_ln1_b, m_w_up, m_conv_w, m_conv_b, m_w_down, m_ln2_g, m_ln2_b, v_ada_w, v_ada_b, v_w_in, v_pool_w, v_pool_scale, v_sgu_ln_g, v_sgu_ln_b, v_sgu_w, v_sgu_b, v_w_out, v_ln1_g, v_ln1_b, v_w_up, v_conv_w, v_conv_b, v_w_down, v_ln2_g, v_ln2_b):
    nl = ada_w.shape[0]
    alpha = (2.0 * nl) ** 0.25
    me = _me()
    x2 = x[0]
    tgt = loss_target[0]
    acols = ada_w.shape[2]
    icols = w_in.shape[2]
    ucols = w_up.shape[2]
    orows = w_out.shape[1]
    drows = w_down.shape[1]
    ccols = conv_w.shape[2]

    winT_sh = jnp.swapaxes(w_in, 1, 2).astype(BF16)
    wupT_sh = jnp.swapaxes(w_up, 1, 2).astype(BF16)
    wout_sh = w_out.astype(BF16)
    wd_sh = w_down.astype(BF16)

    pay = _Payload()
    pay.gather(jnp.broadcast_to(c, (8, D)))
    pay.gather(winT_sh, 0)
    pay.gather(wout_sh, 0)
    pay.gather(conv_w)
    (c_g, winT_g, wout_g, cw_g), _ = _comm_call(pay, "gather_first")
    c_all = c_g[:, 0, :]
    winT = winT_g.reshape(DIN, D)
    wout = wout_g.reshape(D, D)
    cw_full = jnp.transpose(cw_g, (1, 2, 0, 3)).reshape(nl, 3, FF)
    cb_full = conv_b[:, None, :]

    ada_b_my = lax.dynamic_slice(ada_b, (0, me * acols), (nl, acols))[:, None, :]
    mod_blk = _ada_fwd_call(c_all, ada_w, ada_b_my)
    pay = _Payload()
    pay.gather(mod_blk)
    (mod_g,), _ = _comm_call(pay, "gather_mod")
    mod_me = lax.dynamic_index_in_dim(mod_g, me, axis=2, keepdims=False)
    modv = jnp.swapaxes(mod_me, 0, 1).reshape(nl, 6, D)
    modv = jnp.concatenate([modv, jnp.zeros((nl, 2, D), F32)], axis=1)

    ln1 = jnp.stack([ln1_g, ln1_b], axis=1)
    ln2 = jnp.stack([ln2_g, ln2_b], axis=1)
    sln = jnp.stack([sgu_ln_g, sgu_ln_b], axis=1)
    sbf = jnp.broadcast_to(sgu_b[..., None], sgu_b.shape + (GW,))
    small = (pool_w, pool_scale[:, None, :], sln, sgu_w, sbf)

    r1s, r2s, f1s, f2s, ahs, ghs, wins, wouts, wups, wds = ([None] * nl for _ in range(10))
    wins[0], wouts[0] = winT, wout
    xin = x2
    for l in range(nl):
        pay = None
        if l == 0:
            pay = _Payload()
            pay.gather(wupT_sh, 0)
            pay.gather(wd_sh, 0)
        (r1, f1, ah), new, _ = _fa_call(l, l == 0, alpha, xin, modv, ln2, wins[l], wouts[l], small, pay)
        if l == 0:
            wups[0], wds[0] = new[0].reshape(4, FH, D), new[1].reshape(2, FH, D)
        pay = None
        if l + 1 < nl:
            pay = _Payload()
            pay.gather(winT_sh, l + 1)
            pay.gather(wout_sh, l + 1)
            pay.gather(wupT_sh, l + 1)
            pay.gather(wd_sh, l + 1)
        (r2, f2, h2, gs, vs), new, _ = _fb_call(l, alpha, r1, modv, ln1, wups[l], wds[l], cw_full, cb_full, pay)
        if l + 1 < nl:
            wins[l + 1], wouts[l + 1] = new[0].reshape(DIN, D), new[1].reshape(D, D)
            wups[l + 1], wds[l + 1] = new[2].reshape(4, FH, D), new[3].reshape(2, FH, D)
        r1s[l], r2s[l], f1s[l], f2s[l], ahs[l], ghs[l] = r1, r2, f1, f2, ah, (h2, gs, vs)
        xin = r2

    nsmall = 6 * D + 2 * NG * GW * GW + 3 * DP + NG * GW + 4 * D + 4 * FF
    srows = nsmall // 1024
    buf_in = lax.empty((nl, NDEV, icols, D), BF16)
    buf_out = lax.empty((nl, NDEV, orows, D), BF16)
    buf_up = lax.empty((nl, NDEV, ucols, D), BF16)
    buf_down = lax.empty((nl, NDEV, drows, D), BF16)
    buf_small = lax.empty((nl, NDEV, srows, 1024), F32)

    def a_side_payload(l, dwin, dwout, flat):
        p = _Payload()
        p.exchange_into([(dwin.reshape(NDEV, icols, D), 0)], buf_in, (l,))
        p.exchange_into([(dwout.reshape(NDEV, orows, D), 0)], buf_out, (l,))
        p.gather_into(flat, buf_small, (l,))
        return p

    dx = tgt
    loss_acc = None
    pending = None
    for l in reversed(range(nl)):
        last = l == nl - 1
        pay = None if pending is None else a_side_payload(*pending)
        h2, gs, vs = ghs[l]
        res0, _, al = _bb0_call(l, last, h2, r2s[l], dx, f2s[l], gs, vs, modv, ln2, wups[l], wds[l], cw_full,
                                cb_full, pay)
        if pending is not None:
            buf_in, buf_out, buf_small = al
        dh2p, df2, dr2, dwg0, dwv0, dwd0, vb0, cacc0 = res0[:8]
        if last:
            loss_acc = res0[8]
        (dx1, dwg1, dwv1, dwd1, vb1, cacc1), _, _ = _bb1_call(
            l, alpha, h2, r1s[l], df2, dr2, dh2p, gs, vs, modv, ln1, wups[l], wds[l], cw_full, cb_full)
        dwg0, dwv0, dwd0, dwg1, dwv1, dwd1 = _cast_call([dwg0, dwv0, dwd0, dwg1, dwv1, dwd1], f"cast{l}")
        pay = _Payload()
        pay.exchange_into([(dwg0.reshape(2, ucols, D), 0), (dwg1.reshape(2, ucols, D), 2),
                           (dwv0.reshape(2, ucols, D), 4), (dwv1.reshape(2, ucols, D), 6)], buf_up, (l,))
        pay.exchange_into([(dwd0.reshape(4, drows, D), 0), (dwd1.reshape(4, drows, D), 4)], buf_down, (l,))
        xin = x2 if l == 0 else r2s[l - 1]
        (dx, dwin, dwout, dpw, dsw, dsb, va, va5), _, (buf_up, buf_down) = _ba_call(
            l, l == 0, alpha, xin, r1s[l], dx1, f1s[l], ahs[l], modv, ln2, ln1, wins[l], wouts[l], small, pay)
        cacc = jnp.concatenate([cacc0, cacc1], axis=1)
        piece = dict(ada_b=jnp.stack([va[3], va[4], va[2], vb1[0], vb1[1], vb0[2]]), pool_w=dpw,
                     pool_scale=va5[0], sgu_ln_g=va5[1], sgu_ln_b=va5[2], sgu_w=dsw, sgu_b=dsb[:, :, 0],
                     ln1_g=va[0], ln1_b=va[1], conv_w=cacc[1:4], conv_b=cacc[0], ln2_g=vb0[0], ln2_b=vb0[1])
        flat = jnp.concatenate([piece[n].reshape(-1) for n in _SMALL_ORDER]).reshape(srows, 1024)
        pending = (l, dwin, dwout, flat)
    _, (buf_in, buf_out, buf_small) = _comm_call(a_side_payload(*pending), "exchange_last")
    grad_x = dx[None]
    loss = lax.psum((0.5 / D) * jnp.sum(loss_acc), ("x", "y", "c"))

    weights = dict(ada_b=ada_b, pool_w=pool_w, pool_scale=pool_scale, sgu_ln_g=sgu_ln_g, sgu_ln_b=sgu_ln_b,
                   sgu_w=sgu_w, sgu_b=sgu_b, ln1_g=ln1_g, ln1_b=ln1_b, conv_w=conv_w, conv_b=conv_b, ln2_g=ln2_g,
                   ln2_b=ln2_b)
    m_of = dict(ada_b=m_ada_b, pool_w=m_pool_w, pool_scale=m_pool_scale, sgu_ln_g=m_sgu_ln_g, sgu_ln_b=m_sgu_ln_b,
                sgu_w=m_sgu_w, sgu_b=m_sgu_b, ln1_g=m_ln1_g, ln1_b=m_ln1_b, conv_w=m_conv_w, conv_b=m_conv_b,
                ln2_g=m_ln2_g, ln2_b=m_ln2_b)
    v_of = dict(ada_b=v_ada_b, pool_w=v_pool_w, pool_scale=v_pool_scale, sgu_ln_g=v_sgu_ln_g, sgu_ln_b=v_sgu_ln_b,
                sgu_w=v_sgu_w, sgu_b=v_sgu_b, ln1_g=v_ln1_g, ln1_b=v_ln1_b, conv_w=v_conv_w, conv_b=v_conv_b,
                ln2_g=v_ln2_g, ln2_b=v_ln2_b)
    small_all = jnp.swapaxes(buf_small.reshape(nl, NDEV, nsmall), 0, 1)
    res = {}
    off = 0
    dmod_all = None
    for name in _SMALL_ORDER:
        w = weights[name]
        per_layer = (FF * 3 if name == "conv_w" else w[0].size)
        parts = small_all[:, :, off:off + per_layer]
        off += per_layer
        if name == "ada_b":
            dmod_all = parts
        if name == "conv_w":
            parts = lax.dynamic_slice_in_dim(parts.reshape(NDEV, nl, 3, FF), me * ccols, ccols, axis=3)
        cdim = w.shape[-1]
        w2 = w.reshape(1, -1, cdim)
        outs = _adamw_call(parts.reshape(1, NDEV, -1, cdim), w2, m_of[name].reshape(w2.shape),
                           v_of[name].reshape(w2.shape), f"adamw_{name}")
        res[name] = tuple(o.reshape(w.shape) for o in outs)

    dmod_my = jnp.swapaxes(lax.dynamic_slice_in_dim(dmod_all, me * acols, acols, axis=2), 0, 1)
    res["ada_w"] = _ada_bwd_call(jnp.swapaxes(c_all, 0, 1), dmod_my, ada_w, m_ada_w, v_ada_w)

    g_win = jnp.swapaxes(_sum_parts_call(buf_in, "sum_w_in"), 1, 2)
    g_wup = jnp.swapaxes(_sum_parts_call(buf_up, "sum_w_up"), 1, 2)
    res["w_in"] = _adamw_call(g_win[:, None], w_in, m_w_in, v_w_in, "adamw_w_in")
    res["w_up"] = _adamw_call(g_wup[:, None], w_up, m_w_up, v_w_up, "adamw_w_up")
    res["w_out"] = _adamw_call(buf_out, w_out, m_w_out, v_w_out, "adamw_w_out")
    res["w_down"] = _adamw_call(buf_down, w_down, m_w_down, v_w_down, "adamw_w_down")

    order = ["ada_w", "ada_b", "w_in", "pool_w", "pool_scale", "sgu_ln_g", "sgu_ln_b", "sgu_w", "sgu_b", "w_out",
             "ln1_g", "ln1_b", "w_up", "conv_w", "conv_b", "w_down", "ln2_g", "ln2_b"]
    out = [loss, grad_x]
    for k in range(4):
        out += [res[n][k] for n in order]
    return tuple(out)
```

```python
import jax
import jax.numpy as jnp
from jax import lax
from jax.experimental import pallas as pl
from jax.experimental.pallas import tpu as pltpu

F32 = jnp.float32
BF16 = jnp.bfloat16

NDEV = 8
D = 1024
DP = 512
DS = 512
DIN = DP + 2 * DS
FF = 2816
FH = FF // 2
NG = 4
GW = 128
WINDOWS = (2, 4, 8, 16)
AHALO = 16
GHALO = 8
LN_EPS = 1e-5
ADAM_LR, ADAM_B1, ADAM_B2, ADAM_EPS, ADAM_WD, ADAM_STEP = 0.001, 0.9, 0.999, 1e-08, 0.01, 10
TSF = 512
TSB = 256
_K0 = 0.7978845608028654
_K1 = 0.044715
MESH_ID = pl.DeviceIdType.MESH


def _mm(a, b):
    return jnp.dot(a, b, preferred_element_type=F32)


def _mm_nt(a, b):
    return lax.dot_general(a, b, (((1,), (1,)), ((), ())), preferred_element_type=F32)


def _mm_tn(a, b):
    return lax.dot_general(a, b, (((0,), (0,)), ((), ())), preferred_element_type=F32)


def _rowmean(x):
    return jnp.mean(x, axis=-1, keepdims=True)


def _ln_stats(x):
    mu = _rowmean(x)
    xc = x - mu
    rstd = lax.rsqrt(_rowmean(xc * xc) + LN_EPS)
    return xc * rstd, rstd


def _ln_bwd(dy, gamma, xhat, rstd):
    dxh = dy * gamma
    return rstd * (dxh - _rowmean(dxh) - xhat * _rowmean(dxh * xhat))


def _gelu_t(x):
    t = jnp.tanh(_K0 * (x + _K1 * (x * x * x)))
    return 0.5 * x * (1.0 + t), t


def _dgelu(x, t):
    return 0.5 * (1.0 + t) + 0.5 * x * (1.0 - t * t) * (_K0 * (1.0 + 3.0 * _K1 * (x * x)))


def _colsum8(x):
    t, n = x.shape
    return jnp.sum(x.reshape(t // 8, 8, n), axis=0)


def _tril_mask():
    r = lax.broadcasted_iota(jnp.int32, (GW, GW), 0)
    c = lax.broadcasted_iota(jnp.int32, (GW, GW), 1)
    return c <= r


def _full(shape):
    n = len(shape)
    return pl.BlockSpec(shape, lambda *_: (0,) * n)


def _resident(tail, lead=()):
    n = len(tail)
    return pl.BlockSpec((None,) * len(lead) + tuple(tail), lambda *_: tuple(lead) + (0,) * n,
                        pipeline_mode=pl.Buffered(1))


def _layer_vec(rows, width, l):
    return pl.BlockSpec((None, rows, width), lambda *_: (l, 0, 0))


_VMEM_WHOLE = pl.BlockSpec(memory_space=pltpu.VMEM)
_HBM = pl.BlockSpec(memory_space=pl.ANY)
_ARB = pltpu.CompilerParams(dimension_semantics=("arbitrary",))


def _me():
    return 4 * lax.axis_index("x") + 2 * lax.axis_index("y") + lax.axis_index("c")


def _coords(p):
    return (p >> 2, (p >> 1) & 1, p & 1)


class _Payload:
    def __init__(self):
        self.srcs, self.new, self.alias, self.transfers = [], [], [], []

    def _src(self, arr):
        self.srcs.append(arr)
        return len(self.srcs) - 1

    def _alias(self, buf):
        self.alias.append(buf)
        return len(self.alias) - 1

    def gather(self, arr, chunk=None):
        pos = self._src(arr)
        blk = arr.shape if chunk is None else arr.shape[1:]
        self.new.append(jax.ShapeDtypeStruct((NDEV,) + tuple(blk), arr.dtype))
        self.transfers.append(([(pos, chunk)] * NDEV, ("new", len(self.new) - 1), ()))
        return len(self.new) - 1

    def gather_into(self, arr, buf, lead):
        pos = self._src(arr)
        self.transfers.append(([(pos, None)] * NDEV, ("alias", self._alias(buf)), tuple(lead)))

    def exchange_into(self, parts, buf, lead):
        route = {}
        for arr, first in parts:
            pos = self._src(arr)
            for q in range(arr.shape[0]):
                route[first + q] = (pos, q)
        self.transfers.append(([route[p] for p in range(NDEV)], ("alias", self._alias(buf)), tuple(lead)))

    def _ends(self, t, io, dst_dev, src_dev):
        srcs, new_out, alias_out = io
        route, (kind, k), lead = self.transfers[t]
        pos, q = route[dst_dev]
        src = srcs[pos] if q is None else srcs[pos].at[q]
        buf = new_out[k] if kind == "new" else alias_out[k]
        return src, buf.at[lead + (src_dev,)]

    def _remote(self, t, io, sems, src_dev, dst_dev):
        src, dst = self._ends(t, io, dst_dev, src_dev)
        return pltpu.make_async_remote_copy(
            src_ref=src, dst_ref=dst, send_sem=sems[0].at[t, dst_dev], recv_sem=sems[1].at[t, src_dev],
            device_id=_coords(dst_dev), device_id_type=MESH_ID)

    def _local(self, t, io, sems, p):
        src, dst = self._ends(t, io, p, p)
        return pltpu.make_async_copy(src, dst, sems[2].at[t])

    def start(self, io, sems):
        me = _me()
        for p in range(NDEV):
            @pl.when(me == p)
            def _():
                for t in range(len(self.transfers)):
                    self._local(t, io, sems, p).start()

            @pl.when(me != p)
            def _():
                for t in range(len(self.transfers)):
                    self._remote(t, io, sems, me, p).start()

    def wait(self, io, sems):
        me = _me()
        for p in range(NDEV):
            @pl.when(me == p)
            def _():
                for t in range(len(self.transfers)):
                    self._local(t, io, sems, p).wait()

            @pl.when(me != p)
            def _():
                for t in range(len(self.transfers)):
                    self._remote(t, io, sems, p, p).wait_recv()
                    self._remote(t, io, sems, me, p).wait_send()


def _pcall(body, name, nsteps, in_specs, out_specs, out_shape, scratch, args, pay=None):
    n_in, n_out, n_scr = len(args), len(out_shape), len(scratch)
    if pay is None:
        res = pl.pallas_call(body, name=name, grid=(nsteps,), in_specs=list(in_specs), out_specs=list(out_specs),
                             out_shape=list(out_shape), scratch_shapes=list(scratch), compiler_params=_ARB)(*args)
        return list(res), [], []
    ns, nn, na, nt = len(pay.srcs), len(pay.new), len(pay.alias), len(pay.transfers)

    def full(*refs):
        cin = refs[:n_in]
        srcs = refs[n_in:n_in + ns]
        o0 = n_in + ns + na
        cout = refs[o0:o0 + n_out]
        new_out = refs[o0 + n_out:o0 + n_out + nn]
        alias_out = refs[o0 + n_out + nn:o0 + n_out + nn + na]
        s0 = o0 + n_out + nn + na
        cscr = refs[s0:s0 + n_scr]
        sems = refs[s0 + n_scr:]
        io = (srcs, new_out, alias_out)
        i = pl.program_id(0)

        @pl.when(i == 0)
        def _():
            pay.start(io, sems)

        body(*cin, *cout, *cscr)

        @pl.when(i == nsteps - 1)
        def _():
            pay.wait(io, sems)

    res = pl.pallas_call(
        full, name=name, grid=(nsteps,),
        in_specs=list(in_specs) + [_HBM] * (ns + na), out_specs=list(out_specs) + [_HBM] * (nn + na),
        out_shape=list(out_shape) + pay.new + [jax.ShapeDtypeStruct(b.shape, b.dtype) for b in pay.alias],
        input_output_aliases={n_in + ns + k: n_out + nn + k for k in range(na)},
        scratch_shapes=list(scratch) + [pltpu.SemaphoreType.DMA((nt, NDEV)), pltpu.SemaphoreType.DMA((nt, NDEV)),
                                        pltpu.SemaphoreType.DMA((nt,))],
        compiler_params=_ARB,
    )(*args, *pay.srcs, *pay.alias)
    return list(res[:n_out]), list(res[n_out:n_out + nn]), list(res[n_out + nn:])


def _comm_call(pay, name):
    def body():
        pass

    _, new, alias = _pcall(body, name, 1, [], [], [], [], [], pay)
    return new, alias


def _window_sums(x, halo, before):
    ts = x.shape[0]
    ext = jnp.concatenate([halo, x] if before else [x, halo], axis=0)
    n = ts + AHALO
    shift = (lambda k: k) if before else (lambda k: n - k)
    keep = slice(AHALO, n) if before else slice(0, ts)
    out = []
    s = ext
    for level in range(NG):
        s = s + pltpu.roll(s, shift(1 << level), 0)
        out.append(s[keep, 0:GW])
        if level + 1 < NG:
            s = s[:, GW:]
    return out


def _a_forward(a, u, v, halo, pw_ref, ps_ref, sln_ref, sw_ref, sbf_ref, zbuf, tile, ts):
    tglob = tile * ts + lax.broadcasted_iota(jnp.int32, (ts, 1), 0)
    sums = _window_sums(a, halo, True)
    pooled_b, mixed, inv_cnt, pwb = [], [], [], []
    for g, w in enumerate(WINDOWS):
        a_g = a[:, g * GW:(g + 1) * GW]
        s = sums[g]
        inv = 1.0 / jnp.minimum(tglob + 1, w).astype(F32)
        pg = (s * inv - a_g).astype(BF16)
        wg = pw_ref[g].astype(BF16)
        pooled_b.append(pg)
        inv_cnt.append(inv)
        pwb.append(wg)
        mixed.append(_mm(pg, wg))
    mixed = jnp.concatenate(mixed, axis=1)
    ya = mixed * ps_ref[...]
    ug, tu = _gelu_t(u)
    vg, tv = _gelu_t(v)
    vhat, rstdv = _ln_stats(vg)
    vnb = (vhat * sln_ref[0:1, :] + sln_ref[1:2, :]).astype(BF16)
    tri = _tril_mask()
    wt = [jnp.where(tri, sw_ref[h], 0.0).astype(BF16) for h in range(NG)]
    for c in range(ts // GW):
        rs = slice(c * GW, (c + 1) * GW)
        for h in range(NG):
            cs = slice(h * GW, (h + 1) * GW)
            zbuf[rs, cs] = _mm(wt[h], vnb[rs, cs]) + sbf_ref[h]
    z = zbuf[...]
    yb = ug * z
    return dict(u=u, v=v, tu=tu, tv=tv, ug=ug, z=z, vhat=vhat, rstdv=rstdv, vnb=vnb,
                wt=wt, pooled_b=pooled_b, pwb=pwb, inv_cnt=inv_cnt, mixed=mixed, ya=ya, yb=yb)


def _small_specs(l):
    grp = pl.BlockSpec((None, NG, GW, GW), lambda i: (l, 0, 0, 0))
    return [grp, _layer_vec(1, DP, l), _layer_vec(2, DS, l), grp, grp]


def _fa_call(l, first, alpha, xin, modv, lnp, winT, wout, small, pay=None):
    s = xin.shape[0]
    nt = s // TSF

    def body(xin_ref, modv_ref, lnp_ref, winT_ref, wout_ref, pw_ref, ps_ref, sln_ref, sw_ref, sbf_ref,
             r1_ref, f1_ref, h_ref, proj_ref, acarry, zbuf, mixbuf):
        i = pl.program_id(0)

        @pl.when(i == 0)
        def _():
            acarry[...] = jnp.zeros_like(acarry)

        x = xin_ref[...]
        if not first:
            xhat, _ = _ln_stats(x)
            x = xhat * lnp_ref[0:1, :] + lnp_ref[1:2, :]
        hb = (x * (1.0 + modv_ref[1:2, :]) + modv_ref[0:1, :]).astype(BF16)
        proj = _mm_nt(hb, winT_ref[...])
        a = proj[:, 0:DP]
        fw = _a_forward(a, proj[:, DP:DP + DS], proj[:, DP + DS:], acarry[...], pw_ref, ps_ref, sln_ref, sw_ref,
                        sbf_ref, zbuf, i, TSF)
        acarry[...] = a[TSF - AHALO:, :]
        mixbuf[:, 0:DP] = fw["ya"].astype(BF16)
        mixbuf[:, DP:] = fw["yb"].astype(BF16)
        f = _mm(mixbuf[...], wout_ref[...])
        r1_ref[...] = alpha * x + modv_ref[2:3, :] * f
        f1_ref[...] = f.astype(BF16)
        h_ref[...] = hb
        proj_ref[...] = proj.astype(BF16)

    tile = pl.BlockSpec((TSF, D), lambda i: (i, 0))
    return _pcall(
        body, f"fa{l}", nt,
        in_specs=[tile, _layer_vec(8, D, l), _layer_vec(2, D, max(l - 1, 0)), _resident((DIN, D)),
                  _resident((D, D))] + _small_specs(l),
        out_specs=[tile, tile, tile, pl.BlockSpec((TSF, DIN), lambda i: (i, 0))],
        out_shape=[jax.ShapeDtypeStruct((s, D), F32), jax.ShapeDtypeStruct((s, D), BF16),
                   jax.ShapeDtypeStruct((s, D), BF16), jax.ShapeDtypeStruct((s, DIN), BF16)],
        scratch=[pltpu.VMEM((AHALO, DP), F32), pltpu.VMEM((TSF, DS), F32), pltpu.VMEM((TSF, D), BF16)],
        args=[xin, modv, lnp, winT, wout, *small], pay=pay)


def _ba_call(l, first, alpha, xin, r1, dx1, f1, hsave, proj, modv, lnp, ln1, winT, wout, small, pay=None):
    s = xin.shape[0]
    nt = s // TSB
    ts = TSB

    def body(xin_ref, r1_ref, dx1_ref, f1_ref, h_ref, proj_ref, ah_ref, modv_ref, lnp_ref, ln1_ref, winT_ref,
             wout_ref, pw_ref, ps_ref, sln_ref, sw_ref, sbf_ref,
             dx_ref, dwin_ref, dwout_ref, dpw_ref, dsw_ref, dsb_ref, va_ref, va5_ref,
             qnext, zbuf, dvnbuf, mixbuf, dpbuf, dbacc, vacc, vacc5, dwin_acc, dwout_acc):
        i = pl.program_id(0)
        j = nt - 1 - i

        @pl.when(i == 0)
        def _():
            dwin_acc[...] = jnp.zeros_like(dwin_acc)
            dwout_acc[...] = jnp.zeros_like(dwout_acc)
            dpw_ref[...] = jnp.zeros_like(dpw_ref)
            dsw_ref[...] = jnp.zeros_like(dsw_ref)
            dbacc[...] = jnp.zeros_like(dbacc)
            vacc[...] = jnp.zeros_like(vacc)
            vacc5[...] = jnp.zeros_like(vacc5)
            qnext[...] = jnp.zeros_like(qnext)

        x = xin_ref[...]
        if not first:
            xhat, _ = _ln_stats(x)
            x = xhat * lnp_ref[0:1, :] + lnp_ref[1:2, :]
        hb = h_ref[...]
        sc1 = modv_ref[1:2, :]
        halo = jnp.where(j > 0, ah_ref[...].astype(F32), 0.0)
        fw = _a_forward(proj_ref[:, 0:DP].astype(F32), proj_ref[:, DP:DP + DS].astype(F32),
                        proj_ref[:, DP + DS:].astype(F32), halo, pw_ref, ps_ref, sln_ref, sw_ref, sbf_ref, zbuf, j, ts)
        mixbuf[:, 0:DP] = fw["ya"].astype(BF16)
        mixbuf[:, DP:] = fw["yb"].astype(BF16)

        xhat1, rstd1 = _ln_stats(r1_ref[...])
        dy = dx1_ref[...]
        vacc[0] += _colsum8(dy * xhat1)
        vacc[1] += _colsum8(dy)
        dr1 = _ln_bwd(dy, ln1_ref[0:1, :], xhat1, rstd1)
        vacc[2] += _colsum8(dr1 * f1_ref[...].astype(F32))
        dfb = (dr1 * modv_ref[2:3, :]).astype(BF16)
        dwout_acc[...] += _mm_tn(mixbuf[...], dfb)
        dmix = _mm_nt(dfb, wout_ref[...])
        dya = dmix[:, 0:DP]
        dyb = dmix[:, DP:]

        vacc5[0] += _colsum8(dya * fw["mixed"])
        dmixed = (dya * ps_ref[...]).astype(BF16)
        dpooled, q = [], []
        for g in range(NG):
            cs = slice(g * GW, (g + 1) * GW)
            dpw_ref[g] += _mm_tn(fw["pooled_b"][g], dmixed[:, cs])
            dpg = _mm_nt(dmixed[:, cs], fw["pwb"][g])
            dpooled.append(dpg)
            q.append(dpg * fw["inv_cnt"][g])
        q = jnp.concatenate(q, axis=1)
        sums = _window_sums(q, qnext[...], False)
        qnext[...] = q[0:AHALO, :]
        for g in range(NG):
            dpbuf[:, g * GW:(g + 1) * GW] = (sums[g] - dpooled[g]).astype(BF16)

        dug = dyb * fw["z"]
        dz = dyb * fw["ug"]
        dzb = dz.astype(BF16)
        for c in range(ts // GW):
            rs = slice(c * GW, (c + 1) * GW)
            for h in range(NG):
                cs = slice(h * GW, (h + 1) * GW)
                dvnbuf[rs, cs] = _mm_tn(fw["wt"][h], dzb[rs, cs])
                dsw_ref[h] += _mm_nt(dzb[rs, cs], fw["vnb"][rs, cs])
            dbacc[...] += dz[rs, :]
        dvn = dvnbuf[...]
        vacc5[1] += _colsum8(dvn * fw["vhat"])
        vacc5[2] += _colsum8(dvn)
        dvg = _ln_bwd(dvn, sln_ref[0:1, :], fw["vhat"], fw["rstdv"])
        dpbuf[:, DP:DP + DS] = (dug * _dgelu(fw["u"], fw["tu"])).astype(BF16)
        dpbuf[:, DP + DS:] = (dvg * _dgelu(fw["v"], fw["tv"])).astype(BF16)

        dpb = dpbuf[...]
        dwin_acc[...] += _mm_tn(dpb, hb)
        dh = _mm(dpb, winT_ref[...])
        dx_ref[...] = dh * (1.0 + sc1) + alpha * dr1
        vacc[3] += _colsum8(dh)
        vacc[4] += _colsum8(dh * x)

        @pl.when(i == nt - 1)
        def _():
            dwin_ref[...] = dwin_acc[...].astype(BF16)
            dwout_ref[...] = dwout_acc[...].astype(BF16)
            tri = _tril_mask()
            for h in range(NG):
                dsw_ref[h] = jnp.where(tri, dsw_ref[h], 0.0)
                sb = jnp.sum(dbacc[:, h * GW:(h + 1) * GW], axis=1, keepdims=True)
                dsb_ref[h] = jnp.broadcast_to(sb, (GW, GW))
            for n in range(5):
                va_ref[n:n + 1, :] = jnp.sum(vacc[n], axis=0, keepdims=True)
            for n in range(3):
                va5_ref[n:n + 1, :] = jnp.sum(vacc5[n], axis=0, keepdims=True)

    rev = lambda i: (nt - 1 - i, 0)
    tile = pl.BlockSpec((ts, D), rev)
    return _pcall(
        body, f"ba{l}", nt,
        in_specs=[tile, tile, tile, tile, tile, pl.BlockSpec((ts, DIN), rev),
                  pl.BlockSpec((AHALO, DP), lambda i: (jnp.maximum((nt - 1 - i) * (ts // AHALO) - 1, 0), 0)),
                  _layer_vec(8, D, l), _layer_vec(2, D, max(l - 1, 0)), _layer_vec(2, D, l),
                  _resident((DIN, D)), _resident((D, D))] + _small_specs(l),
        out_specs=[tile] + [_VMEM_WHOLE] * 7,
        out_shape=[jax.ShapeDtypeStruct((s, D), F32), jax.ShapeDtypeStruct((DIN, D), BF16),
                   jax.ShapeDtypeStruct((D, D), BF16), jax.ShapeDtypeStruct((NG, GW, GW), F32),
                   jax.ShapeDtypeStruct((NG, GW, GW), F32), jax.ShapeDtypeStruct((NG, GW, GW), F32),
                   jax.ShapeDtypeStruct((5, D), F32), jax.ShapeDtypeStruct((3, DP), F32)],
        scratch=[pltpu.VMEM((AHALO, DP), F32),
                 pltpu.VMEM((ts, DS), F32), pltpu.VMEM((ts, DS), F32), pltpu.VMEM((ts, D), BF16),
                 pltpu.VMEM((ts, DIN), BF16), pltpu.VMEM((GW, DS), F32), pltpu.VMEM((5, 8, D), F32),
                 pltpu.VMEM((3, 8, DP), F32), pltpu.VMEM((DIN, D), F32), pltpu.VMEM((D, D), F32)],
        args=[xin, r1, dx1, f1, hsave, proj, proj, modv, lnp, ln1, winT, wout, *small], pay=pay)


def _cast_call(arrs, name):
    r, c = arrs[0].shape
    rb = _row_block(r, 6 * c * len(arrs))

    def body(*refs):
        for src, dst in zip(refs[:len(arrs)], refs[len(arrs):]):
            dst[...] = src[...].astype(BF16)

    blk = pl.BlockSpec((rb, c), lambda i: (i, 0))
    return pl.pallas_call(
        body, name=name, grid=(r // rb,), in_specs=[blk] * len(arrs), out_specs=[blk] * len(arrs),
        out_shape=[jax.ShapeDtypeStruct((r, c), BF16)] * len(arrs), compiler_params=_ARB)(*arrs)


def _rows_before(halo, x):
    ext = jnp.concatenate([halo, x], axis=0)
    return pltpu.roll(ext, 1, 0)[GHALO:, :], pltpu.roll(ext, 2, 0)[GHALO:, :]


def _rows_after(x, halo):
    ts = x.shape[0]
    ext = jnp.concatenate([x, halo], axis=0)
    n = ts + GHALO
    return pltpu.roll(ext, n - 1, 0)[0:ts, :], pltpu.roll(ext, n - 2, 0)[0:ts, :]


def _fb_call(l, alpha, r1, modv, ln1, wup4, wd2, conv_w, conv_b, pay=None):
    s = r1.shape[0]
    nt = s // TSF

    def body(r1_ref, modv_ref, ln1_ref, wup_ref, wd_ref, cw_ref, cb_ref, r2_ref, f2_ref, h2_ref, gs_ref, vs_ref,
             gbuf):
        i = pl.program_id(0)

        @pl.when(i == 0)
        def _():
            gbuf[...] = jnp.zeros_like(gbuf)

        xhat1, _ = _ln_stats(r1_ref[...])
        x1 = xhat1 * ln1_ref[0:1, :] + ln1_ref[1:2, :]
        h2b = (x1 * (1.0 + modv_ref[4:5, :]) + modv_ref[3:4, :]).astype(BF16)
        f2 = jnp.zeros((TSF, D), F32)
        for hf in range(2):
            cs = slice(hf * FH, (hf + 1) * FH)
            g = _mm_nt(h2b, wup_ref[hf])
            val = _mm_nt(h2b, wup_ref[2 + hf])
            gm1, gm2 = _rows_before(gbuf[hf], g)
            cw = cw_ref[:, cs]
            gc = cb_ref[:, cs] + cw[0:1, :] * gm2 + cw[1:2, :] * gm1 + cw[2:3, :] * g
            ge, _ = _gelu_t(gc)
            f2 = f2 + _mm((ge * val).astype(BF16), wd_ref[hf])
            gs_ref[:, cs] = g.astype(BF16)
            vs_ref[:, cs] = val.astype(BF16)
            gbuf[hf] = g[TSF - GHALO:, :]
        r2_ref[...] = alpha * x1 + modv_ref[5:6, :] * f2
        f2_ref[...] = f2.astype(BF16)
        h2_ref[...] = h2b

    tile = pl.BlockSpec((TSF, D), lambda i: (i, 0))
    wide = pl.BlockSpec((TSF, FF), lambda i: (i, 0))
    return _pcall(
        body, f"fb{l}", nt,
        in_specs=[tile, _layer_vec(8, D, l), _layer_vec(2, D, l), _resident((4, FH, D)), _resident((2, FH, D)),
                  _layer_vec(3, FF, l), _layer_vec(1, FF, l)],
        out_specs=[tile, tile, tile, wide, wide],
        out_shape=[jax.ShapeDtypeStruct((s, D), F32), jax.ShapeDtypeStruct((s, D), BF16),
                   jax.ShapeDtypeStruct((s, D), BF16), jax.ShapeDtypeStruct((s, FF), BF16),
                   jax.ShapeDtypeStruct((s, FF), BF16)],
        scratch=[pltpu.VMEM((2, GHALO, FH), F32)],
        args=[r1, modv, ln1, wup4, wd2, conv_w, conv_b], pay=pay)


def _ff_backward(h2b, df2b, gs_ref, vs_ref, gh_ref, first_tile, dgc_next, cw, cb, wg_ref, wv_ref, wd_ref,
                 dwg_ref, dwv_ref, dwd_ref, cacc):
    g = gs_ref[...].astype(F32)
    val = vs_ref[...].astype(F32)
    halo = gh_ref[...].astype(F32)[GHALO:, :]
    gm1, gm2 = _rows_before(jnp.where(first_tile, 0.0, halo), g)
    gc = cb + cw[0:1, :] * gm2 + cw[1:2, :] * gm1 + cw[2:3, :] * g
    ge, tg = _gelu_t(gc)
    mb = (ge * val).astype(BF16)
    dm = _mm_nt(df2b, wd_ref[...])
    dwd_ref[...] += _mm_tn(mb, df2b)
    dval = dm * ge
    dgc = dm * val * _dgelu(gc, tg)
    cacc[0] += _colsum8(dgc)
    cacc[1] += _colsum8(dgc * gm2)
    cacc[2] += _colsum8(dgc * gm1)
    cacc[3] += _colsum8(dgc * g)
    dgp1, dgp2 = _rows_after(dgc, dgc_next[...])
    dg = cw[2:3, :] * dgc + cw[1:2, :] * dgp1 + cw[0:1, :] * dgp2
    dgc_next[...] = dgc[0:GHALO, :]
    dgb = dg.astype(BF16)
    dvb = dval.astype(BF16)
    dwg_ref[...] += _mm_tn(dgb, h2b)
    dwv_ref[...] += _mm_tn(dvb, h2b)
    return _mm(dgb, wg_ref[...]) + _mm(dvb, wv_ref[...])


def _bb_specs(l, hf, nt, ts):
    wide = pl.BlockSpec((ts, FH), lambda i: (nt - 1 - i, hf))
    halo = pl.BlockSpec((2 * GHALO, FH), lambda i: (jnp.maximum((nt - 1 - i) * (ts // (2 * GHALO)) - 1, 0), hf))
    ins = [wide, wide, halo, _resident((FH, D), (hf,)), _resident((FH, D), (2 + hf,)), _resident((FH, D), (hf,)),
           pl.BlockSpec((None, 3, FH), lambda i: (l, 0, hf)), pl.BlockSpec((None, 1, FH), lambda i: (l, 0, hf))]
    acc_shapes = [jax.ShapeDtypeStruct((FH, D), F32)] * 3
    return ins, acc_shapes


def _bb0_call(l, last, h2, r2, dx2, f2, gs, vs, modv, ln2, wup4, wd2, conv_w, conv_b, pay=None):
    s = r2.shape[0]
    nt = s // TSB
    ts = TSB

    def body(*refs):
        it = iter(refs)
        h2_ref, r2_ref, dx2_ref, f2_ref, gs_ref, vs_ref, gh_ref = (next(it) for _ in range(7))
        wg_ref, wv_ref, wd_ref, cw_ref, cb_ref, modv_ref, ln2_ref = (next(it) for _ in range(7))
        dh2_ref, df2_ref, dr2_ref, dwg_ref, dwv_ref, dwd_ref, vb_ref, cacc_ref = (next(it) for _ in range(8))
        loss_ref = next(it) if last else None
        vacc, cacc, dgc_next = next(it), next(it), next(it)
        lacc = next(it) if last else None
        i = pl.program_id(0)

        @pl.when(i == 0)
        def _():
            dwg_ref[...] = jnp.zeros_like(dwg_ref)
            dwv_ref[...] = jnp.zeros_like(dwv_ref)
            dwd_ref[...] = jnp.zeros_like(dwd_ref)
            vacc[...] = jnp.zeros_like(vacc)
            cacc[...] = jnp.zeros_like(cacc)
            dgc_next[...] = jnp.zeros_like(dgc_next)
            if last:
                lacc[...] = jnp.zeros_like(lacc)

        xhat2, rstd2 = _ln_stats(r2_ref[...])
        if last:
            diff = xhat2 * ln2_ref[0:1, :] + ln2_ref[1:2, :] - dx2_ref[...]
            dy = diff * (1.0 / D)
            lacc[...] += _colsum8(diff * diff)
        else:
            dy = dx2_ref[...]
        dr2 = _ln_bwd(dy, ln2_ref[0:1, :], xhat2, rstd2)
        vacc[0] += _colsum8(dy * xhat2)
        vacc[1] += _colsum8(dy)
        vacc[2] += _colsum8(dr2 * f2_ref[...].astype(F32))
        df2b = (dr2 * modv_ref[5:6, :]).astype(BF16)
        dr2_ref[...] = dr2
        df2_ref[...] = df2b
        dh2_ref[...] = _ff_backward(h2_ref[...], df2b, gs_ref, vs_ref, gh_ref, i == nt - 1, dgc_next, cw_ref[...],
                                    cb_ref[...], wg_ref, wv_ref, wd_ref, dwg_ref, dwv_ref, dwd_ref, cacc)

        @pl.when(i == nt - 1)
        def _():
            for n in range(3):
                vb_ref[n:n + 1, :] = jnp.sum(vacc[n], axis=0, keepdims=True)
            for n in range(4):
                cacc_ref[n:n + 1, :] = jnp.sum(cacc[n], axis=0, keepdims=True)
            if last:
                loss_ref[...] = lacc[...]

    tile = pl.BlockSpec((ts, D), lambda i: (nt - 1 - i, 0))
    ff_ins, acc_shapes = _bb_specs(l, 0, nt, ts)
    out_specs = [tile, tile, tile] + [_VMEM_WHOLE] * 5
    out_shape = [jax.ShapeDtypeStruct((s, D), F32), jax.ShapeDtypeStruct((s, D), BF16),
                 jax.ShapeDtypeStruct((s, D), F32)] + acc_shapes + [jax.ShapeDtypeStruct((3, D), F32),
                                                                    jax.ShapeDtypeStruct((4, FH), F32)]
    scratch = [pltpu.VMEM((3, 8, D), F32), pltpu.VMEM((4, 8, FH), F32), pltpu.VMEM((GHALO, FH), F32)]
    if last:
        out_specs.append(_VMEM_WHOLE)
        out_shape.append(jax.ShapeDtypeStruct((8, D), F32))
        scratch.append(pltpu.VMEM((8, D), F32))
    return _pcall(body, f"bb{l}_0", nt, [tile, tile, tile, tile] + ff_ins + [_layer_vec(8, D, l), _layer_vec(2, D, l)],
                  out_specs, out_shape, scratch,
                  [h2, r2, dx2, f2, gs, vs, gs, wup4, wup4, wd2, conv_w, conv_b, modv, ln2], pay=pay)


def _bb1_call(l, alpha, h2, r1, df2, dr2, dh2_in, gs, vs, modv, ln1, wup4, wd2, conv_w, conv_b, pay=None):
    s = r1.shape[0]
    nt = s // TSB
    ts = TSB

    def body(h2_ref, r1_ref, df2_ref, dr2_ref, dh2_ref, gs_ref, vs_ref, gh_ref, wg_ref, wv_ref, wd_ref, cw_ref,
             cb_ref, modv_ref, ln1_ref, dx1_ref, dwg_ref, dwv_ref, dwd_ref, vb_ref, cacc_ref, vacc, cacc, dgc_next):
        i = pl.program_id(0)

        @pl.when(i == 0)
        def _():
            dwg_ref[...] = jnp.zeros_like(dwg_ref)
            dwv_ref[...] = jnp.zeros_like(dwv_ref)
            dwd_ref[...] = jnp.zeros_like(dwd_ref)
            vacc[...] = jnp.zeros_like(vacc)
            cacc[...] = jnp.zeros_like(cacc)
            dgc_next[...] = jnp.zeros_like(dgc_next)

        dh2 = dh2_ref[...] + _ff_backward(h2_ref[...], df2_ref[...], gs_ref, vs_ref, gh_ref, i == nt - 1, dgc_next,
                                          cw_ref[...], cb_ref[...], wg_ref, wv_ref, wd_ref, dwg_ref, dwv_ref,
                                          dwd_ref, cacc)
        xhat1, _ = _ln_stats(r1_ref[...])
        x1 = xhat1 * ln1_ref[0:1, :] + ln1_ref[1:2, :]
        dx1_ref[...] = dh2 * (1.0 + modv_ref[4:5, :]) + alpha * dr2_ref[...]
        vacc[0] += _colsum8(dh2)
        vacc[1] += _colsum8(dh2 * x1)

        @pl.when(i == nt - 1)
        def _():
            for n in range(2):
                vb_ref[n:n + 1, :] = jnp.sum(vacc[n], axis=0, keepdims=True)
            for n in range(4):
                cacc_ref[n:n + 1, :] = jnp.sum(cacc[n], axis=0, keepdims=True)

    tile = pl.BlockSpec((ts, D), lambda i: (nt - 1 - i, 0))
    ff_ins, acc_shapes = _bb_specs(l, 1, nt, ts)
    out_shape = [jax.ShapeDtypeStruct((s, D), F32)] + acc_shapes + [jax.ShapeDtypeStruct((2, D), F32),
                                                                   jax.ShapeDtypeStruct((4, FH), F32)]
    scratch = [pltpu.VMEM((2, 8, D), F32), pltpu.VMEM((4, 8, FH), F32), pltpu.VMEM((GHALO, FH), F32)]
    return _pcall(body, f"bb{l}_1", nt, [tile] * 5 + ff_ins + [_layer_vec(8, D, l), _layer_vec(2, D, l)],
                  [tile] + [_VMEM_WHOLE] * 5, out_shape, scratch,
                  [h2, r1, df2, dr2, dh2_in, gs, vs, gs, wup4, wup4, wd2, conv_w, conv_b, modv, ln1], pay=pay)


def _silu(c):
    return c * (1.0 / (1.0 + jnp.exp(-c)))


def _ada_fwd_call(c_all, ada_w, ada_b_my):
    nl, _, wcols = ada_w.shape

    def body(c_ref, w_ref, b_ref, o_ref):
        ca = _silu(c_ref[...])
        o_ref[...] = jnp.dot(ca, w_ref[...], preferred_element_type=F32,
                             precision=lax.Precision.HIGHEST) + b_ref[...]

    return pl.pallas_call(
        body, name="ada_fwd", grid=(nl,),
        in_specs=[_full((NDEV, D)), pl.BlockSpec((None, D, wcols), lambda l: (l, 0, 0)),
                  pl.BlockSpec((None, 1, wcols), lambda l: (l, 0, 0))],
        out_specs=pl.BlockSpec((None, NDEV, wcols), lambda l: (l, 0, 0)),
        out_shape=jax.ShapeDtypeStruct((nl, NDEV, wcols), F32),
        compiler_params=_ARB,
    )(c_all, ada_w, ada_b_my)


def _adam_update(w, g, m, v):
    m2 = ADAM_B1 * m + (1.0 - ADAM_B1) * g
    v2 = ADAM_B2 * v + (1.0 - ADAM_B2) * (g * g)
    m_hat = m2 / (1.0 - ADAM_B1 ** ADAM_STEP)
    v_hat = v2 / (1.0 - ADAM_B2 ** ADAM_STEP)
    delta = -ADAM_LR * (m_hat / (jnp.sqrt(v_hat) + ADAM_EPS) + ADAM_WD * w)
    return delta, m2, v2


def _ada_bwd_call(c_t, dmod_my, w, m, v):
    nl, _, wcols = w.shape
    rb = 256

    def body(ct_ref, dm_ref, w_ref, m_ref, v_ref, g_ref, d_ref, m2_ref, v2_ref):
        ca_t = _silu(ct_ref[...])
        dm = dm_ref[...]
        g = ca_t[:, 0:1] * dm[0:1, :]
        for b in range(1, NDEV):
            g = g + ca_t[:, b:b + 1] * dm[b:b + 1, :]
        delta, m2, v2 = _adam_update(w_ref[...], g, m_ref[...], v_ref[...])
        g_ref[...] = g
        d_ref[...] = delta
        m2_ref[...] = m2
        v2_ref[...] = v2

    blk = pl.BlockSpec((None, rb, wcols), lambda l, i: (l, i, 0))
    shp = jax.ShapeDtypeStruct(w.shape, F32)
    return pl.pallas_call(
        body, name="ada_bwd", grid=(nl, D // rb),
        in_specs=[pl.BlockSpec((rb, NDEV), lambda l, i: (i, 0)),
                  pl.BlockSpec((None, NDEV, wcols), lambda l, i: (l, 0, 0)), blk, blk, blk],
        out_specs=[blk, blk, blk, blk], out_shape=[shp, shp, shp, shp],
        compiler_params=pltpu.CompilerParams(dimension_semantics=("arbitrary", "arbitrary")),
    )(c_t, dmod_my, w, m, v)


def _row_block(r, row_bytes):
    budget = 6 * 1024 * 1024
    best = None
    for rb in range(16, min(r, 512) + 1, 16):
        if r % rb == 0 and rb * row_bytes <= budget:
            best = rb
    return best if best is not None else r


def _sum_parts_call(parts, name):
    nl, npart, r, c = parts.shape
    rb = _row_block(r, (npart + 1) * c * 4)

    def body(p_ref, g_ref):
        g = p_ref[0].astype(F32)
        for k in range(1, npart):
            g = g + p_ref[k].astype(F32)
        g_ref[...] = g

    return pl.pallas_call(
        body, name=name, grid=(nl, r // rb),
        in_specs=[pl.BlockSpec((None, npart, rb, c), lambda l, i: (l, 0, i, 0))],
        out_specs=pl.BlockSpec((None, rb, c), lambda l, i: (l, i, 0)),
        out_shape=jax.ShapeDtypeStruct((nl, r, c), F32),
        compiler_params=pltpu.CompilerParams(dimension_semantics=("arbitrary", "arbitrary")),
    )(parts)


def _adamw_call(parts, w, m, v, name):
    nl, npart, r, c = parts.shape
    rb = _row_block(r, (npart + 7) * c * 4)

    def body(p_ref, w_ref, m_ref, v_ref, g_ref, d_ref, m2_ref, v2_ref):
        g = p_ref[0].astype(F32)
        for k in range(1, npart):
            g = g + p_ref[k].astype(F32)
        delta, m2, v2 = _adam_update(w_ref[...], g, m_ref[...], v_ref[...])
        g_ref[...] = g
        d_ref[...] = delta
        m2_ref[...] = m2
        v2_ref[...] = v2

    blk = pl.BlockSpec((None, rb, c), lambda l, i: (l, i, 0))
    shp = jax.ShapeDtypeStruct((nl, r, c), F32)
    return pl.pallas_call(
        body, name=name, grid=(nl, r // rb),
        in_specs=[pl.BlockSpec((None, npart, rb, c), lambda l, i: (l, 0, i, 0)), blk, blk, blk],
        out_specs=[blk, blk, blk, blk], out_shape=[shp, shp, shp, shp],
        compiler_params=pltpu.CompilerParams(dimension_semantics=("arbitrary", "arbitrary")),
    )(parts, w, m, v)


_SMALL_ORDER = ("ada_b", "pool_w", "pool_scale", "sgu_ln_g", "sgu_ln_b", "sgu_w", "sgu_b", "ln1_g", "ln1_b",
                "conv_w", "conv_b", "ln2_g", "ln2_b")


def kernel(x, c, ada_w, ada_b, w_in, pool_w, pool_scale, sgu_ln_g, sgu_ln_b, sgu_w, sgu_b, w_out, ln1_g, ln1_b, w_up, conv_w, conv_b, w_down, ln2_g, ln2_b, loss_target, m_ada_w, m_ada_b, m_w_in, m_pool_w, m_pool_scale, m_sgu_ln_g, m_sgu_ln_b, m_sgu_w, m_sgu_b, m_w_out, m_ln1_g, m_ln1_b, m_w_up, m_conv_w, m_conv_b, m_w_down, m_ln2_g, m_ln2_b, v_ada_w, v_ada_b, v_w_in, v_pool_w, v_pool_scale, v_sgu_ln_g, v_sgu_ln_b, v_sgu_w, v_sgu_b, v_w_out, v_ln1_g, v_ln1_b, v_w_up, v_conv_w, v_conv_b, v_w_down, v_ln2_g, v_ln2_b):
    nl = ada_w.shape[0]
    alpha = (2.0 * nl) ** 0.25
    me = _me()
    x2 = x[0]
    tgt = loss_target[0]
    acols = ada_w.shape[2]
    icols = w_in.shape[2]
    ucols = w_up.shape[2]
    orows = w_out.shape[1]
    drows = w_down.shape[1]
    ccols = conv_w.shape[2]

    winT_sh = jnp.swapaxes(w_in, 1, 2).astype(BF16)
    wupT_sh = jnp.swapaxes(w_up, 1, 2).astype(BF16)
    wout_sh = w_out.astype(BF16)
    wd_sh = w_down.astype(BF16)

    pay = _Payload()
    pay.gather(jnp.broadcast_to(c, (8, D)))
    pay.gather(winT_sh, 0)
    pay.gather(wout_sh, 0)
    pay.gather(conv_w)
    (c_g, winT_g, wout_g, cw_g), _ = _comm_call(pay, "gather_first")
    c_all = c_g[:, 0, :]
    winT = winT_g.reshape(DIN, D)
    wout = wout_g.reshape(D, D)
    cw_full = jnp.transpose(cw_g, (1, 2, 0, 3)).reshape(nl, 3, FF)
    cb_full = conv_b[:, None, :]

    ada_b_my = lax.dynamic_slice(ada_b, (0, me * acols), (nl, acols))[:, None, :]
    mod_blk = _ada_fwd_call(c_all, ada_w, ada_b_my)
    pay = _Payload()
    pay.gather(mod_blk)
    (mod_g,), _ = _comm_call(pay, "gather_mod")
    mod_me = lax.dynamic_index_in_dim(mod_g, me, axis=2, keepdims=False)
    modv = jnp.swapaxes(mod_me, 0, 1).reshape(nl, 6, D)
    modv = jnp.concatenate([modv, jnp.zeros((nl, 2, D), F32)], axis=1)

    ln1 = jnp.stack([ln1_g, ln1_b], axis=1)
    ln2 = jnp.stack([ln2_g, ln2_b], axis=1)
    sln = jnp.stack([sgu_ln_g, sgu_ln_b], axis=1)
    sbf = jnp.broadcast_to(sgu_b[..., None], sgu_b.shape + (GW,))
    small = (pool_w, pool_scale[:, None, :], sln, sgu_w, sbf)

    r1s, r2s, f1s, f2s, ahs, ghs, wins, wouts, wups, wds = ([None] * nl for _ in range(10))
    wins[0], wouts[0] = winT, wout
    xin = x2
    for l in range(nl):
        pay = None
        if l == 0:
            pay = _Payload()
            pay.gather(wupT_sh, 0)
            pay.gather(wd_sh, 0)
        (r1, f1, *ah), new, _ = _fa_call(l, l == 0, alpha, xin, modv, ln2, wins[l], wouts[l], small, pay)
        if l == 0:
            wups[0], wds[0] = new[0].reshape(4, FH, D), new[1].reshape(2, FH, D)
        pay = None
        if l + 1 < nl:
            pay = _Payload()
            pay.gather(winT_sh, l + 1)
            pay.gather(wout_sh, l + 1)
            pay.gather(wupT_sh, l + 1)
            pay.gather(wd_sh, l + 1)
        (r2, f2, h2, gs, vs), new, _ = _fb_call(l, alpha, r1, modv, ln1, wups[l], wds[l], cw_full, cb_full, pay)
        if l + 1 < nl:
            wins[l + 1], wouts[l + 1] = new[0].reshape(DIN, D), new[1].reshape(D, D)
            wups[l + 1], wds[l + 1] = new[2].reshape(4, FH, D), new[3].reshape(2, FH, D)
        r1s[l], r2s[l], f1s[l], f2s[l], ahs[l], ghs[l] = r1, r2, f1, f2, ah, (h2, gs, vs)
        xin = r2

    nsmall = 6 * D + 2 * NG * GW * GW + 3 * DP + NG * GW + 4 * D + 4 * FF
    srows = nsmall // 1024
    buf_in = lax.empty((nl, NDEV, icols, D), BF16)
    buf_out = lax.empty((nl, NDEV, orows, D), BF16)
    buf_up = lax.empty((nl, NDEV, ucols, D), BF16)
    buf_down = lax.empty((nl, NDEV, drows, D), BF16)
    buf_small = lax.empty((nl, NDEV, srows, 1024), F32)

    def a_side_payload(l, dwin, dwout, flat):
        p = _Payload()
        p.exchange_into([(dwin.reshape(NDEV, icols, D), 0)], buf_in, (l,))
        p.exchange_into([(dwout.reshape(NDEV, orows, D), 0)], buf_out, (l,))
        p.gather_into(flat, buf_small, (l,))
        return p

    dx = tgt
    loss_acc = None
    pending = None
    for l in reversed(range(nl)):
        last = l == nl - 1
        pay = None if pending is None else a_side_payload(*pending)
        h2, gs, vs = ghs[l]
        res0, _, al = _bb0_call(l, last, h2, r2s[l], dx, f2s[l], gs, vs, modv, ln2, wups[l], wds[l], cw_full,
                                cb_full, pay)
        if pending is not None:
            buf_in, buf_out, buf_small = al
        dh2p, df2, dr2, dwg0, dwv0, dwd0, vb0, cacc0 = res0[:8]
        if last:
            loss_acc = res0[8]
        (dx1, dwg1, dwv1, dwd1, vb1, cacc1), _, _ = _bb1_call(
            l, alpha, h2, r1s[l], df2, dr2, dh2p, gs, vs, modv, ln1, wups[l], wds[l], cw_full, cb_full)
        dwg0, dwv0, dwd0, dwg1, dwv1, dwd1 = _cast_call([dwg0, dwv0, dwd0, dwg1, dwv1, dwd1], f"cast{l}")
        pay = _Payload()
        pay.exchange_into([(dwg0.reshape(2, ucols, D), 0), (dwg1.reshape(2, ucols, D), 2),
                           (dwv0.reshape(2, ucols, D), 4), (dwv1.reshape(2, ucols, D), 6)], buf_up, (l,))
        pay.exchange_into([(dwd0.reshape(4, drows, D), 0), (dwd1.reshape(4, drows, D), 4)], buf_down, (l,))
        xin = x2 if l == 0 else r2s[l - 1]
        (dx, dwin, dwout, dpw, dsw, dsb, va, va5), _, (buf_up, buf_down) = _ba_call(
            l, l == 0, alpha, xin, r1s[l], dx1, f1s[l], *ahs[l], modv, ln2, ln1, wins[l], wouts[l], small, pay)
        cacc = jnp.concatenate([cacc0, cacc1], axis=1)
        piece = dict(ada_b=jnp.stack([va[3], va[4], va[2], vb1[0], vb1[1], vb0[2]]), pool_w=dpw,
                     pool_scale=va5[0], sgu_ln_g=va5[1], sgu_ln_b=va5[2], sgu_w=dsw, sgu_b=dsb[:, :, 0],
                     ln1_g=va[0], ln1_b=va[1], conv_w=cacc[1:4], conv_b=cacc[0], ln2_g=vb0[0], ln2_b=vb0[1])
        flat = jnp.concatenate([piece[n].reshape(-1) for n in _SMALL_ORDER]).reshape(srows, 1024)
        pending = (l, dwin, dwout, flat)
    _, (buf_in, buf_out, buf_small) = _comm_call(a_side_payload(*pending), "exchange_last")
    grad_x = dx[None]
    loss = lax.psum((0.5 / D) * jnp.sum(loss_acc), ("x", "y", "c"))

    weights = dict(ada_b=ada_b, pool_w=pool_w, pool_scale=pool_scale, sgu_ln_g=sgu_ln_g, sgu_ln_b=sgu_ln_b,
                   sgu_w=sgu_w, sgu_b=sgu_b, ln1_g=ln1_g, ln1_b=ln1_b, conv_w=conv_w, conv_b=conv_b, ln2_g=ln2_g,
                   ln2_b=ln2_b)
    m_of = dict(ada_b=m_ada_b, pool_w=m_pool_w, pool_scale=m_pool_scale, sgu_ln_g=m_sgu_ln_g, sgu_ln_b=m_sgu_ln_b,
                sgu_w=m_sgu_w, sgu_b=m_sgu_b, ln1_g=m_ln1_g, ln1_b=m_ln1_b, conv_w=m_conv_w, conv_b=m_conv_b,
                ln2_g=m_ln2_g, ln2_b=m_ln2_b)
    v_of = dict(ada_b=v_ada_b, pool_w=v_pool_w, pool_scale=v_pool_scale, sgu_ln_g=v_sgu_ln_g, sgu_ln_b=v_sgu_ln_b,
                sgu_w=v_sgu_w, sgu_b=v_sgu_b, ln1_g=v_ln1_g, ln1_b=v_ln1_b, conv_w=v_conv_w, conv_b=v_conv_b,
                ln2_g=v_ln2_g, ln2_b=v_ln2_b)
    small_all = jnp.swapaxes(buf_small.reshape(nl, NDEV, nsmall), 0, 1)
    res = {}
    off = 0
    dmod_all = None
    for name in _SMALL_ORDER:
        w = weights[name]
        per_layer = (FF * 3 if name == "conv_w" else w[0].size)
        parts = small_all[:, :, off:off + per_layer]
        off += per_layer
        if name == "ada_b":
            dmod_all = parts
        if name == "conv_w":
            parts = lax.dynamic_slice_in_dim(parts.reshape(NDEV, nl, 3, FF), me * ccols, ccols, axis=3)
        cdim = w.shape[-1]
        w2 = w.reshape(1, -1, cdim)
        outs = _adamw_call(parts.reshape(1, NDEV, -1, cdim), w2, m_of[name].reshape(w2.shape),
                           v_of[name].reshape(w2.shape), f"adamw_{name}")
        res[name] = tuple(o.reshape(w.shape) for o in outs)

    dmod_my = jnp.swapaxes(lax.dynamic_slice_in_dim(dmod_all, me * acols, acols, axis=2), 0, 1)
    res["ada_w"] = _ada_bwd_call(jnp.swapaxes(c_all, 0, 1), dmod_my, ada_w, m_ada_w, v_ada_w)

    g_win = jnp.swapaxes(_sum_parts_call(buf_in, "sum_w_in"), 1, 2)
    g_wup = jnp.swapaxes(_sum_parts_call(buf_up, "sum_w_up"), 1, 2)
    res["w_in"] = _adamw_call(g_win[:, None], w_in, m_w_in, v_w_in, "adamw_w_in")
    res["w_up"] = _adamw_call(g_wup[:, None], w_up, m_w_up, v_w_up, "adamw_w_up")
    res["w_out"] = _adamw_call(buf_out, w_out, m_w_out, v_w_out, "adamw_w_out")
    res["w_down"] = _adamw_call(buf_down, w_down, m_w_down, v_w_down, "adamw_w_down")

    order = ["ada_w", "ada_b", "w_in", "pool_w", "pool_scale", "sgu_ln_g", "sgu_ln_b", "sgu_w", "sgu_b", "w_out",
             "ln1_g", "ln1_b", "w_up", "conv_w", "conv_b", "w_down", "ln2_g", "ln2_b"]
    out = [loss, grad_x]
    for k in range(4):
        out += [res[n][k] for n in order]
    return tuple(out)
```

```python
import jax
import jax.numpy as jnp
from jax import lax
from jax.experimental import pallas as pl
from jax.experimental.pallas import tpu as pltpu

F32 = jnp.float32
BF16 = jnp.bfloat16

NDEV = 8
D = 1024
DP = 512
DS = 512
DIN = DP + 2 * DS
FF = 2816
FH = FF // 2
NG = 4
GW = 128
WINDOWS = (2, 4, 8, 16)
AHALO = 16
GHALO = 8
LN_EPS = 1e-5
ADAM_LR, ADAM_B1, ADAM_B2, ADAM_EPS, ADAM_WD, ADAM_STEP = 0.001, 0.9, 0.999, 1e-08, 0.01, 10
TSF = 512
TSB = 256
_K0 = 0.7978845608028654
_K1 = 0.044715
MESH_ID = pl.DeviceIdType.MESH


def _mm(a, b):
    return jnp.dot(a, b, preferred_element_type=F32)


def _mm_nt(a, b):
    return lax.dot_general(a, b, (((1,), (1,)), ((), ())), preferred_element_type=F32)


def _mm_tn(a, b):
    return lax.dot_general(a, b, (((0,), (0,)), ((), ())), preferred_element_type=F32)


def _rowmean(x):
    return jnp.mean(x, axis=-1, keepdims=True)


def _ln_stats(x):
    mu = _rowmean(x)
    xc = x - mu
    rstd = lax.rsqrt(_rowmean(xc * xc) + LN_EPS)
    return xc * rstd, rstd


def _ln_bwd(dy, gamma, xhat, rstd):
    dxh = dy * gamma
    return rstd * (dxh - _rowmean(dxh) - xhat * _rowmean(dxh * xhat))


def _gelu_t(x):
    t = jnp.tanh(_K0 * (x + _K1 * (x * x * x)))
    return 0.5 * x * (1.0 + t), t


def _dgelu(x, t):
    return 0.5 * (1.0 + t) + 0.5 * x * (1.0 - t * t) * (_K0 * (1.0 + 3.0 * _K1 * (x * x)))


def _colsum8(x):
    t, n = x.shape
    return jnp.sum(x.reshape(t // 8, 8, n), axis=0)


def _tril_mask():
    r = lax.broadcasted_iota(jnp.int32, (GW, GW), 0)
    c = lax.broadcasted_iota(jnp.int32, (GW, GW), 1)
    return c <= r


def _full(shape):
    n = len(shape)
    return pl.BlockSpec(shape, lambda *_: (0,) * n)


def _resident(tail, lead=()):
    n = len(tail)
    return pl.BlockSpec((None,) * len(lead) + tuple(tail), lambda *_: tuple(lead) + (0,) * n,
                        pipeline_mode=pl.Buffered(1))


def _layer_vec(rows, width, l):
    return pl.BlockSpec((None, rows, width), lambda *_: (l, 0, 0))


_VMEM_WHOLE = pl.BlockSpec(memory_space=pltpu.VMEM)
_HBM = pl.BlockSpec(memory_space=pl.ANY)
_ARB = pltpu.CompilerParams(dimension_semantics=("arbitrary",))


def _me():
    return 4 * lax.axis_index("x") + 2 * lax.axis_index("y") + lax.axis_index("c")


def _coords(p):
    return (p >> 2, (p >> 1) & 1, p & 1)


class _Payload:
    def __init__(self):
        self.srcs, self.new, self.alias, self.transfers = [], [], [], []

    def _src(self, arr):
        self.srcs.append(arr)
        return len(self.srcs) - 1

    def _alias(self, buf):
        self.alias.append(buf)
        return len(self.alias) - 1

    def gather(self, arr, chunk=None):
        pos = self._src(arr)
        blk = arr.shape if chunk is None else arr.shape[1:]
        self.new.append(jax.ShapeDtypeStruct((NDEV,) + tuple(blk), arr.dtype))
        self.transfers.append(([(pos, chunk)] * NDEV, ("new", len(self.new) - 1), ()))
        return len(self.new) - 1

    def gather_into(self, arr, buf, lead):
        pos = self._src(arr)
        self.transfers.append(([(pos, None)] * NDEV, ("alias", self._alias(buf)), tuple(lead)))

    def exchange_into(self, parts, buf, lead):
        route = {}
        for arr, first in parts:
            pos = self._src(arr)
            for q in range(arr.shape[0]):
                route[first + q] = (pos, q)
        self.transfers.append(([route[p] for p in range(NDEV)], ("alias", self._alias(buf)), tuple(lead)))

    def _ends(self, t, io, dst_dev, src_dev):
        srcs, new_out, alias_out = io
        route, (kind, k), lead = self.transfers[t]
        pos, q = route[dst_dev]
        src = srcs[pos] if q is None else srcs[pos].at[q]
        buf = new_out[k] if kind == "new" else alias_out[k]
        return src, buf.at[lead + (src_dev,)]

    def _remote(self, t, io, sems, src_dev, dst_dev):
        src, dst = self._ends(t, io, dst_dev, src_dev)
        return pltpu.make_async_remote_copy(
            src_ref=src, dst_ref=dst, send_sem=sems[0].at[t, dst_dev], recv_sem=sems[1].at[t, src_dev],
            device_id=_coords(dst_dev), device_id_type=MESH_ID)

    def _local(self, t, io, sems, p):
        src, dst = self._ends(t, io, p, p)
        return pltpu.make_async_copy(src, dst, sems[2].at[t])

    def start(self, io, sems):
        me = _me()
        for p in range(NDEV):
            @pl.when(me == p)
            def _():
                for t in range(len(self.transfers)):
                    self._local(t, io, sems, p).start()

            @pl.when(me != p)
            def _():
                for t in range(len(self.transfers)):
                    self._remote(t, io, sems, me, p).start()

    def wait(self, io, sems):
        me = _me()
        for p in range(NDEV):
            @pl.when(me == p)
            def _():
                for t in range(len(self.transfers)):
                    self._local(t, io, sems, p).wait()

            @pl.when(me != p)
            def _():
                for t in range(len(self.transfers)):
                    self._remote(t, io, sems, p, p).wait_recv()
                    self._remote(t, io, sems, me, p).wait_send()


def _pcall(body, name, nsteps, in_specs, out_specs, out_shape, scratch, args, pay=None):
    n_in, n_out, n_scr = len(args), len(out_shape), len(scratch)
    if pay is None:
        res = pl.pallas_call(body, name=name, grid=(nsteps,), in_specs=list(in_specs), out_specs=list(out_specs),
                             out_shape=list(out_shape), scratch_shapes=list(scratch), compiler_params=_ARB)(*args)
        return list(res), [], []
    ns, nn, na, nt = len(pay.srcs), len(pay.new), len(pay.alias), len(pay.transfers)

    def full(*refs):
        cin = refs[:n_in]
        srcs = refs[n_in:n_in + ns]
        o0 = n_in + ns + na
        cout = refs[o0:o0 + n_out]
        new_out = refs[o0 + n_out:o0 + n_out + nn]
        alias_out = refs[o0 + n_out + nn:o0 + n_out + nn + na]
        s0 = o0 + n_out + nn + na
        cscr = refs[s0:s0 + n_scr]
        sems = refs[s0 + n_scr:]
        io = (srcs, new_out, alias_out)
        i = pl.program_id(0)

        @pl.when(i == 0)
        def _():
            pay.start(io, sems)

        body(*cin, *cout, *cscr)

        @pl.when(i == nsteps - 1)
        def _():
            pay.wait(io, sems)

    res = pl.pallas_call(
        full, name=name, grid=(nsteps,),
        in_specs=list(in_specs) + [_HBM] * (ns + na), out_specs=list(out_specs) + [_HBM] * (nn + na),
        out_shape=list(out_shape) + pay.new + [jax.ShapeDtypeStruct(b.shape, b.dtype) for b in pay.alias],
        input_output_aliases={n_in + ns + k: n_out + nn + k for k in range(na)},
        scratch_shapes=list(scratch) + [pltpu.SemaphoreType.DMA((nt, NDEV)), pltpu.SemaphoreType.DMA((nt, NDEV)),
                                        pltpu.SemaphoreType.DMA((nt,))],
        compiler_params=_ARB,
    )(*args, *pay.srcs, *pay.alias)
    return list(res[:n_out]), list(res[n_out:n_out + nn]), list(res[n_out + nn:])


def _comm_call(pay, name):
    def body():
        pass

    _, new, alias = _pcall(body, name, 1, [], [], [], [], [], pay)
    return new, alias


def _window_sums(x, halo, before):
    ts = x.shape[0]
    ext = jnp.concatenate([halo, x] if before else [x, halo], axis=0)
    n = ts + AHALO
    shift = (lambda k: k) if before else (lambda k: n - k)
    keep = slice(AHALO, n) if before else slice(0, ts)
    out = []
    s = ext
    for level in range(NG):
        s = s + pltpu.roll(s, shift(1 << level), 0)
        out.append(s[keep, 0:GW])
        if level + 1 < NG:
            s = s[:, GW:]
    return out


def _a_forward(a, u, v, halo, pw_ref, ps_ref, sln_ref, sw_ref, sbf_ref, zbuf, tile, ts):
    tglob = tile * ts + lax.broadcasted_iota(jnp.int32, (ts, 1), 0)
    sums = _window_sums(a, halo, True)
    pooled_b, mixed, inv_cnt, pwb = [], [], [], []
    for g, w in enumerate(WINDOWS):
        a_g = a[:, g * GW:(g + 1) * GW]
        s = sums[g]
        inv = 1.0 / jnp.minimum(tglob + 1, w).astype(F32)
        pg = (s * inv - a_g).astype(BF16)
        wg = pw_ref[g].astype(BF16)
        pooled_b.append(pg)
        inv_cnt.append(inv)
        pwb.append(wg)
        mixed.append(_mm(pg, wg))
    mixed = jnp.concatenate(mixed, axis=1)
    ya = mixed * ps_ref[...]
    ug, tu = _gelu_t(u)
    vg, tv = _gelu_t(v)
    vhat, rstdv = _ln_stats(vg)
    vnb = (vhat * sln_ref[0:1, :] + sln_ref[1:2, :]).astype(BF16)
    tri = _tril_mask()
    wt = [jnp.where(tri, sw_ref[h], 0.0).astype(BF16) for h in range(NG)]
    for c in range(ts // GW):
        rs = slice(c * GW, (c + 1) * GW)
        for h in range(NG):
            cs = slice(h * GW, (h + 1) * GW)
            zbuf[rs, cs] = _mm(wt[h], vnb[rs, cs]) + sbf_ref[h]
    z = zbuf[...]
    yb = ug * z
    return dict(u=u, v=v, tu=tu, tv=tv, ug=ug, z=z, vhat=vhat, rstdv=rstdv, vnb=vnb,
                wt=wt, pooled_b=pooled_b, pwb=pwb, inv_cnt=inv_cnt, mixed=mixed, ya=ya, yb=yb)


def _small_specs(l):
    grp = pl.BlockSpec((None, NG, GW, GW), lambda i: (l, 0, 0, 0))
    return [grp, _layer_vec(1, DP, l), _layer_vec(2, DS, l), grp, grp]


def _fa_call(l, first, alpha, xin, modv, lnp, winT, wout, small, pay=None):
    s = xin.shape[0]
    nt = s // TSF

    def body(xin_ref, modv_ref, lnp_ref, winT_ref, wout_ref, pw_ref, ps_ref, sln_ref, sw_ref, sbf_ref,
             r1_ref, f1_ref, h_ref, proj_ref, acarry, zbuf, mixbuf):
        i = pl.program_id(0)

        @pl.when(i == 0)
        def _():
            acarry[...] = jnp.zeros_like(acarry)

        x = xin_ref[...]
        if not first:
            xhat, _ = _ln_stats(x)
            x = xhat * lnp_ref[0:1, :] + lnp_ref[1:2, :]
        hb = (x * (1.0 + modv_ref[1:2, :]) + modv_ref[0:1, :]).astype(BF16)
        proj = _mm_nt(hb, winT_ref[...])
        a = proj[:, 0:DP]
        fw = _a_forward(a, proj[:, DP:DP + DS], proj[:, DP + DS:], acarry[...], pw_ref, ps_ref, sln_ref, sw_ref,
                        sbf_ref, zbuf, i, TSF)
        acarry[...] = a[TSF - AHALO:, :]
        mixbuf[:, 0:DP] = fw["ya"].astype(BF16)
        mixbuf[:, DP:] = fw["yb"].astype(BF16)
        f = _mm(mixbuf[...], wout_ref[...])
        r1_ref[...] = alpha * x + modv_ref[2:3, :] * f
        f1_ref[...] = f.astype(BF16)
        h_ref[...] = hb
        proj_ref[...] = proj.astype(BF16)

    tile = pl.BlockSpec((TSF, D), lambda i: (i, 0))
    return _pcall(
        body, f"fa{l}", nt,
        in_specs=[tile, _layer_vec(8, D, l), _layer_vec(2, D, max(l - 1, 0)), _resident((DIN, D)),
                  _resident((D, D))] + _small_specs(l),
        out_specs=[tile, tile, tile, pl.BlockSpec((TSF, DIN), lambda i: (i, 0))],
        out_shape=[jax.ShapeDtypeStruct((s, D), F32), jax.ShapeDtypeStruct((s, D), BF16),
                   jax.ShapeDtypeStruct((s, D), BF16), jax.ShapeDtypeStruct((s, DIN), BF16)],
        scratch=[pltpu.VMEM((AHALO, DP), F32), pltpu.VMEM((TSF, DS), F32), pltpu.VMEM((TSF, D), BF16)],
        args=[xin, modv, lnp, winT, wout, *small], pay=pay)


def _ba_call(l, first, alpha, xin, r1, dx1, f1, hsave, proj, modv, lnp, ln1, winT, wout, small, pay=None):
    s = xin.shape[0]
    nt = s // TSB
    ts = TSB

    def body(xin_ref, r1_ref, dx1_ref, f1_ref, h_ref, proj_ref, ah_ref, modv_ref, lnp_ref, ln1_ref, winT_ref,
             wout_ref, pw_ref, ps_ref, sln_ref, sw_ref, sbf_ref,
             dx_ref, dwin_ref, dwout_ref, dpw_ref, dsw_ref, dsb_ref, va_ref, va5_ref,
             qnext, zbuf, dvnbuf, mixbuf, dpbuf, dbacc, vacc, vacc5, dwin_acc, dwout_acc):
        i = pl.program_id(0)
        j = nt - 1 - i

        @pl.when(i == 0)
        def _():
            dwin_acc[...] = jnp.zeros_like(dwin_acc)
            dwout_acc[...] = jnp.zeros_like(dwout_acc)
            dpw_ref[...] = jnp.zeros_like(dpw_ref)
            dsw_ref[...] = jnp.zeros_like(dsw_ref)
            dbacc[...] = jnp.zeros_like(dbacc)
            vacc[...] = jnp.zeros_like(vacc)
            vacc5[...] = jnp.zeros_like(vacc5)
            qnext[...] = jnp.zeros_like(qnext)

        x = xin_ref[...]
        if not first:
            xhat, _ = _ln_stats(x)
            x = xhat * lnp_ref[0:1, :] + lnp_ref[1:2, :]
        hb = h_ref[...]
        sc1 = modv_ref[1:2, :]
        halo = jnp.where(j > 0, ah_ref[...].astype(F32), 0.0)
        fw = _a_forward(proj_ref[:, 0:DP].astype(F32), proj_ref[:, DP:DP + DS].astype(F32),
                        proj_ref[:, DP + DS:].astype(F32), halo, pw_ref, ps_ref, sln_ref, sw_ref, sbf_ref, zbuf, j, ts)
        mixbuf[:, 0:DP] = fw["ya"].astype(BF16)
        mixbuf[:, DP:] = fw["yb"].astype(BF16)

        xhat1, rstd1 = _ln_stats(r1_ref[...])
        dy = dx1_ref[...]
        vacc[0] += _colsum8(dy * xhat1)
        vacc[1] += _colsum8(dy)
        dr1 = _ln_bwd(dy, ln1_ref[0:1, :], xhat1, rstd1)
        vacc[2] += _colsum8(dr1 * f1_ref[...].astype(F32))
        dfb = (dr1 * modv_ref[2:3, :]).astype(BF16)
        dwout_acc[...] += _mm_tn(mixbuf[...], dfb)
        dmix = _mm_nt(dfb, wout_ref[...])
        dya = dmix[:, 0:DP]
        dyb = dmix[:, DP:]

        vacc5[0] += _colsum8(dya * fw["mixed"])
        dmixed = (dya * ps_ref[...]).astype(BF16)
        dpooled, q = [], []
        for g in range(NG):
            cs = slice(g * GW, (g + 1) * GW)
            dpw_ref[g] += _mm_tn(fw["pooled_b"][g], dmixed[:, cs])
            dpg = _mm_nt(dmixed[:, cs], fw["pwb"][g])
            dpooled.append(dpg)
            q.append(dpg * fw["inv_cnt"][g])
        q = jnp.concatenate(q, axis=1)
        sums = _window_sums(q, qnext[...], False)
        qnext[...] = q[0:AHALO, :]
        for g in range(NG):
            dpbuf[:, g * GW:(g + 1) * GW] = (sums[g] - dpooled[g]).astype(BF16)

        dug = dyb * fw["z"]
        dz = dyb * fw["ug"]
        dzb = dz.astype(BF16)
        for c in range(ts // GW):
            rs = slice(c * GW, (c + 1) * GW)
            for h in range(NG):
                cs = slice(h * GW, (h + 1) * GW)
                dvnbuf[rs, cs] = _mm_tn(fw["wt"][h], dzb[rs, cs])
                dsw_ref[h] += _mm_nt(dzb[rs, cs], fw["vnb"][rs, cs])
            dbacc[...] += dz[rs, :]
        dvn = dvnbuf[...]
        vacc5[1] += _colsum8(dvn * fw["vhat"])
        vacc5[2] += _colsum8(dvn)
        dvg = _ln_bwd(dvn, sln_ref[0:1, :], fw["vhat"], fw["rstdv"])
        dpbuf[:, DP:DP + DS] = (dug * _dgelu(fw["u"], fw["tu"])).astype(BF16)
        dpbuf[:, DP + DS:] = (dvg * _dgelu(fw["v"], fw["tv"])).astype(BF16)

        dpb = dpbuf[...]
        dwin_acc[...] += _mm_tn(dpb, hb)
        dh = _mm(dpb, winT_ref[...])
        dx_ref[...] = dh * (1.0 + sc1) + alpha * dr1
        vacc[3] += _colsum8(dh)
        vacc[4] += _colsum8(dh * x)

        @pl.when(i == nt - 1)
        def _():
            dwin_ref[...] = dwin_acc[...].astype(BF16)
            dwout_ref[...] = dwout_acc[...].astype(BF16)
            tri = _tril_mask()
            for h in range(NG):
                dsw_ref[h] = jnp.where(tri, dsw_ref[h], 0.0)
                sb = jnp.sum(dbacc[:, h * GW:(h + 1) * GW], axis=1, keepdims=True)
                dsb_ref[h] = jnp.broadcast_to(sb, (GW, GW))
            for n in range(5):
                va_ref[n:n + 1, :] = jnp.sum(vacc[n], axis=0, keepdims=True)
            for n in range(3):
                va5_ref[n:n + 1, :] = jnp.sum(vacc5[n], axis=0, keepdims=True)

    rev = lambda i: (nt - 1 - i, 0)
    tile = pl.BlockSpec((ts, D), rev)
    return _pcall(
        body, f"ba{l}", nt,
        in_specs=[tile, tile, tile, tile, tile, pl.BlockSpec((ts, DIN), rev),
                  pl.BlockSpec((AHALO, DP), lambda i: (jnp.maximum((nt - 1 - i) * (ts // AHALO) - 1, 0), 0)),
                  _layer_vec(8, D, l), _layer_vec(2, D, max(l - 1, 0)), _layer_vec(2, D, l),
                  _resident((DIN, D)), _resident((D, D))] + _small_specs(l),
        out_specs=[tile] + [_VMEM_WHOLE] * 7,
        out_shape=[jax.ShapeDtypeStruct((s, D), F32), jax.ShapeDtypeStruct((DIN, D), BF16),
                   jax.ShapeDtypeStruct((D, D), BF16), jax.ShapeDtypeStruct((NG, GW, GW), F32),
                   jax.ShapeDtypeStruct((NG, GW, GW), F32), jax.ShapeDtypeStruct((NG, GW, GW), F32),
                   jax.ShapeDtypeStruct((5, D), F32), jax.ShapeDtypeStruct((3, DP), F32)],
        scratch=[pltpu.VMEM((AHALO, DP), F32),
                 pltpu.VMEM((ts, DS), F32), pltpu.VMEM((ts, DS), F32), pltpu.VMEM((ts, D), BF16),
                 pltpu.VMEM((ts, DIN), BF16), pltpu.VMEM((GW, DS), F32), pltpu.VMEM((5, 8, D), F32),
                 pltpu.VMEM((3, 8, DP), F32), pltpu.VMEM((DIN, D), F32), pltpu.VMEM((D, D), F32)],
        args=[xin, r1, dx1, f1, hsave, proj, proj, modv, lnp, ln1, winT, wout, *small], pay=pay)


def _cast_call(arrs, name):
    r, c = arrs[0].shape
    rb = _row_block(r, 6 * c * len(arrs))

    def body(*refs):
        for src, dst in zip(refs[:len(arrs)], refs[len(arrs):]):
            dst[...] = src[...].astype(BF16)

    blk = pl.BlockSpec((rb, c), lambda i: (i, 0))
    return pl.pallas_call(
        body, name=name, grid=(r // rb,), in_specs=[blk] * len(arrs), out_specs=[blk] * len(arrs),
        out_shape=[jax.ShapeDtypeStruct((r, c), BF16)] * len(arrs), compiler_params=_ARB)(*arrs)


def _rows_before(halo, x):
    ext = jnp.concatenate([halo, x], axis=0)
    return pltpu.roll(ext, 1, 0)[GHALO:, :], pltpu.roll(ext, 2, 0)[GHALO:, :]


def _rows_after(x, halo):
    ts = x.shape[0]
    ext = jnp.concatenate([x, halo], axis=0)
    n = ts + GHALO
    return pltpu.roll(ext, n - 1, 0)[0:ts, :], pltpu.roll(ext, n - 2, 0)[0:ts, :]


def _fb_call(l, alpha, r1, modv, ln1, wup4, wd2, conv_w, conv_b, pay=None):
    s = r1.shape[0]
    nt = s // TSF

    def body(r1_ref, modv_ref, ln1_ref, wup_ref, wd_ref, cw_ref, cb_ref, r2_ref, f2_ref, h2_ref, gs_ref, vs_ref,
             gbuf):
        i = pl.program_id(0)

        @pl.when(i == 0)
        def _():
            gbuf[...] = jnp.zeros_like(gbuf)

        xhat1, _ = _ln_stats(r1_ref[...])
        x1 = xhat1 * ln1_ref[0:1, :] + ln1_ref[1:2, :]
        h2b = (x1 * (1.0 + modv_ref[4:5, :]) + modv_ref[3:4, :]).astype(BF16)
        f2 = jnp.zeros((TSF, D), F32)
        for hf in range(2):
            cs = slice(hf * FH, (hf + 1) * FH)
            g = _mm_nt(h2b, wup_ref[hf])
            val = _mm_nt(h2b, wup_ref[2 + hf])
            gm1, gm2 = _rows_before(gbuf[hf], g)
            cw = cw_ref[:, cs]
            gc = cb_ref[:, cs] + cw[0:1, :] * gm2 + cw[1:2, :] * gm1 + cw[2:3, :] * g
            ge, _ = _gelu_t(gc)
            f2 = f2 + _mm((ge * val).astype(BF16), wd_ref[hf])
            gs_ref[:, cs] = g.astype(BF16)
            vs_ref[:, cs] = val.astype(BF16)
            gbuf[hf] = g[TSF - GHALO:, :]
        r2_ref[...] = alpha * x1 + modv_ref[5:6, :] * f2
        f2_ref[...] = f2.astype(BF16)
        h2_ref[...] = h2b

    tile = pl.BlockSpec((TSF, D), lambda i: (i, 0))
    wide = pl.BlockSpec((TSF, FF), lambda i: (i, 0))
    return _pcall(
        body, f"fb{l}", nt,
        in_specs=[tile, _layer_vec(8, D, l), _layer_vec(2, D, l), _resident((4, FH, D)), _resident((2, FH, D)),
                  _layer_vec(3, FF, l), _layer_vec(1, FF, l)],
        out_specs=[tile, tile, tile, wide, wide],
        out_shape=[jax.ShapeDtypeStruct((s, D), F32), jax.ShapeDtypeStruct((s, D), BF16),
                   jax.ShapeDtypeStruct((s, D), BF16), jax.ShapeDtypeStruct((s, FF), BF16),
                   jax.ShapeDtypeStruct((s, FF), BF16)],
        scratch=[pltpu.VMEM((2, GHALO, FH), F32)],
        args=[r1, modv, ln1, wup4, wd2, conv_w, conv_b], pay=pay)


def _ff_backward(h2b, df2b, gs_ref, vs_ref, gh_ref, first_tile, dgc_next, cw, cb, wg_ref, wv_ref, wd_ref,
                 dwg_ref, dwv_ref, dwd_ref, cacc):
    g = gs_ref[...].astype(F32)
    val = vs_ref[...].astype(F32)
    halo = gh_ref[...].astype(F32)[GHALO:, :]
    gm1, gm2 = _rows_before(jnp.where(first_tile, 0.0, halo), g)
    gc = cb + cw[0:1, :] * gm2 + cw[1:2, :] * gm1 + cw[2:3, :] * g
    ge, tg = _gelu_t(gc)
    mb = (ge * val).astype(BF16)
    dm = _mm_nt(df2b, wd_ref[...])
    dwd_ref[...] += _mm_tn(mb, df2b)
    dval = dm * ge
    dgc = dm * val * _dgelu(gc, tg)
    cacc[0] += _colsum8(dgc)
    cacc[1] += _colsum8(dgc * gm2)
    cacc[2] += _colsum8(dgc * gm1)
    cacc[3] += _colsum8(dgc * g)
    dgp1, dgp2 = _rows_after(dgc, dgc_next[...])
    dg = cw[2:3, :] * dgc + cw[1:2, :] * dgp1 + cw[0:1, :] * dgp2
    dgc_next[...] = dgc[0:GHALO, :]
    dgb = dg.astype(BF16)
    dvb = dval.astype(BF16)
    dwg_ref[...] += _mm_tn(dgb, h2b)
    dwv_ref[...] += _mm_tn(dvb, h2b)
    return _mm(dgb, wg_ref[...]) + _mm(dvb, wv_ref[...])


def _bb_specs(l, hf, nt, ts):
    wide = pl.BlockSpec((ts, FH), lambda i: (nt - 1 - i, hf))
    halo = pl.BlockSpec((2 * GHALO, FH), lambda i: (jnp.maximum((nt - 1 - i) * (ts // (2 * GHALO)) - 1, 0), hf))
    ins = [wide, wide, halo, _resident((FH, D), (hf,)), _resident((FH, D), (2 + hf,)), _resident((FH, D), (hf,)),
           pl.BlockSpec((None, 3, FH), lambda i: (l, 0, hf)), pl.BlockSpec((None, 1, FH), lambda i: (l, 0, hf))]
    acc_shapes = [jax.ShapeDtypeStruct((FH, D), F32)] * 3
    return ins, acc_shapes


def _bb0_call(l, last, h2, r2, dx2, f2, gs, vs, modv, ln2, wup4, wd2, conv_w, conv_b, pay=None):
    s = r2.shape[0]
    nt = s // TSB
    ts = TSB

    def body(*refs):
        it = iter(refs)
        h2_ref, r2_ref, dx2_ref, f2_ref, gs_ref, vs_ref, gh_ref = (next(it) for _ in range(7))
        wg_ref, wv_ref, wd_ref, cw_ref, cb_ref, modv_ref, ln2_ref = (next(it) for _ in range(7))
        dh2_ref, df2_ref, dr2_ref, dwg_ref, dwv_ref, dwd_ref, vb_ref, cacc_ref = (next(it) for _ in range(8))
        loss_ref = next(it) if last else None
        vacc, cacc, dgc_next = next(it), next(it), next(it)
        lacc = next(it) if last else None
        i = pl.program_id(0)

        @pl.when(i == 0)
        def _():
            dwg_ref[...] = jnp.zeros_like(dwg_ref)
            dwv_ref[...] = jnp.zeros_like(dwv_ref)
            dwd_ref[...] = jnp.zeros_like(dwd_ref)
            vacc[...] = jnp.zeros_like(vacc)
            cacc[...] = jnp.zeros_like(cacc)
            dgc_next[...] = jnp.zeros_like(dgc_next)
            if last:
                lacc[...] = jnp.zeros_like(lacc)

        xhat2, rstd2 = _ln_stats(r2_ref[...])
        if last:
            diff = xhat2 * ln2_ref[0:1, :] + ln2_ref[1:2, :] - dx2_ref[...]
            dy = diff * (1.0 / D)
            lacc[...] += _colsum8(diff * diff)
        else:
            dy = dx2_ref[...]
        dr2 = _ln_bwd(dy, ln2_ref[0:1, :], xhat2, rstd2)
        vacc[0] += _colsum8(dy * xhat2)
        vacc[1] += _colsum8(dy)
        vacc[2] += _colsum8(dr2 * f2_ref[...].astype(F32))
        df2b = (dr2 * modv_ref[5:6, :]).astype(BF16)
        dr2_ref[...] = dr2
        df2_ref[...] = df2b
        dh2_ref[...] = _ff_backward(h2_ref[...], df2b, gs_ref, vs_ref, gh_ref, i == nt - 1, dgc_next, cw_ref[...],
                                    cb_ref[...], wg_ref, wv_ref, wd_ref, dwg_ref, dwv_ref, dwd_ref, cacc)

        @pl.when(i == nt - 1)
        def _():
            for n in range(3):
                vb_ref[n:n + 1, :] = jnp.sum(vacc[n], axis=0, keepdims=True)
            for n in range(4):
                cacc_ref[n:n + 1, :] = jnp.sum(cacc[n], axis=0, keepdims=True)
            if last:
                loss_ref[...] = lacc[...]

    tile = pl.BlockSpec((ts, D), lambda i: (nt - 1 - i, 0))
    ff_ins, acc_shapes = _bb_specs(l, 0, nt, ts)
    out_specs = [tile, tile, tile] + [_VMEM_WHOLE] * 5
    out_shape = [jax.ShapeDtypeStruct((s, D), F32), jax.ShapeDtypeStruct((s, D), BF16),
                 jax.ShapeDtypeStruct((s, D), F32)] + acc_shapes + [jax.ShapeDtypeStruct((3, D), F32),
                                                                    jax.ShapeDtypeStruct((4, FH), F32)]
    scratch = [pltpu.VMEM((3, 8, D), F32), pltpu.VMEM((4, 8, FH), F32), pltpu.VMEM((GHALO, FH), F32)]
    if last:
        out_specs.append(_VMEM_WHOLE)
        out_shape.append(jax.ShapeDtypeStruct((8, D), F32))
        scratch.append(pltpu.VMEM((8, D), F32))
    return _pcall(body, f"bb{l}_0", nt, [tile, tile, tile, tile] + ff_ins + [_layer_vec(8, D, l), _layer_vec(2, D, l)],
                  out_specs, out_shape, scratch,
                  [h2, r2, dx2, f2, gs, vs, gs, wup4, wup4, wd2, conv_w, conv_b, modv, ln2], pay=pay)


def _bb1_call(l, alpha, h2, r1, df2, dr2, dh2_in, gs, vs, modv, ln1, wup4, wd2, conv_w, conv_b, pay=None):
    s = r1.shape[0]
    nt = s // TSB
    ts = TSB

    def body(h2_ref, r1_ref, df2_ref, dr2_ref, dh2_ref, gs_ref, vs_ref, gh_ref, wg_ref, wv_ref, wd_ref, cw_ref,
             cb_ref, modv_ref, ln1_ref, dx1_ref, dwg_ref, dwv_ref, dwd_ref, vb_ref, cacc_ref, vacc, cacc, dgc_next):
        i = pl.program_id(0)

        @pl.when(i == 0)
        def _():
            dwg_ref[...] = jnp.zeros_like(dwg_ref)
            dwv_ref[...] = jnp.zeros_like(dwv_ref)
            dwd_ref[...] = jnp.zeros_like(dwd_ref)
            vacc[...] = jnp.zeros_like(vacc)
            cacc[...] = jnp.zeros_like(cacc)
            dgc_next[...] = jnp.zeros_like(dgc_next)

        dh2 = dh2_ref[...] + _ff_backward(h2_ref[...], df2_ref[...], gs_ref, vs_ref, gh_ref, i == nt - 1, dgc_next,
                                          cw_ref[...], cb_ref[...], wg_ref, wv_ref, wd_ref, dwg_ref, dwv_ref,
                                          dwd_ref, cacc)
        xhat1, _ = _ln_stats(r1_ref[...])
        x1 = xhat1 * ln1_ref[0:1, :] + ln1_ref[1:2, :]
        dx1_ref[...] = dh2 * (1.0 + modv_ref[4:5, :]) + alpha * dr2_ref[...]
        vacc[0] += _colsum8(dh2)
        vacc[1] += _colsum8(dh2 * x1)

        @pl.when(i == nt - 1)
        def _():
            for n in range(2):
                vb_ref[n:n + 1, :] = jnp.sum(vacc[n], axis=0, keepdims=True)
            for n in range(4):
                cacc_ref[n:n + 1, :] = jnp.sum(cacc[n], axis=0, keepdims=True)

    tile = pl.BlockSpec((ts, D), lambda i: (nt - 1 - i, 0))
    ff_ins, acc_shapes = _bb_specs(l, 1, nt, ts)
    out_shape = [jax.ShapeDtypeStruct((s, D), F32)] + acc_shapes + [jax.ShapeDtypeStruct((2, D), F32),
                                                                   jax.ShapeDtypeStruct((4, FH), F32)]
    scratch = [pltpu.VMEM((2, 8, D), F32), pltpu.VMEM((4, 8, FH), F32), pltpu.VMEM((GHALO, FH), F32)]
    return _pcall(body, f"bb{l}_1", nt, [tile] * 5 + ff_ins + [_layer_vec(8, D, l), _layer_vec(2, D, l)],
                  [tile] + [_VMEM_WHOLE] * 5, out_shape, scratch,
                  [h2, r1, df2, dr2, dh2_in, gs, vs, gs, wup4, wup4, wd2, conv_w, conv_b, modv, ln1], pay=pay)


def _silu(c):
    return c * (1.0 / (1.0 + jnp.exp(-c)))


def _ada_fwd_call(c_all, ada_w, ada_b_my):
    nl, _, wcols = ada_w.shape

    def body(c_ref, w_ref, b_ref, o_ref):
        ca = _silu(c_ref[...])
        o_ref[...] = jnp.dot(ca, w_ref[...], preferred_element_type=F32,
                             precision=lax.Precision.HIGHEST) + b_ref[...]

    return pl.pallas_call(
        body, name="ada_fwd", grid=(nl,),
        in_specs=[_full((NDEV, D)), pl.BlockSpec((None, D, wcols), lambda l: (l, 0, 0)),
                  pl.BlockSpec((None, 1, wcols), lambda l: (l, 0, 0))],
        out_specs=pl.BlockSpec((None, NDEV, wcols), lambda l: (l, 0, 0)),
        out_shape=jax.ShapeDtypeStruct((nl, NDEV, wcols), F32),
        compiler_params=_ARB,
    )(c_all, ada_w, ada_b_my)


def _adam_update(w, g, m, v):
    m2 = ADAM_B1 * m + (1.0 - ADAM_B1) * g
    v2 = ADAM_B2 * v + (1.0 - ADAM_B2) * (g * g)
    m_hat = m2 / (1.0 - ADAM_B1 ** ADAM_STEP)
    v_hat = v2 / (1.0 - ADAM_B2 ** ADAM_STEP)
    delta = -ADAM_LR * (m_hat / (jnp.sqrt(v_hat) + ADAM_EPS) + ADAM_WD * w)
    return delta, m2, v2


def _ada_bwd_call(c_t, dmod_my, w, m, v):
    nl, _, wcols = w.shape
    rb = 256

    def body(ct_ref, dm_ref, w_ref, m_ref, v_ref, g_ref, d_ref, m2_ref, v2_ref):
        ca_t = _silu(ct_ref[...])
        dm = dm_ref[...]
        g = ca_t[:, 0:1] * dm[0:1, :]
        for b in range(1, NDEV):
            g = g + ca_t[:, b:b + 1] * dm[b:b + 1, :]
        delta, m2, v2 = _adam_update(w_ref[...], g, m_ref[...], v_ref[...])
        g_ref[...] = g
        d_ref[...] = delta
        m2_ref[...] = m2
        v2_ref[...] = v2

    blk = pl.BlockSpec((None, rb, wcols), lambda l, i: (l, i, 0))
    shp = jax.ShapeDtypeStruct(w.shape, F32)
    return pl.pallas_call(
        body, name="ada_bwd", grid=(nl, D // rb),
        in_specs=[pl.BlockSpec((rb, NDEV), lambda l, i: (i, 0)),
                  pl.BlockSpec((None, NDEV, wcols), lambda l, i: (l, 0, 0)), blk, blk, blk],
        out_specs=[blk, blk, blk, blk], out_shape=[shp, shp, shp, shp],
        compiler_params=pltpu.CompilerParams(dimension_semantics=("arbitrary", "arbitrary")),
    )(c_t, dmod_my, w, m, v)


def _row_block(r, row_bytes):
    budget = 6 * 1024 * 1024
    best = None
    for rb in range(16, min(r, 512) + 1, 16):
        if r % rb == 0 and rb * row_bytes <= budget:
            best = rb
    return best if best is not None else r


def _adamw_call(parts, w, m, v, name, pay=None):
    nl, npart, r, c = parts.shape
    rb = _row_block(r, (npart + 7) * c * 4)
    nb = r // rb

    def body(p_ref, w_ref, m_ref, v_ref, g_ref, d_ref, m2_ref, v2_ref):
        g = p_ref[0].astype(F32)
        for k in range(1, npart):
            g = g + p_ref[k].astype(F32)
        delta, m2, v2 = _adam_update(w_ref[...], g, m_ref[...], v_ref[...])
        g_ref[...] = g
        d_ref[...] = delta
        m2_ref[...] = m2
        v2_ref[...] = v2

    blk = pl.BlockSpec((None, rb, c), lambda i: (i // nb, i % nb, 0))
    shp = jax.ShapeDtypeStruct((nl, r, c), F32)
    outs, _, alias = _pcall(body, name, nl * nb,
                            [pl.BlockSpec((None, npart, rb, c), lambda i: (i // nb, 0, i % nb, 0)), blk, blk, blk],
                            [blk] * 4, [shp] * 4, [], [parts, w, m, v], pay)
    return tuple(outs), alias


_SMALL_ORDER = ("ada_b", "pool_w", "sgu_w", "ln1_g", "ln1_b", "ln2_g", "ln2_b", "pool_scale", "sgu_ln_g", "sgu_ln_b",
                "sgu_b", "conv_w", "conv_b")


def kernel(x, c, ada_w, ada_b, w_in, pool_w, pool_scale, sgu_ln_g, sgu_ln_b, sgu_w, sgu_b, w_out, ln1_g, ln1_b, w_up, conv_w, conv_b, w_down, ln2_g, ln2_b, loss_target, m_ada_w, m_ada_b, m_w_in, m_pool_w, m_pool_scale, m_sgu_ln_g, m_sgu_ln_b, m_sgu_w, m_sgu_b, m_w_out, m_ln1_g, m_ln1_b, m_w_up, m_conv_w, m_conv_b, m_w_down, m_ln2_g, m_ln2_b, v_ada_w, v_ada_b, v_w_in, v_pool_w, v_pool_scale, v_sgu_ln_g, v_sgu_ln_b, v_sgu_w, v_sgu_b, v_w_out, v_ln1_g, v_ln1_b, v_w_up, v_conv_w, v_conv_b, v_w_down, v_ln2_g, v_ln2_b):
    nl = ada_w.shape[0]
    alpha = (2.0 * nl) ** 0.25
    me = _me()
    x2 = x[0]
    tgt = loss_target[0]
    acols = ada_w.shape[2]
    icols = w_in.shape[2]
    ucols = w_up.shape[2]
    orows = w_out.shape[1]
    drows = w_down.shape[1]
    ccols = conv_w.shape[2]

    winT_sh = jnp.swapaxes(w_in, 1, 2).astype(BF16)
    wupT_sh = jnp.swapaxes(w_up, 1, 2).astype(BF16)
    wout_sh = w_out.astype(BF16)
    wd_sh = w_down.astype(BF16)

    pay = _Payload()
    pay.gather(jnp.broadcast_to(c, (8, D)))
    pay.gather(winT_sh, 0)
    pay.gather(wout_sh, 0)
    pay.gather(conv_w)
    (c_g, winT_g, wout_g, cw_g), _ = _comm_call(pay, "gather_first")
    c_all = c_g[:, 0, :]
    winT = winT_g.reshape(DIN, D)
    wout = wout_g.reshape(D, D)
    cw_full = jnp.transpose(cw_g, (1, 2, 0, 3)).reshape(nl, 3, FF)
    cb_full = conv_b[:, None, :]

    ada_b_my = lax.dynamic_slice(ada_b, (0, me * acols), (nl, acols))[:, None, :]
    mod_blk = _ada_fwd_call(c_all, ada_w, ada_b_my)
    pay = _Payload()
    pay.gather(mod_blk)
    (mod_g,), _ = _comm_call(pay, "gather_mod")
    mod_me = lax.dynamic_index_in_dim(mod_g, me, axis=2, keepdims=False)
    modv = jnp.swapaxes(mod_me, 0, 1).reshape(nl, 6, D)
    modv = jnp.concatenate([modv, jnp.zeros((nl, 2, D), F32)], axis=1)

    ln1 = jnp.stack([ln1_g, ln1_b], axis=1)
    ln2 = jnp.stack([ln2_g, ln2_b], axis=1)
    sln = jnp.stack([sgu_ln_g, sgu_ln_b], axis=1)
    sbf = jnp.broadcast_to(sgu_b[..., None], sgu_b.shape + (GW,))
    small = (pool_w, pool_scale[:, None, :], sln, sgu_w, sbf)

    r1s, r2s, f1s, f2s, ahs, ghs, wins, wouts, wups, wds = ([None] * nl for _ in range(10))
    wins[0], wouts[0] = winT, wout
    xin = x2
    for l in range(nl):
        pay = None
        if l == 0:
            pay = _Payload()
            pay.gather(wupT_sh, 0)
            pay.gather(wd_sh, 0)
        (r1, f1, *ah), new, _ = _fa_call(l, l == 0, alpha, xin, modv, ln2, wins[l], wouts[l], small, pay)
        if l == 0:
            wups[0], wds[0] = new[0].reshape(4, FH, D), new[1].reshape(2, FH, D)
        pay = None
        if l + 1 < nl:
            pay = _Payload()
            pay.gather(winT_sh, l + 1)
            pay.gather(wout_sh, l + 1)
            pay.gather(wupT_sh, l + 1)
            pay.gather(wd_sh, l + 1)
        (r2, f2, h2, gs, vs), new, _ = _fb_call(l, alpha, r1, modv, ln1, wups[l], wds[l], cw_full, cb_full, pay)
        if l + 1 < nl:
            wins[l + 1], wouts[l + 1] = new[0].reshape(DIN, D), new[1].reshape(D, D)
            wups[l + 1], wds[l + 1] = new[2].reshape(4, FH, D), new[3].reshape(2, FH, D)
        r1s[l], r2s[l], f1s[l], f2s[l], ahs[l], ghs[l] = r1, r2, f1, f2, ah, (h2, gs, vs)
        xin = r2

    nsmall = 6 * D + 2 * NG * GW * GW + 3 * DP + NG * GW + 4 * D + 4 * FF
    srows = nsmall // 1024
    buf_in = lax.empty((nl, NDEV, icols, D), BF16)
    buf_out = lax.empty((nl, NDEV, orows, D), BF16)
    buf_up = lax.empty((nl, NDEV, ucols, D), BF16)
    buf_down = lax.empty((nl, NDEV, drows, D), BF16)
    buf_small = lax.empty((nl, NDEV, srows, 1024), F32)

    def a_side_payload(l, dwin, dwout, flat):
        p = _Payload()
        p.exchange_into([(dwin.reshape(NDEV, icols, D), 0)], buf_in, (l,))
        p.exchange_into([(dwout.reshape(NDEV, orows, D), 0)], buf_out, (l,))
        p.gather_into(flat, buf_small, (l,))
        return p

    dx = tgt
    loss_acc = None
    pending = None
    for l in reversed(range(nl)):
        last = l == nl - 1
        pay = None if pending is None else a_side_payload(*pending)
        h2, gs, vs = ghs[l]
        res0, _, al = _bb0_call(l, last, h2, r2s[l], dx, f2s[l], gs, vs, modv, ln2, wups[l], wds[l], cw_full,
                                cb_full, pay)
        if pending is not None:
            buf_in, buf_out, buf_small = al
        dh2p, df2, dr2, dwg0, dwv0, dwd0, vb0, cacc0 = res0[:8]
        if last:
            loss_acc = res0[8]
        (dx1, dwg1, dwv1, dwd1, vb1, cacc1), _, _ = _bb1_call(
            l, alpha, h2, r1s[l], df2, dr2, dh2p, gs, vs, modv, ln1, wups[l], wds[l], cw_full, cb_full)
        dwg0, dwv0, dwd0, dwg1, dwv1, dwd1 = _cast_call([dwg0, dwv0, dwd0, dwg1, dwv1, dwd1], f"cast{l}")
        pay = _Payload()
        pay.exchange_into([(dwg0.reshape(2, ucols, D), 0), (dwg1.reshape(2, ucols, D), 2),
                           (dwv0.reshape(2, ucols, D), 4), (dwv1.reshape(2, ucols, D), 6)], buf_up, (l,))
        pay.exchange_into([(dwd0.reshape(4, drows, D), 0), (dwd1.reshape(4, drows, D), 4)], buf_down, (l,))
        xin = x2 if l == 0 else r2s[l - 1]
        (dx, dwin, dwout, dpw, dsw, dsb, va, va5), _, (buf_up, buf_down) = _ba_call(
            l, l == 0, alpha, xin, r1s[l], dx1, f1s[l], *ahs[l], modv, ln2, ln1, wins[l], wouts[l], small, pay)
        cacc = jnp.concatenate([cacc0, cacc1], axis=1)
        piece = dict(ada_b=jnp.stack([va[3], va[4], va[2], vb1[0], vb1[1], vb0[2]]), pool_w=dpw,
                     pool_scale=va5[0], sgu_ln_g=va5[1], sgu_ln_b=va5[2], sgu_w=dsw, sgu_b=dsb[:, :, 0],
                     ln1_g=va[0], ln1_b=va[1], conv_w=cacc[1:4], conv_b=cacc[0], ln2_g=vb0[0], ln2_b=vb0[1])
        flat = jnp.concatenate([piece[n].reshape(-1) for n in _SMALL_ORDER]).reshape(srows, 1024)
        pending = (l, dwin, dwout, flat)
    grad_x = dx[None]
    loss = lax.psum((0.5 / D) * jnp.sum(loss_acc), ("x", "y", "c"))

    res = {}
    _, dwin0, dwout0, flat0 = pending
    t_ = lambda a: jnp.swapaxes(a, 1, 2)
    pay = _Payload()
    pay.exchange_into([(dwout0.reshape(NDEV, orows, D), 0)], buf_out, (0,))
    pay.gather_into(flat0, buf_small, (0,))
    outs, (buf_out, buf_small) = _adamw_call(buf_up, t_(w_up), t_(m_w_up), t_(v_w_up), "adamw_w_up", pay)
    res["w_up"] = tuple(t_(o) for o in outs)
    pay = _Payload()
    pay.exchange_into([(dwin0.reshape(NDEV, icols, D), 0)], buf_in, (0,))
    res["w_down"], (buf_in,) = _adamw_call(buf_down, w_down, m_w_down, v_w_down, "adamw_w_down", pay)
    outs, _ = _adamw_call(buf_in, t_(w_in), t_(m_w_in), t_(v_w_in), "adamw_w_in")
    res["w_in"] = tuple(t_(o) for o in outs)
    res["w_out"], _ = _adamw_call(buf_out, w_out, m_w_out, v_w_out, "adamw_w_out")

    weights = dict(ada_b=ada_b, pool_w=pool_w, pool_scale=pool_scale, sgu_ln_g=sgu_ln_g, sgu_ln_b=sgu_ln_b,
                   sgu_w=sgu_w, sgu_b=sgu_b, ln1_g=ln1_g, ln1_b=ln1_b, conv_w=conv_w, conv_b=conv_b, ln2_g=ln2_g,
                   ln2_b=ln2_b)
    m_of = dict(ada_b=m_ada_b, pool_w=m_pool_w, pool_scale=m_pool_scale, sgu_ln_g=m_sgu_ln_g, sgu_ln_b=m_sgu_ln_b,
                sgu_w=m_sgu_w, sgu_b=m_sgu_b, ln1_g=m_ln1_g, ln1_b=m_ln1_b, conv_w=m_conv_w, conv_b=m_conv_b,
                ln2_g=m_ln2_g, ln2_b=m_ln2_b)
    v_of = dict(ada_b=v_ada_b, pool_w=v_pool_w, pool_scale=v_pool_scale, sgu_ln_g=v_sgu_ln_g, sgu_ln_b=v_sgu_ln_b,
                sgu_w=v_sgu_w, sgu_b=v_sgu_b, ln1_g=v_ln1_g, ln1_b=v_ln1_b, conv_w=v_conv_w, conv_b=v_conv_b,
                ln2_g=v_ln2_g, ln2_b=v_ln2_b)
    row = 0
    dmod_my = None
    tail = None
    for name in _SMALL_ORDER:
        w = weights[name]
        per_layer = FF * 3 if name == "conv_w" else w[0].size
        if per_layer % 1024 == 0:
            nrow = per_layer // 1024
            parts = buf_small[:, :, row:row + nrow, :]
            row += nrow
            if name == "ada_b":
                dmod_my = lax.dynamic_slice_in_dim(parts.reshape(nl, NDEV, 6 * D), me * acols, acols, axis=2)
        else:
            if tail is None:
                tail = buf_small[:, :, row:, :].reshape(nl, NDEV, (srows - row) * 1024)
                off = 0
            parts = tail[:, :, off:off + per_layer]
            off += per_layer
        if name == "conv_w":
            parts = lax.dynamic_slice_in_dim(parts.reshape(nl, NDEV, 3, FF), me * ccols, ccols, axis=3)
        cdim = w.shape[-1]
        w3 = w.reshape(nl, -1, cdim)
        outs, _ = _adamw_call(parts.reshape(nl, NDEV, -1, cdim), w3, m_of[name].reshape(w3.shape),
                              v_of[name].reshape(w3.shape), f"adamw_{name}")
        res[name] = tuple(o.reshape(w.shape) for o in outs)

    res["ada_w"] = _ada_bwd_call(jnp.swapaxes(c_all, 0, 1), dmod_my, ada_w, m_ada_w, v_ada_w)

    order = ["ada_w", "ada_b", "w_in", "pool_w", "pool_scale", "sgu_ln_g", "sgu_ln_b", "sgu_w", "sgu_b", "w_out",
             "ln1_g", "ln1_b", "w_up", "conv_w", "conv_b", "w_down", "ln2_g", "ln2_b"]
    out = [loss, grad_x]
    for k in range(4):
        out += [res[n][k] for n in order]
    return tuple(out)
```

```python
import jax
import jax.numpy as jnp
from jax import lax
from jax.experimental import pallas as pl
from jax.experimental.pallas import tpu as pltpu

F32 = jnp.float32
BF16 = jnp.bfloat16

NDEV = 8
D = 1024
DP = 512
DS = 512
DIN = DP + 2 * DS
FF = 2816
FH = FF // 2
FF_CHUNKS = ((0, 1536), (1536, FF))
NG = 4
GW = 128
WINDOWS = (2, 4, 8, 16)
AHALO = 16
GHALO = 8
LN_EPS = 1e-5
ADAM_LR, ADAM_B1, ADAM_B2, ADAM_EPS, ADAM_WD, ADAM_STEP = 0.001, 0.9, 0.999, 1e-08, 0.01, 10
TSF = 512
TSB = 256
_K0 = 0.7978845608028654
_K1 = 0.044715
MESH_ID = pl.DeviceIdType.MESH


def _mm(a, b):
    return jnp.dot(a, b, preferred_element_type=F32)


def _mm_nt(a, b):
    return lax.dot_general(a, b, (((1,), (1,)), ((), ())), preferred_element_type=F32)


def _mm_tn(a, b):
    return lax.dot_general(a, b, (((0,), (0,)), ((), ())), preferred_element_type=F32)


def _rowmean(x):
    return jnp.mean(x, axis=-1, keepdims=True)


def _ln_stats(x):
    mu = _rowmean(x)
    xc = x - mu
    rstd = lax.rsqrt(_rowmean(xc * xc) + LN_EPS)
    return xc * rstd, rstd


def _ln_bwd(dy, gamma, xhat, rstd):
    dxh = dy * gamma
    return rstd * (dxh - _rowmean(dxh) - xhat * _rowmean(dxh * xhat))


def _gelu_t(x):
    t = jnp.tanh(x * (_K0 + (_K0 * _K1) * (x * x)))
    hx = 0.5 * x
    return hx + hx * t, t


def _dgelu(x, t):
    return (0.5 + 0.5 * t) + (0.5 * x) * (1.0 - t * t) * (_K0 + (3.0 * _K0 * _K1) * (x * x))


def _colsum8(x):
    t, n = x.shape
    return jnp.sum(x.reshape(t // 8, 8, n), axis=0)


def _tril_mask():
    r = lax.broadcasted_iota(jnp.int32, (GW, GW), 0)
    c = lax.broadcasted_iota(jnp.int32, (GW, GW), 1)
    return c <= r


def _full(shape):
    n = len(shape)
    return pl.BlockSpec(shape, lambda *_: (0,) * n)


def _resident(tail, lead=()):
    n = len(tail)
    return pl.BlockSpec((None,) * len(lead) + tuple(tail), lambda *_: tuple(lead) + (0,) * n,
                        pipeline_mode=pl.Buffered(1))


def _layer_vec(rows, width, l):
    return pl.BlockSpec((None, rows, width), lambda *_: (l, 0, 0))


_VMEM_WHOLE = pl.BlockSpec(memory_space=pltpu.VMEM)
_HBM = pl.BlockSpec(memory_space=pl.ANY)
_ARB = pltpu.CompilerParams(dimension_semantics=("arbitrary",))


def _me():
    return 4 * lax.axis_index("x") + 2 * lax.axis_index("y") + lax.axis_index("c")


def _coords(p):
    return (p >> 2, (p >> 1) & 1, p & 1)


class _Payload:
    def __init__(self):
        self.srcs, self.new, self.alias, self.transfers = [], [], [], []

    def _src(self, arr):
        self.srcs.append(arr)
        return len(self.srcs) - 1

    def _alias(self, buf):
        self.alias.append(buf)
        return len(self.alias) - 1

    def gather(self, arr, chunk=None):
        pos = self._src(arr)
        blk = arr.shape if chunk is None else arr.shape[1:]
        self.new.append(jax.ShapeDtypeStruct((NDEV,) + tuple(blk), arr.dtype))
        self.transfers.append(([(pos, chunk)] * NDEV, ("new", len(self.new) - 1), ()))
        return len(self.new) - 1

    def gather_into(self, arr, buf, lead):
        pos = self._src(arr)
        self.transfers.append(([(pos, None)] * NDEV, ("alias", self._alias(buf)), tuple(lead)))

    def exchange_into(self, parts, buf, lead):
        route = {}
        for arr, first in parts:
            pos = self._src(arr)
            for q in range(arr.shape[0]):
                route[first + q] = (pos, q)
        self.transfers.append(([route[p] for p in range(NDEV)], ("alias", self._alias(buf)), tuple(lead)))

    def _ends(self, t, io, dst_dev, src_dev):
        srcs, new_out, alias_out = io
        route, (kind, k), lead = self.transfers[t]
        pos, q = route[dst_dev]
        src = srcs[pos] if q is None else srcs[pos].at[q]
        buf = new_out[k] if kind == "new" else alias_out[k]
        return src, buf.at[lead + (src_dev,)]

    def _remote(self, t, io, sems, src_dev, dst_dev):
        src, dst = self._ends(t, io, dst_dev, src_dev)
        return pltpu.make_async_remote_copy(
            src_ref=src, dst_ref=dst, send_sem=sems[0].at[t, dst_dev], recv_sem=sems[1].at[t, src_dev],
            device_id=_coords(dst_dev), device_id_type=MESH_ID)

    def _local(self, t, io, sems, p):
        src, dst = self._ends(t, io, p, p)
        return pltpu.make_async_copy(src, dst, sems[2].at[t])

    def start(self, io, sems):
        me = _me()
        for p in range(NDEV):
            @pl.when(me == p)
            def _():
                for t in range(len(self.transfers)):
                    self._local(t, io, sems, p).start()

            @pl.when(me != p)
            def _():
                for t in range(len(self.transfers)):
                    self._remote(t, io, sems, me, p).start()

    def wait(self, io, sems):
        me = _me()
        for p in range(NDEV):
            @pl.when(me == p)
            def _():
                for t in range(len(self.transfers)):
                    self._local(t, io, sems, p).wait()

            @pl.when(me != p)
            def _():
                for t in range(len(self.transfers)):
                    self._remote(t, io, sems, p, p).wait_recv()
                    self._remote(t, io, sems, me, p).wait_send()


def _pcall(body, name, nsteps, in_specs, out_specs, out_shape, scratch, args, pay=None):
    n_in, n_out, n_scr = len(args), len(out_shape), len(scratch)
    if pay is None:
        res = pl.pallas_call(body, name=name, grid=(nsteps,), in_specs=list(in_specs), out_specs=list(out_specs),
                             out_shape=list(out_shape), scratch_shapes=list(scratch), compiler_params=_ARB)(*args)
        return list(res), [], []
    ns, nn, na, nt = len(pay.srcs), len(pay.new), len(pay.alias), len(pay.transfers)

    def full(*refs):
        cin = refs[:n_in]
        srcs = refs[n_in:n_in + ns]
        o0 = n_in + ns + na
        cout = refs[o0:o0 + n_out]
        new_out = refs[o0 + n_out:o0 + n_out + nn]
        alias_out = refs[o0 + n_out + nn:o0 + n_out + nn + na]
        s0 = o0 + n_out + nn + na
        cscr = refs[s0:s0 + n_scr]
        sems = refs[s0 + n_scr:]
        io = (srcs, new_out, alias_out)
        i = pl.program_id(0)

        @pl.when(i == 0)
        def _():
            pay.start(io, sems)

        body(*cin, *cout, *cscr)

        @pl.when(i == nsteps - 1)
        def _():
            pay.wait(io, sems)

    res = pl.pallas_call(
        full, name=name, grid=(nsteps,),
        in_specs=list(in_specs) + [_HBM] * (ns + na), out_specs=list(out_specs) + [_HBM] * (nn + na),
        out_shape=list(out_shape) + pay.new + [jax.ShapeDtypeStruct(b.shape, b.dtype) for b in pay.alias],
        input_output_aliases={n_in + ns + k: n_out + nn + k for k in range(na)},
        scratch_shapes=list(scratch) + [pltpu.SemaphoreType.DMA((nt, NDEV)), pltpu.SemaphoreType.DMA((nt, NDEV)),
                                        pltpu.SemaphoreType.DMA((nt,))],
        compiler_params=_ARB,
    )(*args, *pay.srcs, *pay.alias)
    return list(res[:n_out]), list(res[n_out:n_out + nn]), list(res[n_out + nn:])


def _comm_call(pay, name):
    def body():
        pass

    _, new, alias = _pcall(body, name, 1, [], [], [], [], [], pay)
    return new, alias


def _window_sums(x, halo, before):
    ts = x.shape[0]
    ext = jnp.concatenate([halo, x] if before else [x, halo], axis=0)
    n = ts + AHALO
    shift = (lambda k: k) if before else (lambda k: n - k)
    keep = slice(AHALO, n) if before else slice(0, ts)
    out = []
    s = ext
    for level in range(NG):
        s = s + pltpu.roll(s, shift(1 << level), 0)
        out.append(s[keep, 0:GW])
        if level + 1 < NG:
            s = s[:, GW:]
    return out


def _a_forward(a, u, v, halo, pw_ref, ps_ref, sln_ref, sw_ref, sbf_ref, zbuf, tile, ts):
    tglob = tile * ts + lax.broadcasted_iota(jnp.int32, (ts, 1), 0)
    sums = _window_sums(a, halo, True)
    pooled_b, mixed, inv_cnt, pwb = [], [], [], []
    for g, w in enumerate(WINDOWS):
        a_g = a[:, g * GW:(g + 1) * GW]
        s = sums[g]
        inv = 1.0 / jnp.minimum(tglob + 1, w).astype(F32)
        pg = (s * inv - a_g).astype(BF16)
        wg = pw_ref[g].astype(BF16)
        pooled_b.append(pg)
        inv_cnt.append(inv)
        pwb.append(wg)
        mixed.append(_mm(pg, wg))
    mixed = jnp.concatenate(mixed, axis=1)
    ya = mixed * ps_ref[...]
    ug, tu = _gelu_t(u)
    vg, tv = _gelu_t(v)
    vhat, rstdv = _ln_stats(vg)
    vnb = (vhat * sln_ref[0:1, :] + sln_ref[1:2, :]).astype(BF16)
    tri = _tril_mask()
    wt = [jnp.where(tri, sw_ref[h], 0.0).astype(BF16) for h in range(NG)]
    for c in range(ts // GW):
        rs = slice(c * GW, (c + 1) * GW)
        for h in range(NG):
            cs = slice(h * GW, (h + 1) * GW)
            zbuf[rs, cs] = _mm(wt[h], vnb[rs, cs]) + sbf_ref[h]
    z = zbuf[...]
    yb = ug * z
    return dict(u=u, v=v, tu=tu, tv=tv, ug=ug, z=z, vhat=vhat, rstdv=rstdv, vnb=vnb,
                wt=wt, pooled_b=pooled_b, pwb=pwb, inv_cnt=inv_cnt, mixed=mixed, ya=ya, yb=yb)


def _small_specs(l):
    grp = pl.BlockSpec((None, NG, GW, GW), lambda i: (l, 0, 0, 0))
    return [grp, _layer_vec(1, DP, l), _layer_vec(2, DS, l), grp, grp]


def _fa_call(l, first, alpha, xin, modv, lnp, winT, wout, small, pay=None):
    s = xin.shape[0]
    nt = s // TSF

    def body(xin_ref, modv_ref, lnp_ref, winT_ref, wout_ref, pw_ref, ps_ref, sln_ref, sw_ref, sbf_ref,
             r1_ref, f1_ref, h_ref, proj_ref, x_ref, acarry, zbuf, mixbuf):
        i = pl.program_id(0)

        @pl.when(i == 0)
        def _():
            acarry[...] = jnp.zeros_like(acarry)

        x = xin_ref[...]
        if not first:
            xhat, _ = _ln_stats(x)
            x = xhat * lnp_ref[0:1, :] + lnp_ref[1:2, :]
        hb = (x * (1.0 + modv_ref[1:2, :]) + modv_ref[0:1, :]).astype(BF16)
        proj = _mm_nt(hb, winT_ref[...])
        a = proj[:, 0:DP]
        fw = _a_forward(a, proj[:, DP:DP + DS], proj[:, DP + DS:], acarry[...], pw_ref, ps_ref, sln_ref, sw_ref,
                        sbf_ref, zbuf, i, TSF)
        acarry[...] = a[TSF - AHALO:, :]
        mixbuf[:, 0:DP] = fw["ya"].astype(BF16)
        mixbuf[:, DP:] = fw["yb"].astype(BF16)
        f = _mm(mixbuf[...], wout_ref[...])
        r1_ref[...] = alpha * x + modv_ref[2:3, :] * f
        f1_ref[...] = f.astype(BF16)
        h_ref[...] = hb
        proj_ref[...] = proj.astype(BF16)
        x_ref[...] = x.astype(BF16)

    tile = pl.BlockSpec((TSF, D), lambda i: (i, 0))
    return _pcall(
        body, f"fa{l}", nt,
        in_specs=[tile, _layer_vec(8, D, l), _layer_vec(2, D, max(l - 1, 0)), _resident((DIN, D)),
                  _resident((D, D))] + _small_specs(l),
        out_specs=[tile, tile, tile, pl.BlockSpec((TSF, DIN), lambda i: (i, 0)), tile],
        out_shape=[jax.ShapeDtypeStruct((s, D), F32), jax.ShapeDtypeStruct((s, D), BF16),
                   jax.ShapeDtypeStruct((s, D), BF16), jax.ShapeDtypeStruct((s, DIN), BF16),
                   jax.ShapeDtypeStruct((s, D), BF16)],
        scratch=[pltpu.VMEM((AHALO, DP), F32), pltpu.VMEM((TSF, DS), F32), pltpu.VMEM((TSF, D), BF16)],
        args=[xin, modv, lnp, winT, wout, *small], pay=pay)


def _ba_call(l, alpha, r1, dx1, f1, hsave, proj, xsave, modv, ln1, winT, wout, small, pay=None):
    s = r1.shape[0]
    nt = s // TSB
    ts = TSB

    def body(r1_ref, dx1_ref, f1_ref, h_ref, proj_ref, xs_ref, ah_ref, modv_ref, ln1_ref, winT_ref,
             wout_ref, pw_ref, ps_ref, sln_ref, sw_ref, sbf_ref,
             dx_ref, dwin_ref, dwout_ref, dpw_ref, dsw_ref, dsb_ref, va_ref, va5_ref,
             qnext, zbuf, dvnbuf, mixbuf, dpbuf, dbacc, vacc, vacc5, dwin_acc, dwout_acc):
        i = pl.program_id(0)
        j = nt - 1 - i

        @pl.when(i == 0)
        def _():
            dwin_acc[...] = jnp.zeros_like(dwin_acc)
            dwout_acc[...] = jnp.zeros_like(dwout_acc)
            dpw_ref[...] = jnp.zeros_like(dpw_ref)
            dsw_ref[...] = jnp.zeros_like(dsw_ref)
            dbacc[...] = jnp.zeros_like(dbacc)
            vacc[...] = jnp.zeros_like(vacc)
            vacc5[...] = jnp.zeros_like(vacc5)
            qnext[...] = jnp.zeros_like(qnext)

        hb = h_ref[...]
        sc1 = modv_ref[1:2, :]
        halo = jnp.where(j > 0, ah_ref[...].astype(F32), 0.0)
        fw = _a_forward(proj_ref[:, 0:DP].astype(F32), proj_ref[:, DP:DP + DS].astype(F32),
                        proj_ref[:, DP + DS:].astype(F32), halo, pw_ref, ps_ref, sln_ref, sw_ref, sbf_ref, zbuf, j, ts)
        mixbuf[:, 0:DP] = fw["ya"].astype(BF16)
        mixbuf[:, DP:] = fw["yb"].astype(BF16)

        xhat1, rstd1 = _ln_stats(r1_ref[...])
        dy = dx1_ref[...]
        vacc[0] += _colsum8(dy * xhat1)
        vacc[1] += _colsum8(dy)
        dr1 = _ln_bwd(dy, ln1_ref[0:1, :], xhat1, rstd1)
        vacc[2] += _colsum8(dr1 * f1_ref[...].astype(F32))
        dfb = (dr1 * modv_ref[2:3, :]).astype(BF16)
        dwout_acc[...] += _mm_tn(mixbuf[...], dfb)
        dmix = _mm_nt(dfb, wout_ref[...])
        dya = dmix[:, 0:DP]
        dyb = dmix[:, DP:]

        vacc5[0] += _colsum8(dya * fw["mixed"])
        dmixed = (dya * ps_ref[...]).astype(BF16)
        dpooled, q = [], []
        for g in range(NG):
            cs = slice(g * GW, (g + 1) * GW)
            dpw_ref[g] += _mm_tn(fw["pooled_b"][g], dmixed[:, cs])
            dpg = _mm_nt(dmixed[:, cs], fw["pwb"][g])
            dpooled.append(dpg)
            q.append(dpg * fw["inv_cnt"][g])
        q = jnp.concatenate(q, axis=1)
        sums = _window_sums(q, qnext[...], False)
        qnext[...] = q[0:AHALO, :]
        for g in range(NG):
            dpbuf[:, g * GW:(g + 1) * GW] = (sums[g] - dpooled[g]).astype(BF16)

        dug = dyb * fw["z"]
        dz = dyb * fw["ug"]
        dzb = dz.astype(BF16)
        for c in range(ts // GW):
            rs = slice(c * GW, (c + 1) * GW)
            for h in range(NG):
                cs = slice(h * GW, (h + 1) * GW)
                dvnbuf[rs, cs] = _mm_tn(fw["wt"][h], dzb[rs, cs])
                dsw_ref[h] += _mm_nt(dzb[rs, cs], fw["vnb"][rs, cs])
            dbacc[...] += dz[rs, :]
        dvn = dvnbuf[...]
        vacc5[1] += _colsum8(dvn * fw["vhat"])
        vacc5[2] += _colsum8(dvn)
        dvg = _ln_bwd(dvn, sln_ref[0:1, :], fw["vhat"], fw["rstdv"])
        dpbuf[:, DP:DP + DS] = (dug * _dgelu(fw["u"], fw["tu"])).astype(BF16)
        dpbuf[:, DP + DS:] = (dvg * _dgelu(fw["v"], fw["tv"])).astype(BF16)

        dpb = dpbuf[...]
        dwin_acc[...] += _mm_tn(dpb, hb)
        dh = _mm(dpb, winT_ref[...])
        dx_ref[...] = dh * (1.0 + sc1) + alpha * dr1
        vacc[3] += _colsum8(dh)
        vacc[4] += _colsum8(dh * xs_ref[...].astype(F32))

        @pl.when(i == nt - 1)
        def _():
            dwin_ref[...] = dwin_acc[...].astype(BF16)
            dwout_ref[...] = dwout_acc[...].astype(BF16)
            tri = _tril_mask()
            for h in range(NG):
                dsw_ref[h] = jnp.where(tri, dsw_ref[h], 0.0)
                sb = jnp.sum(dbacc[:, h * GW:(h + 1) * GW], axis=1, keepdims=True)
                dsb_ref[h] = jnp.broadcast_to(sb, (GW, GW))
            for n in range(5):
                va_ref[n:n + 1, :] = jnp.sum(vacc[n], axis=0, keepdims=True)
            for n in range(3):
                va5_ref[n:n + 1, :] = jnp.sum(vacc5[n], axis=0, keepdims=True)

    rev = lambda i: (nt - 1 - i, 0)
    tile = pl.BlockSpec((ts, D), rev)
    return _pcall(
        body, f"ba{l}", nt,
        in_specs=[tile, tile, tile, tile, pl.BlockSpec((ts, DIN), rev), tile,
                  pl.BlockSpec((AHALO, DP), lambda i: (jnp.maximum((nt - 1 - i) * (ts // AHALO) - 1, 0), 0)),
                  _layer_vec(8, D, l), _layer_vec(2, D, l),
                  _resident((DIN, D)), _resident((D, D))] + _small_specs(l),
        out_specs=[tile] + [_VMEM_WHOLE] * 7,
        out_shape=[jax.ShapeDtypeStruct((s, D), F32), jax.ShapeDtypeStruct((DIN, D), BF16),
                   jax.ShapeDtypeStruct((D, D), BF16), jax.ShapeDtypeStruct((NG, GW, GW), F32),
                   jax.ShapeDtypeStruct((NG, GW, GW), F32), jax.ShapeDtypeStruct((NG, GW, GW), F32),
                   jax.ShapeDtypeStruct((5, D), F32), jax.ShapeDtypeStruct((3, DP), F32)],
        scratch=[pltpu.VMEM((AHALO, DP), F32),
                 pltpu.VMEM((ts, DS), F32), pltpu.VMEM((ts, DS), F32), pltpu.VMEM((ts, D), BF16),
                 pltpu.VMEM((ts, DIN), BF16), pltpu.VMEM((GW, DS), F32), pltpu.VMEM((5, 8, D), F32),
                 pltpu.VMEM((3, 8, DP), F32), pltpu.VMEM((DIN, D), F32), pltpu.VMEM((D, D), F32)],
        args=[r1, dx1, f1, hsave, proj, xsave, proj, modv, ln1, winT, wout, *small], pay=pay)


def _cast_call(arrs, name):
    r, c = arrs[0].shape
    rb = _row_block(r, 6 * c * len(arrs))

    def body(*refs):
        for src, dst in zip(refs[:len(arrs)], refs[len(arrs):]):
            dst[...] = src[...].astype(BF16)

    blk = pl.BlockSpec((rb, c), lambda i: (i, 0))
    return pl.pallas_call(
        body, name=name, grid=(r // rb,), in_specs=[blk] * len(arrs), out_specs=[blk] * len(arrs),
        out_shape=[jax.ShapeDtypeStruct((r, c), BF16)] * len(arrs), compiler_params=_ARB)(*arrs)


def _rows_before(halo, x):
    ext = jnp.concatenate([halo, x], axis=0)
    return pltpu.roll(ext, 1, 0)[GHALO:, :], pltpu.roll(ext, 2, 0)[GHALO:, :]


def _rows_after(x, halo):
    ts = x.shape[0]
    ext = jnp.concatenate([x, halo], axis=0)
    n = ts + GHALO
    return pltpu.roll(ext, n - 1, 0)[0:ts, :], pltpu.roll(ext, n - 2, 0)[0:ts, :]


def _fb_call(l, alpha, r1, modv, ln1, wup4, wd2, conv_w, conv_b, pay=None):
    s = r1.shape[0]
    nt = s // TSF

    def body(r1_ref, modv_ref, ln1_ref, wup_ref, wd_ref, cw_ref, cb_ref, r2_ref, f2_ref, h2_ref, gs_ref, vs_ref,
             x1_ref, gbuf):
        i = pl.program_id(0)

        @pl.when(i == 0)
        def _():
            gbuf[...] = jnp.zeros_like(gbuf)

        xhat1, _ = _ln_stats(r1_ref[...])
        x1 = xhat1 * ln1_ref[0:1, :] + ln1_ref[1:2, :]
        h2b = (x1 * (1.0 + modv_ref[4:5, :]) + modv_ref[3:4, :]).astype(BF16)
        f2 = jnp.zeros((TSF, D), F32)
        for c0, c1 in FF_CHUNKS:
            cs = slice(c0, c1)
            g = _mm_nt(h2b, wup_ref[0, cs, :])
            val = _mm_nt(h2b, wup_ref[1, cs, :])
            gm1, gm2 = _rows_before(gbuf[:, cs], g)
            cw = cw_ref[:, cs]
            gc = cb_ref[:, cs] + cw[0:1, :] * gm2 + cw[1:2, :] * gm1 + cw[2:3, :] * g
            ge, _ = _gelu_t(gc)
            f2 = f2 + _mm((ge * val).astype(BF16), wd_ref[cs, :])
            gs_ref[:, cs] = g.astype(BF16)
            vs_ref[:, cs] = val.astype(BF16)
            gbuf[:, cs] = g[TSF - GHALO:, :]
        r2_ref[...] = alpha * x1 + modv_ref[5:6, :] * f2
        f2_ref[...] = f2.astype(BF16)
        h2_ref[...] = h2b
        x1_ref[...] = x1.astype(BF16)

    tile = pl.BlockSpec((TSF, D), lambda i: (i, 0))
    wide = pl.BlockSpec((TSF, FF), lambda i: (i, 0))
    return _pcall(
        body, f"fb{l}", nt,
        in_specs=[tile, _layer_vec(8, D, l), _layer_vec(2, D, l), _resident((2, FF, D)), _resident((FF, D)),
                  _layer_vec(3, FF, l), _layer_vec(1, FF, l)],
        out_specs=[tile, tile, tile, wide, wide, tile],
        out_shape=[jax.ShapeDtypeStruct((s, D), F32), jax.ShapeDtypeStruct((s, D), BF16),
                   jax.ShapeDtypeStruct((s, D), BF16), jax.ShapeDtypeStruct((s, FF), BF16),
                   jax.ShapeDtypeStruct((s, FF), BF16), jax.ShapeDtypeStruct((s, D), BF16)],
        scratch=[pltpu.VMEM((GHALO, FF), F32)],
        args=[r1, modv, ln1, wup4.reshape(2, FF, D), wd2.reshape(FF, D), conv_w, conv_b], pay=pay)


def _ff_backward(h2b, df2b, gs_ref, vs_ref, gh_ref, first_tile, dgc_next, cw, cb, wg_ref, wv_ref, wd_ref,
                 dwg_ref, dwv_ref, dwd_ref, cacc):
    g = gs_ref[...].astype(F32)
    val = vs_ref[...].astype(F32)
    halo = gh_ref[...].astype(F32)[GHALO:, :]
    gm1, gm2 = _rows_before(jnp.where(first_tile, 0.0, halo), g)
    gc = cb + cw[0:1, :] * gm2 + cw[1:2, :] * gm1 + cw[2:3, :] * g
    ge, tg = _gelu_t(gc)
    mb = (ge * val).astype(BF16)
    dm = _mm_nt(df2b, wd_ref[...])
    dwd_ref[...] += _mm_tn(mb, df2b)
    dval = dm * ge
    dgc = dm * val * _dgelu(gc, tg)
    cacc[0] += _colsum8(dgc)
    cacc[1] += _colsum8(dgc * gm2)
    cacc[2] += _colsum8(dgc * gm1)
    cacc[3] += _colsum8(dgc * g)
    dgp1, dgp2 = _rows_after(dgc, dgc_next[...])
    dg = cw[2:3, :] * dgc + cw[1:2, :] * dgp1 + cw[0:1, :] * dgp2
    dgc_next[...] = dgc[0:GHALO, :]
    dgb = dg.astype(BF16)
    dvb = dval.astype(BF16)
    dwg_ref[...] += _mm_tn(dgb, h2b)
    dwv_ref[...] += _mm_tn(dvb, h2b)
    return _mm(dgb, wg_ref[...]) + _mm(dvb, wv_ref[...])


def _bb_specs(l, hf, nt, ts):
    wide = pl.BlockSpec((ts, FH), lambda i: (nt - 1 - i, hf))
    halo = pl.BlockSpec((2 * GHALO, FH), lambda i: (jnp.maximum((nt - 1 - i) * (ts // (2 * GHALO)) - 1, 0), hf))
    ins = [wide, wide, halo, _resident((FH, D), (hf,)), _resident((FH, D), (2 + hf,)), _resident((FH, D), (hf,)),
           pl.BlockSpec((None, 3, FH), lambda i: (l, 0, hf)), pl.BlockSpec((None, 1, FH), lambda i: (l, 0, hf))]
    acc_shapes = [jax.ShapeDtypeStruct((FH, D), F32)] * 3
    return ins, acc_shapes


def _bb0_call(l, last, h2, r2, dx2, f2, gs, vs, modv, ln2, wup4, wd2, conv_w, conv_b, pay=None):
    s = r2.shape[0]
    nt = s // TSB
    ts = TSB

    def body(*refs):
        it = iter(refs)
        h2_ref, r2_ref, dx2_ref, f2_ref, gs_ref, vs_ref, gh_ref = (next(it) for _ in range(7))
        wg_ref, wv_ref, wd_ref, cw_ref, cb_ref, modv_ref, ln2_ref = (next(it) for _ in range(7))
        dh2_ref, df2_ref, dr2_ref, dwg_ref, dwv_ref, dwd_ref, vb_ref, cacc_ref = (next(it) for _ in range(8))
        loss_ref = next(it) if last else None
        vacc, cacc, dgc_next = next(it), next(it), next(it)
        lacc = next(it) if last else None
        i = pl.program_id(0)

        @pl.when(i == 0)
        def _():
            dwg_ref[...] = jnp.zeros_like(dwg_ref)
            dwv_ref[...] = jnp.zeros_like(dwv_ref)
            dwd_ref[...] = jnp.zeros_like(dwd_ref)
            vacc[...] = jnp.zeros_like(vacc)
            cacc[...] = jnp.zeros_like(cacc)
            dgc_next[...] = jnp.zeros_like(dgc_next)
            if last:
                lacc[...] = jnp.zeros_like(lacc)

        xhat2, rstd2 = _ln_stats(r2_ref[...])
        if last:
            diff = xhat2 * ln2_ref[0:1, :] + ln2_ref[1:2, :] - dx2_ref[...]
            dy = diff * (1.0 / D)
            lacc[...] += _colsum8(diff * diff)
        else:
            dy = dx2_ref[...]
        dr2 = _ln_bwd(dy, ln2_ref[0:1, :], xhat2, rstd2)
        vacc[0] += _colsum8(dy * xhat2)
        vacc[1] += _colsum8(dy)
        vacc[2] += _colsum8(dr2 * f2_ref[...].astype(F32))
        df2b = (dr2 * modv_ref[5:6, :]).astype(BF16)
        dr2_ref[...] = dr2
        df2_ref[...] = df2b
        dh2_ref[...] = _ff_backward(h2_ref[...], df2b, gs_ref, vs_ref, gh_ref, i == nt - 1, dgc_next, cw_ref[...],
                                    cb_ref[...], wg_ref, wv_ref, wd_ref, dwg_ref, dwv_ref, dwd_ref, cacc)

        @pl.when(i == nt - 1)
        def _():
            for n in range(3):
                vb_ref[n:n + 1, :] = jnp.sum(vacc[n], axis=0, keepdims=True)
            for n in range(4):
                cacc_ref[n:n + 1, :] = jnp.sum(cacc[n], axis=0, keepdims=True)
            if last:
                loss_ref[...] = lacc[...]

    tile = pl.BlockSpec((ts, D), lambda i: (nt - 1 - i, 0))
    ff_ins, acc_shapes = _bb_specs(l, 0, nt, ts)
    out_specs = [tile, tile, tile] + [_VMEM_WHOLE] * 5
    out_shape = [jax.ShapeDtypeStruct((s, D), F32), jax.ShapeDtypeStruct((s, D), BF16),
                 jax.ShapeDtypeStruct((s, D), F32)] + acc_shapes + [jax.ShapeDtypeStruct((3, D), F32),
                                                                    jax.ShapeDtypeStruct((4, FH), F32)]
    scratch = [pltpu.VMEM((3, 8, D), F32), pltpu.VMEM((4, 8, FH), F32), pltpu.VMEM((GHALO, FH), F32)]
    if last:
        out_specs.append(_VMEM_WHOLE)
        out_shape.append(jax.ShapeDtypeStruct((8, D), F32))
        scratch.append(pltpu.VMEM((8, D), F32))
    return _pcall(body, f"bb{l}_0", nt, [tile, tile, tile, tile] + ff_ins + [_layer_vec(8, D, l), _layer_vec(2, D, l)],
                  out_specs, out_shape, scratch,
                  [h2, r2, dx2, f2, gs, vs, gs, wup4, wup4, wd2, conv_w, conv_b, modv, ln2], pay=pay)


def _bb1_call(l, alpha, h2, x1s, df2, dr2, dh2_in, gs, vs, modv, wup4, wd2, conv_w, conv_b, pay=None):
    s = h2.shape[0]
    nt = s // TSB
    ts = TSB

    def body(h2_ref, x1_ref, df2_ref, dr2_ref, dh2_ref, gs_ref, vs_ref, gh_ref, wg_ref, wv_ref, wd_ref, cw_ref,
             cb_ref, modv_ref, dx1_ref, dwg_ref, dwv_ref, dwd_ref, vb_ref, cacc_ref, vacc, cacc, dgc_next):
        i = pl.program_id(0)

        @pl.when(i == 0)
        def _():
            dwg_ref[...] = jnp.zeros_like(dwg_ref)
            dwv_ref[...] = jnp.zeros_like(dwv_ref)
            dwd_ref[...] = jnp.zeros_like(dwd_ref)
            vacc[...] = jnp.zeros_like(vacc)
            cacc[...] = jnp.zeros_like(cacc)
            dgc_next[...] = jnp.zeros_like(dgc_next)

        dh2 = dh2_ref[...] + _ff_backward(h2_ref[...], df2_ref[...], gs_ref, vs_ref, gh_ref, i == nt - 1, dgc_next,
                                          cw_ref[...], cb_ref[...], wg_ref, wv_ref, wd_ref, dwg_ref, dwv_ref,
                                          dwd_ref, cacc)
        dx1_ref[...] = dh2 * (1.0 + modv_ref[4:5, :]) + alpha * dr2_ref[...]
        vacc[0] += _colsum8(dh2)
        vacc[1] += _colsum8(dh2 * x1_ref[...].astype(F32))

        @pl.when(i == nt - 1)
        def _():
            for n in range(2):
                vb_ref[n:n + 1, :] = jnp.sum(vacc[n], axis=0, keepdims=True)
            for n in range(4):
                cacc_ref[n:n + 1, :] = jnp.sum(cacc[n], axis=0, keepdims=True)

    tile = pl.BlockSpec((ts, D), lambda i: (nt - 1 - i, 0))
    ff_ins, acc_shapes = _bb_specs(l, 1, nt, ts)
    out_shape = [jax.ShapeDtypeStruct((s, D), F32)] + acc_shapes + [jax.ShapeDtypeStruct((2, D), F32),
                                                                   jax.ShapeDtypeStruct((4, FH), F32)]
    scratch = [pltpu.VMEM((2, 8, D), F32), pltpu.VMEM((4, 8, FH), F32), pltpu.VMEM((GHALO, FH), F32)]
    return _pcall(body, f"bb{l}_1", nt, [tile] * 5 + ff_ins + [_layer_vec(8, D, l)],
                  [tile] + [_VMEM_WHOLE] * 5, out_shape, scratch,
                  [h2, x1s, df2, dr2, dh2_in, gs, vs, gs, wup4, wup4, wd2, conv_w, conv_b, modv], pay=pay)


def _silu(c):
    return c * (1.0 / (1.0 + jnp.exp(-c)))


def _ada_fwd_call(c_all, ada_w, ada_b_my):
    nl, _, wcols = ada_w.shape

    def body(c_ref, w_ref, b_ref, o_ref):
        ca = _silu(c_ref[...])
        o_ref[...] = jnp.dot(ca, w_ref[...], preferred_element_type=F32,
                             precision=lax.Precision.HIGHEST) + b_ref[...]

    return pl.pallas_call(
        body, name="ada_fwd", grid=(nl,),
        in_specs=[_full((NDEV, D)), pl.BlockSpec((None, D, wcols), lambda l: (l, 0, 0)),
                  pl.BlockSpec((None, 1, wcols), lambda l: (l, 0, 0))],
        out_specs=pl.BlockSpec((None, NDEV, wcols), lambda l: (l, 0, 0)),
        out_shape=jax.ShapeDtypeStruct((nl, NDEV, wcols), F32),
        compiler_params=_ARB,
    )(c_all, ada_w, ada_b_my)


def _adam_update(w, g, m, v):
    m2 = ADAM_B1 * m + (1.0 - ADAM_B1) * g
    v2 = ADAM_B2 * v + (1.0 - ADAM_B2) * (g * g)
    m_hat = m2 / (1.0 - ADAM_B1 ** ADAM_STEP)
    v_hat = v2 / (1.0 - ADAM_B2 ** ADAM_STEP)
    delta = -ADAM_LR * (m_hat / (jnp.sqrt(v_hat) + ADAM_EPS) + ADAM_WD * w)
    return delta, m2, v2


def _ada_bwd_call(c_t, dmod_my, w, m, v):
    nl, _, wcols = w.shape
    rb = 256

    def body(ct_ref, dm_ref, w_ref, m_ref, v_ref, g_ref, d_ref, m2_ref, v2_ref):
        ca_t = _silu(ct_ref[...])
        dm = dm_ref[...]
        g = ca_t[:, 0:1] * dm[0:1, :]
        for b in range(1, NDEV):
            g = g + ca_t[:, b:b + 1] * dm[b:b + 1, :]
        delta, m2, v2 = _adam_update(w_ref[...], g, m_ref[...], v_ref[...])
        g_ref[...] = g
        d_ref[...] = delta
        m2_ref[...] = m2
        v2_ref[...] = v2

    blk = pl.BlockSpec((None, rb, wcols), lambda l, i: (l, i, 0))
    shp = jax.ShapeDtypeStruct(w.shape, F32)
    return pl.pallas_call(
        body, name="ada_bwd", grid=(nl, D // rb),
        in_specs=[pl.BlockSpec((rb, NDEV), lambda l, i: (i, 0)),
                  pl.BlockSpec((None, NDEV, wcols), lambda l, i: (l, 0, 0)), blk, blk, blk],
        out_specs=[blk, blk, blk, blk], out_shape=[shp, shp, shp, shp],
        compiler_params=pltpu.CompilerParams(dimension_semantics=("arbitrary", "arbitrary")),
    )(c_t, dmod_my, w, m, v)


def _row_block(r, row_bytes):
    budget = 6 * 1024 * 1024
    best = None
    for rb in range(16, min(r, 512) + 1, 16):
        if r % rb == 0 and rb * row_bytes <= budget:
            best = rb
    return best if best is not None else r


def _adamw_call(parts, w, m, v, name, pay=None):
    nl, npart, r, c = parts.shape
    rb = _row_block(r, (npart + 7) * c * 4)
    nb = r // rb

    def body(p_ref, w_ref, m_ref, v_ref, g_ref, d_ref, m2_ref, v2_ref):
        g = p_ref[0].astype(F32)
        for k in range(1, npart):
            g = g + p_ref[k].astype(F32)
        delta, m2, v2 = _adam_update(w_ref[...], g, m_ref[...], v_ref[...])
        g_ref[...] = g
        d_ref[...] = delta
        m2_ref[...] = m2
        v2_ref[...] = v2

    blk = pl.BlockSpec((None, rb, c), lambda i: (i // nb, i % nb, 0))
    shp = jax.ShapeDtypeStruct((nl, r, c), F32)
    outs, _, alias = _pcall(body, name, nl * nb,
                            [pl.BlockSpec((None, npart, rb, c), lambda i: (i // nb, 0, i % nb, 0)), blk, blk, blk],
                            [blk] * 4, [shp] * 4, [], [parts, w, m, v], pay)
    return tuple(outs), alias


_SMALL_ORDER = ("ada_b", "pool_w", "sgu_w", "ln1_g", "ln1_b", "ln2_g", "ln2_b", "pool_scale", "sgu_ln_g", "sgu_ln_b",
                "sgu_b", "conv_w", "conv_b")


def kernel(x, c, ada_w, ada_b, w_in, pool_w, pool_scale, sgu_ln_g, sgu_ln_b, sgu_w, sgu_b, w_out, ln1_g, ln1_b, w_up, conv_w, conv_b, w_down, ln2_g, ln2_b, loss_target, m_ada_w, m_ada_b, m_w_in, m_pool_w, m_pool_scale, m_sgu_ln_g, m_sgu_ln_b, m_sgu_w, m_sgu_b, m_w_out, m_ln1_g, m_ln1_b, m_w_up, m_conv_w, m_conv_b, m_w_down, m_ln2_g, m_ln2_b, v_ada_w, v_ada_b, v_w_in, v_pool_w, v_pool_scale, v_sgu_ln_g, v_sgu_ln_b, v_sgu_w, v_sgu_b, v_w_out, v_ln1_g, v_ln1_b, v_w_up, v_conv_w, v_conv_b, v_w_down, v_ln2_g, v_ln2_b):
    nl = ada_w.shape[0]
    alpha = (2.0 * nl) ** 0.25
    me = _me()
    x2 = x[0]
    tgt = loss_target[0]
    acols = ada_w.shape[2]
    icols = w_in.shape[2]
    ucols = w_up.shape[2]
    orows = w_out.shape[1]
    drows = w_down.shape[1]
    ccols = conv_w.shape[2]

    winT_sh = jnp.swapaxes(w_in, 1, 2).astype(BF16)
    wupT_sh = jnp.swapaxes(w_up, 1, 2).astype(BF16)
    wout_sh = w_out.astype(BF16)
    wd_sh = w_down.astype(BF16)

    pay = _Payload()
    pay.gather(jnp.broadcast_to(c, (8, D)))
    pay.gather(winT_sh, 0)
    pay.gather(wout_sh, 0)
    pay.gather(conv_w)
    (c_g, winT_g, wout_g, cw_g), _ = _comm_call(pay, "gather_first")
    c_all = c_g[:, 0, :]
    winT = winT_g.reshape(DIN, D)
    wout = wout_g.reshape(D, D)
    cw_full = jnp.transpose(cw_g, (1, 2, 0, 3)).reshape(nl, 3, FF)
    cb_full = conv_b[:, None, :]

    ada_b_my = lax.dynamic_slice(ada_b, (0, me * acols), (nl, acols))[:, None, :]
    mod_blk = _ada_fwd_call(c_all, ada_w, ada_b_my)
    pay = _Payload()
    pay.gather(mod_blk)
    (mod_g,), _ = _comm_call(pay, "gather_mod")
    mod_me = lax.dynamic_index_in_dim(mod_g, me, axis=2, keepdims=False)
    modv = jnp.swapaxes(mod_me, 0, 1).reshape(nl, 6, D)
    modv = jnp.concatenate([modv, jnp.zeros((nl, 2, D), F32)], axis=1)

    ln1 = jnp.stack([ln1_g, ln1_b], axis=1)
    ln2 = jnp.stack([ln2_g, ln2_b], axis=1)
    sln = jnp.stack([sgu_ln_g, sgu_ln_b], axis=1)
    sbf = jnp.broadcast_to(sgu_b[..., None], sgu_b.shape + (GW,))
    small = (pool_w, pool_scale[:, None, :], sln, sgu_w, sbf)

    r1s, r2s, f1s, f2s, ahs, ghs, wins, wouts, wups, wds = ([None] * nl for _ in range(10))
    wins[0], wouts[0] = winT, wout
    xin = x2
    for l in range(nl):
        pay = None
        if l == 0:
            pay = _Payload()
            pay.gather(wupT_sh, 0)
            pay.gather(wd_sh, 0)
        (r1, f1, *ah), new, _ = _fa_call(l, l == 0, alpha, xin, modv, ln2, wins[l], wouts[l], small, pay)
        if l == 0:
            wups[0], wds[0] = new[0].reshape(4, FH, D), new[1].reshape(2, FH, D)
        pay = None
        if l + 1 < nl:
            pay = _Payload()
            pay.gather(winT_sh, l + 1)
            pay.gather(wout_sh, l + 1)
            pay.gather(wupT_sh, l + 1)
            pay.gather(wd_sh, l + 1)
        (r2, f2, *saved_b), new, _ = _fb_call(l, alpha, r1, modv, ln1, wups[l], wds[l], cw_full, cb_full, pay)
        if l + 1 < nl:
            wins[l + 1], wouts[l + 1] = new[0].reshape(DIN, D), new[1].reshape(D, D)
            wups[l + 1], wds[l + 1] = new[2].reshape(4, FH, D), new[3].reshape(2, FH, D)
        r1s[l], r2s[l], f1s[l], f2s[l], ahs[l], ghs[l] = r1, r2, f1, f2, ah, saved_b
        xin = r2

    nsmall = 6 * D + 2 * NG * GW * GW + 3 * DP + NG * GW + 4 * D + 4 * FF
    srows = nsmall // 1024
    buf_in = lax.empty((nl, NDEV, icols, D), BF16)
    buf_out = lax.empty((nl, NDEV, orows, D), BF16)
    buf_up = lax.empty((nl, NDEV, ucols, D), BF16)
    buf_down = lax.empty((nl, NDEV, drows, D), BF16)
    buf_small = lax.empty((nl, NDEV, srows, 1024), F32)

    def a_side_payload(l, dwin, dwout, flat):
        p = _Payload()
        p.exchange_into([(dwin.reshape(NDEV, icols, D), 0)], buf_in, (l,))
        p.exchange_into([(dwout.reshape(NDEV, orows, D), 0)], buf_out, (l,))
        p.gather_into(flat, buf_small, (l,))
        return p

    dx = tgt
    loss_acc = None
    pending = None
    for l in reversed(range(nl)):
        last = l == nl - 1
        pay = None if pending is None else a_side_payload(*pending)
        h2, gs, vs, x1s = ghs[l]
        res0, _, al = _bb0_call(l, last, h2, r2s[l], dx, f2s[l], gs, vs, modv, ln2, wups[l], wds[l], cw_full,
                                cb_full, pay)
        if pending is not None:
            buf_in, buf_out, buf_small = al
        dh2p, df2, dr2, dwg0, dwv0, dwd0, vb0, cacc0 = res0[:8]
        if last:
            loss_acc = res0[8]
        (dx1, dwg1, dwv1, dwd1, vb1, cacc1), _, _ = _bb1_call(
            l, alpha, h2, x1s, df2, dr2, dh2p, gs, vs, modv, wups[l], wds[l], cw_full, cb_full)
        dwg0, dwv0, dwd0, dwg1, dwv1, dwd1 = _cast_call([dwg0, dwv0, dwd0, dwg1, dwv1, dwd1], f"cast{l}")
        pay = _Payload()
        pay.exchange_into([(dwg0.reshape(2, ucols, D), 0), (dwg1.reshape(2, ucols, D), 2),
                           (dwv0.reshape(2, ucols, D), 4), (dwv1.reshape(2, ucols, D), 6)], buf_up, (l,))
        pay.exchange_into([(dwd0.reshape(4, drows, D), 0), (dwd1.reshape(4, drows, D), 4)], buf_down, (l,))
        (dx, dwin, dwout, dpw, dsw, dsb, va, va5), _, (buf_up, buf_down) = _ba_call(
            l, alpha, r1s[l], dx1, f1s[l], *ahs[l], modv, ln1, wins[l], wouts[l], small, pay)
        cacc = jnp.concatenate([cacc0, cacc1], axis=1)
        piece = dict(ada_b=jnp.stack([va[3], va[4], va[2], vb1[0], vb1[1], vb0[2]]), pool_w=dpw,
                     pool_scale=va5[0], sgu_ln_g=va5[1], sgu_ln_b=va5[2], sgu_w=dsw, sgu_b=dsb[:, :, 0],
                     ln1_g=va[0], ln1_b=va[1], conv_w=cacc[1:4], conv_b=cacc[0], ln2_g=vb0[0], ln2_b=vb0[1])
        flat = jnp.concatenate([piece[n].reshape(-1) for n in _SMALL_ORDER]).reshape(srows, 1024)
        pending = (l, dwin, dwout, flat)
    grad_x = dx[None]
    loss = lax.psum((0.5 / D) * jnp.sum(loss_acc), ("x", "y", "c"))

    res = {}
    _, dwin0, dwout0, flat0 = pending
    t_ = lambda a: jnp.swapaxes(a, 1, 2)
    pay = _Payload()
    pay.exchange_into([(dwout0.reshape(NDEV, orows, D), 0)], buf_out, (0,))
    pay.gather_into(flat0, buf_small, (0,))
    outs, (buf_out, buf_small) = _adamw_call(buf_up, t_(w_up), t_(m_w_up), t_(v_w_up), "adamw_w_up", pay)
    res["w_up"] = tuple(t_(o) for o in outs)
    pay = _Payload()
    pay.exchange_into([(dwin0.reshape(NDEV, icols, D), 0)], buf_in, (0,))
    res["w_down"], (buf_in,) = _adamw_call(buf_down, w_down, m_w_down, v_w_down, "adamw_w_down", pay)
    outs, _ = _adamw_call(buf_in, t_(w_in), t_(m_w_in), t_(v_w_in), "adamw_w_in")
    res["w_in"] = tuple(t_(o) for o in outs)
    res["w_out"], _ = _adamw_call(buf_out, w_out, m_w_out, v_w_out, "adamw_w_out")

    weights = dict(ada_b=ada_b, pool_w=pool_w, pool_scale=pool_scale, sgu_ln_g=sgu_ln_g, sgu_ln_b=sgu_ln_b,
                   sgu_w=sgu_w, sgu_b=sgu_b, ln1_g=ln1_g, ln1_b=ln1_b, conv_w=conv_w, conv_b=conv_b, ln2_g=ln2_g,
                   ln2_b=ln2_b)
    m_of = dict(ada_b=m_ada_b, pool_w=m_pool_w, pool_scale=m_pool_scale, sgu_ln_g=m_sgu_ln_g, sgu_ln_b=m_sgu_ln_b,
                sgu_w=m_sgu_w, sgu_b=m_sgu_b, ln1_g=m_ln1_g, ln1_b=m_ln1_b, conv_w=m_conv_w, conv_b=m_conv_b,
                ln2_g=m_ln2_g, ln2_b=m_ln2_b)
    v_of = dict(ada_b=v_ada_b, pool_w=v_pool_w, pool_scale=v_pool_scale, sgu_ln_g=v_sgu_ln_g, sgu_ln_b=v_sgu_ln_b,
                sgu_w=v_sgu_w, sgu_b=v_sgu_b, ln1_g=v_ln1_g, ln1_b=v_ln1_b, conv_w=v_conv_w, conv_b=v_conv_b,
                ln2_g=v_ln2_g, ln2_b=v_ln2_b)
    row = 0
    dmod_my = None
    tail = None
    for name in _SMALL_ORDER:
        w = weights[name]
        per_layer = FF * 3 if name == "conv_w" else w[0].size
        if per_layer % 1024 == 0:
            nrow = per_layer // 1024
            parts = buf_small[:, :, row:row + nrow, :]
            row += nrow
            if name == "ada_b":
                dmod_my = lax.dynamic_slice_in_dim(parts.reshape(nl, NDEV, 6 * D), me * acols, acols, axis=2)
        else:
            if tail is None:
                tail = buf_small[:, :, row:, :].reshape(nl, NDEV, (srows - row) * 1024)
                off = 0
            parts = tail[:, :, off:off + per_layer]
            off += per_layer
        if name == "conv_w":
            parts = lax.dynamic_slice_in_dim(parts.reshape(nl, NDEV, 3, FF), me * ccols, ccols, axis=3)
        cdim = w.shape[-1]
        w3 = w.reshape(nl, -1, cdim)
        outs, _ = _adamw_call(parts.reshape(nl, NDEV, -1, cdim), w3, m_of[name].reshape(w3.shape),
                              v_of[name].reshape(w3.shape), f"adamw_{name}")
        res[name] = tuple(o.reshape(w.shape) for o in outs)

    res["ada_w"] = _ada_bwd_call(jnp.swapaxes(c_all, 0, 1), dmod_my, ada_w, m_ada_w, v_ada_w)

    order = ["ada_w", "ada_b", "w_in", "pool_w", "pool_scale", "sgu_ln_g", "sgu_ln_b", "sgu_w", "sgu_b", "w_out",
             "ln1_g", "ln1_b", "w_up", "conv_w", "conv_b", "w_down", "ln2_g", "ln2_b"]
    out = [loss, grad_x]
    for k in range(4):
        out += [res[n][k] for n in order]
    return tuple(out)
```

```python
import jax
import jax.numpy as jnp
from jax import lax
from jax.experimental import pallas as pl
from jax.experimental.pallas import tpu as pltpu

F32 = jnp.float32
BF16 = jnp.bfloat16

NDEV = 8
D = 1024
DP = 512
DS = 512
DIN = DP + 2 * DS
FF = 2816
FH = FF // 2
FF_CHUNKS = ((0, 1536), (1536, FF))
NG = 4
GW = 128
WINDOWS = (2, 4, 8, 16)
AHALO = 16
GHALO = 8
LN_EPS = 1e-5
ADAM_LR, ADAM_B1, ADAM_B2, ADAM_EPS, ADAM_WD, ADAM_STEP = 0.001, 0.9, 0.999, 1e-08, 0.01, 10
TSF = 512
TSB = 256
_K0 = 0.7978845608028654
_K1 = 0.044715
MESH_ID = pl.DeviceIdType.MESH


def _mm(a, b):
    return jnp.dot(a, b, preferred_element_type=F32)


def _mm_nt(a, b):
    return lax.dot_general(a, b, (((1,), (1,)), ((), ())), preferred_element_type=F32)


def _mm_tn(a, b):
    return lax.dot_general(a, b, (((0,), (0,)), ((), ())), preferred_element_type=F32)


def _rowmean(x):
    return jnp.mean(x, axis=-1, keepdims=True)


def _ln_stats(x):
    mu = _rowmean(x)
    xc = x - mu
    rstd = lax.rsqrt(_rowmean(xc * xc) + LN_EPS)
    return xc * rstd, rstd


def _ln_stats_tile(x):
    mu = _rowmean(x)
    xc = x - mu
    rstd = lax.rsqrt(_rowmean(xc * xc) + LN_EPS)
    lane = lax.broadcasted_iota(jnp.int32, (x.shape[0], 128), 1)
    return xc * rstd, jnp.where(lane == 0, mu, jnp.where(lane == 1, rstd, 0.0))


def _ln_from_tile(x, st):
    rstd = st[:, 1:2]
    return (x - st[:, 0:1]) * rstd, rstd


def _ln_bwd(dy, gamma, xhat, rstd):
    dxh = dy * gamma
    return rstd * (dxh - _rowmean(dxh) - xhat * _rowmean(dxh * xhat))


def _gelu_t(x):
    t = jnp.tanh(x * (_K0 + (_K0 * _K1) * (x * x)))
    hx = 0.5 * x
    return hx + hx * t, t


def _dgelu(x, t):
    return (0.5 + 0.5 * t) + (0.5 * x) * (1.0 - t * t) * (_K0 + (3.0 * _K0 * _K1) * (x * x))


def _colsum8(x):
    t, n = x.shape
    return jnp.sum(x.reshape(t // 8, 8, n), axis=0)


def _tril_mask():
    r = lax.broadcasted_iota(jnp.int32, (GW, GW), 0)
    c = lax.broadcasted_iota(jnp.int32, (GW, GW), 1)
    return c <= r


def _full(shape):
    n = len(shape)
    return pl.BlockSpec(shape, lambda *_: (0,) * n)


def _resident(tail, lead=()):
    n = len(tail)
    return pl.BlockSpec((None,) * len(lead) + tuple(tail), lambda *_: tuple(lead) + (0,) * n,
                        pipeline_mode=pl.Buffered(1))


def _layer_vec(rows, width, l):
    return pl.BlockSpec((None, rows, width), lambda *_: (l, 0, 0))


_VMEM_WHOLE = pl.BlockSpec(memory_space=pltpu.VMEM)
_HBM = pl.BlockSpec(memory_space=pl.ANY)
_ARB = pltpu.CompilerParams(dimension_semantics=("arbitrary",))


def _me():
    return 4 * lax.axis_index("x") + 2 * lax.axis_index("y") + lax.axis_index("c")


def _coords(p):
    return (p >> 2, (p >> 1) & 1, p & 1)


class _Payload:
    def __init__(self):
        self.srcs, self.new, self.alias, self.transfers = [], [], [], []

    def _src(self, arr):
        self.srcs.append(arr)
        return len(self.srcs) - 1

    def _alias(self, buf):
        self.alias.append(buf)
        return len(self.alias) - 1

    def gather(self, arr, chunk=None):
        pos = self._src(arr)
        blk = arr.shape if chunk is None else arr.shape[1:]
        self.new.append(jax.ShapeDtypeStruct((NDEV,) + tuple(blk), arr.dtype))
        self.transfers.append(([(pos, chunk)] * NDEV, ("new", len(self.new) - 1), ()))
        return len(self.new) - 1

    def gather_into(self, arr, buf, lead):
        pos = self._src(arr)
        self.transfers.append(([(pos, None)] * NDEV, ("alias", self._alias(buf)), tuple(lead)))

    def exchange_into(self, parts, buf, lead):
        route = {}
        for arr, first in parts:
            pos = self._src(arr)
            for q in range(arr.shape[0]):
                route[first + q] = (pos, q)
        self.transfers.append(([route[p] for p in range(NDEV)], ("alias", self._alias(buf)), tuple(lead)))

    def _ends(self, t, io, dst_dev, src_dev):
        srcs, new_out, alias_out = io
        route, (kind, k), lead = self.transfers[t]
        pos, q = route[dst_dev]
        src = srcs[pos] if q is None else srcs[pos].at[q]
        buf = new_out[k] if kind == "new" else alias_out[k]
        return src, buf.at[lead + (src_dev,)]

    def _remote(self, t, io, sems, src_dev, dst_dev):
        src, dst = self._ends(t, io, dst_dev, src_dev)
        return pltpu.make_async_remote_copy(
            src_ref=src, dst_ref=dst, send_sem=sems[0].at[t, dst_dev], recv_sem=sems[1].at[t, src_dev],
            device_id=_coords(dst_dev), device_id_type=MESH_ID)

    def _local(self, t, io, sems, p):
        src, dst = self._ends(t, io, p, p)
        return pltpu.make_async_copy(src, dst, sems[2].at[t])

    def start(self, io, sems):
        me = _me()
        for p in range(NDEV):
            @pl.when(me == p)
            def _():
                for t in range(len(self.transfers)):
                    self._local(t, io, sems, p).start()

            @pl.when(me != p)
            def _():
                for t in range(len(self.transfers)):
                    self._remote(t, io, sems, me, p).start()

    def wait(self, io, sems):
        me = _me()
        for p in range(NDEV):
            @pl.when(me == p)
            def _():
                for t in range(len(self.transfers)):
                    self._local(t, io, sems, p).wait()

            @pl.when(me != p)
            def _():
                for t in range(len(self.transfers)):
                    self._remote(t, io, sems, p, p).wait_recv()
                    self._remote(t, io, sems, me, p).wait_send()


def _pcall(body, name, nsteps, in_specs, out_specs, out_shape, scratch, args, pay=None):
    n_in, n_out, n_scr = len(args), len(out_shape), len(scratch)
    if pay is None:
        res = pl.pallas_call(body, name=name, grid=(nsteps,), in_specs=list(in_specs), out_specs=list(out_specs),
                             out_shape=list(out_shape), scratch_shapes=list(scratch), compiler_params=_ARB)(*args)
        return list(res), [], []
    ns, nn, na, nt = len(pay.srcs), len(pay.new), len(pay.alias), len(pay.transfers)

    def full(*refs):
        cin = refs[:n_in]
        srcs = refs[n_in:n_in + ns]
        o0 = n_in + ns + na
        cout = refs[o0:o0 + n_out]
        new_out = refs[o0 + n_out:o0 + n_out + nn]
        alias_out = refs[o0 + n_out + nn:o0 + n_out + nn + na]
        s0 = o0 + n_out + nn + na
        cscr = refs[s0:s0 + n_scr]
        sems = refs[s0 + n_scr:]
        io = (srcs, new_out, alias_out)
        i = pl.program_id(0)

        @pl.when(i == 0)
        def _():
            pay.start(io, sems)

        body(*cin, *cout, *cscr)

        @pl.when(i == nsteps - 1)
        def _():
            pay.wait(io, sems)

    res = pl.pallas_call(
        full, name=name, grid=(nsteps,),
        in_specs=list(in_specs) + [_HBM] * (ns + na), out_specs=list(out_specs) + [_HBM] * (nn + na),
        out_shape=list(out_shape) + pay.new + [jax.ShapeDtypeStruct(b.shape, b.dtype) for b in pay.alias],
        input_output_aliases={n_in + ns + k: n_out + nn + k for k in range(na)},
        scratch_shapes=list(scratch) + [pltpu.SemaphoreType.DMA((nt, NDEV)), pltpu.SemaphoreType.DMA((nt, NDEV)),
                                        pltpu.SemaphoreType.DMA((nt,))],
        compiler_params=_ARB,
    )(*args, *pay.srcs, *pay.alias)
    return list(res[:n_out]), list(res[n_out:n_out + nn]), list(res[n_out + nn:])


def _comm_call(pay, name):
    def body():
        pass

    _, new, alias = _pcall(body, name, 1, [], [], [], [], [], pay)
    return new, alias


def _window_sums(x, halo, before):
    ts = x.shape[0]
    ext = jnp.concatenate([halo, x] if before else [x, halo], axis=0)
    n = ts + AHALO
    shift = (lambda k: k) if before else (lambda k: n - k)
    keep = slice(AHALO, n) if before else slice(0, ts)
    out = []
    s = ext
    for level in range(NG):
        s = s + pltpu.roll(s, shift(1 << level), 0)
        out.append(s[keep, 0:GW])
        if level + 1 < NG:
            s = s[:, GW:]
    return out


def _a_forward(a, u, v, halo, pw_ref, ps_ref, sln_ref, sw_ref, sbf_ref, zbuf, tile, ts):
    tglob = tile * ts + lax.broadcasted_iota(jnp.int32, (ts, 1), 0)
    sums = _window_sums(a, halo, True)
    pooled_b, mixed, inv_cnt, pwb = [], [], [], []
    for g, w in enumerate(WINDOWS):
        a_g = a[:, g * GW:(g + 1) * GW]
        s = sums[g]
        inv = 1.0 / jnp.minimum(tglob + 1, w).astype(F32)
        pg = (s * inv - a_g).astype(BF16)
        wg = pw_ref[g].astype(BF16)
        pooled_b.append(pg)
        inv_cnt.append(inv)
        pwb.append(wg)
        mixed.append(_mm(pg, wg))
    mixed = jnp.concatenate(mixed, axis=1)
    ya = mixed * ps_ref[...]
    ug, tu = _gelu_t(u)
    vg, tv = _gelu_t(v)
    vhat, rstdv = _ln_stats(vg)
    vnb = (vhat * sln_ref[0:1, :] + sln_ref[1:2, :]).astype(BF16)
    tri = _tril_mask()
    wt = [jnp.where(tri, sw_ref[h], 0.0).astype(BF16) for h in range(NG)]
    for c in range(ts // GW):
        rs = slice(c * GW, (c + 1) * GW)
        for h in range(NG):
            cs = slice(h * GW, (h + 1) * GW)
            zbuf[rs, cs] = _mm(wt[h], vnb[rs, cs]) + sbf_ref[h]
    z = zbuf[...]
    yb = ug * z
    return dict(u=u, v=v, tu=tu, tv=tv, ug=ug, z=z, vhat=vhat, rstdv=rstdv, vnb=vnb,
                wt=wt, pooled_b=pooled_b, pwb=pwb, inv_cnt=inv_cnt, mixed=mixed, ya=ya, yb=yb)


def _small_specs(l):
    grp = pl.BlockSpec((None, NG, GW, GW), lambda i: (l, 0, 0, 0))
    return [grp, _layer_vec(1, DP, l), _layer_vec(2, DS, l), grp, grp]


def _fa_call(l, first, alpha, xin, modv, lnp, winT, wout, small, pay=None):
    s = xin.shape[0]
    nt = s // TSF

    def body(xin_ref, modv_ref, lnp_ref, winT_ref, wout_ref, pw_ref, ps_ref, sln_ref, sw_ref, sbf_ref,
             r1_ref, f1_ref, h_ref, proj_ref, x_ref, st_ref, acarry, zbuf, mixbuf):
        i = pl.program_id(0)

        @pl.when(i == 0)
        def _():
            acarry[...] = jnp.zeros_like(acarry)

        x = xin_ref[...]
        if first:
            st_ref[...] = jnp.zeros_like(st_ref)
        else:
            xhat, st_ref[...] = _ln_stats_tile(x)
            x = xhat * lnp_ref[0:1, :] + lnp_ref[1:2, :]
        hb = (x * (1.0 + modv_ref[1:2, :]) + modv_ref[0:1, :]).astype(BF16)
        proj = _mm_nt(hb, winT_ref[...])
        a = proj[:, 0:DP]
        fw = _a_forward(a, proj[:, DP:DP + DS], proj[:, DP + DS:], acarry[...], pw_ref, ps_ref, sln_ref, sw_ref,
                        sbf_ref, zbuf, i, TSF)
        acarry[...] = a[TSF - AHALO:, :]
        mixbuf[:, 0:DP] = fw["ya"].astype(BF16)
        mixbuf[:, DP:] = fw["yb"].astype(BF16)
        f = _mm(mixbuf[...], wout_ref[...])
        r1_ref[...] = alpha * x + modv_ref[2:3, :] * f
        f1_ref[...] = f.astype(BF16)
        h_ref[...] = hb
        proj_ref[...] = proj.astype(BF16)
        x_ref[...] = x.astype(BF16)

    tile = pl.BlockSpec((TSF, D), lambda i: (i, 0))
    return _pcall(
        body, f"fa{l}", nt,
        in_specs=[tile, _layer_vec(8, D, l), _layer_vec(2, D, max(l - 1, 0)), _resident((DIN, D)),
                  _resident((D, D))] + _small_specs(l),
        out_specs=[tile, tile, tile, pl.BlockSpec((TSF, DIN), lambda i: (i, 0)), tile,
                   pl.BlockSpec((TSF, 128), lambda i: (i, 0))],
        out_shape=[jax.ShapeDtypeStruct((s, D), F32), jax.ShapeDtypeStruct((s, D), BF16),
                   jax.ShapeDtypeStruct((s, D), BF16), jax.ShapeDtypeStruct((s, DIN), BF16),
                   jax.ShapeDtypeStruct((s, D), BF16), jax.ShapeDtypeStruct((s, 128), F32)],
        scratch=[pltpu.VMEM((AHALO, DP), F32), pltpu.VMEM((TSF, DS), F32), pltpu.VMEM((TSF, D), BF16)],
        args=[xin, modv, lnp, winT, wout, *small], pay=pay)


def _ba_call(l, alpha, r1, st1, dx1, f1, hsave, proj, xsave, modv, ln1, winT, wout, small, pay=None):
    s = r1.shape[0]
    nt = s // TSB
    ts = TSB

    def body(r1_ref, st1_ref, dx1_ref, f1_ref, h_ref, proj_ref, xs_ref, ah_ref, modv_ref, ln1_ref, winT_ref,
             wout_ref, pw_ref, ps_ref, sln_ref, sw_ref, sbf_ref,
             dx_ref, dwin_ref, dwout_ref, dpw_ref, dsw_ref, dsb_ref, va_ref, va5_ref,
             qnext, zbuf, dvnbuf, mixbuf, dpbuf, dbacc, vacc, vacc5, dwin_acc, dwout_acc):
        i = pl.program_id(0)
        j = nt - 1 - i

        @pl.when(i == 0)
        def _():
            dwin_acc[...] = jnp.zeros_like(dwin_acc)
            dwout_acc[...] = jnp.zeros_like(dwout_acc)
            dpw_ref[...] = jnp.zeros_like(dpw_ref)
            dsw_ref[...] = jnp.zeros_like(dsw_ref)
            dbacc[...] = jnp.zeros_like(dbacc)
            vacc[...] = jnp.zeros_like(vacc)
            vacc5[...] = jnp.zeros_like(vacc5)
            qnext[...] = jnp.zeros_like(qnext)

        hb = h_ref[...]
        sc1 = modv_ref[1:2, :]
        halo = jnp.where(j > 0, ah_ref[...].astype(F32), 0.0)
        fw = _a_forward(proj_ref[:, 0:DP].astype(F32), proj_ref[:, DP:DP + DS].astype(F32),
                        proj_ref[:, DP + DS:].astype(F32), halo, pw_ref, ps_ref, sln_ref, sw_ref, sbf_ref, zbuf, j, ts)
        mixbuf[:, 0:DP] = fw["ya"].astype(BF16)
        mixbuf[:, DP:] = fw["yb"].astype(BF16)

        xhat1, rstd1 = _ln_from_tile(r1_ref[...], st1_ref[...])
        dy = dx1_ref[...]
        vacc[0] += _colsum8(dy * xhat1)
        vacc[1] += _colsum8(dy)
        dr1 = _ln_bwd(dy, ln1_ref[0:1, :], xhat1, rstd1)
        vacc[2] += _colsum8(dr1 * f1_ref[...].astype(F32))
        dfb = (dr1 * modv_ref[2:3, :]).astype(BF16)
        dwout_acc[...] += _mm_tn(mixbuf[...], dfb)
        dmix = _mm_nt(dfb, wout_ref[...])
        dya = dmix[:, 0:DP]
        dyb = dmix[:, DP:]

        vacc5[0] += _colsum8(dya * fw["mixed"])
        dmixed = (dya * ps_ref[...]).astype(BF16)
        dpooled, q = [], []
        for g in range(NG):
            cs = slice(g * GW, (g + 1) * GW)
            dpw_ref[g] += _mm_tn(fw["pooled_b"][g], dmixed[:, cs])
            dpg = _mm_nt(dmixed[:, cs], fw["pwb"][g])
            dpooled.append(dpg)
            q.append(dpg * fw["inv_cnt"][g])
        q = jnp.concatenate(q, axis=1)
        sums = _window_sums(q, qnext[...], False)
        qnext[...] = q[0:AHALO, :]
        for g in range(NG):
            dpbuf[:, g * GW:(g + 1) * GW] = (sums[g] - dpooled[g]).astype(BF16)

        dug = dyb * fw["z"]
        dz = dyb * fw["ug"]
        dzb = dz.astype(BF16)
        for c in range(ts // GW):
            rs = slice(c * GW, (c + 1) * GW)
            for h in range(NG):
                cs = slice(h * GW, (h + 1) * GW)
                dvnbuf[rs, cs] = _mm_tn(fw["wt"][h], dzb[rs, cs])
                dsw_ref[h] += _mm_nt(dzb[rs, cs], fw["vnb"][rs, cs])
            dbacc[...] += dz[rs, :]
        dvn = dvnbuf[...]
        vacc5[1] += _colsum8(dvn * fw["vhat"])
        vacc5[2] += _colsum8(dvn)
        dvg = _ln_bwd(dvn, sln_ref[0:1, :], fw["vhat"], fw["rstdv"])
        dpbuf[:, DP:DP + DS] = (dug * _dgelu(fw["u"], fw["tu"])).astype(BF16)
        dpbuf[:, DP + DS:] = (dvg * _dgelu(fw["v"], fw["tv"])).astype(BF16)

        dpb = dpbuf[...]
        dwin_acc[...] += _mm_tn(dpb, hb)
        dh = _mm(dpb, winT_ref[...])
        dx_ref[...] = dh * (1.0 + sc1) + alpha * dr1
        vacc[3] += _colsum8(dh)
        vacc[4] += _colsum8(dh * xs_ref[...].astype(F32))

        @pl.when(i == nt - 1)
        def _():
            dwin_ref[...] = dwin_acc[...].astype(BF16)
            dwout_ref[...] = dwout_acc[...].astype(BF16)
            tri = _tril_mask()
            for h in range(NG):
                dsw_ref[h] = jnp.where(tri, dsw_ref[h], 0.0)
                sb = jnp.sum(dbacc[:, h * GW:(h + 1) * GW], axis=1, keepdims=True)
                dsb_ref[h] = jnp.broadcast_to(sb, (GW, GW))
            for n in range(5):
                va_ref[n:n + 1, :] = jnp.sum(vacc[n], axis=0, keepdims=True)
            for n in range(3):
                va5_ref[n:n + 1, :] = jnp.sum(vacc5[n], axis=0, keepdims=True)

    rev = lambda i: (nt - 1 - i, 0)
    tile = pl.BlockSpec((ts, D), rev)
    return _pcall(
        body, f"ba{l}", nt,
        in_specs=[tile, pl.BlockSpec((ts, 128), rev), tile, tile, tile, pl.BlockSpec((ts, DIN), rev), tile,
                  pl.BlockSpec((AHALO, DP), lambda i: (jnp.maximum((nt - 1 - i) * (ts // AHALO) - 1, 0), 0)),
                  _layer_vec(8, D, l), _layer_vec(2, D, l),
                  _resident((DIN, D)), _resident((D, D))] + _small_specs(l),
        out_specs=[tile] + [_VMEM_WHOLE] * 7,
        out_shape=[jax.ShapeDtypeStruct((s, D), F32), jax.ShapeDtypeStruct((DIN, D), BF16),
                   jax.ShapeDtypeStruct((D, D), BF16), jax.ShapeDtypeStruct((NG, GW, GW), F32),
                   jax.ShapeDtypeStruct((NG, GW, GW), F32), jax.ShapeDtypeStruct((NG, GW, GW), F32),
                   jax.ShapeDtypeStruct((5, D), F32), jax.ShapeDtypeStruct((3, DP), F32)],
        scratch=[pltpu.VMEM((AHALO, DP), F32),
                 pltpu.VMEM((ts, DS), F32), pltpu.VMEM((ts, DS), F32), pltpu.VMEM((ts, D), BF16),
                 pltpu.VMEM((ts, DIN), BF16), pltpu.VMEM((GW, DS), F32), pltpu.VMEM((5, 8, D), F32),
                 pltpu.VMEM((3, 8, DP), F32), pltpu.VMEM((DIN, D), F32), pltpu.VMEM((D, D), F32)],
        args=[r1, st1, dx1, f1, hsave, proj, xsave, proj, modv, ln1, winT, wout, *small], pay=pay)


def _cast_call(arrs, name):
    r, c = arrs[0].shape
    rb = _row_block(r, 6 * c * len(arrs))

    def body(*refs):
        for src, dst in zip(refs[:len(arrs)], refs[len(arrs):]):
            dst[...] = src[...].astype(BF16)

    blk = pl.BlockSpec((rb, c), lambda i: (i, 0))
    return pl.pallas_call(
        body, name=name, grid=(r // rb,), in_specs=[blk] * len(arrs), out_specs=[blk] * len(arrs),
        out_shape=[jax.ShapeDtypeStruct((r, c), BF16)] * len(arrs), compiler_params=_ARB)(*arrs)


def _rows_before(halo, x):
    ext = jnp.concatenate([halo, x], axis=0)
    return pltpu.roll(ext, 1, 0)[GHALO:, :], pltpu.roll(ext, 2, 0)[GHALO:, :]


def _rows_after(x, halo):
    ts = x.shape[0]
    ext = jnp.concatenate([x, halo], axis=0)
    n = ts + GHALO
    return pltpu.roll(ext, n - 1, 0)[0:ts, :], pltpu.roll(ext, n - 2, 0)[0:ts, :]


def _fb_call(l, alpha, r1, modv, ln1, wup4, wd2, conv_w, conv_b, pay=None):
    s = r1.shape[0]
    nt = s // TSF

    def body(r1_ref, modv_ref, ln1_ref, wup_ref, wd_ref, cw_ref, cb_ref, r2_ref, f2_ref, h2_ref, gs_ref, vs_ref,
             x1_ref, st_ref, gbuf):
        i = pl.program_id(0)

        @pl.when(i == 0)
        def _():
            gbuf[...] = jnp.zeros_like(gbuf)

        xhat1, st_ref[...] = _ln_stats_tile(r1_ref[...])
        x1 = xhat1 * ln1_ref[0:1, :] + ln1_ref[1:2, :]
        h2b = (x1 * (1.0 + modv_ref[4:5, :]) + modv_ref[3:4, :]).astype(BF16)
        f2 = jnp.zeros((TSF, D), F32)
        for c0, c1 in FF_CHUNKS:
            cs = slice(c0, c1)
            g = _mm_nt(h2b, wup_ref[0, cs, :])
            val = _mm_nt(h2b, wup_ref[1, cs, :])
            gm1, gm2 = _rows_before(gbuf[:, cs], g)
            cw = cw_ref[:, cs]
            gc = cb_ref[:, cs] + cw[0:1, :] * gm2 + cw[1:2, :] * gm1 + cw[2:3, :] * g
            ge, _ = _gelu_t(gc)
            f2 = f2 + _mm((ge * val).astype(BF16), wd_ref[cs, :])
            gs_ref[:, cs] = g.astype(BF16)
            vs_ref[:, cs] = val.astype(BF16)
            gbuf[:, cs] = g[TSF - GHALO:, :]
        r2_ref[...] = alpha * x1 + modv_ref[5:6, :] * f2
        f2_ref[...] = f2.astype(BF16)
        h2_ref[...] = h2b
        x1_ref[...] = x1.astype(BF16)

    tile = pl.BlockSpec((TSF, D), lambda i: (i, 0))
    wide = pl.BlockSpec((TSF, FF), lambda i: (i, 0))
    return _pcall(
        body, f"fb{l}", nt,
        in_specs=[tile, _layer_vec(8, D, l), _layer_vec(2, D, l), _resident((2, FF, D)), _resident((FF, D)),
                  _layer_vec(3, FF, l), _layer_vec(1, FF, l)],
        out_specs=[tile, tile, tile, wide, wide, tile, pl.BlockSpec((TSF, 128), lambda i: (i, 0))],
        out_shape=[jax.ShapeDtypeStruct((s, D), F32), jax.ShapeDtypeStruct((s, D), BF16),
                   jax.ShapeDtypeStruct((s, D), BF16), jax.ShapeDtypeStruct((s, FF), BF16),
                   jax.ShapeDtypeStruct((s, FF), BF16), jax.ShapeDtypeStruct((s, D), BF16),
                   jax.ShapeDtypeStruct((s, 128), F32)],
        scratch=[pltpu.VMEM((GHALO, FF), F32)],
        args=[r1, modv, ln1, wup4.reshape(2, FF, D), wd2.reshape(FF, D), conv_w, conv_b], pay=pay)


def _ff_backward(h2b, df2b, gs_ref, vs_ref, gh_ref, first_tile, dgc_next, cw, cb, wg_ref, wv_ref, wd_ref,
                 dwg_ref, dwv_ref, dwd_ref, cacc):
    g = gs_ref[...].astype(F32)
    val = vs_ref[...].astype(F32)
    halo = gh_ref[...].astype(F32)[GHALO:, :]
    gm1, gm2 = _rows_before(jnp.where(first_tile, 0.0, halo), g)
    gc = cb + cw[0:1, :] * gm2 + cw[1:2, :] * gm1 + cw[2:3, :] * g
    ge, tg = _gelu_t(gc)
    mb = (ge * val).astype(BF16)
    dm = _mm_nt(df2b, wd_ref[...])
    dwd_ref[...] += _mm_tn(mb, df2b)
    dval = dm * ge
    dgc = dm * val * _dgelu(gc, tg)
    cacc[0] += _colsum8(dgc)
    cacc[1] += _colsum8(dgc * gm2)
    cacc[2] += _colsum8(dgc * gm1)
    cacc[3] += _colsum8(dgc * g)
    dgp1, dgp2 = _rows_after(dgc, dgc_next[...])
    dg = cw[2:3, :] * dgc + cw[1:2, :] * dgp1 + cw[0:1, :] * dgp2
    dgc_next[...] = dgc[0:GHALO, :]
    dgb = dg.astype(BF16)
    dvb = dval.astype(BF16)
    dwg_ref[...] += _mm_tn(dgb, h2b)
    dwv_ref[...] += _mm_tn(dvb, h2b)
    return _mm(dgb, wg_ref[...]) + _mm(dvb, wv_ref[...])


def _bb_specs(l, hf, nt, ts):
    wide = pl.BlockSpec((ts, FH), lambda i: (nt - 1 - i, hf))
    halo = pl.BlockSpec((2 * GHALO, FH), lambda i: (jnp.maximum((nt - 1 - i) * (ts // (2 * GHALO)) - 1, 0), hf))
    ins = [wide, wide, halo, _resident((FH, D), (hf,)), _resident((FH, D), (2 + hf,)), _resident((FH, D), (hf,)),
           pl.BlockSpec((None, 3, FH), lambda i: (l, 0, hf)), pl.BlockSpec((None, 1, FH), lambda i: (l, 0, hf))]
    acc_shapes = [jax.ShapeDtypeStruct((FH, D), BF16)] * 3
    return ins, acc_shapes, [pltpu.VMEM((FH, D), F32)] * 3


def _bb0_call(l, last, h2, r2, st2, dx2, f2, gs, vs, modv, ln2, wup4, wd2, conv_w, conv_b, pay=None):
    s = r2.shape[0]
    nt = s // TSB
    ts = TSB

    def body(*refs):
        it = iter(refs)
        h2_ref, r2_ref, dx2_ref, f2_ref, gs_ref, vs_ref, gh_ref = (next(it) for _ in range(7))
        wg_ref, wv_ref, wd_ref, cw_ref, cb_ref, modv_ref, ln2_ref = (next(it) for _ in range(7))
        st2_ref = None if last else next(it)
        dh2_ref, df2_ref, dr2_ref, dwg_out, dwv_out, dwd_out, vb_ref, cacc_ref = (next(it) for _ in range(8))
        loss_ref = next(it) if last else None
        vacc, cacc, dgc_next, dwg_ref, dwv_ref, dwd_ref = (next(it) for _ in range(6))
        lacc = next(it) if last else None
        i = pl.program_id(0)

        @pl.when(i == 0)
        def _():
            dwg_ref[...] = jnp.zeros_like(dwg_ref)
            dwv_ref[...] = jnp.zeros_like(dwv_ref)
            dwd_ref[...] = jnp.zeros_like(dwd_ref)
            vacc[...] = jnp.zeros_like(vacc)
            cacc[...] = jnp.zeros_like(cacc)
            dgc_next[...] = jnp.zeros_like(dgc_next)
            if last:
                lacc[...] = jnp.zeros_like(lacc)

        if last:
            xhat2, rstd2 = _ln_stats(r2_ref[...])
        else:
            xhat2, rstd2 = _ln_from_tile(r2_ref[...], st2_ref[...])
        if last:
            diff = xhat2 * ln2_ref[0:1, :] + ln2_ref[1:2, :] - dx2_ref[...]
            dy = diff * (1.0 / D)
            lacc[...] += _colsum8(diff * diff)
        else:
            dy = dx2_ref[...]
        dr2 = _ln_bwd(dy, ln2_ref[0:1, :], xhat2, rstd2)
        vacc[0] += _colsum8(dy * xhat2)
        vacc[1] += _colsum8(dy)
        vacc[2] += _colsum8(dr2 * f2_ref[...].astype(F32))
        df2b = (dr2 * modv_ref[5:6, :]).astype(BF16)
        dr2_ref[...] = dr2
        df2_ref[...] = df2b
        dh2_ref[...] = _ff_backward(h2_ref[...], df2b, gs_ref, vs_ref, gh_ref, i == nt - 1, dgc_next, cw_ref[...],
                                    cb_ref[...], wg_ref, wv_ref, wd_ref, dwg_ref, dwv_ref, dwd_ref, cacc)

        @pl.when(i == nt - 1)
        def _():
            for n in range(3):
                vb_ref[n:n + 1, :] = jnp.sum(vacc[n], axis=0, keepdims=True)
            for n in range(4):
                cacc_ref[n:n + 1, :] = jnp.sum(cacc[n], axis=0, keepdims=True)
            dwg_out[...] = dwg_ref[...].astype(BF16)
            dwv_out[...] = dwv_ref[...].astype(BF16)
            dwd_out[...] = dwd_ref[...].astype(BF16)
            if last:
                loss_ref[...] = lacc[...]

    tile = pl.BlockSpec((ts, D), lambda i: (nt - 1 - i, 0))
    ff_ins, acc_shapes, acc_scratch = _bb_specs(l, 0, nt, ts)
    in_specs = [tile, tile, tile, tile] + ff_ins + [_layer_vec(8, D, l), _layer_vec(2, D, l)]
    args = [h2, r2, dx2, f2, gs, vs, gs, wup4, wup4, wd2, conv_w, conv_b, modv, ln2]
    if not last:
        in_specs.append(pl.BlockSpec((ts, 128), lambda i: (nt - 1 - i, 0)))
        args.append(st2)
    out_specs = [tile, tile, tile] + [_VMEM_WHOLE] * 5
    out_shape = [jax.ShapeDtypeStruct((s, D), F32), jax.ShapeDtypeStruct((s, D), BF16),
                 jax.ShapeDtypeStruct((s, D), F32)] + acc_shapes + [jax.ShapeDtypeStruct((3, D), F32),
                                                                    jax.ShapeDtypeStruct((4, FH), F32)]
    scratch = [pltpu.VMEM((3, 8, D), F32), pltpu.VMEM((4, 8, FH), F32), pltpu.VMEM((GHALO, FH), F32)] + acc_scratch
    if last:
        out_specs.append(_VMEM_WHOLE)
        out_shape.append(jax.ShapeDtypeStruct((8, D), F32))
        scratch.append(pltpu.VMEM((8, D), F32))
    return _pcall(body, f"bb{l}_0", nt, in_specs, out_specs, out_shape, scratch, args, pay=pay)


def _bb1_call(l, alpha, h2, x1s, df2, dr2, dh2_in, gs, vs, modv, wup4, wd2, conv_w, conv_b, pay=None):
    s = h2.shape[0]
    nt = s // TSB
    ts = TSB

    def body(h2_ref, x1_ref, df2_ref, dr2_ref, dh2_ref, gs_ref, vs_ref, gh_ref, wg_ref, wv_ref, wd_ref, cw_ref,
             cb_ref, modv_ref, dx1_ref, dwg_out, dwv_out, dwd_out, vb_ref, cacc_ref, vacc, cacc, dgc_next,
             dwg_ref, dwv_ref, dwd_ref):
        i = pl.program_id(0)

        @pl.when(i == 0)
        def _():
            dwg_ref[...] = jnp.zeros_like(dwg_ref)
            dwv_ref[...] = jnp.zeros_like(dwv_ref)
            dwd_ref[...] = jnp.zeros_like(dwd_ref)
            vacc[...] = jnp.zeros_like(vacc)
            cacc[...] = jnp.zeros_like(cacc)
            dgc_next[...] = jnp.zeros_like(dgc_next)

        dh2 = dh2_ref[...] + _ff_backward(h2_ref[...], df2_ref[...], gs_ref, vs_ref, gh_ref, i == nt - 1, dgc_next,
                                          cw_ref[...], cb_ref[...], wg_ref, wv_ref, wd_ref, dwg_ref, dwv_ref,
                                          dwd_ref, cacc)
        dx1_ref[...] = dh2 * (1.0 + modv_ref[4:5, :]) + alpha * dr2_ref[...]
        vacc[0] += _colsum8(dh2)
        vacc[1] += _colsum8(dh2 * x1_ref[...].astype(F32))

        @pl.when(i == nt - 1)
        def _():
            dwg_out[...] = dwg_ref[...].astype(BF16)
            dwv_out[...] = dwv_ref[...].astype(BF16)
            dwd_out[...] = dwd_ref[...].astype(BF16)
            for n in range(2):
                vb_ref[n:n + 1, :] = jnp.sum(vacc[n], axis=0, keepdims=True)
            for n in range(4):
                cacc_ref[n:n + 1, :] = jnp.sum(cacc[n], axis=0, keepdims=True)

    tile = pl.BlockSpec((ts, D), lambda i: (nt - 1 - i, 0))
    ff_ins, acc_shapes, acc_scratch = _bb_specs(l, 1, nt, ts)
    out_shape = [jax.ShapeDtypeStruct((s, D), F32)] + acc_shapes + [jax.ShapeDtypeStruct((2, D), F32),
                                                                   jax.ShapeDtypeStruct((4, FH), F32)]
    scratch = [pltpu.VMEM((2, 8, D), F32), pltpu.VMEM((4, 8, FH), F32), pltpu.VMEM((GHALO, FH), F32)] + acc_scratch
    return _pcall(body, f"bb{l}_1", nt, [tile] * 5 + ff_ins + [_layer_vec(8, D, l)],
                  [tile] + [_VMEM_WHOLE] * 5, out_shape, scratch,
                  [h2, x1s, df2, dr2, dh2_in, gs, vs, gs, wup4, wup4, wd2, conv_w, conv_b, modv], pay=pay)


def _silu(c):
    return c * (1.0 / (1.0 + jnp.exp(-c)))


def _ada_fwd_call(c_all, ada_w, ada_b_my):
    nl, _, wcols = ada_w.shape

    def body(c_ref, w_ref, b_ref, o_ref):
        ca = _silu(c_ref[...])
        o_ref[...] = jnp.dot(ca, w_ref[...], preferred_element_type=F32,
                             precision=lax.Precision.HIGHEST) + b_ref[...]

    return pl.pallas_call(
        body, name="ada_fwd", grid=(nl,),
        in_specs=[_full((NDEV, D)), pl.BlockSpec((None, D, wcols), lambda l: (l, 0, 0)),
                  pl.BlockSpec((None, 1, wcols), lambda l: (l, 0, 0))],
        out_specs=pl.BlockSpec((None, NDEV, wcols), lambda l: (l, 0, 0)),
        out_shape=jax.ShapeDtypeStruct((nl, NDEV, wcols), F32),
        compiler_params=_ARB,
    )(c_all, ada_w, ada_b_my)


def _adam_update(w, g, m, v):
    m2 = ADAM_B1 * m + (1.0 - ADAM_B1) * g
    v2 = ADAM_B2 * v + (1.0 - ADAM_B2) * (g * g)
    m_hat = m2 / (1.0 - ADAM_B1 ** ADAM_STEP)
    v_hat = v2 / (1.0 - ADAM_B2 ** ADAM_STEP)
    delta = -ADAM_LR * (m_hat / (jnp.sqrt(v_hat) + ADAM_EPS) + ADAM_WD * w)
    return delta, m2, v2


def _ada_bwd_call(c_t, dmod_my, w, m, v):
    nl, _, wcols = w.shape
    rb = 256

    def body(ct_ref, dm_ref, w_ref, m_ref, v_ref, g_ref, d_ref, m2_ref, v2_ref):
        ca_t = _silu(ct_ref[...])
        dm = dm_ref[...]
        g = ca_t[:, 0:1] * dm[0:1, :]
        for b in range(1, NDEV):
            g = g + ca_t[:, b:b + 1] * dm[b:b + 1, :]
        delta, m2, v2 = _adam_update(w_ref[...], g, m_ref[...], v_ref[...])
        g_ref[...] = g
        d_ref[...] = delta
        m2_ref[...] = m2
        v2_ref[...] = v2

    blk = pl.BlockSpec((None, rb, wcols), lambda l, i: (l, i, 0))
    shp = jax.ShapeDtypeStruct(w.shape, F32)
    return pl.pallas_call(
        body, name="ada_bwd", grid=(nl, D // rb),
        in_specs=[pl.BlockSpec((rb, NDEV), lambda l, i: (i, 0)),
                  pl.BlockSpec((None, NDEV, wcols), lambda l, i: (l, 0, 0)), blk, blk, blk],
        out_specs=[blk, blk, blk, blk], out_shape=[shp, shp, shp, shp],
        compiler_params=pltpu.CompilerParams(dimension_semantics=("arbitrary", "arbitrary")),
    )(c_t, dmod_my, w, m, v)


def _row_block(r, row_bytes):
    budget = 6 * 1024 * 1024
    best = None
    for rb in range(16, min(r, 512) + 1, 16):
        if r % rb == 0 and rb * row_bytes <= budget:
            best = rb
    return best if best is not None else r


def _adamw_call(parts, w, m, v, name, pay=None):
    nl, npart, r, c = parts.shape
    rb = _row_block(r, (npart + 7) * c * 4)
    nb = r // rb

    def body(p_ref, w_ref, m_ref, v_ref, g_ref, d_ref, m2_ref, v2_ref):
        g = p_ref[0].astype(F32)
        for k in range(1, npart):
            g = g + p_ref[k].astype(F32)
        delta, m2, v2 = _adam_update(w_ref[...], g, m_ref[...], v_ref[...])
        g_ref[...] = g
        d_ref[...] = delta
        m2_ref[...] = m2
        v2_ref[...] = v2

    blk = pl.BlockSpec((None, rb, c), lambda i: (i // nb, i % nb, 0))
    shp = jax.ShapeDtypeStruct((nl, r, c), F32)
    outs, _, alias = _pcall(body, name, nl * nb,
                            [pl.BlockSpec((None, npart, rb, c), lambda i: (i // nb, 0, i % nb, 0)), blk, blk, blk],
                            [blk] * 4, [shp] * 4, [], [parts, w, m, v], pay)
    return tuple(outs), alias


_SMALL_ORDER = ("ada_b", "pool_w", "sgu_w", "ln1_g", "ln1_b", "ln2_g", "ln2_b", "pool_scale", "sgu_ln_g", "sgu_ln_b",
                "sgu_b", "conv_w", "conv_b")


def kernel(x, c, ada_w, ada_b, w_in, pool_w, pool_scale, sgu_ln_g, sgu_ln_b, sgu_w, sgu_b, w_out, ln1_g, ln1_b, w_up, conv_w, conv_b, w_down, ln2_g, ln2_b, loss_target, m_ada_w, m_ada_b, m_w_in, m_pool_w, m_pool_scale, m_sgu_ln_g, m_sgu_ln_b, m_sgu_w, m_sgu_b, m_w_out, m_ln1_g, m_ln1_b, m_w_up, m_conv_w, m_conv_b, m_w_down, m_ln2_g, m_ln2_b, v_ada_w, v_ada_b, v_w_in, v_pool_w, v_pool_scale, v_sgu_ln_g, v_sgu_ln_b, v_sgu_w, v_sgu_b, v_w_out, v_ln1_g, v_ln1_b, v_w_up, v_conv_w, v_conv_b, v_w_down, v_ln2_g, v_ln2_b):
    nl = ada_w.shape[0]
    alpha = (2.0 * nl) ** 0.25
    me = _me()
    x2 = x[0]
    tgt = loss_target[0]
    acols = ada_w.shape[2]
    icols = w_in.shape[2]
    ucols = w_up.shape[2]
    orows = w_out.shape[1]
    drows = w_down.shape[1]
    ccols = conv_w.shape[2]

    winT_sh = jnp.swapaxes(w_in, 1, 2).astype(BF16)
    wupT_sh = jnp.swapaxes(w_up, 1, 2).astype(BF16)
    wout_sh = w_out.astype(BF16)
    wd_sh = w_down.astype(BF16)

    pay = _Payload()
    pay.gather(jnp.broadcast_to(c, (8, D)))
    pay.gather(winT_sh, 0)
    pay.gather(wout_sh, 0)
    pay.gather(conv_w)
    (c_g, winT_g, wout_g, cw_g), _ = _comm_call(pay, "gather_first")
    c_all = c_g[:, 0, :]
    winT = winT_g.reshape(DIN, D)
    wout = wout_g.reshape(D, D)
    cw_full = jnp.transpose(cw_g, (1, 2, 0, 3)).reshape(nl, 3, FF)
    cb_full = conv_b[:, None, :]

    ada_b_my = lax.dynamic_slice(ada_b, (0, me * acols), (nl, acols))[:, None, :]
    mod_blk = _ada_fwd_call(c_all, ada_w, ada_b_my)
    pay = _Payload()
    pay.gather(mod_blk)
    (mod_g,), _ = _comm_call(pay, "gather_mod")
    mod_me = lax.dynamic_index_in_dim(mod_g, me, axis=2, keepdims=False)
    modv = jnp.swapaxes(mod_me, 0, 1).reshape(nl, 6, D)
    modv = jnp.concatenate([modv, jnp.zeros((nl, 2, D), F32)], axis=1)

    ln1 = jnp.stack([ln1_g, ln1_b], axis=1)
    ln2 = jnp.stack([ln2_g, ln2_b], axis=1)
    sln = jnp.stack([sgu_ln_g, sgu_ln_b], axis=1)
    sbf = jnp.broadcast_to(sgu_b[..., None], sgu_b.shape + (GW,))
    small = (pool_w, pool_scale[:, None, :], sln, sgu_w, sbf)

    r1s, r2s, f1s, f2s, ahs, ghs, wins, wouts, wups, wds, st1s, st2s = ([None] * nl for _ in range(12))
    wins[0], wouts[0] = winT, wout
    xin = x2
    for l in range(nl):
        pay = None
        if l == 0:
            pay = _Payload()
            pay.gather(wupT_sh, 0)
            pay.gather(wd_sh, 0)
        (r1, f1, *ah, st_prev), new, _ = _fa_call(l, l == 0, alpha, xin, modv, ln2, wins[l], wouts[l], small, pay)
        if l > 0:
            st2s[l - 1] = st_prev
        if l == 0:
            wups[0], wds[0] = new[0].reshape(4, FH, D), new[1].reshape(2, FH, D)
        pay = None
        if l + 1 < nl:
            pay = _Payload()
            pay.gather(winT_sh, l + 1)
            pay.gather(wout_sh, l + 1)
            pay.gather(wupT_sh, l + 1)
            pay.gather(wd_sh, l + 1)
        (r2, f2, *saved_b, st1s[l]), new, _ = _fb_call(l, alpha, r1, modv, ln1, wups[l], wds[l], cw_full, cb_full,
                                                       pay)
        if l + 1 < nl:
            wins[l + 1], wouts[l + 1] = new[0].reshape(DIN, D), new[1].reshape(D, D)
            wups[l + 1], wds[l + 1] = new[2].reshape(4, FH, D), new[3].reshape(2, FH, D)
        r1s[l], r2s[l], f1s[l], f2s[l], ahs[l], ghs[l] = r1, r2, f1, f2, ah, saved_b
        xin = r2

    nsmall = 6 * D + 2 * NG * GW * GW + 3 * DP + NG * GW + 4 * D + 4 * FF
    srows = nsmall // 1024
    buf_in = lax.empty((nl, NDEV, icols, D), BF16)
    buf_out = lax.empty((nl, NDEV, orows, D), BF16)
    buf_up = lax.empty((nl, NDEV, ucols, D), BF16)
    buf_down = lax.empty((nl, NDEV, drows, D), BF16)
    buf_small = lax.empty((nl, NDEV, srows, 1024), F32)

    def a_side_payload(l, dwin, dwout, flat):
        p = _Payload()
        p.exchange_into([(dwin.reshape(NDEV, icols, D), 0)], buf_in, (l,))
        p.exchange_into([(dwout.reshape(NDEV, orows, D), 0)], buf_out, (l,))
        p.gather_into(flat, buf_small, (l,))
        return p

    dx = tgt
    loss_acc = None
    pending = None
    for l in reversed(range(nl)):
        last = l == nl - 1
        pay = None if pending is None else a_side_payload(*pending)
        h2, gs, vs, x1s = ghs[l]
        res0, _, al = _bb0_call(l, last, h2, r2s[l], st2s[l], dx, f2s[l], gs, vs, modv, ln2, wups[l], wds[l],
                                cw_full, cb_full, pay)
        if pending is not None:
            buf_in, buf_out, buf_small = al
        dh2p, df2, dr2, dwg0, dwv0, dwd0, vb0, cacc0 = res0[:8]
        if last:
            loss_acc = res0[8]
        (dx1, dwg1, dwv1, dwd1, vb1, cacc1), _, _ = _bb1_call(
            l, alpha, h2, x1s, df2, dr2, dh2p, gs, vs, modv, wups[l], wds[l], cw_full, cb_full)
        pay = _Payload()
        pay.exchange_into([(dwg0.reshape(2, ucols, D), 0), (dwg1.reshape(2, ucols, D), 2),
                           (dwv0.reshape(2, ucols, D), 4), (dwv1.reshape(2, ucols, D), 6)], buf_up, (l,))
        pay.exchange_into([(dwd0.reshape(4, drows, D), 0), (dwd1.reshape(4, drows, D), 4)], buf_down, (l,))
        (dx, dwin, dwout, dpw, dsw, dsb, va, va5), _, (buf_up, buf_down) = _ba_call(
            l, alpha, r1s[l], st1s[l], dx1, f1s[l], *ahs[l], modv, ln1, wins[l], wouts[l], small, pay)
        cacc = jnp.concatenate([cacc0, cacc1], axis=1)
        piece = dict(ada_b=jnp.stack([va[3], va[4], va[2], vb1[0], vb1[1], vb0[2]]), pool_w=dpw,
                     pool_scale=va5[0], sgu_ln_g=va5[1], sgu_ln_b=va5[2], sgu_w=dsw, sgu_b=dsb[:, :, 0],
                     ln1_g=va[0], ln1_b=va[1], conv_w=cacc[1:4], conv_b=cacc[0], ln2_g=vb0[0], ln2_b=vb0[1])
        flat = jnp.concatenate([piece[n].reshape(-1) for n in _SMALL_ORDER]).reshape(srows, 1024)
        pending = (l, dwin, dwout, flat)
    grad_x = dx[None]
    loss = lax.psum((0.5 / D) * jnp.sum(loss_acc), ("x", "y", "c"))

    res = {}
    _, dwin0, dwout0, flat0 = pending
    t_ = lambda a: jnp.swapaxes(a, 1, 2)
    pay = _Payload()
    pay.exchange_into([(dwout0.reshape(NDEV, orows, D), 0)], buf_out, (0,))
    pay.gather_into(flat0, buf_small, (0,))
    outs, (buf_out, buf_small) = _adamw_call(buf_up, t_(w_up), t_(m_w_up), t_(v_w_up), "adamw_w_up", pay)
    res["w_up"] = tuple(t_(o) for o in outs)
    pay = _Payload()
    pay.exchange_into([(dwin0.reshape(NDEV, icols, D), 0)], buf_in, (0,))
    res["w_down"], (buf_in,) = _adamw_call(buf_down, w_down, m_w_down, v_w_down, "adamw_w_down", pay)
    outs, _ = _adamw_call(buf_in, t_(w_in), t_(m_w_in), t_(v_w_in), "adamw_w_in")
    res["w_in"] = tuple(t_(o) for o in outs)
    res["w_out"], _ = _adamw_call(buf_out, w_out, m_w_out, v_w_out, "adamw_w_out")

    weights = dict(ada_b=ada_b, pool_w=pool_w, pool_scale=pool_scale, sgu_ln_g=sgu_ln_g, sgu_ln_b=sgu_ln_b,
                   sgu_w=sgu_w, sgu_b=sgu_b, ln1_g=ln1_g, ln1_b=ln1_b, conv_w=conv_w, conv_b=conv_b, ln2_g=ln2_g,
                   ln2_b=ln2_b)
    m_of = dict(ada_b=m_ada_b, pool_w=m_pool_w, pool_scale=m_pool_scale, sgu_ln_g=m_sgu_ln_g, sgu_ln_b=m_sgu_ln_b,
                sgu_w=m_sgu_w, sgu_b=m_sgu_b, ln1_g=m_ln1_g, ln1_b=m_ln1_b, conv_w=m_conv_w, conv_b=m_conv_b,
                ln2_g=m_ln2_g, ln2_b=m_ln2_b)
    v_of = dict(ada_b=v_ada_b, pool_w=v_pool_w, pool_scale=v_pool_scale, sgu_ln_g=v_sgu_ln_g, sgu_ln_b=v_sgu_ln_b,
                sgu_w=v_sgu_w, sgu_b=v_sgu_b, ln1_g=v_ln1_g, ln1_b=v_ln1_b, conv_w=v_conv_w, conv_b=v_conv_b,
                ln2_g=v_ln2_g, ln2_b=v_ln2_b)
    row = 0
    dmod_my = None
    tail = None
    for name in _SMALL_ORDER:
        w = weights[name]
        per_layer = FF * 3 if name == "conv_w" else w[0].size
        if per_layer % 1024 == 0:
            nrow = per_layer // 1024
            parts = buf_small[:, :, row:row + nrow, :]
            row += nrow
            if name == "ada_b":
                dmod_my = lax.dynamic_slice_in_dim(parts.reshape(nl, NDEV, 6 * D), me * acols, acols, axis=2)
        else:
            if tail is None:
                tail = buf_small[:, :, row:, :].reshape(nl, NDEV, (srows - row) * 1024)
                off = 0
            parts = tail[:, :, off:off + per_layer]
            off += per_layer
        if name == "conv_w":
            parts = lax.dynamic_slice_in_dim(parts.reshape(nl, NDEV, 3, FF), me * ccols, ccols, axis=3)
        cdim = w.shape[-1]
        w3 = w.reshape(nl, -1, cdim)
        outs, _ = _adamw_call(parts.reshape(nl, NDEV, -1, cdim), w3, m_of[name].reshape(w3.shape),
                              v_of[name].reshape(w3.shape), f"adamw_{name}")
        res[name] = tuple(o.reshape(w.shape) for o in outs)

    res["ada_w"] = _ada_bwd_call(jnp.swapaxes(c_all, 0, 1), dmod_my, ada_w, m_ada_w, v_ada_w)

    order = ["ada_w", "ada_b", "w_in", "pool_w", "pool_scale", "sgu_ln_g", "sgu_ln_b", "sgu_w", "sgu_b", "w_out",
             "ln1_g", "ln1_b", "w_up", "conv_w", "conv_b", "w_down", "ln2_g", "ln2_b"]
    out = [loss, grad_x]
    for k in range(4):
        out += [res[n][k] for n in order]
    return tuple(out)
```

```python
import jax
import jax.numpy as jnp
from jax import lax
from jax.experimental import pallas as pl
from jax.experimental.pallas import tpu as pltpu

F32 = jnp.float32
BF16 = jnp.bfloat16

NDEV = 8
D = 1024
DP = 512
DS = 512
DIN = DP + 2 * DS
FF = 2816
FH = FF // 2
FF_CHUNKS = ((0, 1536), (1536, FF))
NG = 4
GW = 128
WINDOWS = (2, 4, 8, 16)
AHALO = 16
GHALO = 8
LN_EPS = 1e-5
ADAM_LR, ADAM_B1, ADAM_B2, ADAM_EPS, ADAM_WD, ADAM_STEP = 0.001, 0.9, 0.999, 1e-08, 0.01, 10
TSF = 512
TSB = 256
_K0 = 0.7978845608028654
_K1 = 0.044715
MESH_ID = pl.DeviceIdType.MESH


def _mm(a, b):
    return jnp.dot(a, b, preferred_element_type=F32)


def _mm_nt(a, b):
    return lax.dot_general(a, b, (((1,), (1,)), ((), ())), preferred_element_type=F32)


def _mm_tn(a, b):
    return lax.dot_general(a, b, (((0,), (0,)), ((), ())), preferred_element_type=F32)


def _rowmean(x):
    return jnp.mean(x, axis=-1, keepdims=True)


def _ln_stats(x):
    mu = _rowmean(x)
    xc = x - mu
    rstd = lax.rsqrt(_rowmean(xc * xc) + LN_EPS)
    return xc * rstd, rstd


def _ln_stats_tile(x):
    mu = _rowmean(x)
    xc = x - mu
    rstd = lax.rsqrt(_rowmean(xc * xc) + LN_EPS)
    lane = lax.broadcasted_iota(jnp.int32, (x.shape[0], 128), 1)
    return xc * rstd, jnp.where(lane == 0, mu, jnp.where(lane == 1, rstd, 0.0))


def _ln_from_tile(x, st):
    rstd = st[:, 1:2]
    return (x - st[:, 0:1]) * rstd, rstd


def _ln_bwd(dy, gamma, xhat, rstd):
    dxh = dy * gamma
    return rstd * (dxh - _rowmean(dxh) - xhat * _rowmean(dxh * xhat))


def _gelu_t(x):
    t = jnp.tanh(x * (_K0 + (_K0 * _K1) * (x * x)))
    hx = 0.5 * x
    return hx + hx * t, t


def _dgelu(x, t):
    return (0.5 + 0.5 * t) + (0.5 * x) * (1.0 - t * t) * (_K0 + (3.0 * _K0 * _K1) * (x * x))


def _colsum8(x):
    t, n = x.shape
    return jnp.sum(x.reshape(t // 8, 8, n), axis=0)


def _tril_mask():
    r = lax.broadcasted_iota(jnp.int32, (GW, GW), 0)
    c = lax.broadcasted_iota(jnp.int32, (GW, GW), 1)
    return c <= r


def _full(shape):
    n = len(shape)
    return pl.BlockSpec(shape, lambda *_: (0,) * n)


def _resident(tail, lead=()):
    n = len(tail)
    return pl.BlockSpec((None,) * len(lead) + tuple(tail), lambda *_: tuple(lead) + (0,) * n,
                        pipeline_mode=pl.Buffered(1))


def _layer_vec(rows, width, l):
    return pl.BlockSpec((None, rows, width), lambda *_: (l, 0, 0))


_VMEM_WHOLE = pl.BlockSpec(memory_space=pltpu.VMEM)
_HBM = pl.BlockSpec(memory_space=pl.ANY)
_ARB = pltpu.CompilerParams(dimension_semantics=("arbitrary",))


def _me():
    return 4 * lax.axis_index("x") + 2 * lax.axis_index("y") + lax.axis_index("c")


def _coords(p):
    return (p >> 2, (p >> 1) & 1, p & 1)


class _Payload:
    def __init__(self):
        self.srcs, self.new, self.alias, self.transfers = [], [], [], []

    def _src(self, arr):
        self.srcs.append(arr)
        return len(self.srcs) - 1

    def _alias(self, buf):
        self.alias.append(buf)
        return len(self.alias) - 1

    def gather(self, arr, chunk=None):
        pos = self._src(arr)
        blk = arr.shape if chunk is None else arr.shape[1:]
        self.new.append(jax.ShapeDtypeStruct((NDEV,) + tuple(blk), arr.dtype))
        self.transfers.append(([(pos, chunk)] * NDEV, ("new", len(self.new) - 1), ()))
        return len(self.new) - 1

    def gather_into(self, arr, buf, lead):
        pos = self._src(arr)
        self.transfers.append(([(pos, None)] * NDEV, ("alias", self._alias(buf)), tuple(lead)))

    def exchange_into(self, parts, buf, lead):
        route = {}
        for arr, first in parts:
            pos = self._src(arr)
            for q in range(arr.shape[0]):
                route[first + q] = (pos, q)
        self.transfers.append(([route[p] for p in range(NDEV)], ("alias", self._alias(buf)), tuple(lead)))

    def _ends(self, t, io, dst_dev, src_dev):
        srcs, new_out, alias_out = io
        route, (kind, k), lead = self.transfers[t]
        pos, q = route[dst_dev]
        src = srcs[pos] if q is None else srcs[pos].at[q]
        buf = new_out[k] if kind == "new" else alias_out[k]
        return src, buf.at[lead + (src_dev,)]

    def _remote(self, t, io, sems, src_dev, dst_dev):
        src, dst = self._ends(t, io, dst_dev, src_dev)
        return pltpu.make_async_remote_copy(
            src_ref=src, dst_ref=dst, send_sem=sems[0].at[t, dst_dev], recv_sem=sems[1].at[t, src_dev],
            device_id=_coords(dst_dev), device_id_type=MESH_ID)

    def _local(self, t, io, sems, p):
        src, dst = self._ends(t, io, p, p)
        return pltpu.make_async_copy(src, dst, sems[2].at[t])

    def start(self, io, sems):
        me = _me()
        for p in range(NDEV):
            @pl.when(me == p)
            def _():
                for t in range(len(self.transfers)):
                    self._local(t, io, sems, p).start()

            @pl.when(me != p)
            def _():
                for t in range(len(self.transfers)):
                    self._remote(t, io, sems, me, p).start()

    def wait(self, io, sems):
        me = _me()
        for p in range(NDEV):
            @pl.when(me == p)
            def _():
                for t in range(len(self.transfers)):
                    self._local(t, io, sems, p).wait()

            @pl.when(me != p)
            def _():
                for t in range(len(self.transfers)):
                    self._remote(t, io, sems, p, p).wait_recv()
                    self._remote(t, io, sems, me, p).wait_send()


def _pcall(body, name, nsteps, in_specs, out_specs, out_shape, scratch, args, pay=None):
    n_in, n_out, n_scr = len(args), len(out_shape), len(scratch)
    if pay is None:
        res = pl.pallas_call(body, name=name, grid=(nsteps,), in_specs=list(in_specs), out_specs=list(out_specs),
                             out_shape=list(out_shape), scratch_shapes=list(scratch), compiler_params=_ARB)(*args)
        return list(res), [], []
    ns, nn, na, nt = len(pay.srcs), len(pay.new), len(pay.alias), len(pay.transfers)

    def full(*refs):
        cin = refs[:n_in]
        srcs = refs[n_in:n_in + ns]
        o0 = n_in + ns + na
        cout = refs[o0:o0 + n_out]
        new_out = refs[o0 + n_out:o0 + n_out + nn]
        alias_out = refs[o0 + n_out + nn:o0 + n_out + nn + na]
        s0 = o0 + n_out + nn + na
        cscr = refs[s0:s0 + n_scr]
        sems = refs[s0 + n_scr:]
        io = (srcs, new_out, alias_out)
        i = pl.program_id(0)

        @pl.when(i == 0)
        def _():
            pay.start(io, sems)

        body(*cin, *cout, *cscr)

        @pl.when(i == nsteps - 1)
        def _():
            pay.wait(io, sems)

    res = pl.pallas_call(
        full, name=name, grid=(nsteps,),
        in_specs=list(in_specs) + [_HBM] * (ns + na), out_specs=list(out_specs) + [_HBM] * (nn + na),
        out_shape=list(out_shape) + pay.new + [jax.ShapeDtypeStruct(b.shape, b.dtype) for b in pay.alias],
        input_output_aliases={n_in + ns + k: n_out + nn + k for k in range(na)},
        scratch_shapes=list(scratch) + [pltpu.SemaphoreType.DMA((nt, NDEV)), pltpu.SemaphoreType.DMA((nt, NDEV)),
                                        pltpu.SemaphoreType.DMA((nt,))],
        compiler_params=_ARB,
    )(*args, *pay.srcs, *pay.alias)
    return list(res[:n_out]), list(res[n_out:n_out + nn]), list(res[n_out + nn:])


def _comm_call(pay, name):
    def body():
        pass

    _, new, alias = _pcall(body, name, 1, [], [], [], [], [], pay)
    return new, alias


def _window_sums(x, halo, before):
    ts = x.shape[0]
    ext = jnp.concatenate([halo, x] if before else [x, halo], axis=0)
    n = ts + AHALO
    shift = (lambda k: k) if before else (lambda k: n - k)
    keep = slice(AHALO, n) if before else slice(0, ts)
    out = []
    s = ext
    for level in range(NG):
        s = s + pltpu.roll(s, shift(1 << level), 0)
        out.append(s[keep, 0:GW])
        if level + 1 < NG:
            s = s[:, GW:]
    return out


def _a_forward(a, u, v, halo, pw_ref, ps_ref, sln_ref, sw_ref, sbf_ref, zbuf, tile, ts):
    tglob = tile * ts + lax.broadcasted_iota(jnp.int32, (ts, 1), 0)
    sums = _window_sums(a, halo, True)
    pooled_b, mixed, inv_cnt, pwb = [], [], [], []
    for g, w in enumerate(WINDOWS):
        a_g = a[:, g * GW:(g + 1) * GW]
        s = sums[g]
        inv = 1.0 / jnp.minimum(tglob + 1, w).astype(F32)
        pg = (s * inv - a_g).astype(BF16)
        wg = pw_ref[g].astype(BF16)
        pooled_b.append(pg)
        inv_cnt.append(inv)
        pwb.append(wg)
        mixed.append(_mm(pg, wg))
    mixed = jnp.concatenate(mixed, axis=1)
    ya = mixed * ps_ref[...]
    ug, tu = _gelu_t(u)
    vg, tv = _gelu_t(v)
    vhat, rstdv = _ln_stats(vg)
    vnb = (vhat * sln_ref[0:1, :] + sln_ref[1:2, :]).astype(BF16)
    tri = _tril_mask()
    wt = [jnp.where(tri, sw_ref[h], 0.0).astype(BF16) for h in range(NG)]
    for c in range(ts // GW):
        rs = slice(c * GW, (c + 1) * GW)
        for h in range(NG):
            cs = slice(h * GW, (h + 1) * GW)
            zbuf[rs, cs] = _mm(wt[h], vnb[rs, cs]) + sbf_ref[h]
    z = zbuf[...]
    yb = ug * z
    return dict(u=u, v=v, tu=tu, tv=tv, ug=ug, z=z, vhat=vhat, rstdv=rstdv, vnb=vnb,
                wt=wt, pooled_b=pooled_b, pwb=pwb, inv_cnt=inv_cnt, mixed=mixed, ya=ya, yb=yb)


def _small_specs(l):
    grp = pl.BlockSpec((None, NG, GW, GW), lambda i: (l, 0, 0, 0))
    return [grp, _layer_vec(1, DP, l), _layer_vec(2, DS, l), grp, grp]


def _fa_call(l, first, alpha, xin, modv, lnp, winT, wout, small, pay=None):
    s = xin.shape[0]
    nt = s // TSF

    def body(xin_ref, modv_ref, lnp_ref, winT_ref, wout_ref, pw_ref, ps_ref, sln_ref, sw_ref, sbf_ref,
             r1_ref, f1_ref, h_ref, proj_ref, x_ref, st_ref, acarry, zbuf, mixbuf):
        i = pl.program_id(0)

        @pl.when(i == 0)
        def _():
            acarry[...] = jnp.zeros_like(acarry)

        x = xin_ref[...]
        if first:
            st_ref[...] = jnp.zeros_like(st_ref)
        else:
            xhat, st_ref[...] = _ln_stats_tile(x)
            x = xhat * lnp_ref[0:1, :] + lnp_ref[1:2, :]
        hb = (x * (1.0 + modv_ref[1:2, :]) + modv_ref[0:1, :]).astype(BF16)
        proj = _mm_nt(hb, winT_ref[...])
        a = proj[:, 0:DP]
        fw = _a_forward(a, proj[:, DP:DP + DS], proj[:, DP + DS:], acarry[...], pw_ref, ps_ref, sln_ref, sw_ref,
                        sbf_ref, zbuf, i, TSF)
        acarry[...] = a[TSF - AHALO:, :]
        mixbuf[:, 0:DP] = fw["ya"].astype(BF16)
        mixbuf[:, DP:] = fw["yb"].astype(BF16)
        f = _mm(mixbuf[...], wout_ref[...])
        r1_ref[...] = alpha * x + modv_ref[2:3, :] * f
        f1_ref[...] = f.astype(BF16)
        h_ref[...] = hb
        proj_ref[...] = proj.astype(BF16)
        x_ref[...] = x.astype(BF16)

    tile = pl.BlockSpec((TSF, D), lambda i: (i, 0))
    return _pcall(
        body, f"fa{l}", nt,
        in_specs=[tile, _layer_vec(8, D, l), _layer_vec(2, D, max(l - 1, 0)), _resident((DIN, D)),
                  _resident((D, D))] + _small_specs(l),
        out_specs=[tile, tile, tile, pl.BlockSpec((TSF, DIN), lambda i: (i, 0)), tile,
                   pl.BlockSpec((TSF, 128), lambda i: (i, 0))],
        out_shape=[jax.ShapeDtypeStruct((s, D), F32), jax.ShapeDtypeStruct((s, D), BF16),
                   jax.ShapeDtypeStruct((s, D), BF16), jax.ShapeDtypeStruct((s, DIN), BF16),
                   jax.ShapeDtypeStruct((s, D), BF16), jax.ShapeDtypeStruct((s, 128), F32)],
        scratch=[pltpu.VMEM((AHALO, DP), F32), pltpu.VMEM((TSF, DS), F32), pltpu.VMEM((TSF, D), BF16)],
        args=[xin, modv, lnp, winT, wout, *small], pay=pay)


def _ba_call(l, alpha, r1, st1, dx1, f1, hsave, proj, xsave, modv, ln1, winT, wout, small, pay=None):
    s = r1.shape[0]
    nt = s // TSB
    ts = TSB

    def body(r1_ref, st1_ref, dx1_ref, f1_ref, h_ref, proj_ref, xs_ref, ah_ref, modv_ref, ln1_ref, winT_ref,
             wout_ref, pw_ref, ps_ref, sln_ref, sw_ref, sbf_ref,
             dx_ref, dwin_ref, dwout_ref, dpw_ref, dsw_ref, dsb_ref, va_ref, va5_ref,
             qnext, zbuf, dvnbuf, mixbuf, dpbuf, dbacc, vacc, vacc5, dwin_acc, dwout_acc):
        i = pl.program_id(0)
        j = nt - 1 - i

        @pl.when(i == 0)
        def _():
            dwin_acc[...] = jnp.zeros_like(dwin_acc)
            dwout_acc[...] = jnp.zeros_like(dwout_acc)
            dpw_ref[...] = jnp.zeros_like(dpw_ref)
            dsw_ref[...] = jnp.zeros_like(dsw_ref)
            dbacc[...] = jnp.zeros_like(dbacc)
            vacc[...] = jnp.zeros_like(vacc)
            vacc5[...] = jnp.zeros_like(vacc5)
            qnext[...] = jnp.zeros_like(qnext)

        hb = h_ref[...]
        sc1 = modv_ref[1:2, :]
        halo = jnp.where(j > 0, ah_ref[...].astype(F32), 0.0)
        fw = _a_forward(proj_ref[:, 0:DP].astype(F32), proj_ref[:, DP:DP + DS].astype(F32),
                        proj_ref[:, DP + DS:].astype(F32), halo, pw_ref, ps_ref, sln_ref, sw_ref, sbf_ref, zbuf, j, ts)
        mixbuf[:, 0:DP] = fw["ya"].astype(BF16)
        mixbuf[:, DP:] = fw["yb"].astype(BF16)

        xhat1, rstd1 = _ln_from_tile(r1_ref[...], st1_ref[...])
        dy = dx1_ref[...]
        vacc[0] += _colsum8(dy * xhat1)
        vacc[1] += _colsum8(dy)
        dr1 = _ln_bwd(dy, ln1_ref[0:1, :], xhat1, rstd1)
        vacc[2] += _colsum8(dr1 * f1_ref[...].astype(F32))
        dfb = (dr1 * modv_ref[2:3, :]).astype(BF16)
        dwout_acc[...] += _mm_tn(mixbuf[...], dfb)
        dmix = _mm_nt(dfb, wout_ref[...])
        dya = dmix[:, 0:DP]
        dyb = dmix[:, DP:]

        vacc5[0] += _colsum8(dya * fw["mixed"])
        dmixed = (dya * ps_ref[...]).astype(BF16)
        dpooled, q = [], []
        for g in range(NG):
            cs = slice(g * GW, (g + 1) * GW)
            dpw_ref[g] += _mm_tn(fw["pooled_b"][g], dmixed[:, cs])
            dpg = _mm_nt(dmixed[:, cs], fw["pwb"][g])
            dpooled.append(dpg)
            q.append(dpg * fw["inv_cnt"][g])
        q = jnp.concatenate(q, axis=1)
        sums = _window_sums(q, qnext[...], False)
        qnext[...] = q[0:AHALO, :]
        for g in range(NG):
            dpbuf[:, g * GW:(g + 1) * GW] = (sums[g] - dpooled[g]).astype(BF16)

        dug = dyb * fw["z"]
        dz = dyb * fw["ug"]
        dzb = dz.astype(BF16)
        for c in range(ts // GW):
            rs = slice(c * GW, (c + 1) * GW)
            for h in range(NG):
                cs = slice(h * GW, (h + 1) * GW)
                dvnbuf[rs, cs] = _mm_tn(fw["wt"][h], dzb[rs, cs])
                dsw_ref[h] += _mm_nt(dzb[rs, cs], fw["vnb"][rs, cs])
            dbacc[...] += dz[rs, :]
        dvn = dvnbuf[...]
        vacc5[1] += _colsum8(dvn * fw["vhat"])
        vacc5[2] += _colsum8(dvn)
        dvg = _ln_bwd(dvn, sln_ref[0:1, :], fw["vhat"], fw["rstdv"])
        dpbuf[:, DP:DP + DS] = (dug * _dgelu(fw["u"], fw["tu"])).astype(BF16)
        dpbuf[:, DP + DS:] = (dvg * _dgelu(fw["v"], fw["tv"])).astype(BF16)

        dpb = dpbuf[...]
        dwin_acc[...] += _mm_tn(dpb, hb)
        dh = _mm(dpb, winT_ref[...])
        dx_ref[...] = dh * (1.0 + sc1) + alpha * dr1
        vacc[3] += _colsum8(dh)
        vacc[4] += _colsum8(dh * xs_ref[...].astype(F32))

        @pl.when(i == nt - 1)
        def _():
            dwin_ref[...] = dwin_acc[...].astype(BF16)
            dwout_ref[...] = dwout_acc[...].astype(BF16)
            tri = _tril_mask()
            for h in range(NG):
                dsw_ref[h] = jnp.where(tri, dsw_ref[h], 0.0)
                sb = jnp.sum(dbacc[:, h * GW:(h + 1) * GW], axis=1, keepdims=True)
                dsb_ref[h] = jnp.broadcast_to(sb, (GW, GW))
            for n in range(5):
                va_ref[n:n + 1, :] = jnp.sum(vacc[n], axis=0, keepdims=True)
            for n in range(3):
                va5_ref[n:n + 1, :] = jnp.sum(vacc5[n], axis=0, keepdims=True)

    rev = lambda i: (nt - 1 - i, 0)
    tile = pl.BlockSpec((ts, D), rev)
    return _pcall(
        body, f"ba{l}", nt,
        in_specs=[tile, pl.BlockSpec((ts, 128), rev), tile, tile, tile, pl.BlockSpec((ts, DIN), rev), tile,
                  pl.BlockSpec((AHALO, DP), lambda i: (jnp.maximum((nt - 1 - i) * (ts // AHALO) - 1, 0), 0)),
                  _layer_vec(8, D, l), _layer_vec(2, D, l),
                  _resident((DIN, D)), _resident((D, D))] + _small_specs(l),
        out_specs=[tile] + [_VMEM_WHOLE] * 7,
        out_shape=[jax.ShapeDtypeStruct((s, D), F32), jax.ShapeDtypeStruct((DIN, D), BF16),
                   jax.ShapeDtypeStruct((D, D), BF16), jax.ShapeDtypeStruct((NG, GW, GW), F32),
                   jax.ShapeDtypeStruct((NG, GW, GW), F32), jax.ShapeDtypeStruct((NG, GW, GW), F32),
                   jax.ShapeDtypeStruct((5, D), F32), jax.ShapeDtypeStruct((3, DP), F32)],
        scratch=[pltpu.VMEM((AHALO, DP), F32),
                 pltpu.VMEM((ts, DS), F32), pltpu.VMEM((ts, DS), F32), pltpu.VMEM((ts, D), BF16),
                 pltpu.VMEM((ts, DIN), BF16), pltpu.VMEM((GW, DS), F32), pltpu.VMEM((5, 8, D), F32),
                 pltpu.VMEM((3, 8, DP), F32), pltpu.VMEM((DIN, D), F32), pltpu.VMEM((D, D), F32)],
        args=[r1, st1, dx1, f1, hsave, proj, xsave, proj, modv, ln1, winT, wout, *small], pay=pay)


def _cast_call(arrs, name):
    r, c = arrs[0].shape
    rb = _row_block(r, 6 * c * len(arrs))

    def body(*refs):
        for src, dst in zip(refs[:len(arrs)], refs[len(arrs):]):
            dst[...] = src[...].astype(BF16)

    blk = pl.BlockSpec((rb, c), lambda i: (i, 0))
    return pl.pallas_call(
        body, name=name, grid=(r // rb,), in_specs=[blk] * len(arrs), out_specs=[blk] * len(arrs),
        out_shape=[jax.ShapeDtypeStruct((r, c), BF16)] * len(arrs), compiler_params=_ARB)(*arrs)


def _rows_before(halo, x):
    ext = jnp.concatenate([halo, x], axis=0)
    return pltpu.roll(ext, 1, 0)[GHALO:, :], pltpu.roll(ext, 2, 0)[GHALO:, :]


def _rows_after(x, halo):
    ts = x.shape[0]
    ext = jnp.concatenate([x, halo], axis=0)
    n = ts + GHALO
    return pltpu.roll(ext, n - 1, 0)[0:ts, :], pltpu.roll(ext, n - 2, 0)[0:ts, :]


def _fb_call(l, alpha, r1, modv, ln1, wup4, wd2, conv_w, conv_b, pay=None):
    s = r1.shape[0]
    nt = s // TSF

    def body(r1_ref, modv_ref, ln1_ref, wup_ref, wd_ref, cw_ref, cb_ref, r2_ref, f2_ref, h2_ref, gs_ref, vs_ref,
             gc_ref, x1_ref, st_ref, gbuf):
        i = pl.program_id(0)

        @pl.when(i == 0)
        def _():
            gbuf[...] = jnp.zeros_like(gbuf)

        xhat1, st_ref[...] = _ln_stats_tile(r1_ref[...])
        x1 = xhat1 * ln1_ref[0:1, :] + ln1_ref[1:2, :]
        h2b = (x1 * (1.0 + modv_ref[4:5, :]) + modv_ref[3:4, :]).astype(BF16)
        f2 = jnp.zeros((TSF, D), F32)
        for c0, c1 in FF_CHUNKS:
            cs = slice(c0, c1)
            g = _mm_nt(h2b, wup_ref[0, cs, :])
            val = _mm_nt(h2b, wup_ref[1, cs, :])
            gm1, gm2 = _rows_before(gbuf[:, cs], g)
            cw = cw_ref[:, cs]
            gc = cb_ref[:, cs] + cw[0:1, :] * gm2 + cw[1:2, :] * gm1 + cw[2:3, :] * g
            ge, _ = _gelu_t(gc)
            f2 = f2 + _mm((ge * val).astype(BF16), wd_ref[cs, :])
            gs_ref[:, cs] = g.astype(BF16)
            vs_ref[:, cs] = val.astype(BF16)
            gc_ref[:, cs] = gc.astype(BF16)
            gbuf[:, cs] = g[TSF - GHALO:, :]
        r2_ref[...] = alpha * x1 + modv_ref[5:6, :] * f2
        f2_ref[...] = f2.astype(BF16)
        h2_ref[...] = h2b
        x1_ref[...] = x1.astype(BF16)

    tile = pl.BlockSpec((TSF, D), lambda i: (i, 0))
    wide = pl.BlockSpec((TSF, FF), lambda i: (i, 0))
    return _pcall(
        body, f"fb{l}", nt,
        in_specs=[tile, _layer_vec(8, D, l), _layer_vec(2, D, l), _resident((2, FF, D)), _resident((FF, D)),
                  _layer_vec(3, FF, l), _layer_vec(1, FF, l)],
        out_specs=[tile, tile, tile, wide, wide, wide, tile, pl.BlockSpec((TSF, 128), lambda i: (i, 0))],
        out_shape=[jax.ShapeDtypeStruct((s, D), F32), jax.ShapeDtypeStruct((s, D), BF16),
                   jax.ShapeDtypeStruct((s, D), BF16), jax.ShapeDtypeStruct((s, FF), BF16),
                   jax.ShapeDtypeStruct((s, FF), BF16), jax.ShapeDtypeStruct((s, FF), BF16),
                   jax.ShapeDtypeStruct((s, D), BF16), jax.ShapeDtypeStruct((s, 128), F32)],
        scratch=[pltpu.VMEM((GHALO, FF), F32)],
        args=[r1, modv, ln1, wup4.reshape(2, FF, D), wd2.reshape(FF, D), conv_w, conv_b], pay=pay)


def _ff_backward(h2b, df2b, gs_ref, vs_ref, gc_ref, dgc_next, cw, wg_ref, wv_ref, wd_ref,
                 dwg_ref, dwv_ref, dwd_ref, cacc):
    g = gs_ref[...].astype(F32)
    val = vs_ref[...].astype(F32)
    gc = gc_ref[...].astype(F32)
    ge, tg = _gelu_t(gc)
    mb = (ge * val).astype(BF16)
    dm = _mm_nt(df2b, wd_ref[...])
    dwd_ref[...] += _mm_tn(mb, df2b)
    dval = dm * ge
    dgc = dm * val * _dgelu(gc, tg)
    dgp1, dgp2 = _rows_after(dgc, dgc_next[...])
    cacc[0] += _colsum8(dgc)
    cacc[1] += _colsum8(dgp2 * g)
    cacc[2] += _colsum8(dgp1 * g)
    cacc[3] += _colsum8(dgc * g)
    dg = cw[2:3, :] * dgc + cw[1:2, :] * dgp1 + cw[0:1, :] * dgp2
    dgc_next[...] = dgc[0:GHALO, :]
    dgb = dg.astype(BF16)
    dvb = dval.astype(BF16)
    dwg_ref[...] += _mm_tn(dgb, h2b)
    dwv_ref[...] += _mm_tn(dvb, h2b)
    return _mm(dgb, wg_ref[...]) + _mm(dvb, wv_ref[...])


def _bb_specs(l, hf, nt, ts):
    wide = pl.BlockSpec((ts, FH), lambda i: (nt - 1 - i, hf))
    ins = [wide, wide, wide, _resident((FH, D), (hf,)), _resident((FH, D), (2 + hf,)), _resident((FH, D), (hf,)),
           pl.BlockSpec((None, 3, FH), lambda i: (l, 0, hf))]
    acc_shapes = [jax.ShapeDtypeStruct((FH, D), BF16)] * 3
    return ins, acc_shapes, [pltpu.VMEM((FH, D), F32)] * 3


def _bb0_call(l, last, h2, r2, st2, dx2, f2, gs, vs, gcs, modv, ln2, wup4, wd2, conv_w, pay=None):
    s = r2.shape[0]
    nt = s // TSB
    ts = TSB

    def body(*refs):
        it = iter(refs)
        h2_ref, r2_ref, dx2_ref, f2_ref, gs_ref, vs_ref, gc_ref = (next(it) for _ in range(7))
        wg_ref, wv_ref, wd_ref, cw_ref, modv_ref, ln2_ref = (next(it) for _ in range(6))
        st2_ref = None if last else next(it)
        dh2_ref, df2_ref, dr2_ref, dwg_out, dwv_out, dwd_out, vb_ref, cacc_ref = (next(it) for _ in range(8))
        loss_ref = next(it) if last else None
        vacc, cacc, dgc_next, dwg_ref, dwv_ref, dwd_ref = (next(it) for _ in range(6))
        lacc = next(it) if last else None
        i = pl.program_id(0)

        @pl.when(i == 0)
        def _():
            dwg_ref[...] = jnp.zeros_like(dwg_ref)
            dwv_ref[...] = jnp.zeros_like(dwv_ref)
            dwd_ref[...] = jnp.zeros_like(dwd_ref)
            vacc[...] = jnp.zeros_like(vacc)
            cacc[...] = jnp.zeros_like(cacc)
            dgc_next[...] = jnp.zeros_like(dgc_next)
            if last:
                lacc[...] = jnp.zeros_like(lacc)

        if last:
            xhat2, rstd2 = _ln_stats(r2_ref[...])
        else:
            xhat2, rstd2 = _ln_from_tile(r2_ref[...], st2_ref[...])
        if last:
            diff = xhat2 * ln2_ref[0:1, :] + ln2_ref[1:2, :] - dx2_ref[...]
            dy = diff * (1.0 / D)
            lacc[...] += _colsum8(diff * diff)
        else:
            dy = dx2_ref[...]
        dr2 = _ln_bwd(dy, ln2_ref[0:1, :], xhat2, rstd2)
        vacc[0] += _colsum8(dy * xhat2)
        vacc[1] += _colsum8(dy)
        vacc[2] += _colsum8(dr2 * f2_ref[...].astype(F32))
        df2b = (dr2 * modv_ref[5:6, :]).astype(BF16)
        dr2_ref[...] = dr2
        df2_ref[...] = df2b
        dh2_ref[...] = _ff_backward(h2_ref[...], df2b, gs_ref, vs_ref, gc_ref, dgc_next, cw_ref[...],
                                    wg_ref, wv_ref, wd_ref, dwg_ref, dwv_ref, dwd_ref, cacc)

        @pl.when(i == nt - 1)
        def _():
            for n in range(3):
                vb_ref[n:n + 1, :] = jnp.sum(vacc[n], axis=0, keepdims=True)
            for n in range(4):
                cacc_ref[n:n + 1, :] = jnp.sum(cacc[n], axis=0, keepdims=True)
            dwg_out[...] = dwg_ref[...].astype(BF16)
            dwv_out[...] = dwv_ref[...].astype(BF16)
            dwd_out[...] = dwd_ref[...].astype(BF16)
            if last:
                loss_ref[...] = lacc[...]

    tile = pl.BlockSpec((ts, D), lambda i: (nt - 1 - i, 0))
    ff_ins, acc_shapes, acc_scratch = _bb_specs(l, 0, nt, ts)
    in_specs = [tile, tile, tile, tile] + ff_ins + [_layer_vec(8, D, l), _layer_vec(2, D, l)]
    args = [h2, r2, dx2, f2, gs, vs, gcs, wup4, wup4, wd2, conv_w, modv, ln2]
    if not last:
        in_specs.append(pl.BlockSpec((ts, 128), lambda i: (nt - 1 - i, 0)))
        args.append(st2)
    out_specs = [tile, tile, tile] + [_VMEM_WHOLE] * 5
    out_shape = [jax.ShapeDtypeStruct((s, D), F32), jax.ShapeDtypeStruct((s, D), BF16),
                 jax.ShapeDtypeStruct((s, D), F32)] + acc_shapes + [jax.ShapeDtypeStruct((3, D), F32),
                                                                    jax.ShapeDtypeStruct((4, FH), F32)]
    scratch = [pltpu.VMEM((3, 8, D), F32), pltpu.VMEM((4, 8, FH), F32), pltpu.VMEM((GHALO, FH), F32)] + acc_scratch
    if last:
        out_specs.append(_VMEM_WHOLE)
        out_shape.append(jax.ShapeDtypeStruct((8, D), F32))
        scratch.append(pltpu.VMEM((8, D), F32))
    return _pcall(body, f"bb{l}_0", nt, in_specs, out_specs, out_shape, scratch, args, pay=pay)


def _bb1_call(l, alpha, h2, x1s, df2, dr2, dh2_in, gs, vs, gcs, modv, wup4, wd2, conv_w, pay=None):
    s = h2.shape[0]
    nt = s // TSB
    ts = TSB

    def body(h2_ref, x1_ref, df2_ref, dr2_ref, dh2_ref, gs_ref, vs_ref, gc_ref, wg_ref, wv_ref, wd_ref, cw_ref,
             modv_ref, dx1_ref, dwg_out, dwv_out, dwd_out, vb_ref, cacc_ref, vacc, cacc, dgc_next,
             dwg_ref, dwv_ref, dwd_ref):
        i = pl.program_id(0)

        @pl.when(i == 0)
        def _():
            dwg_ref[...] = jnp.zeros_like(dwg_ref)
            dwv_ref[...] = jnp.zeros_like(dwv_ref)
            dwd_ref[...] = jnp.zeros_like(dwd_ref)
            vacc[...] = jnp.zeros_like(vacc)
            cacc[...] = jnp.zeros_like(cacc)
            dgc_next[...] = jnp.zeros_like(dgc_next)

        dh2 = dh2_ref[...] + _ff_backward(h2_ref[...], df2_ref[...], gs_ref, vs_ref, gc_ref, dgc_next,
                                          cw_ref[...], wg_ref, wv_ref, wd_ref, dwg_ref, dwv_ref, dwd_ref, cacc)
        dx1_ref[...] = dh2 * (1.0 + modv_ref[4:5, :]) + alpha * dr2_ref[...]
        vacc[0] += _colsum8(dh2)
        vacc[1] += _colsum8(dh2 * x1_ref[...].astype(F32))

        @pl.when(i == nt - 1)
        def _():
            dwg_out[...] = dwg_ref[...].astype(BF16)
            dwv_out[...] = dwv_ref[...].astype(BF16)
            dwd_out[...] = dwd_ref[...].astype(BF16)
            for n in range(2):
                vb_ref[n:n + 1, :] = jnp.sum(vacc[n], axis=0, keepdims=True)
            for n in range(4):
                cacc_ref[n:n + 1, :] = jnp.sum(cacc[n], axis=0, keepdims=True)

    tile = pl.BlockSpec((ts, D), lambda i: (nt - 1 - i, 0))
    ff_ins, acc_shapes, acc_scratch = _bb_specs(l, 1, nt, ts)
    out_shape = [jax.ShapeDtypeStruct((s, D), F32)] + acc_shapes + [jax.ShapeDtypeStruct((2, D), F32),
                                                                   jax.ShapeDtypeStruct((4, FH), F32)]
    scratch = [pltpu.VMEM((2, 8, D), F32), pltpu.VMEM((4, 8, FH), F32), pltpu.VMEM((GHALO, FH), F32)] + acc_scratch
    return _pcall(body, f"bb{l}_1", nt, [tile] * 5 + ff_ins + [_layer_vec(8, D, l)],
                  [tile] + [_VMEM_WHOLE] * 5, out_shape, scratch,
                  [h2, x1s, df2, dr2, dh2_in, gs, vs, gcs, wup4, wup4, wd2, conv_w, modv], pay=pay)


def _silu(c):
    return c * (1.0 / (1.0 + jnp.exp(-c)))


def _ada_fwd_call(c_all, ada_w, ada_b_my):
    nl, _, wcols = ada_w.shape

    def body(c_ref, w_ref, b_ref, o_ref):
        ca = _silu(c_ref[...])
        o_ref[...] = jnp.dot(ca, w_ref[...], preferred_element_type=F32,
                             precision=lax.Precision.HIGHEST) + b_ref[...]

    return pl.pallas_call(
        body, name="ada_fwd", grid=(nl,),
        in_specs=[_full((NDEV, D)), pl.BlockSpec((None, D, wcols), lambda l: (l, 0, 0)),
                  pl.BlockSpec((None, 1, wcols), lambda l: (l, 0, 0))],
        out_specs=pl.BlockSpec((None, NDEV, wcols), lambda l: (l, 0, 0)),
        out_shape=jax.ShapeDtypeStruct((nl, NDEV, wcols), F32),
        compiler_params=_ARB,
    )(c_all, ada_w, ada_b_my)


def _adam_update(w, g, m, v):
    m2 = ADAM_B1 * m + (1.0 - ADAM_B1) * g
    v2 = ADAM_B2 * v + (1.0 - ADAM_B2) * (g * g)
    m_hat = m2 / (1.0 - ADAM_B1 ** ADAM_STEP)
    v_hat = v2 / (1.0 - ADAM_B2 ** ADAM_STEP)
    delta = -ADAM_LR * (m_hat / (jnp.sqrt(v_hat) + ADAM_EPS) + ADAM_WD * w)
    return delta, m2, v2


def _ada_bwd_call(c_t, dmod_my, w, m, v):
    nl, _, wcols = w.shape
    rb = 256

    def body(ct_ref, dm_ref, w_ref, m_ref, v_ref, g_ref, d_ref, m2_ref, v2_ref):
        ca_t = _silu(ct_ref[...])
        dm = dm_ref[...]
        g = ca_t[:, 0:1] * dm[0:1, :]
        for b in range(1, NDEV):
            g = g + ca_t[:, b:b + 1] * dm[b:b + 1, :]
        delta, m2, v2 = _adam_update(w_ref[...], g, m_ref[...], v_ref[...])
        g_ref[...] = g
        d_ref[...] = delta
        m2_ref[...] = m2
        v2_ref[...] = v2

    blk = pl.BlockSpec((None, rb, wcols), lambda l, i: (l, i, 0))
    shp = jax.ShapeDtypeStruct(w.shape, F32)
    return pl.pallas_call(
        body, name="ada_bwd", grid=(nl, D // rb),
        in_specs=[pl.BlockSpec((rb, NDEV), lambda l, i: (i, 0)),
                  pl.BlockSpec((None, NDEV, wcols), lambda l, i: (l, 0, 0)), blk, blk, blk],
        out_specs=[blk, blk, blk, blk], out_shape=[shp, shp, shp, shp],
        compiler_params=pltpu.CompilerParams(dimension_semantics=("arbitrary", "arbitrary")),
    )(c_t, dmod_my, w, m, v)


def _row_block(r, row_bytes):
    budget = 6 * 1024 * 1024
    best = None
    for rb in range(16, min(r, 512) + 1, 16):
        if r % rb == 0 and rb * row_bytes <= budget:
            best = rb
    return best if best is not None else r


def _adamw_call(parts, w, m, v, name, pay=None):
    nl, npart, r, c = parts.shape
    rb = _row_block(r, (npart + 7) * c * 4)
    nb = r // rb

    def body(p_ref, w_ref, m_ref, v_ref, g_ref, d_ref, m2_ref, v2_ref):
        g = p_ref[0].astype(F32)
        for k in range(1, npart):
            g = g + p_ref[k].astype(F32)
        delta, m2, v2 = _adam_update(w_ref[...], g, m_ref[...], v_ref[...])
        g_ref[...] = g
        d_ref[...] = delta
        m2_ref[...] = m2
        v2_ref[...] = v2

    blk = pl.BlockSpec((None, rb, c), lambda i: (i // nb, i % nb, 0))
    shp = jax.ShapeDtypeStruct((nl, r, c), F32)
    outs, _, alias = _pcall(body, name, nl * nb,
                            [pl.BlockSpec((None, npart, rb, c), lambda i: (i // nb, 0, i % nb, 0)), blk, blk, blk],
                            [blk] * 4, [shp] * 4, [], [parts, w, m, v], pay)
    return tuple(outs), alias


_SMALL_ORDER = ("ada_b", "pool_w", "sgu_w", "ln1_g", "ln1_b", "ln2_g", "ln2_b", "pool_scale", "sgu_ln_g", "sgu_ln_b",
                "sgu_b", "conv_w", "conv_b")


def kernel(x, c, ada_w, ada_b, w_in, pool_w, pool_scale, sgu_ln_g, sgu_ln_b, sgu_w, sgu_b, w_out, ln1_g, ln1_b, w_up, conv_w, conv_b, w_down, ln2_g, ln2_b, loss_target, m_ada_w, m_ada_b, m_w_in, m_pool_w, m_pool_scale, m_sgu_ln_g, m_sgu_ln_b, m_sgu_w, m_sgu_b, m_w_out, m_ln1_g, m_ln1_b, m_w_up, m_conv_w, m_conv_b, m_w_down, m_ln2_g, m_ln2_b, v_ada_w, v_ada_b, v_w_in, v_pool_w, v_pool_scale, v_sgu_ln_g, v_sgu_ln_b, v_sgu_w, v_sgu_b, v_w_out, v_ln1_g, v_ln1_b, v_w_up, v_conv_w, v_conv_b, v_w_down, v_ln2_g, v_ln2_b):
    nl = ada_w.shape[0]
    alpha = (2.0 * nl) ** 0.25
    me = _me()
    x2 = x[0]
    tgt = loss_target[0]
    acols = ada_w.shape[2]
    icols = w_in.shape[2]
    ucols = w_up.shape[2]
    orows = w_out.shape[1]
    drows = w_down.shape[1]
    ccols = conv_w.shape[2]

    winT_sh = jnp.swapaxes(w_in, 1, 2).astype(BF16)
    wupT_sh = jnp.swapaxes(w_up, 1, 2).astype(BF16)
    wout_sh = w_out.astype(BF16)
    wd_sh = w_down.astype(BF16)

    pay = _Payload()
    pay.gather(jnp.broadcast_to(c, (8, D)))
    pay.gather(winT_sh, 0)
    pay.gather(wout_sh, 0)
    pay.gather(conv_w)
    (c_g, winT_g, wout_g, cw_g), _ = _comm_call(pay, "gather_first")
    c_all = c_g[:, 0, :]
    winT = winT_g.reshape(DIN, D)
    wout = wout_g.reshape(D, D)
    cw_full = jnp.transpose(cw_g, (1, 2, 0, 3)).reshape(nl, 3, FF)
    cb_full = conv_b[:, None, :]

    ada_b_my = lax.dynamic_slice(ada_b, (0, me * acols), (nl, acols))[:, None, :]
    mod_blk = _ada_fwd_call(c_all, ada_w, ada_b_my)
    pay = _Payload()
    pay.gather(mod_blk)
    (mod_g,), _ = _comm_call(pay, "gather_mod")
    mod_me = lax.dynamic_index_in_dim(mod_g, me, axis=2, keepdims=False)
    modv = jnp.swapaxes(mod_me, 0, 1).reshape(nl, 6, D)
    modv = jnp.concatenate([modv, jnp.zeros((nl, 2, D), F32)], axis=1)

    ln1 = jnp.stack([ln1_g, ln1_b], axis=1)
    ln2 = jnp.stack([ln2_g, ln2_b], axis=1)
    sln = jnp.stack([sgu_ln_g, sgu_ln_b], axis=1)
    sbf = jnp.broadcast_to(sgu_b[..., None], sgu_b.shape + (GW,))
    small = (pool_w, pool_scale[:, None, :], sln, sgu_w, sbf)

    r1s, r2s, f1s, f2s, ahs, ghs, wins, wouts, wups, wds, st1s, st2s = ([None] * nl for _ in range(12))
    wins[0], wouts[0] = winT, wout
    xin = x2
    for l in range(nl):
        pay = None
        if l == 0:
            pay = _Payload()
            pay.gather(wupT_sh, 0)
            pay.gather(wd_sh, 0)
        (r1, f1, *ah, st_prev), new, _ = _fa_call(l, l == 0, alpha, xin, modv, ln2, wins[l], wouts[l], small, pay)
        if l > 0:
            st2s[l - 1] = st_prev
        if l == 0:
            wups[0], wds[0] = new[0].reshape(4, FH, D), new[1].reshape(2, FH, D)
        pay = None
        if l + 1 < nl:
            pay = _Payload()
            pay.gather(winT_sh, l + 1)
            pay.gather(wout_sh, l + 1)
            pay.gather(wupT_sh, l + 1)
            pay.gather(wd_sh, l + 1)
        (r2, f2, *saved_b, st1s[l]), new, _ = _fb_call(l, alpha, r1, modv, ln1, wups[l], wds[l], cw_full, cb_full,
                                                       pay)
        if l + 1 < nl:
            wins[l + 1], wouts[l + 1] = new[0].reshape(DIN, D), new[1].reshape(D, D)
            wups[l + 1], wds[l + 1] = new[2].reshape(4, FH, D), new[3].reshape(2, FH, D)
        r1s[l], r2s[l], f1s[l], f2s[l], ahs[l], ghs[l] = r1, r2, f1, f2, ah, saved_b
        xin = r2

    nsmall = 6 * D + 2 * NG * GW * GW + 3 * DP + NG * GW + 4 * D + 4 * FF
    srows = nsmall // 1024
    buf_in = lax.empty((nl, NDEV, icols, D), BF16)
    buf_out = lax.empty((nl, NDEV, orows, D), BF16)
    buf_up = lax.empty((nl, NDEV, ucols, D), BF16)
    buf_down = lax.empty((nl, NDEV, drows, D), BF16)
    buf_small = lax.empty((nl, NDEV, srows, 1024), F32)

    def a_side_payload(l, dwin, dwout, flat):
        p = _Payload()
        p.exchange_into([(dwin.reshape(NDEV, icols, D), 0)], buf_in, (l,))
        p.exchange_into([(dwout.reshape(NDEV, orows, D), 0)], buf_out, (l,))
        p.gather_into(flat, buf_small, (l,))
        return p

    dx = tgt
    loss_acc = None
    pending = None
    for l in reversed(range(nl)):
        last = l == nl - 1
        pay = None if pending is None else a_side_payload(*pending)
        h2, gs, vs, gcs, x1s = ghs[l]
        res0, _, al = _bb0_call(l, last, h2, r2s[l], st2s[l], dx, f2s[l], gs, vs, gcs, modv, ln2, wups[l], wds[l],
                                cw_full, pay)
        if pending is not None:
            buf_in, buf_out, buf_small = al
        dh2p, df2, dr2, dwg0, dwv0, dwd0, vb0, cacc0 = res0[:8]
        if last:
            loss_acc = res0[8]
        (dx1, dwg1, dwv1, dwd1, vb1, cacc1), _, _ = _bb1_call(
            l, alpha, h2, x1s, df2, dr2, dh2p, gs, vs, gcs, modv, wups[l], wds[l], cw_full)
        pay = _Payload()
        pay.exchange_into([(dwg0.reshape(2, ucols, D), 0), (dwg1.reshape(2, ucols, D), 2),
                           (dwv0.reshape(2, ucols, D), 4), (dwv1.reshape(2, ucols, D), 6)], buf_up, (l,))
        pay.exchange_into([(dwd0.reshape(4, drows, D), 0), (dwd1.reshape(4, drows, D), 4)], buf_down, (l,))
        (dx, dwin, dwout, dpw, dsw, dsb, va, va5), _, (buf_up, buf_down) = _ba_call(
            l, alpha, r1s[l], st1s[l], dx1, f1s[l], *ahs[l], modv, ln1, wins[l], wouts[l], small, pay)
        cacc = jnp.concatenate([cacc0, cacc1], axis=1)
        piece = dict(ada_b=jnp.stack([va[3], va[4], va[2], vb1[0], vb1[1], vb0[2]]), pool_w=dpw,
                     pool_scale=va5[0], sgu_ln_g=va5[1], sgu_ln_b=va5[2], sgu_w=dsw, sgu_b=dsb[:, :, 0],
                     ln1_g=va[0], ln1_b=va[1], conv_w=cacc[1:4], conv_b=cacc[0], ln2_g=vb0[0], ln2_b=vb0[1])
        flat = jnp.concatenate([piece[n].reshape(-1) for n in _SMALL_ORDER]).reshape(srows, 1024)
        pending = (l, dwin, dwout, flat)
    grad_x = dx[None]
    loss = lax.psum((0.5 / D) * jnp.sum(loss_acc), ("x", "y", "c"))

    res = {}
    _, dwin0, dwout0, flat0 = pending
    t_ = lambda a: jnp.swapaxes(a, 1, 2)
    pay = _Payload()
    pay.exchange_into([(dwout0.reshape(NDEV, orows, D), 0)], buf_out, (0,))
    pay.gather_into(flat0, buf_small, (0,))
    outs, (buf_out, buf_small) = _adamw_call(buf_up, t_(w_up), t_(m_w_up), t_(v_w_up), "adamw_w_up", pay)
    res["w_up"] = tuple(t_(o) for o in outs)
    pay = _Payload()
    pay.exchange_into([(dwin0.reshape(NDEV, icols, D), 0)], buf_in, (0,))
    res["w_down"], (buf_in,) = _adamw_call(buf_down, w_down, m_w_down, v_w_down, "adamw_w_down", pay)
    outs, _ = _adamw_call(buf_in, t_(w_in), t_(m_w_in), t_(v_w_in), "adamw_w_in")
    res["w_in"] = tuple(t_(o) for o in outs)
    res["w_out"], _ = _adamw_call(buf_out, w_out, m_w_out, v_w_out, "adamw_w_out")

    weights = dict(ada_b=ada_b, pool_w=pool_w, pool_scale=pool_scale, sgu_ln_g=sgu_ln_g, sgu_ln_b=sgu_ln_b,
                   sgu_w=sgu_w, sgu_b=sgu_b, ln1_g=ln1_g, ln1_b=ln1_b, conv_w=conv_w, conv_b=conv_b, ln2_g=ln2_g,
                   ln2_b=ln2_b)
    m_of = dict(ada_b=m_ada_b, pool_w=m_pool_w, pool_scale=m_pool_scale, sgu_ln_g=m_sgu_ln_g, sgu_ln_b=m_sgu_ln_b,
                sgu_w=m_sgu_w, sgu_b=m_sgu_b, ln1_g=m_ln1_g, ln1_b=m_ln1_b, conv_w=m_conv_w, conv_b=m_conv_b,
                ln2_g=m_ln2_g, ln2_b=m_ln2_b)
    v_of = dict(ada_b=v_ada_b, pool_w=v_pool_w, pool_scale=v_pool_scale, sgu_ln_g=v_sgu_ln_g, sgu_ln_b=v_sgu_ln_b,
                sgu_w=v_sgu_w, sgu_b=v_sgu_b, ln1_g=v_ln1_g, ln1_b=v_ln1_b, conv_w=v_conv_w, conv_b=v_conv_b,
                ln2_g=v_ln2_g, ln2_b=v_ln2_b)
    row = 0
    dmod_my = None
    tail = None
    for name in _SMALL_ORDER:
        w = weights[name]
        per_layer = FF * 3 if name == "conv_w" else w[0].size
        if per_layer % 1024 == 0:
            nrow = per_layer // 1024
            parts = buf_small[:, :, row:row + nrow, :]
            row += nrow
            if name == "ada_b":
                dmod_my = lax.dynamic_slice_in_dim(parts.reshape(nl, NDEV, 6 * D), me * acols, acols, axis=2)
        else:
            if tail is None:
                tail = buf_small[:, :, row:, :].reshape(nl, NDEV, (srows - row) * 1024)
                off = 0
            parts = tail[:, :, off:off + per_layer]
            off += per_layer
        if name == "conv_w":
            parts = lax.dynamic_slice_in_dim(parts.reshape(nl, NDEV, 3, FF), me * ccols, ccols, axis=3)
        cdim = w.shape[-1]
        w3 = w.reshape(nl, -1, cdim)
        outs, _ = _adamw_call(parts.reshape(nl, NDEV, -1, cdim), w3, m_of[name].reshape(w3.shape),
                              v_of[name].reshape(w3.shape), f"adamw_{name}")
        res[name] = tuple(o.reshape(w.shape) for o in outs)

    res["ada_w"] = _ada_bwd_call(jnp.swapaxes(c_all, 0, 1), dmod_my, ada_w, m_ada_w, v_ada_w)

    order = ["ada_w", "ada_b", "w_in", "pool_w", "pool_scale", "sgu_ln_g", "sgu_ln_b", "sgu_w", "sgu_b", "w_out",
             "ln1_g", "ln1_b", "w_up", "conv_w", "conv_b", "w_down", "ln2_g", "ln2_b"]
    out = [loss, grad_x]
    for k in range(4):
        out += [res[n][k] for n in order]
    return tuple(out)
```

```python
import jax
import jax.numpy as jnp
from jax import lax
from jax.experimental import pallas as pl
from jax.experimental.pallas import tpu as pltpu

F32 = jnp.float32
BF16 = jnp.bfloat16

NDEV = 8
D = 1024
DP = 512
DS = 512
DIN = DP + 2 * DS
FF = 2816
FH = FF // 2
FF_CHUNKS = ((0, 1536), (1536, FF))
NG = 4
GW = 128
WINDOWS = (2, 4, 8, 16)
AHALO = 16
GHALO = 8
LN_EPS = 1e-5
ADAM_LR, ADAM_B1, ADAM_B2, ADAM_EPS, ADAM_WD, ADAM_STEP = 0.001, 0.9, 0.999, 1e-08, 0.01, 10
TSF = 512
TSB = 256
_K0 = 0.7978845608028654
_K1 = 0.044715
MESH_ID = pl.DeviceIdType.MESH


def _mm(a, b):
    return jnp.dot(a, b, preferred_element_type=F32)


def _mm_nt(a, b):
    return lax.dot_general(a, b, (((1,), (1,)), ((), ())), preferred_element_type=F32)


def _mm_tn(a, b):
    return lax.dot_general(a, b, (((0,), (0,)), ((), ())), preferred_element_type=F32)


def _rowmean(x):
    return jnp.mean(x, axis=-1, keepdims=True)


def _ln_stats(x):
    mu = _rowmean(x)
    xc = x - mu
    rstd = lax.rsqrt(_rowmean(xc * xc) + LN_EPS)
    return xc * rstd, rstd


def _ln_stats_tile(x):
    mu = _rowmean(x)
    xc = x - mu
    rstd = lax.rsqrt(_rowmean(xc * xc) + LN_EPS)
    lane = lax.broadcasted_iota(jnp.int32, (x.shape[0], 128), 1)
    return xc * rstd, jnp.where(lane == 0, mu, jnp.where(lane == 1, rstd, 0.0))


def _ln_from_tile(x, st):
    rstd = st[:, 1:2]
    return (x - st[:, 0:1]) * rstd, rstd


def _ln_bwd(dy, gamma, xhat, rstd):
    dxh = dy * gamma
    return rstd * (dxh - _rowmean(dxh) - xhat * _rowmean(dxh * xhat))


def _gelu_t(x):
    t = jnp.tanh(x * (_K0 + (_K0 * _K1) * (x * x)))
    hx = 0.5 * x
    return hx + hx * t, t


def _dgelu(x, t):
    return (0.5 + 0.5 * t) + (0.5 * x) * (1.0 - t * t) * (_K0 + (3.0 * _K0 * _K1) * (x * x))


def _colsum8(x):
    t, n = x.shape
    return jnp.sum(x.reshape(t // 8, 8, n), axis=0)


def _tril_mask():
    r = lax.broadcasted_iota(jnp.int32, (GW, GW), 0)
    c = lax.broadcasted_iota(jnp.int32, (GW, GW), 1)
    return c <= r


def _full(shape):
    n = len(shape)
    return pl.BlockSpec(shape, lambda *_: (0,) * n)


def _resident(tail, lead=()):
    n = len(tail)
    return pl.BlockSpec((None,) * len(lead) + tuple(tail), lambda *_: tuple(lead) + (0,) * n,
                        pipeline_mode=pl.Buffered(1))


def _layer_vec(rows, width, l):
    return pl.BlockSpec((None, rows, width), lambda *_: (l, 0, 0))


_VMEM_WHOLE = pl.BlockSpec(memory_space=pltpu.VMEM)
_HBM = pl.BlockSpec(memory_space=pl.ANY)
_ARB = pltpu.CompilerParams(dimension_semantics=("arbitrary",))


def _me():
    return 4 * lax.axis_index("x") + 2 * lax.axis_index("y") + lax.axis_index("c")


def _coords(p):
    return (p >> 2, (p >> 1) & 1, p & 1)


class _Payload:
    def __init__(self):
        self.srcs, self.new, self.alias, self.transfers = [], [], [], []

    def _src(self, arr):
        self.srcs.append(arr)
        return len(self.srcs) - 1

    def _alias(self, buf):
        self.alias.append(buf)
        return len(self.alias) - 1

    def gather(self, arr, chunk=None):
        pos = self._src(arr)
        blk = arr.shape if chunk is None else arr.shape[1:]
        self.new.append(jax.ShapeDtypeStruct((NDEV,) + tuple(blk), arr.dtype))
        self.transfers.append(([(pos, chunk)] * NDEV, ("new", len(self.new) - 1), ()))
        return len(self.new) - 1

    def gather_into(self, arr, buf, lead):
        pos = self._src(arr)
        self.transfers.append(([(pos, None)] * NDEV, ("alias", self._alias(buf)), tuple(lead)))

    def exchange_into(self, parts, buf, lead):
        route = {}
        for arr, first in parts:
            pos = self._src(arr)
            for q in range(arr.shape[0]):
                route[first + q] = (pos, q)
        self.transfers.append(([route[p] for p in range(NDEV)], ("alias", self._alias(buf)), tuple(lead)))

    def _ends(self, t, io, dst_dev, src_dev):
        srcs, new_out, alias_out = io
        route, (kind, k), lead = self.transfers[t]
        pos, q = route[dst_dev]
        src = srcs[pos] if q is None else srcs[pos].at[q]
        buf = new_out[k] if kind == "new" else alias_out[k]
        return src, buf.at[lead + (src_dev,)]

    def _remote(self, t, io, sems, src_dev, dst_dev):
        src, dst = self._ends(t, io, dst_dev, src_dev)
        return pltpu.make_async_remote_copy(
            src_ref=src, dst_ref=dst, send_sem=sems[0].at[t, dst_dev], recv_sem=sems[1].at[t, src_dev],
            device_id=_coords(dst_dev), device_id_type=MESH_ID)

    def _local(self, t, io, sems, p):
        src, dst = self._ends(t, io, p, p)
        return pltpu.make_async_copy(src, dst, sems[2].at[t])

    def start(self, io, sems):
        me = _me()
        for p in range(NDEV):
            @pl.when(me == p)
            def _():
                for t in range(len(self.transfers)):
                    self._local(t, io, sems, p).start()

            @pl.when(me != p)
            def _():
                for t in range(len(self.transfers)):
                    self._remote(t, io, sems, me, p).start()

    def wait(self, io, sems):
        me = _me()
        for p in range(NDEV):
            @pl.when(me == p)
            def _():
                for t in range(len(self.transfers)):
                    self._local(t, io, sems, p).wait()

            @pl.when(me != p)
            def _():
                for t in range(len(self.transfers)):
                    self._remote(t, io, sems, p, p).wait_recv()
                    self._remote(t, io, sems, me, p).wait_send()


def _pcall(body, name, nsteps, in_specs, out_specs, out_shape, scratch, args, pay=None):
    n_in, n_out, n_scr = len(args), len(out_shape), len(scratch)
    if pay is None:
        res = pl.pallas_call(body, name=name, grid=(nsteps,), in_specs=list(in_specs), out_specs=list(out_specs),
                             out_shape=list(out_shape), scratch_shapes=list(scratch), compiler_params=_ARB)(*args)
        return list(res), [], []
    ns, nn, na, nt = len(pay.srcs), len(pay.new), len(pay.alias), len(pay.transfers)

    def full(*refs):
        cin = refs[:n_in]
        srcs = refs[n_in:n_in + ns]
        o0 = n_in + ns + na
        cout = refs[o0:o0 + n_out]
        new_out = refs[o0 + n_out:o0 + n_out + nn]
        alias_out = refs[o0 + n_out + nn:o0 + n_out + nn + na]
        s0 = o0 + n_out + nn + na
        cscr = refs[s0:s0 + n_scr]
        sems = refs[s0 + n_scr:]
        io = (srcs, new_out, alias_out)
        i = pl.program_id(0)

        @pl.when(i == 0)
        def _():
            pay.start(io, sems)

        body(*cin, *cout, *cscr)

        @pl.when(i == nsteps - 1)
        def _():
            pay.wait(io, sems)

    res = pl.pallas_call(
        full, name=name, grid=(nsteps,),
        in_specs=list(in_specs) + [_HBM] * (ns + na), out_specs=list(out_specs) + [_HBM] * (nn + na),
        out_shape=list(out_shape) + pay.new + [jax.ShapeDtypeStruct(b.shape, b.dtype) for b in pay.alias],
        input_output_aliases={n_in + ns + k: n_out + nn + k for k in range(na)},
        scratch_shapes=list(scratch) + [pltpu.SemaphoreType.DMA((nt, NDEV)), pltpu.SemaphoreType.DMA((nt, NDEV)),
                                        pltpu.SemaphoreType.DMA((nt,))],
        compiler_params=_ARB,
    )(*args, *pay.srcs, *pay.alias)
    return list(res[:n_out]), list(res[n_out:n_out + nn]), list(res[n_out + nn:])


def _comm_call(pay, name):
    def body():
        pass

    _, new, alias = _pcall(body, name, 1, [], [], [], [], [], pay)
    return new, alias


def _window_sums(x, halo, before):
    ts = x.shape[0]
    ext = jnp.concatenate([halo, x] if before else [x, halo], axis=0)
    n = ts + AHALO
    shift = (lambda k: k) if before else (lambda k: n - k)
    keep = slice(AHALO, n) if before else slice(0, ts)
    out = []
    s = ext
    for level in range(NG):
        s = s + pltpu.roll(s, shift(1 << level), 0)
        out.append(s[keep, 0:GW])
        if level + 1 < NG:
            s = s[:, GW:]
    return out


def _a_forward(a, u, v, halo, pw_ref, ps_ref, sln_ref, sw_ref, sbf_ref, zbuf, tile, ts, pooled=None):
    tglob = tile * ts + lax.broadcasted_iota(jnp.int32, (ts, 1), 0)
    sums = _window_sums(a, halo, True) if pooled is None else None
    pooled_b, mixed, inv_cnt, pwb = [], [], [], []
    for g, w in enumerate(WINDOWS):
        inv = 1.0 / jnp.minimum(tglob + 1, w).astype(F32)
        if pooled is None:
            pg = (sums[g] * inv - a[:, g * GW:(g + 1) * GW]).astype(BF16)
        else:
            pg = pooled[:, g * GW:(g + 1) * GW]
        wg = pw_ref[g].astype(BF16)
        pooled_b.append(pg)
        inv_cnt.append(inv)
        pwb.append(wg)
        mixed.append(_mm(pg, wg))
    mixed = jnp.concatenate(mixed, axis=1)
    ya = mixed * ps_ref[...]
    ug, tu = _gelu_t(u)
    vg, tv = _gelu_t(v)
    vhat, rstdv = _ln_stats(vg)
    vnb = (vhat * sln_ref[0:1, :] + sln_ref[1:2, :]).astype(BF16)
    tri = _tril_mask()
    wt = [jnp.where(tri, sw_ref[h], 0.0).astype(BF16) for h in range(NG)]
    for c in range(ts // GW):
        rs = slice(c * GW, (c + 1) * GW)
        for h in range(NG):
            cs = slice(h * GW, (h + 1) * GW)
            zbuf[rs, cs] = _mm(wt[h], vnb[rs, cs]) + sbf_ref[h]
    z = zbuf[...]
    yb = ug * z
    return dict(u=u, v=v, tu=tu, tv=tv, ug=ug, z=z, vhat=vhat, rstdv=rstdv, vnb=vnb,
                wt=wt, pooled_b=pooled_b, pwb=pwb, inv_cnt=inv_cnt, mixed=mixed, ya=ya, yb=yb)


def _small_specs(l):
    grp = pl.BlockSpec((None, NG, GW, GW), lambda i: (l, 0, 0, 0))
    return [grp, _layer_vec(1, DP, l), _layer_vec(2, DS, l), grp, grp]


def _fa_call(l, first, alpha, xin, modv, lnp, winT, wout, small, pay=None):
    s = xin.shape[0]
    nt = s // TSF

    def body(xin_ref, modv_ref, lnp_ref, winT_ref, wout_ref, pw_ref, ps_ref, sln_ref, sw_ref, sbf_ref,
             r1_ref, f1_ref, h_ref, proj_ref, x_ref, pooled_ref, st_ref, acarry, zbuf, mixbuf):
        i = pl.program_id(0)

        @pl.when(i == 0)
        def _():
            acarry[...] = jnp.zeros_like(acarry)

        x = xin_ref[...]
        if first:
            st_ref[...] = jnp.zeros_like(st_ref)
        else:
            xhat, st_ref[...] = _ln_stats_tile(x)
            x = xhat * lnp_ref[0:1, :] + lnp_ref[1:2, :]
        hb = (x * (1.0 + modv_ref[1:2, :]) + modv_ref[0:1, :]).astype(BF16)
        proj = _mm_nt(hb, winT_ref[...])
        a = proj[:, 0:DP]
        fw = _a_forward(a, proj[:, DP:DP + DS], proj[:, DP + DS:], acarry[...], pw_ref, ps_ref, sln_ref, sw_ref,
                        sbf_ref, zbuf, i, TSF)
        acarry[...] = a[TSF - AHALO:, :]
        mixbuf[:, 0:DP] = fw["ya"].astype(BF16)
        mixbuf[:, DP:] = fw["yb"].astype(BF16)
        f = _mm(mixbuf[...], wout_ref[...])
        r1_ref[...] = alpha * x + modv_ref[2:3, :] * f
        f1_ref[...] = f.astype(BF16)
        h_ref[...] = hb
        proj_ref[...] = proj.astype(BF16)
        x_ref[...] = x.astype(BF16)
        pooled_ref[...] = jnp.concatenate(fw["pooled_b"], axis=1)

    tile = pl.BlockSpec((TSF, D), lambda i: (i, 0))
    return _pcall(
        body, f"fa{l}", nt,
        in_specs=[tile, _layer_vec(8, D, l), _layer_vec(2, D, max(l - 1, 0)), _resident((DIN, D)),
                  _resident((D, D))] + _small_specs(l),
        out_specs=[tile, tile, tile, pl.BlockSpec((TSF, DIN), lambda i: (i, 0)), tile,
                   pl.BlockSpec((TSF, DP), lambda i: (i, 0)), pl.BlockSpec((TSF, 128), lambda i: (i, 0))],
        out_shape=[jax.ShapeDtypeStruct((s, D), F32), jax.ShapeDtypeStruct((s, D), BF16),
                   jax.ShapeDtypeStruct((s, D), BF16), jax.ShapeDtypeStruct((s, DIN), BF16),
                   jax.ShapeDtypeStruct((s, D), BF16), jax.ShapeDtypeStruct((s, DP), BF16),
                   jax.ShapeDtypeStruct((s, 128), F32)],
        scratch=[pltpu.VMEM((AHALO, DP), F32), pltpu.VMEM((TSF, DS), F32), pltpu.VMEM((TSF, D), BF16)],
        args=[xin, modv, lnp, winT, wout, *small], pay=pay)


def _ba_call(l, alpha, r1, st1, dx1, f1, hsave, proj, xsave, pooled, modv, ln1, winT, wout, small, pay=None):
    s = r1.shape[0]
    nt = s // TSB
    ts = TSB

    def body(r1_ref, st1_ref, dx1_ref, f1_ref, h_ref, proj_ref, xs_ref, pooled_ref, modv_ref, ln1_ref, winT_ref,
             wout_ref, pw_ref, ps_ref, sln_ref, sw_ref, sbf_ref,
             dx_ref, dwin_ref, dwout_ref, dpw_ref, dsw_ref, dsb_ref, va_ref, va5_ref,
             qnext, zbuf, dvnbuf, mixbuf, dpbuf, dbacc, vacc, vacc5, dwin_acc, dwout_acc):
        i = pl.program_id(0)
        j = nt - 1 - i

        @pl.when(i == 0)
        def _():
            dwin_acc[...] = jnp.zeros_like(dwin_acc)
            dwout_acc[...] = jnp.zeros_like(dwout_acc)
            dpw_ref[...] = jnp.zeros_like(dpw_ref)
            dsw_ref[...] = jnp.zeros_like(dsw_ref)
            dbacc[...] = jnp.zeros_like(dbacc)
            vacc[...] = jnp.zeros_like(vacc)
            vacc5[...] = jnp.zeros_like(vacc5)
            qnext[...] = jnp.zeros_like(qnext)

        hb = h_ref[...]
        sc1 = modv_ref[1:2, :]
        fw = _a_forward(None, proj_ref[:, DP:DP + DS].astype(F32), proj_ref[:, DP + DS:].astype(F32), None, pw_ref,
                        ps_ref, sln_ref, sw_ref, sbf_ref, zbuf, j, ts, pooled=pooled_ref[...])
        mixbuf[:, 0:DP] = fw["ya"].astype(BF16)
        mixbuf[:, DP:] = fw["yb"].astype(BF16)

        xhat1, rstd1 = _ln_from_tile(r1_ref[...], st1_ref[...])
        dy = dx1_ref[...]
        vacc[0] += _colsum8(dy * xhat1)
        vacc[1] += _colsum8(dy)
        dr1 = _ln_bwd(dy, ln1_ref[0:1, :], xhat1, rstd1)
        vacc[2] += _colsum8(dr1 * f1_ref[...].astype(F32))
        dfb = (dr1 * modv_ref[2:3, :]).astype(BF16)
        dwout_acc[...] += _mm_tn(mixbuf[...], dfb)
        dmix = _mm_nt(dfb, wout_ref[...])
        dya = dmix[:, 0:DP]
        dyb = dmix[:, DP:]

        vacc5[0] += _colsum8(dya * fw["mixed"])
        dmixed = (dya * ps_ref[...]).astype(BF16)
        dpooled, q = [], []
        for g in range(NG):
            cs = slice(g * GW, (g + 1) * GW)
            dpw_ref[g] += _mm_tn(fw["pooled_b"][g], dmixed[:, cs])
            dpg = _mm_nt(dmixed[:, cs], fw["pwb"][g])
            dpooled.append(dpg)
            q.append(dpg * fw["inv_cnt"][g])
        q = jnp.concatenate(q, axis=1)
        sums = _window_sums(q, qnext[...], False)
        qnext[...] = q[0:AHALO, :]
        for g in range(NG):
            dpbuf[:, g * GW:(g + 1) * GW] = (sums[g] - dpooled[g]).astype(BF16)

        dug = dyb * fw["z"]
        dz = dyb * fw["ug"]
        dzb = dz.astype(BF16)
        for c in range(ts // GW):
            rs = slice(c * GW, (c + 1) * GW)
            for h in range(NG):
                cs = slice(h * GW, (h + 1) * GW)
                dvnbuf[rs, cs] = _mm_tn(fw["wt"][h], dzb[rs, cs])
                dsw_ref[h] += _mm_nt(dzb[rs, cs], fw["vnb"][rs, cs])
            dbacc[...] += dz[rs, :]
        dvn = dvnbuf[...]
        vacc5[1] += _colsum8(dvn * fw["vhat"])
        vacc5[2] += _colsum8(dvn)
        dvg = _ln_bwd(dvn, sln_ref[0:1, :], fw["vhat"], fw["rstdv"])
        dpbuf[:, DP:DP + DS] = (dug * _dgelu(fw["u"], fw["tu"])).astype(BF16)
        dpbuf[:, DP + DS:] = (dvg * _dgelu(fw["v"], fw["tv"])).astype(BF16)

        dpb = dpbuf[...]
        dwin_acc[...] += _mm_tn(dpb, hb)
        dh = _mm(dpb, winT_ref[...])
        dx_ref[...] = dh * (1.0 + sc1) + alpha * dr1
        vacc[3] += _colsum8(dh)
        vacc[4] += _colsum8(dh * xs_ref[...].astype(F32))

        @pl.when(i == nt - 1)
        def _():
            dwin_ref[...] = dwin_acc[...].astype(BF16)
            dwout_ref[...] = dwout_acc[...].astype(BF16)
            tri = _tril_mask()
            for h in range(NG):
                dsw_ref[h] = jnp.where(tri, dsw_ref[h], 0.0)
                sb = jnp.sum(dbacc[:, h * GW:(h + 1) * GW], axis=1, keepdims=True)
                dsb_ref[h] = jnp.broadcast_to(sb, (GW, GW))
            for n in range(5):
                va_ref[n:n + 1, :] = jnp.sum(vacc[n], axis=0, keepdims=True)
            for n in range(3):
                va5_ref[n:n + 1, :] = jnp.sum(vacc5[n], axis=0, keepdims=True)

    rev = lambda i: (nt - 1 - i, 0)
    tile = pl.BlockSpec((ts, D), rev)
    return _pcall(
        body, f"ba{l}", nt,
        in_specs=[tile, pl.BlockSpec((ts, 128), rev), tile, tile, tile, pl.BlockSpec((ts, DIN), rev), tile,
                  pl.BlockSpec((ts, DP), rev),
                  _layer_vec(8, D, l), _layer_vec(2, D, l),
                  _resident((DIN, D)), _resident((D, D))] + _small_specs(l),
        out_specs=[tile] + [_VMEM_WHOLE] * 7,
        out_shape=[jax.ShapeDtypeStruct((s, D), F32), jax.ShapeDtypeStruct((DIN, D), BF16),
                   jax.ShapeDtypeStruct((D, D), BF16), jax.ShapeDtypeStruct((NG, GW, GW), F32),
                   jax.ShapeDtypeStruct((NG, GW, GW), F32), jax.ShapeDtypeStruct((NG, GW, GW), F32),
                   jax.ShapeDtypeStruct((5, D), F32), jax.ShapeDtypeStruct((3, DP), F32)],
        scratch=[pltpu.VMEM((AHALO, DP), F32),
                 pltpu.VMEM((ts, DS), F32), pltpu.VMEM((ts, DS), F32), pltpu.VMEM((ts, D), BF16),
                 pltpu.VMEM((ts, DIN), BF16), pltpu.VMEM((GW, DS), F32), pltpu.VMEM((5, 8, D), F32),
                 pltpu.VMEM((3, 8, DP), F32), pltpu.VMEM((DIN, D), F32), pltpu.VMEM((D, D), F32)],
        args=[r1, st1, dx1, f1, hsave, proj, xsave, pooled, modv, ln1, winT, wout, *small], pay=pay)


def _cast_call(arrs, name):
    r, c = arrs[0].shape
    rb = _row_block(r, 6 * c * len(arrs))

    def body(*refs):
        for src, dst in zip(refs[:len(arrs)], refs[len(arrs):]):
            dst[...] = src[...].astype(BF16)

    blk = pl.BlockSpec((rb, c), lambda i: (i, 0))
    return pl.pallas_call(
        body, name=name, grid=(r // rb,), in_specs=[blk] * len(arrs), out_specs=[blk] * len(arrs),
        out_shape=[jax.ShapeDtypeStruct((r, c), BF16)] * len(arrs), compiler_params=_ARB)(*arrs)


def _rows_before(halo, x):
    ext = jnp.concatenate([halo, x], axis=0)
    return pltpu.roll(ext, 1, 0)[GHALO:, :], pltpu.roll(ext, 2, 0)[GHALO:, :]


def _rows_after(x, halo):
    ts = x.shape[0]
    ext = jnp.concatenate([x, halo], axis=0)
    n = ts + GHALO
    return pltpu.roll(ext, n - 1, 0)[0:ts, :], pltpu.roll(ext, n - 2, 0)[0:ts, :]


def _fb_call(l, alpha, r1, modv, ln1, wup4, wd2, conv_w, conv_b, pay=None):
    s = r1.shape[0]
    nt = s // TSF

    def body(r1_ref, modv_ref, ln1_ref, wup_ref, wd_ref, cw_ref, cb_ref, r2_ref, f2_ref, h2_ref, gs_ref, vs_ref,
             gc_ref, x1_ref, st_ref, gbuf):
        i = pl.program_id(0)

        @pl.when(i == 0)
        def _():
            gbuf[...] = jnp.zeros_like(gbuf)

        xhat1, st_ref[...] = _ln_stats_tile(r1_ref[...])
        x1 = xhat1 * ln1_ref[0:1, :] + ln1_ref[1:2, :]
        h2b = (x1 * (1.0 + modv_ref[4:5, :]) + modv_ref[3:4, :]).astype(BF16)
        f2 = jnp.zeros((TSF, D), F32)
        for c0, c1 in FF_CHUNKS:
            cs = slice(c0, c1)
            g = _mm_nt(h2b, wup_ref[0, cs, :])
            val = _mm_nt(h2b, wup_ref[1, cs, :])
            gm1, gm2 = _rows_before(gbuf[:, cs], g)
            cw = cw_ref[:, cs]
            gc = cb_ref[:, cs] + cw[0:1, :] * gm2 + cw[1:2, :] * gm1 + cw[2:3, :] * g
            ge, _ = _gelu_t(gc)
            f2 = f2 + _mm((ge * val).astype(BF16), wd_ref[cs, :])
            gs_ref[:, cs] = g.astype(BF16)
            vs_ref[:, cs] = val.astype(BF16)
            gc_ref[:, cs] = gc.astype(BF16)
            gbuf[:, cs] = g[TSF - GHALO:, :]
        r2_ref[...] = alpha * x1 + modv_ref[5:6, :] * f2
        f2_ref[...] = f2.astype(BF16)
        h2_ref[...] = h2b
        x1_ref[...] = x1.astype(BF16)

    tile = pl.BlockSpec((TSF, D), lambda i: (i, 0))
    wide = pl.BlockSpec((TSF, FF), lambda i: (i, 0))
    return _pcall(
        body, f"fb{l}", nt,
        in_specs=[tile, _layer_vec(8, D, l), _layer_vec(2, D, l), _resident((2, FF, D)), _resident((FF, D)),
                  _layer_vec(3, FF, l), _layer_vec(1, FF, l)],
        out_specs=[tile, tile, tile, wide, wide, wide, tile, pl.BlockSpec((TSF, 128), lambda i: (i, 0))],
        out_shape=[jax.ShapeDtypeStruct((s, D), F32), jax.ShapeDtypeStruct((s, D), BF16),
                   jax.ShapeDtypeStruct((s, D), BF16), jax.ShapeDtypeStruct((s, FF), BF16),
                   jax.ShapeDtypeStruct((s, FF), BF16), jax.ShapeDtypeStruct((s, FF), BF16),
                   jax.ShapeDtypeStruct((s, D), BF16), jax.ShapeDtypeStruct((s, 128), F32)],
        scratch=[pltpu.VMEM((GHALO, FF), F32)],
        args=[r1, modv, ln1, wup4.reshape(2, FF, D), wd2.reshape(FF, D), conv_w, conv_b], pay=pay)


def _ff_backward(h2b, df2b, gs_ref, vs_ref, gc_ref, dgc_next, cw, wg_ref, wv_ref, wd_ref,
                 dwg_ref, dwv_ref, dwd_ref, cacc):
    g = gs_ref[...].astype(F32)
    val = vs_ref[...].astype(F32)
    gc = gc_ref[...].astype(F32)
    ge, tg = _gelu_t(gc)
    mb = (ge * val).astype(BF16)
    dm = _mm_nt(df2b, wd_ref[...])
    dwd_ref[...] += _mm_tn(mb, df2b)
    dval = dm * ge
    dgc = dm * val * _dgelu(gc, tg)
    dgp1, dgp2 = _rows_after(dgc, dgc_next[...])
    cacc[0] += _colsum8(dgc)
    cacc[1] += _colsum8(dgp2 * g)
    cacc[2] += _colsum8(dgp1 * g)
    cacc[3] += _colsum8(dgc * g)
    dg = cw[2:3, :] * dgc + cw[1:2, :] * dgp1 + cw[0:1, :] * dgp2
    dgc_next[...] = dgc[0:GHALO, :]
    dgb = dg.astype(BF16)
    dvb = dval.astype(BF16)
    dwg_ref[...] += _mm_tn(dgb, h2b)
    dwv_ref[...] += _mm_tn(dvb, h2b)
    return _mm(dgb, wg_ref[...]) + _mm(dvb, wv_ref[...])


def _bb_specs(l, hf, nt, ts):
    wide = pl.BlockSpec((ts, FH), lambda i: (nt - 1 - i, hf))
    ins = [wide, wide, wide, _resident((FH, D), (hf,)), _resident((FH, D), (2 + hf,)), _resident((FH, D), (hf,)),
           pl.BlockSpec((None, 3, FH), lambda i: (l, 0, hf))]
    acc_shapes = [jax.ShapeDtypeStruct((FH, D), BF16)] * 3
    return ins, acc_shapes, [pltpu.VMEM((FH, D), F32)] * 3


def _bb0_call(l, last, h2, r2, st2, dx2, f2, gs, vs, gcs, modv, ln2, wup4, wd2, conv_w, pay=None):
    s = r2.shape[0]
    nt = s // TSB
    ts = TSB

    def body(*refs):
        it = iter(refs)
        h2_ref, r2_ref, dx2_ref, f2_ref, gs_ref, vs_ref, gc_ref = (next(it) for _ in range(7))
        wg_ref, wv_ref, wd_ref, cw_ref, modv_ref, ln2_ref = (next(it) for _ in range(6))
        st2_ref = None if last else next(it)
        dh2_ref, df2_ref, dr2_ref, dwg_out, dwv_out, dwd_out, vb_ref, cacc_ref = (next(it) for _ in range(8))
        loss_ref = next(it) if last else None
        vacc, cacc, dgc_next, dwg_ref, dwv_ref, dwd_ref = (next(it) for _ in range(6))
        lacc = next(it) if last else None
        i = pl.program_id(0)

        @pl.when(i == 0)
        def _():
            dwg_ref[...] = jnp.zeros_like(dwg_ref)
            dwv_ref[...] = jnp.zeros_like(dwv_ref)
            dwd_ref[...] = jnp.zeros_like(dwd_ref)
            vacc[...] = jnp.zeros_like(vacc)
            cacc[...] = jnp.zeros_like(cacc)
            dgc_next[...] = jnp.zeros_like(dgc_next)
            if last:
                lacc[...] = jnp.zeros_like(lacc)

        if last:
            xhat2, rstd2 = _ln_stats(r2_ref[...])
        else:
            xhat2, rstd2 = _ln_from_tile(r2_ref[...], st2_ref[...])
        if last:
            diff = xhat2 * ln2_ref[0:1, :] + ln2_ref[1:2, :] - dx2_ref[...]
            dy = diff * (1.0 / D)
            lacc[...] += _colsum8(diff * diff)
        else:
            dy = dx2_ref[...]
        dr2 = _ln_bwd(dy, ln2_ref[0:1, :], xhat2, rstd2)
        vacc[0] += _colsum8(dy * xhat2)
        vacc[1] += _colsum8(dy)
        vacc[2] += _colsum8(dr2 * f2_ref[...].astype(F32))
        df2b = (dr2 * modv_ref[5:6, :]).astype(BF16)
        dr2_ref[...] = dr2
        df2_ref[...] = df2b
        dh2_ref[...] = _ff_backward(h2_ref[...], df2b, gs_ref, vs_ref, gc_ref, dgc_next, cw_ref[...],
                                    wg_ref, wv_ref, wd_ref, dwg_ref, dwv_ref, dwd_ref, cacc)

        @pl.when(i == nt - 1)
        def _():
            for n in range(3):
                vb_ref[n:n + 1, :] = jnp.sum(vacc[n], axis=0, keepdims=True)
            for n in range(4):
                cacc_ref[n:n + 1, :] = jnp.sum(cacc[n], axis=0, keepdims=True)
            dwg_out[...] = dwg_ref[...].astype(BF16)
            dwv_out[...] = dwv_ref[...].astype(BF16)
            dwd_out[...] = dwd_ref[...].astype(BF16)
            if last:
                loss_ref[...] = lacc[...]

    tile = pl.BlockSpec((ts, D), lambda i: (nt - 1 - i, 0))
    ff_ins, acc_shapes, acc_scratch = _bb_specs(l, 0, nt, ts)
    in_specs = [tile, tile, tile, tile] + ff_ins + [_layer_vec(8, D, l), _layer_vec(2, D, l)]
    args = [h2, r2, dx2, f2, gs, vs, gcs, wup4, wup4, wd2, conv_w, modv, ln2]
    if not last:
        in_specs.append(pl.BlockSpec((ts, 128), lambda i: (nt - 1 - i, 0)))
        args.append(st2)
    out_specs = [tile, tile, tile] + [_VMEM_WHOLE] * 5
    out_shape = [jax.ShapeDtypeStruct((s, D), F32), jax.ShapeDtypeStruct((s, D), BF16),
                 jax.ShapeDtypeStruct((s, D), F32)] + acc_shapes + [jax.ShapeDtypeStruct((3, D), F32),
                                                                    jax.ShapeDtypeStruct((4, FH), F32)]
    scratch = [pltpu.VMEM((3, 8, D), F32), pltpu.VMEM((4, 8, FH), F32), pltpu.VMEM((GHALO, FH), F32)] + acc_scratch
    if last:
        out_specs.append(_VMEM_WHOLE)
        out_shape.append(jax.ShapeDtypeStruct((8, D), F32))
        scratch.append(pltpu.VMEM((8, D), F32))
    return _pcall(body, f"bb{l}_0", nt, in_specs, out_specs, out_shape, scratch, args, pay=pay)


def _bb1_call(l, alpha, h2, x1s, df2, dr2, dh2_in, gs, vs, gcs, modv, wup4, wd2, conv_w, pay=None):
    s = h2.shape[0]
    nt = s // TSB
    ts = TSB

    def body(h2_ref, x1_ref, df2_ref, dr2_ref, dh2_ref, gs_ref, vs_ref, gc_ref, wg_ref, wv_ref, wd_ref, cw_ref,
             modv_ref, dx1_ref, dwg_out, dwv_out, dwd_out, vb_ref, cacc_ref, vacc, cacc, dgc_next,
             dwg_ref, dwv_ref, dwd_ref):
        i = pl.program_id(0)

        @pl.when(i == 0)
        def _():
            dwg_ref[...] = jnp.zeros_like(dwg_ref)
            dwv_ref[...] = jnp.zeros_like(dwv_ref)
            dwd_ref[...] = jnp.zeros_like(dwd_ref)
            vacc[...] = jnp.zeros_like(vacc)
            cacc[...] = jnp.zeros_like(cacc)
            dgc_next[...] = jnp.zeros_like(dgc_next)

        dh2 = dh2_ref[...] + _ff_backward(h2_ref[...], df2_ref[...], gs_ref, vs_ref, gc_ref, dgc_next,
                                          cw_ref[...], wg_ref, wv_ref, wd_ref, dwg_ref, dwv_ref, dwd_ref, cacc)
        dx1_ref[...] = dh2 * (1.0 + modv_ref[4:5, :]) + alpha * dr2_ref[...]
        vacc[0] += _colsum8(dh2)
        vacc[1] += _colsum8(dh2 * x1_ref[...].astype(F32))

        @pl.when(i == nt - 1)
        def _():
            dwg_out[...] = dwg_ref[...].astype(BF16)
            dwv_out[...] = dwv_ref[...].astype(BF16)
            dwd_out[...] = dwd_ref[...].astype(BF16)
            for n in range(2):
                vb_ref[n:n + 1, :] = jnp.sum(vacc[n], axis=0, keepdims=True)
            for n in range(4):
                cacc_ref[n:n + 1, :] = jnp.sum(cacc[n], axis=0, keepdims=True)

    tile = pl.BlockSpec((ts, D), lambda i: (nt - 1 - i, 0))
    ff_ins, acc_shapes, acc_scratch = _bb_specs(l, 1, nt, ts)
    out_shape = [jax.ShapeDtypeStruct((s, D), F32)] + acc_shapes + [jax.ShapeDtypeStruct((2, D), F32),
                                                                   jax.ShapeDtypeStruct((4, FH), F32)]
    scratch = [pltpu.VMEM((2, 8, D), F32), pltpu.VMEM((4, 8, FH), F32), pltpu.VMEM((GHALO, FH), F32)] + acc_scratch
    return _pcall(body, f"bb{l}_1", nt, [tile] * 5 + ff_ins + [_layer_vec(8, D, l)],
                  [tile] + [_VMEM_WHOLE] * 5, out_shape, scratch,
                  [h2, x1s, df2, dr2, dh2_in, gs, vs, gcs, wup4, wup4, wd2, conv_w, modv], pay=pay)


def _silu(c):
    return c * (1.0 / (1.0 + jnp.exp(-c)))


def _ada_fwd_call(c_all, ada_w, ada_b_my):
    nl, _, wcols = ada_w.shape

    def body(c_ref, w_ref, b_ref, o_ref):
        ca = _silu(c_ref[...])
        o_ref[...] = jnp.dot(ca, w_ref[...], preferred_element_type=F32,
                             precision=lax.Precision.HIGHEST) + b_ref[...]

    return pl.pallas_call(
        body, name="ada_fwd", grid=(nl,),
        in_specs=[_full((NDEV, D)), pl.BlockSpec((None, D, wcols), lambda l: (l, 0, 0)),
                  pl.BlockSpec((None, 1, wcols), lambda l: (l, 0, 0))],
        out_specs=pl.BlockSpec((None, NDEV, wcols), lambda l: (l, 0, 0)),
        out_shape=jax.ShapeDtypeStruct((nl, NDEV, wcols), F32),
        compiler_params=_ARB,
    )(c_all, ada_w, ada_b_my)


def _adam_update(w, g, m, v):
    m2 = ADAM_B1 * m + (1.0 - ADAM_B1) * g
    v2 = ADAM_B2 * v + (1.0 - ADAM_B2) * (g * g)
    m_hat = m2 / (1.0 - ADAM_B1 ** ADAM_STEP)
    v_hat = v2 / (1.0 - ADAM_B2 ** ADAM_STEP)
    delta = -ADAM_LR * (m_hat / (jnp.sqrt(v_hat) + ADAM_EPS) + ADAM_WD * w)
    return delta, m2, v2


def _ada_bwd_call(c_t, dmod_my, w, m, v):
    nl, _, wcols = w.shape
    rb = 256

    def body(ct_ref, dm_ref, w_ref, m_ref, v_ref, g_ref, d_ref, m2_ref, v2_ref):
        ca_t = _silu(ct_ref[...])
        dm = dm_ref[...]
        g = ca_t[:, 0:1] * dm[0:1, :]
        for b in range(1, NDEV):
            g = g + ca_t[:, b:b + 1] * dm[b:b + 1, :]
        delta, m2, v2 = _adam_update(w_ref[...], g, m_ref[...], v_ref[...])
        g_ref[...] = g
        d_ref[...] = delta
        m2_ref[...] = m2
        v2_ref[...] = v2

    blk = pl.BlockSpec((None, rb, wcols), lambda l, i: (l, i, 0))
    shp = jax.ShapeDtypeStruct(w.shape, F32)
    return pl.pallas_call(
        body, name="ada_bwd", grid=(nl, D // rb),
        in_specs=[pl.BlockSpec((rb, NDEV), lambda l, i: (i, 0)),
                  pl.BlockSpec((None, NDEV, wcols), lambda l, i: (l, 0, 0)), blk, blk, blk],
        out_specs=[blk, blk, blk, blk], out_shape=[shp, shp, shp, shp],
        compiler_params=pltpu.CompilerParams(dimension_semantics=("arbitrary", "arbitrary")),
    )(c_t, dmod_my, w, m, v)


def _row_block(r, row_bytes):
    budget = 6 * 1024 * 1024
    best = None
    for rb in range(16, min(r, 512) + 1, 16):
        if r % rb == 0 and rb * row_bytes <= budget:
            best = rb
    return best if best is not None else r


def _adamw_call(parts, w, m, v, name, pay=None):
    nl, npart, r, c = parts.shape
    rb = _row_block(r, (npart + 7) * c * 4)
    nb = r // rb

    def body(p_ref, w_ref, m_ref, v_ref, g_ref, d_ref, m2_ref, v2_ref):
        g = p_ref[0].astype(F32)
        for k in range(1, npart):
            g = g + p_ref[k].astype(F32)
        delta, m2, v2 = _adam_update(w_ref[...], g, m_ref[...], v_ref[...])
        g_ref[...] = g
        d_ref[...] = delta
        m2_ref[...] = m2
        v2_ref[...] = v2

    blk = pl.BlockSpec((None, rb, c), lambda i: (i // nb, i % nb, 0))
    shp = jax.ShapeDtypeStruct((nl, r, c), F32)
    outs, _, alias = _pcall(body, name, nl * nb,
                            [pl.BlockSpec((None, npart, rb, c), lambda i: (i // nb, 0, i % nb, 0)), blk, blk, blk],
                            [blk] * 4, [shp] * 4, [], [parts, w, m, v], pay)
    return tuple(outs), alias


_SMALL_ORDER = ("ada_b", "pool_w", "sgu_w", "ln1_g", "ln1_b", "ln2_g", "ln2_b", "pool_scale", "sgu_ln_g", "sgu_ln_b",
                "sgu_b", "conv_w", "conv_b")


def kernel(x, c, ada_w, ada_b, w_in, pool_w, pool_scale, sgu_ln_g, sgu_ln_b, sgu_w, sgu_b, w_out, ln1_g, ln1_b, w_up, conv_w, conv_b, w_down, ln2_g, ln2_b, loss_target, m_ada_w, m_ada_b, m_w_in, m_pool_w, m_pool_scale, m_sgu_ln_g, m_sgu_ln_b, m_sgu_w, m_sgu_b, m_w_out, m_ln1_g, m_ln1_b, m_w_up, m_conv_w, m_conv_b, m_w_down, m_ln2_g, m_ln2_b, v_ada_w, v_ada_b, v_w_in, v_pool_w, v_pool_scale, v_sgu_ln_g, v_sgu_ln_b, v_sgu_w, v_sgu_b, v_w_out, v_ln1_g, v_ln1_b, v_w_up, v_conv_w, v_conv_b, v_w_down, v_ln2_g, v_ln2_b):
    nl = ada_w.shape[0]
    alpha = (2.0 * nl) ** 0.25
    me = _me()
    x2 = x[0]
    tgt = loss_target[0]
    acols = ada_w.shape[2]
    icols = w_in.shape[2]
    ucols = w_up.shape[2]
    orows = w_out.shape[1]
    drows = w_down.shape[1]
    ccols = conv_w.shape[2]

    winT_sh = jnp.swapaxes(w_in, 1, 2).astype(BF16)
    wupT_sh = jnp.swapaxes(w_up, 1, 2).astype(BF16)
    wout_sh = w_out.astype(BF16)
    wd_sh = w_down.astype(BF16)

    pay = _Payload()
    pay.gather(jnp.broadcast_to(c, (8, D)))
    pay.gather(winT_sh, 0)
    pay.gather(wout_sh, 0)
    pay.gather(conv_w)
    (c_g, winT_g, wout_g, cw_g), _ = _comm_call(pay, "gather_first")
    c_all = c_g[:, 0, :]
    winT = winT_g.reshape(DIN, D)
    wout = wout_g.reshape(D, D)
    cw_full = jnp.transpose(cw_g, (1, 2, 0, 3)).reshape(nl, 3, FF)
    cb_full = conv_b[:, None, :]

    ada_b_my = lax.dynamic_slice(ada_b, (0, me * acols), (nl, acols))[:, None, :]
    mod_blk = _ada_fwd_call(c_all, ada_w, ada_b_my)
    pay = _Payload()
    pay.gather(mod_blk)
    (mod_g,), _ = _comm_call(pay, "gather_mod")
    mod_me = lax.dynamic_index_in_dim(mod_g, me, axis=2, keepdims=False)
    modv = jnp.swapaxes(mod_me, 0, 1).reshape(nl, 6, D)
    modv = jnp.concatenate([modv, jnp.zeros((nl, 2, D), F32)], axis=1)

    ln1 = jnp.stack([ln1_g, ln1_b], axis=1)
    ln2 = jnp.stack([ln2_g, ln2_b], axis=1)
    sln = jnp.stack([sgu_ln_g, sgu_ln_b], axis=1)
    sbf = jnp.broadcast_to(sgu_b[..., None], sgu_b.shape + (GW,))
    small = (pool_w, pool_scale[:, None, :], sln, sgu_w, sbf)

    r1s, r2s, f1s, f2s, ahs, ghs, wins, wouts, wups, wds, st1s, st2s = ([None] * nl for _ in range(12))
    wins[0], wouts[0] = winT, wout
    xin = x2
    for l in range(nl):
        pay = None
        if l == 0:
            pay = _Payload()
            pay.gather(wupT_sh, 0)
            pay.gather(wd_sh, 0)
        (r1, f1, *ah, st_prev), new, _ = _fa_call(l, l == 0, alpha, xin, modv, ln2, wins[l], wouts[l], small, pay)
        if l > 0:
            st2s[l - 1] = st_prev
        if l == 0:
            wups[0], wds[0] = new[0].reshape(4, FH, D), new[1].reshape(2, FH, D)
        pay = None
        if l + 1 < nl:
            pay = _Payload()
            pay.gather(winT_sh, l + 1)
            pay.gather(wout_sh, l + 1)
            pay.gather(wupT_sh, l + 1)
            pay.gather(wd_sh, l + 1)
        (r2, f2, *saved_b, st1s[l]), new, _ = _fb_call(l, alpha, r1, modv, ln1, wups[l], wds[l], cw_full, cb_full,
                                                       pay)
        if l + 1 < nl:
            wins[l + 1], wouts[l + 1] = new[0].reshape(DIN, D), new[1].reshape(D, D)
            wups[l + 1], wds[l + 1] = new[2].reshape(4, FH, D), new[3].reshape(2, FH, D)
        r1s[l], r2s[l], f1s[l], f2s[l], ahs[l], ghs[l] = r1, r2, f1, f2, ah, saved_b
        xin = r2

    nsmall = 6 * D + 2 * NG * GW * GW + 3 * DP + NG * GW + 4 * D + 4 * FF
    srows = nsmall // 1024
    buf_in = lax.empty((nl, NDEV, icols, D), BF16)
    buf_out = lax.empty((nl, NDEV, orows, D), BF16)
    buf_up = lax.empty((nl, NDEV, ucols, D), BF16)
    buf_down = lax.empty((nl, NDEV, drows, D), BF16)
    buf_small = lax.empty((nl, NDEV, srows, 1024), F32)

    def a_side_payload(l, dwin, dwout, flat):
        p = _Payload()
        p.exchange_into([(dwin.reshape(NDEV, icols, D), 0)], buf_in, (l,))
        p.exchange_into([(dwout.reshape(NDEV, orows, D), 0)], buf_out, (l,))
        p.gather_into(flat, buf_small, (l,))
        return p

    dx = tgt
    loss_acc = None
    pending = None
    for l in reversed(range(nl)):
        last = l == nl - 1
        pay = None if pending is None else a_side_payload(*pending)
        h2, gs, vs, gcs, x1s = ghs[l]
        res0, _, al = _bb0_call(l, last, h2, r2s[l], st2s[l], dx, f2s[l], gs, vs, gcs, modv, ln2, wups[l], wds[l],
                                cw_full, pay)
        if pending is not None:
            buf_in, buf_out, buf_small = al
        dh2p, df2, dr2, dwg0, dwv0, dwd0, vb0, cacc0 = res0[:8]
        if last:
            loss_acc = res0[8]
        (dx1, dwg1, dwv1, dwd1, vb1, cacc1), _, _ = _bb1_call(
            l, alpha, h2, x1s, df2, dr2, dh2p, gs, vs, gcs, modv, wups[l], wds[l], cw_full)
        pay = _Payload()
        pay.exchange_into([(dwg0.reshape(2, ucols, D), 0), (dwg1.reshape(2, ucols, D), 2),
                           (dwv0.reshape(2, ucols, D), 4), (dwv1.reshape(2, ucols, D), 6)], buf_up, (l,))
        pay.exchange_into([(dwd0.reshape(4, drows, D), 0), (dwd1.reshape(4, drows, D), 4)], buf_down, (l,))
        (dx, dwin, dwout, dpw, dsw, dsb, va, va5), _, (buf_up, buf_down) = _ba_call(
            l, alpha, r1s[l], st1s[l], dx1, f1s[l], *ahs[l], modv, ln1, wins[l], wouts[l], small, pay)
        cacc = jnp.concatenate([cacc0, cacc1], axis=1)
        piece = dict(ada_b=jnp.stack([va[3], va[4], va[2], vb1[0], vb1[1], vb0[2]]), pool_w=dpw,
                     pool_scale=va5[0], sgu_ln_g=va5[1], sgu_ln_b=va5[2], sgu_w=dsw, sgu_b=dsb[:, :, 0],
                     ln1_g=va[0], ln1_b=va[1], conv_w=cacc[1:4], conv_b=cacc[0], ln2_g=vb0[0], ln2_b=vb0[1])
        flat = jnp.concatenate([piece[n].reshape(-1) for n in _SMALL_ORDER]).reshape(srows, 1024)
        pending = (l, dwin, dwout, flat)
    grad_x = dx[None]
    loss = lax.psum((0.5 / D) * jnp.sum(loss_acc), ("x", "y", "c"))

    res = {}
    _, dwin0, dwout0, flat0 = pending
    t_ = lambda a: jnp.swapaxes(a, 1, 2)
    pay = _Payload()
    pay.exchange_into([(dwout0.reshape(NDEV, orows, D), 0)], buf_out, (0,))
    pay.gather_into(flat0, buf_small, (0,))
    outs, (buf_out, buf_small) = _adamw_call(buf_up, t_(w_up), t_(m_w_up), t_(v_w_up), "adamw_w_up", pay)
    res["w_up"] = tuple(t_(o) for o in outs)
    pay = _Payload()
    pay.exchange_into([(dwin0.reshape(NDEV, icols, D), 0)], buf_in, (0,))
    res["w_down"], (buf_in,) = _adamw_call(buf_down, w_down, m_w_down, v_w_down, "adamw_w_down", pay)
    outs, _ = _adamw_call(buf_in, t_(w_in), t_(m_w_in), t_(v_w_in), "adamw_w_in")
    res["w_in"] = tuple(t_(o) for o in outs)
    res["w_out"], _ = _adamw_call(buf_out, w_out, m_w_out, v_w_out, "adamw_w_out")

    weights = dict(ada_b=ada_b, pool_w=pool_w, pool_scale=pool_scale, sgu_ln_g=sgu_ln_g, sgu_ln_b=sgu_ln_b,
                   sgu_w=sgu_w, sgu_b=sgu_b, ln1_g=ln1_g, ln1_b=ln1_b, conv_w=conv_w, conv_b=conv_b, ln2_g=ln2_g,
                   ln2_b=ln2_b)
    m_of = dict(ada_b=m_ada_b, pool_w=m_pool_w, pool_scale=m_pool_scale, sgu_ln_g=m_sgu_ln_g, sgu_ln_b=m_sgu_ln_b,
                sgu_w=m_sgu_w, sgu_b=m_sgu_b, ln1_g=m_ln1_g, ln1_b=m_ln1_b, conv_w=m_conv_w, conv_b=m_conv_b,
                ln2_g=m_ln2_g, ln2_b=m_ln2_b)
    v_of = dict(ada_b=v_ada_b, pool_w=v_pool_w, pool_scale=v_pool_scale, sgu_ln_g=v_sgu_ln_g, sgu_ln_b=v_sgu_ln_b,
                sgu_w=v_sgu_w, sgu_b=v_sgu_b, ln1_g=v_ln1_g, ln1_b=v_ln1_b, conv_w=v_conv_w, conv_b=v_conv_b,
                ln2_g=v_ln2_g, ln2_b=v_ln2_b)
    row = 0
    dmod_my = None
    tail = None
    for name in _SMALL_ORDER:
        w = weights[name]
        per_layer = FF * 3 if name == "conv_w" else w[0].size
        if per_layer % 1024 == 0:
            nrow = per_layer // 1024
            parts = buf_small[:, :, row:row + nrow, :]
            row += nrow
            if name == "ada_b":
                dmod_my = lax.dynamic_slice_in_dim(parts.reshape(nl, NDEV, 6 * D), me * acols, acols, axis=2)
        else:
            if tail is None:
                tail = buf_small[:, :, row:, :].reshape(nl, NDEV, (srows - row) * 1024)
                off = 0
            parts = tail[:, :, off:off + per_layer]
            off += per_layer
        if name == "conv_w":
            parts = lax.dynamic_slice_in_dim(parts.reshape(nl, NDEV, 3, FF), me * ccols, ccols, axis=3)
        cdim = w.shape[-1]
        w3 = w.reshape(nl, -1, cdim)
        outs, _ = _adamw_call(parts.reshape(nl, NDEV, -1, cdim), w3, m_of[name].reshape(w3.shape),
                              v_of[name].reshape(w3.shape), f"adamw_{name}")
        res[name] = tuple(o.reshape(w.shape) for o in outs)

    res["ada_w"] = _ada_bwd_call(jnp.swapaxes(c_all, 0, 1), dmod_my, ada_w, m_ada_w, v_ada_w)

    order = ["ada_w", "ada_b", "w_in", "pool_w", "pool_scale", "sgu_ln_g", "sgu_ln_b", "sgu_w", "sgu_b", "w_out",
             "ln1_g", "ln1_b", "w_up", "conv_w", "conv_b", "w_down", "ln2_g", "ln2_b"]
    out = [loss, grad_x]
    for k in range(4):
        out += [res[n][k] for n in order]
    return tuple(out)
```

```python
import jax
import jax.numpy as jnp
from jax import lax
from jax.experimental import pallas as pl
from jax.experimental.pallas import tpu as pltpu

F32 = jnp.float32
BF16 = jnp.bfloat16

NDEV = 8
D = 1024
DP = 512
DS = 512
DIN = DP + 2 * DS
FF = 2816
FH = FF // 2
FF_CHUNKS = ((0, 1536), (1536, FF))
NG = 4
GW = 128
WINDOWS = (2, 4, 8, 16)
AHALO = 16
GHALO = 8
LN_EPS = 1e-5
ADAM_LR, ADAM_B1, ADAM_B2, ADAM_EPS, ADAM_WD, ADAM_STEP = 0.001, 0.9, 0.999, 1e-08, 0.01, 10
TSF = 512
TSB = 256
_K0 = 0.7978845608028654
_K1 = 0.044715
MESH_ID = pl.DeviceIdType.MESH


def _mm(a, b):
    return jnp.dot(a, b, preferred_element_type=F32)


def _mm_nt(a, b):
    return lax.dot_general(a, b, (((1,), (1,)), ((), ())), preferred_element_type=F32)


def _mm_tn(a, b):
    return lax.dot_general(a, b, (((0,), (0,)), ((), ())), preferred_element_type=F32)


def _rowmean(x):
    return jnp.mean(x, axis=-1, keepdims=True)


def _ln_stats(x):
    mu = _rowmean(x)
    xc = x - mu
    rstd = lax.rsqrt(_rowmean(xc * xc) + LN_EPS)
    return xc * rstd, rstd


def _ln_stats_tile(x):
    mu = _rowmean(x)
    xc = x - mu
    rstd = lax.rsqrt(_rowmean(xc * xc) + LN_EPS)
    lane = lax.broadcasted_iota(jnp.int32, (x.shape[0], 128), 1)
    return xc * rstd, jnp.where(lane == 0, mu, jnp.where(lane == 1, rstd, 0.0))


def _ln_from_tile(x, st):
    rstd = st[:, 1:2]
    return (x - st[:, 0:1]) * rstd, rstd


def _ln_bwd(dy, gamma, xhat, rstd):
    dxh = dy * gamma
    return rstd * (dxh - _rowmean(dxh) - xhat * _rowmean(dxh * xhat))


def _gelu_t(x):
    t = jnp.tanh(x * (_K0 + (_K0 * _K1) * (x * x)))
    hx = 0.5 * x
    return hx + hx * t, t


def _dgelu(x, t):
    return (0.5 + 0.5 * t) + (0.5 * x) * (1.0 - t * t) * (_K0 + (3.0 * _K0 * _K1) * (x * x))


def _colsum8(x):
    t, n = x.shape
    return jnp.sum(x.reshape(t // 8, 8, n), axis=0)


def _tril_mask():
    r = lax.broadcasted_iota(jnp.int32, (GW, GW), 0)
    c = lax.broadcasted_iota(jnp.int32, (GW, GW), 1)
    return c <= r


def _full(shape):
    n = len(shape)
    return pl.BlockSpec(shape, lambda *_: (0,) * n)


def _resident(tail, lead=()):
    n = len(tail)
    return pl.BlockSpec((None,) * len(lead) + tuple(tail), lambda *_: tuple(lead) + (0,) * n,
                        pipeline_mode=pl.Buffered(1))


def _layer_vec(rows, width, l):
    return pl.BlockSpec((None, rows, width), lambda *_: (l, 0, 0))


_VMEM_WHOLE = pl.BlockSpec(memory_space=pltpu.VMEM)
_HBM = pl.BlockSpec(memory_space=pl.ANY)
_ARB = pltpu.CompilerParams(dimension_semantics=("arbitrary",))


def _me():
    return 4 * lax.axis_index("x") + 2 * lax.axis_index("y") + lax.axis_index("c")


def _coords(p):
    return (p >> 2, (p >> 1) & 1, p & 1)


class _Payload:
    def __init__(self):
        self.srcs, self.new, self.alias, self.transfers = [], [], [], []

    def _src(self, arr):
        self.srcs.append(arr)
        return len(self.srcs) - 1

    def _alias(self, buf):
        self.alias.append(buf)
        return len(self.alias) - 1

    def gather(self, arr, chunk=None):
        pos = self._src(arr)
        blk = arr.shape if chunk is None else arr.shape[1:]
        self.new.append(jax.ShapeDtypeStruct((NDEV,) + tuple(blk), arr.dtype))
        self.transfers.append(([(pos, chunk)] * NDEV, ("new", len(self.new) - 1), ()))
        return len(self.new) - 1

    def gather_into(self, arr, buf, lead):
        pos = self._src(arr)
        self.transfers.append(([(pos, None)] * NDEV, ("alias", self._alias(buf)), tuple(lead)))

    def exchange_into(self, parts, buf, lead):
        route = {}
        for arr, first in parts:
            pos = self._src(arr)
            for q in range(arr.shape[0]):
                route[first + q] = (pos, q)
        self.transfers.append(([route[p] for p in range(NDEV)], ("alias", self._alias(buf)), tuple(lead)))

    def _ends(self, t, io, dst_dev, src_dev):
        srcs, new_out, alias_out = io
        route, (kind, k), lead = self.transfers[t]
        pos, q = route[dst_dev]
        src = srcs[pos] if q is None else srcs[pos].at[q]
        buf = new_out[k] if kind == "new" else alias_out[k]
        return src, buf.at[lead + (src_dev,)]

    def _remote(self, t, io, sems, src_dev, dst_dev):
        src, dst = self._ends(t, io, dst_dev, src_dev)
        return pltpu.make_async_remote_copy(
            src_ref=src, dst_ref=dst, send_sem=sems[0].at[t, dst_dev], recv_sem=sems[1].at[t, src_dev],
            device_id=_coords(dst_dev), device_id_type=MESH_ID)

    def _local(self, t, io, sems, p):
        src, dst = self._ends(t, io, p, p)
        return pltpu.make_async_copy(src, dst, sems[2].at[t])

    def start(self, io, sems):
        me = _me()
        for p in range(NDEV):
            @pl.when(me == p)
            def _():
                for t in range(len(self.transfers)):
                    self._local(t, io, sems, p).start()

            @pl.when(me != p)
            def _():
                for t in range(len(self.transfers)):
                    self._remote(t, io, sems, me, p).start()

    def wait(self, io, sems):
        me = _me()
        for p in range(NDEV):
            @pl.when(me == p)
            def _():
                for t in range(len(self.transfers)):
                    self._local(t, io, sems, p).wait()

            @pl.when(me != p)
            def _():
                for t in range(len(self.transfers)):
                    self._remote(t, io, sems, p, p).wait_recv()
                    self._remote(t, io, sems, me, p).wait_send()


def _pcall(body, name, nsteps, in_specs, out_specs, out_shape, scratch, args, pay=None):
    n_in, n_out, n_scr = len(args), len(out_shape), len(scratch)
    if pay is None:
        res = pl.pallas_call(body, name=name, grid=(nsteps,), in_specs=list(in_specs), out_specs=list(out_specs),
                             out_shape=list(out_shape), scratch_shapes=list(scratch), compiler_params=_ARB)(*args)
        return list(res), [], []
    ns, nn, na, nt = len(pay.srcs), len(pay.new), len(pay.alias), len(pay.transfers)

    def full(*refs):
        cin = refs[:n_in]
        srcs = refs[n_in:n_in + ns]
        o0 = n_in + ns + na
        cout = refs[o0:o0 + n_out]
        new_out = refs[o0 + n_out:o0 + n_out + nn]
        alias_out = refs[o0 + n_out + nn:o0 + n_out + nn + na]
        s0 = o0 + n_out + nn + na
        cscr = refs[s0:s0 + n_scr]
        sems = refs[s0 + n_scr:]
        io = (srcs, new_out, alias_out)
        i = pl.program_id(0)

        @pl.when(i == 0)
        def _():
            pay.start(io, sems)

        body(*cin, *cout, *cscr)

        @pl.when(i == nsteps - 1)
        def _():
            pay.wait(io, sems)

    res = pl.pallas_call(
        full, name=name, grid=(nsteps,),
        in_specs=list(in_specs) + [_HBM] * (ns + na), out_specs=list(out_specs) + [_HBM] * (nn + na),
        out_shape=list(out_shape) + pay.new + [jax.ShapeDtypeStruct(b.shape, b.dtype) for b in pay.alias],
        input_output_aliases={n_in + ns + k: n_out + nn + k for k in range(na)},
        scratch_shapes=list(scratch) + [pltpu.SemaphoreType.DMA((nt, NDEV)), pltpu.SemaphoreType.DMA((nt, NDEV)),
                                        pltpu.SemaphoreType.DMA((nt,))],
        compiler_params=_ARB,
    )(*args, *pay.srcs, *pay.alias)
    return list(res[:n_out]), list(res[n_out:n_out + nn]), list(res[n_out + nn:])


def _comm_call(pay, name):
    def body():
        pass

    _, new, alias = _pcall(body, name, 1, [], [], [], [], [], pay)
    return new, alias


def _window_sums(x, halo, before):
    ts = x.shape[0]
    ext = jnp.concatenate([halo, x] if before else [x, halo], axis=0)
    n = ts + AHALO
    shift = (lambda k: k) if before else (lambda k: n - k)
    keep = slice(AHALO, n) if before else slice(0, ts)
    out = []
    s = ext
    for level in range(NG):
        s = s + pltpu.roll(s, shift(1 << level), 0)
        out.append(s[keep, 0:GW])
        if level + 1 < NG:
            s = s[:, GW:]
    return out


def _a_forward(a, u, v, halo, pw_ref, ps_ref, sln_ref, sw_ref, sbf_ref, zbuf, tile, ts, pooled=None):
    tglob = tile * ts + lax.broadcasted_iota(jnp.int32, (ts, 1), 0)
    sums = _window_sums(a, halo, True) if pooled is None else None
    pooled_b, mixed, inv_cnt, pwb = [], [], [], []
    for g, w in enumerate(WINDOWS):
        inv = 1.0 / jnp.minimum(tglob + 1, w).astype(F32)
        if pooled is None:
            pg = (sums[g] * inv - a[:, g * GW:(g + 1) * GW]).astype(BF16)
        else:
            pg = pooled[:, g * GW:(g + 1) * GW]
        wg = pw_ref[g].astype(BF16)
        pooled_b.append(pg)
        inv_cnt.append(inv)
        pwb.append(wg)
        mixed.append(_mm(pg, wg))
    mixed = jnp.concatenate(mixed, axis=1)
    ya = mixed * ps_ref[...]
    ug, tu = _gelu_t(u)
    vg, tv = _gelu_t(v)
    vhat, rstdv = _ln_stats(vg)
    vnb = (vhat * sln_ref[0:1, :] + sln_ref[1:2, :]).astype(BF16)
    tri = _tril_mask()
    wt = [jnp.where(tri, sw_ref[h], 0.0).astype(BF16) for h in range(NG)]
    for c in range(ts // GW):
        rs = slice(c * GW, (c + 1) * GW)
        for h in range(NG):
            cs = slice(h * GW, (h + 1) * GW)
            zbuf[rs, cs] = _mm(wt[h], vnb[rs, cs]) + sbf_ref[h]
    z = zbuf[...]
    yb = ug * z
    return dict(u=u, v=v, tu=tu, tv=tv, ug=ug, z=z, vhat=vhat, rstdv=rstdv, vnb=vnb,
                wt=wt, pooled_b=pooled_b, pwb=pwb, inv_cnt=inv_cnt, mixed=mixed, ya=ya, yb=yb)


def _small_specs(l):
    grp = pl.BlockSpec((None, NG, GW, GW), lambda i: (l, 0, 0, 0))
    return [grp, _layer_vec(1, DP, l), _layer_vec(2, DS, l), grp, grp]


def _fa_call(l, first, alpha, xin, modv, lnp, winT, wout, small, pay=None):
    s = xin.shape[0]
    nt = s // TSF

    def body(xin_ref, modv_ref, lnp_ref, winT_ref, wout_ref, pw_ref, ps_ref, sln_ref, sw_ref, sbf_ref,
             r1_ref, f1_ref, h_ref, proj_ref, x_ref, pooled_ref, st_ref, acarry, zbuf, mixbuf):
        i = pl.program_id(0)

        @pl.when(i == 0)
        def _():
            acarry[...] = jnp.zeros_like(acarry)

        x = xin_ref[...]
        if first:
            st_ref[...] = jnp.zeros_like(st_ref)
        else:
            xhat, st_ref[...] = _ln_stats_tile(x)
            x = xhat * lnp_ref[0:1, :] + lnp_ref[1:2, :]
        hb = (x * (1.0 + modv_ref[1:2, :]) + modv_ref[0:1, :]).astype(BF16)
        proj = _mm_nt(hb, winT_ref[...])
        a = proj[:, 0:DP]
        fw = _a_forward(a, proj[:, DP:DP + DS], proj[:, DP + DS:], acarry[...], pw_ref, ps_ref, sln_ref, sw_ref,
                        sbf_ref, zbuf, i, TSF)
        acarry[...] = a[TSF - AHALO:, :]
        mixbuf[:, 0:DP] = fw["ya"].astype(BF16)
        mixbuf[:, DP:] = fw["yb"].astype(BF16)
        f = _mm(mixbuf[...], wout_ref[...])
        r1_ref[...] = alpha * x + modv_ref[2:3, :] * f
        f1_ref[...] = f.astype(BF16)
        h_ref[...] = hb
        proj_ref[...] = proj.astype(BF16)
        x_ref[...] = x.astype(BF16)
        pooled_ref[...] = jnp.concatenate(fw["pooled_b"], axis=1)

    tile = pl.BlockSpec((TSF, D), lambda i: (i, 0))
    return _pcall(
        body, f"fa{l}", nt,
        in_specs=[tile, _layer_vec(8, D, l), _layer_vec(2, D, max(l - 1, 0)), _resident((DIN, D)),
                  _resident((D, D))] + _small_specs(l),
        out_specs=[tile, tile, tile, pl.BlockSpec((TSF, DIN), lambda i: (i, 0)), tile,
                   pl.BlockSpec((TSF, DP), lambda i: (i, 0)), pl.BlockSpec((TSF, 128), lambda i: (i, 0))],
        out_shape=[jax.ShapeDtypeStruct((s, D), F32), jax.ShapeDtypeStruct((s, D), BF16),
                   jax.ShapeDtypeStruct((s, D), BF16), jax.ShapeDtypeStruct((s, DIN), BF16),
                   jax.ShapeDtypeStruct((s, D), BF16), jax.ShapeDtypeStruct((s, DP), BF16),
                   jax.ShapeDtypeStruct((s, 128), F32)],
        scratch=[pltpu.VMEM((AHALO, DP), F32), pltpu.VMEM((TSF, DS), F32), pltpu.VMEM((TSF, D), BF16)],
        args=[xin, modv, lnp, winT, wout, *small], pay=pay)


def _ba_call(l, alpha, r1, st1, dx1, f1, hsave, proj, xsave, pooled, modv, ln1, winT, wout, small, pay=None):
    s = r1.shape[0]
    nt = s // TSB
    ts = TSB

    def body(r1_ref, st1_ref, dx1_ref, f1_ref, h_ref, proj_ref, xs_ref, pooled_ref, modv_ref, ln1_ref, winT_ref,
             wout_ref, pw_ref, ps_ref, sln_ref, sw_ref, sbf_ref,
             dx_ref, dwin_ref, dwout_ref, dpw_ref, dsw_ref, dsb_ref, va_ref, va5_ref,
             qnext, zbuf, dvnbuf, mixbuf, dpbuf, dbacc, vacc, vacc5, dwin_acc, dwout_acc):
        i = pl.program_id(0)
        j = nt - 1 - i

        @pl.when(i == 0)
        def _():
            dwin_acc[...] = jnp.zeros_like(dwin_acc)
            dwout_acc[...] = jnp.zeros_like(dwout_acc)
            dpw_ref[...] = jnp.zeros_like(dpw_ref)
            dsw_ref[...] = jnp.zeros_like(dsw_ref)
            dbacc[...] = jnp.zeros_like(dbacc)
            vacc[...] = jnp.zeros_like(vacc)
            vacc5[...] = jnp.zeros_like(vacc5)
            qnext[...] = jnp.zeros_like(qnext)

        hb = h_ref[...]
        sc1 = modv_ref[1:2, :]
        fw = _a_forward(None, proj_ref[:, DP:DP + DS].astype(F32), proj_ref[:, DP + DS:].astype(F32), None, pw_ref,
                        ps_ref, sln_ref, sw_ref, sbf_ref, zbuf, j, ts, pooled=pooled_ref[...])
        mixbuf[:, 0:DP] = fw["ya"].astype(BF16)
        mixbuf[:, DP:] = fw["yb"].astype(BF16)

        xhat1, rstd1 = _ln_from_tile(r1_ref[...], st1_ref[...])
        dy = dx1_ref[...]
        vacc[0] += _colsum8(dy * xhat1)
        vacc[1] += _colsum8(dy)
        dr1 = _ln_bwd(dy, ln1_ref[0:1, :], xhat1, rstd1)
        vacc[2] += _colsum8(dr1 * f1_ref[...].astype(F32))
        dfb = (dr1 * modv_ref[2:3, :]).astype(BF16)
        dwout_acc[...] += _mm_tn(mixbuf[...], dfb)
        dmix = _mm_nt(dfb, wout_ref[...])
        dya = dmix[:, 0:DP]
        dyb = dmix[:, DP:]

        vacc5[0] += _colsum8(dya * fw["mixed"])
        dmixed = (dya * ps_ref[...]).astype(BF16)
        dpooled, q = [], []
        for g in range(NG):
            cs = slice(g * GW, (g + 1) * GW)
            dpw_ref[g] += _mm_tn(fw["pooled_b"][g], dmixed[:, cs])
            dpg = _mm_nt(dmixed[:, cs], fw["pwb"][g])
            dpooled.append(dpg)
            q.append(dpg * fw["inv_cnt"][g])
        q = jnp.concatenate(q, axis=1)
        sums = _window_sums(q, qnext[...], False)
        qnext[...] = q[0:AHALO, :]
        for g in range(NG):
            dpbuf[:, g * GW:(g + 1) * GW] = (sums[g] - dpooled[g]).astype(BF16)

        dug = dyb * fw["z"]
        dz = dyb * fw["ug"]
        dzb = dz.astype(BF16)
        for c in range(ts // GW):
            rs = slice(c * GW, (c + 1) * GW)
            for h in range(NG):
                cs = slice(h * GW, (h + 1) * GW)
                dvnbuf[rs, cs] = _mm_tn(fw["wt"][h], dzb[rs, cs])
                dsw_ref[h] += _mm_nt(dzb[rs, cs], fw["vnb"][rs, cs])
            dbacc[...] += dz[rs, :]
        dvn = dvnbuf[...]
        vacc5[1] += _colsum8(dvn * fw["vhat"])
        vacc5[2] += _colsum8(dvn)
        dvg = _ln_bwd(dvn, sln_ref[0:1, :], fw["vhat"], fw["rstdv"])
        dpbuf[:, DP:DP + DS] = (dug * _dgelu(fw["u"], fw["tu"])).astype(BF16)
        dpbuf[:, DP + DS:] = (dvg * _dgelu(fw["v"], fw["tv"])).astype(BF16)

        dpb = dpbuf[...]
        dwin_acc[...] += _mm_tn(dpb, hb)
        dh = _mm(dpb, winT_ref[...])
        dx_ref[...] = dh * (1.0 + sc1) + alpha * dr1
        vacc[3] += _colsum8(dh)
        vacc[4] += _colsum8(dh * xs_ref[...].astype(F32))

        @pl.when(i == nt - 1)
        def _():
            dwin_ref[...] = dwin_acc[...].astype(BF16)
            dwout_ref[...] = dwout_acc[...].astype(BF16)
            tri = _tril_mask()
            for h in range(NG):
                dsw_ref[h] = jnp.where(tri, dsw_ref[h], 0.0)
                sb = jnp.sum(dbacc[:, h * GW:(h + 1) * GW], axis=1, keepdims=True)
                dsb_ref[h] = jnp.broadcast_to(sb, (GW, GW))
            for n in range(5):
                va_ref[n:n + 1, :] = jnp.sum(vacc[n], axis=0, keepdims=True)
            for n in range(3):
                va5_ref[n:n + 1, :] = jnp.sum(vacc5[n], axis=0, keepdims=True)

    rev = lambda i: (nt - 1 - i, 0)
    tile = pl.BlockSpec((ts, D), rev)
    return _pcall(
        body, f"ba{l}", nt,
        in_specs=[tile, pl.BlockSpec((ts, 128), rev), tile, tile, tile, pl.BlockSpec((ts, DIN), rev), tile,
                  pl.BlockSpec((ts, DP), rev),
                  _layer_vec(8, D, l), _layer_vec(2, D, l),
                  _resident((DIN, D)), _resident((D, D))] + _small_specs(l),
        out_specs=[tile] + [_VMEM_WHOLE] * 7,
        out_shape=[jax.ShapeDtypeStruct((s, D), F32), jax.ShapeDtypeStruct((DIN, D), BF16),
                   jax.ShapeDtypeStruct((D, D), BF16), jax.ShapeDtypeStruct((NG, GW, GW), F32),
                   jax.ShapeDtypeStruct((NG, GW, GW), F32), jax.ShapeDtypeStruct((NG, GW, GW), F32),
                   jax.ShapeDtypeStruct((5, D), F32), jax.ShapeDtypeStruct((3, DP), F32)],
        scratch=[pltpu.VMEM((AHALO, DP), F32),
                 pltpu.VMEM((ts, DS), F32), pltpu.VMEM((ts, DS), F32), pltpu.VMEM((ts, D), BF16),
                 pltpu.VMEM((ts, DIN), BF16), pltpu.VMEM((GW, DS), F32), pltpu.VMEM((5, 8, D), F32),
                 pltpu.VMEM((3, 8, DP), F32), pltpu.VMEM((DIN, D), F32), pltpu.VMEM((D, D), F32)],
        args=[r1, st1, dx1, f1, hsave, proj, xsave, pooled, modv, ln1, winT, wout, *small], pay=pay)


def _cast_call(arrs, name):
    r, c = arrs[0].shape
    rb = _row_block(r, 6 * c * len(arrs))

    def body(*refs):
        for src, dst in zip(refs[:len(arrs)], refs[len(arrs):]):
            dst[...] = src[...].astype(BF16)

    blk = pl.BlockSpec((rb, c), lambda i: (i, 0))
    return pl.pallas_call(
        body, name=name, grid=(r // rb,), in_specs=[blk] * len(arrs), out_specs=[blk] * len(arrs),
        out_shape=[jax.ShapeDtypeStruct((r, c), BF16)] * len(arrs), compiler_params=_ARB)(*arrs)


def _rows_before(halo, x):
    ext = jnp.concatenate([halo, x], axis=0)
    return pltpu.roll(ext, 1, 0)[GHALO:, :], pltpu.roll(ext, 2, 0)[GHALO:, :]


def _rows_after(x, halo):
    ts = x.shape[0]
    ext = jnp.concatenate([x, halo], axis=0)
    n = ts + GHALO
    return pltpu.roll(ext, n - 1, 0)[0:ts, :], pltpu.roll(ext, n - 2, 0)[0:ts, :]


def _fb_call(l, alpha, r1, modv, ln1, wup4, wd2, conv_w, conv_b, pay=None):
    s = r1.shape[0]
    nt = s // TSF

    def body(r1_ref, modv_ref, ln1_ref, wup_ref, wd_ref, cw_ref, cb_ref, r2_ref, f2_ref, h2_ref, gs_ref, vs_ref,
             gc_ref, x1_ref, st_ref, gbuf):
        i = pl.program_id(0)

        @pl.when(i == 0)
        def _():
            gbuf[...] = jnp.zeros_like(gbuf)

        xhat1, st_ref[...] = _ln_stats_tile(r1_ref[...])
        x1 = xhat1 * ln1_ref[0:1, :] + ln1_ref[1:2, :]
        h2b = (x1 * (1.0 + modv_ref[4:5, :]) + modv_ref[3:4, :]).astype(BF16)
        f2 = jnp.zeros((TSF, D), F32)
        for c0, c1 in FF_CHUNKS:
            cs = slice(c0, c1)
            g = _mm_nt(h2b, wup_ref[0, cs, :])
            val = _mm_nt(h2b, wup_ref[1, cs, :])
            gm1, gm2 = _rows_before(gbuf[:, cs], g)
            cw = cw_ref[:, cs]
            gc = cb_ref[:, cs] + cw[0:1, :] * gm2 + cw[1:2, :] * gm1 + cw[2:3, :] * g
            ge, _ = _gelu_t(gc)
            f2 = f2 + _mm((ge * val).astype(BF16), wd_ref[cs, :])
            gs_ref[:, cs] = g.astype(BF16)
            vs_ref[:, cs] = val.astype(BF16)
            gc_ref[:, cs] = gc.astype(BF16)
            gbuf[:, cs] = g[TSF - GHALO:, :]
        r2_ref[...] = alpha * x1 + modv_ref[5:6, :] * f2
        f2_ref[...] = f2.astype(BF16)
        h2_ref[...] = h2b
        x1_ref[...] = x1.astype(BF16)

    tile = pl.BlockSpec((TSF, D), lambda i: (i, 0))
    wide = pl.BlockSpec((TSF, FF), lambda i: (i, 0))
    return _pcall(
        body, f"fb{l}", nt,
        in_specs=[tile, _layer_vec(8, D, l), _layer_vec(2, D, l), _resident((2, FF, D)), _resident((FF, D)),
                  _layer_vec(3, FF, l), _layer_vec(1, FF, l)],
        out_specs=[tile, tile, tile, wide, wide, wide, tile, pl.BlockSpec((TSF, 128), lambda i: (i, 0))],
        out_shape=[jax.ShapeDtypeStruct((s, D), F32), jax.ShapeDtypeStruct((s, D), BF16),
                   jax.ShapeDtypeStruct((s, D), BF16), jax.ShapeDtypeStruct((s, FF), BF16),
                   jax.ShapeDtypeStruct((s, FF), BF16), jax.ShapeDtypeStruct((s, FF), BF16),
                   jax.ShapeDtypeStruct((s, D), BF16), jax.ShapeDtypeStruct((s, 128), F32)],
        scratch=[pltpu.VMEM((GHALO, FF), F32)],
        args=[r1, modv, ln1, wup4.reshape(2, FF, D), wd2.reshape(FF, D), conv_w, conv_b], pay=pay)


def _ff_backward(h2b, df2b, gs_ref, vs_ref, gc_ref, dgc_next, cw, wg_ref, wv_ref, wd_ref,
                 dwg_ref, dwv_ref, dwd_ref, cacc):
    g = gs_ref[...].astype(F32)
    val = vs_ref[...].astype(F32)
    gc = gc_ref[...].astype(F32)
    ge, tg = _gelu_t(gc)
    mb = (ge * val).astype(BF16)
    dm = _mm_nt(df2b, wd_ref[...])
    dwd_ref[...] += _mm_tn(mb, df2b)
    dval = dm * ge
    dgc = dm * val * _dgelu(gc, tg)
    dgp1, dgp2 = _rows_after(dgc, dgc_next[...])
    cacc[0] += _colsum8(dgc)
    cacc[1] += _colsum8(dgp2 * g)
    cacc[2] += _colsum8(dgp1 * g)
    cacc[3] += _colsum8(dgc * g)
    dg = cw[2:3, :] * dgc + cw[1:2, :] * dgp1 + cw[0:1, :] * dgp2
    dgc_next[...] = dgc[0:GHALO, :]
    dgb = dg.astype(BF16)
    dvb = dval.astype(BF16)
    dwg_ref[...] += _mm_tn(dgb, h2b)
    dwv_ref[...] += _mm_tn(dvb, h2b)
    return _mm(dgb, wg_ref[...]) + _mm(dvb, wv_ref[...])


def _bb_specs(l, hf, nt, ts):
    wide = pl.BlockSpec((ts, FH), lambda i: (nt - 1 - i, hf))
    ins = [wide, wide, wide, _resident((FH, D), (hf,)), _resident((FH, D), (2 + hf,)), _resident((FH, D), (hf,)),
           pl.BlockSpec((None, 3, FH), lambda i: (l, 0, hf))]
    acc_shapes = [jax.ShapeDtypeStruct((FH, D), BF16)] * 3
    return ins, acc_shapes, [pltpu.VMEM((FH, D), F32)] * 3


def _bb0_call(l, last, h2, r2, st2, dx2, f2, gs, vs, gcs, modv, ln2, wup4, wd2, conv_w, pay=None):
    s = r2.shape[0]
    nt = s // TSB
    ts = TSB

    def body(*refs):
        it = iter(refs)
        h2_ref, r2_ref, dx2_ref, f2_ref, gs_ref, vs_ref, gc_ref = (next(it) for _ in range(7))
        wg_ref, wv_ref, wd_ref, cw_ref, modv_ref, ln2_ref = (next(it) for _ in range(6))
        st2_ref = None if last else next(it)
        dh2_ref, df2_ref, dr2_ref, dwg_out, dwv_out, dwd_out, vb_ref, cacc_ref = (next(it) for _ in range(8))
        loss_ref = next(it) if last else None
        vacc, cacc, dgc_next, dwg_ref, dwv_ref, dwd_ref = (next(it) for _ in range(6))
        lacc = next(it) if last else None
        i = pl.program_id(0)

        @pl.when(i == 0)
        def _():
            dwg_ref[...] = jnp.zeros_like(dwg_ref)
            dwv_ref[...] = jnp.zeros_like(dwv_ref)
            dwd_ref[...] = jnp.zeros_like(dwd_ref)
            vacc[...] = jnp.zeros_like(vacc)
            cacc[...] = jnp.zeros_like(cacc)
            dgc_next[...] = jnp.zeros_like(dgc_next)
            if last:
                lacc[...] = jnp.zeros_like(lacc)

        if last:
            xhat2, rstd2 = _ln_stats(r2_ref[...])
        else:
            xhat2, rstd2 = _ln_from_tile(r2_ref[...], st2_ref[...])
        if last:
            diff = xhat2 * ln2_ref[0:1, :] + ln2_ref[1:2, :] - dx2_ref[...]
            dy = diff * (1.0 / D)
            lacc[...] += _colsum8(diff * diff)
        else:
            dy = dx2_ref[...]
        dr2 = _ln_bwd(dy, ln2_ref[0:1, :], xhat2, rstd2)
        vacc[0] += _colsum8(dy * xhat2)
        vacc[1] += _colsum8(dy)
        vacc[2] += _colsum8(dr2 * f2_ref[...].astype(F32))
        df2b = (dr2 * modv_ref[5:6, :]).astype(BF16)
        dr2_ref[...] = dr2
        df2_ref[...] = df2b
        dh2_ref[...] = _ff_backward(h2_ref[...], df2b, gs_ref, vs_ref, gc_ref, dgc_next, cw_ref[...],
                                    wg_ref, wv_ref, wd_ref, dwg_ref, dwv_ref, dwd_ref, cacc)

        @pl.when(i == nt - 1)
        def _():
            for n in range(3):
                vb_ref[n:n + 1, :] = jnp.sum(vacc[n], axis=0, keepdims=True)
            for n in range(4):
                cacc_ref[n:n + 1, :] = jnp.sum(cacc[n], axis=0, keepdims=True)
            dwg_out[...] = dwg_ref[...].astype(BF16)
            dwv_out[...] = dwv_ref[...].astype(BF16)
            dwd_out[...] = dwd_ref[...].astype(BF16)
            if last:
                loss_ref[...] = lacc[...]

    tile = pl.BlockSpec((ts, D), lambda i: (nt - 1 - i, 0))
    ff_ins, acc_shapes, acc_scratch = _bb_specs(l, 0, nt, ts)
    in_specs = [tile, tile, tile, tile] + ff_ins + [_layer_vec(8, D, l), _layer_vec(2, D, l)]
    args = [h2, r2, dx2, f2, gs, vs, gcs, wup4, wup4, wd2, conv_w, modv, ln2]
    if not last:
        in_specs.append(pl.BlockSpec((ts, 128), lambda i: (nt - 1 - i, 0)))
        args.append(st2)
    out_specs = [tile, tile, tile] + [_VMEM_WHOLE] * 5
    out_shape = [jax.ShapeDtypeStruct((s, D), F32), jax.ShapeDtypeStruct((s, D), BF16),
                 jax.ShapeDtypeStruct((s, D), F32)] + acc_shapes + [jax.ShapeDtypeStruct((3, D), F32),
                                                                    jax.ShapeDtypeStruct((4, FH), F32)]
    scratch = [pltpu.VMEM((3, 8, D), F32), pltpu.VMEM((4, 8, FH), F32), pltpu.VMEM((GHALO, FH), F32)] + acc_scratch
    if last:
        out_specs.append(_VMEM_WHOLE)
        out_shape.append(jax.ShapeDtypeStruct((8, D), F32))
        scratch.append(pltpu.VMEM((8, D), F32))
    return _pcall(body, f"bb{l}_0", nt, in_specs, out_specs, out_shape, scratch, args, pay=pay)


def _bb1_call(l, alpha, h2, x1s, df2, dr2, dh2_in, gs, vs, gcs, modv, wup4, wd2, conv_w, pay=None):
    s = h2.shape[0]
    nt = s // TSB
    ts = TSB

    def body(h2_ref, x1_ref, df2_ref, dr2_ref, dh2_ref, gs_ref, vs_ref, gc_ref, wg_ref, wv_ref, wd_ref, cw_ref,
             modv_ref, dx1_ref, dwg_out, dwv_out, dwd_out, vb_ref, cacc_ref, vacc, cacc, dgc_next,
             dwg_ref, dwv_ref, dwd_ref):
        i = pl.program_id(0)

        @pl.when(i == 0)
        def _():
            dwg_ref[...] = jnp.zeros_like(dwg_ref)
            dwv_ref[...] = jnp.zeros_like(dwv_ref)
            dwd_ref[...] = jnp.zeros_like(dwd_ref)
            vacc[...] = jnp.zeros_like(vacc)
            cacc[...] = jnp.zeros_like(cacc)
            dgc_next[...] = jnp.zeros_like(dgc_next)

        dh2 = dh2_ref[...] + _ff_backward(h2_ref[...], df2_ref[...], gs_ref, vs_ref, gc_ref, dgc_next,
                                          cw_ref[...], wg_ref, wv_ref, wd_ref, dwg_ref, dwv_ref, dwd_ref, cacc)
        dx1_ref[...] = dh2 * (1.0 + modv_ref[4:5, :]) + alpha * dr2_ref[...]
        vacc[0] += _colsum8(dh2)
        vacc[1] += _colsum8(dh2 * x1_ref[...].astype(F32))

        @pl.when(i == nt - 1)
        def _():
            dwg_out[...] = dwg_ref[...].astype(BF16)
            dwv_out[...] = dwv_ref[...].astype(BF16)
            dwd_out[...] = dwd_ref[...].astype(BF16)
            for n in range(2):
                vb_ref[n:n + 1, :] = jnp.sum(vacc[n], axis=0, keepdims=True)
            for n in range(4):
                cacc_ref[n:n + 1, :] = jnp.sum(cacc[n], axis=0, keepdims=True)

    tile = pl.BlockSpec((ts, D), lambda i: (nt - 1 - i, 0))
    ff_ins, acc_shapes, acc_scratch = _bb_specs(l, 1, nt, ts)
    out_shape = [jax.ShapeDtypeStruct((s, D), F32)] + acc_shapes + [jax.ShapeDtypeStruct((2, D), F32),
                                                                   jax.ShapeDtypeStruct((4, FH), F32)]
    scratch = [pltpu.VMEM((2, 8, D), F32), pltpu.VMEM((4, 8, FH), F32), pltpu.VMEM((GHALO, FH), F32)] + acc_scratch
    return _pcall(body, f"bb{l}_1", nt, [tile] * 5 + ff_ins + [_layer_vec(8, D, l)],
                  [tile] + [_VMEM_WHOLE] * 5, out_shape, scratch,
                  [h2, x1s, df2, dr2, dh2_in, gs, vs, gcs, wup4, wup4, wd2, conv_w, modv], pay=pay)


def _silu(c):
    return c * (1.0 / (1.0 + jnp.exp(-c)))


def _ada_fwd_call(c_all, ada_w, ada_b_my):
    nl, _, wcols = ada_w.shape

    def body(c_ref, w_ref, b_ref, o_ref):
        ca = _silu(c_ref[...])
        o_ref[...] = jnp.dot(ca, w_ref[...], preferred_element_type=F32,
                             precision=lax.Precision.HIGHEST) + b_ref[...]

    return pl.pallas_call(
        body, name="ada_fwd", grid=(nl,),
        in_specs=[_full((NDEV, D)), pl.BlockSpec((None, D, wcols), lambda l: (l, 0, 0)),
                  pl.BlockSpec((None, 1, wcols), lambda l: (l, 0, 0))],
        out_specs=pl.BlockSpec((None, NDEV, wcols), lambda l: (l, 0, 0)),
        out_shape=jax.ShapeDtypeStruct((nl, NDEV, wcols), F32),
        compiler_params=_ARB,
    )(c_all, ada_w, ada_b_my)


def _adam_update(w, g, m, v):
    m2 = ADAM_B1 * m + (1.0 - ADAM_B1) * g
    v2 = ADAM_B2 * v + (1.0 - ADAM_B2) * (g * g)
    m_hat = m2 / (1.0 - ADAM_B1 ** ADAM_STEP)
    v_hat = v2 / (1.0 - ADAM_B2 ** ADAM_STEP)
    delta = -ADAM_LR * (m_hat / (jnp.sqrt(v_hat) + ADAM_EPS) + ADAM_WD * w)
    return delta, m2, v2


def _ada_bwd_call(c_t, dmod_my, w, m, v):
    nl, _, wcols = w.shape
    rb = 256

    def body(ct_ref, dm_ref, w_ref, m_ref, v_ref, g_ref, d_ref, m2_ref, v2_ref):
        ca_t = _silu(ct_ref[...])
        dm = dm_ref[...]
        g = ca_t[:, 0:1] * dm[0:1, :]
        for b in range(1, NDEV):
            g = g + ca_t[:, b:b + 1] * dm[b:b + 1, :]
        delta, m2, v2 = _adam_update(w_ref[...], g, m_ref[...], v_ref[...])
        g_ref[...] = g
        d_ref[...] = delta
        m2_ref[...] = m2
        v2_ref[...] = v2

    blk = pl.BlockSpec((None, rb, wcols), lambda l, i: (l, i, 0))
    shp = jax.ShapeDtypeStruct(w.shape, F32)
    return pl.pallas_call(
        body, name="ada_bwd", grid=(nl, D // rb),
        in_specs=[pl.BlockSpec((rb, NDEV), lambda l, i: (i, 0)),
                  pl.BlockSpec((None, NDEV, wcols), lambda l, i: (l, 0, 0)), blk, blk, blk],
        out_specs=[blk, blk, blk, blk], out_shape=[shp, shp, shp, shp],
        compiler_params=pltpu.CompilerParams(dimension_semantics=("arbitrary", "arbitrary")),
    )(c_t, dmod_my, w, m, v)


def _row_block(r, row_bytes):
    budget = 6 * 1024 * 1024
    best = None
    for rb in range(16, min(r, 512) + 1, 16):
        if r % rb == 0 and rb * row_bytes <= budget:
            best = rb
    return best if best is not None else r


def _adamw_rows_call(buf, items):
    nl = buf.shape[0]
    n = len(items)

    def body(*refs):
        buf_ref, ins, outs = refs[0], refs[1:1 + 3 * n], refs[1 + 3 * n:]
        for k, (w, _, _, row, lane0) in enumerate(items):
            w_ref, m_ref, v_ref = ins[3 * k:3 * k + 3]
            g_ref, d_ref, m2_ref, v2_ref = outs[4 * k:4 * k + 4]
            width = w.shape[1]
            for l in range(nl):
                for c0 in range(0, width, 1024):
                    n_ = min(1024, width - c0)
                    r = row + c0 // 1024
                    g = buf_ref[l, 0, r:r + 1, lane0:lane0 + n_]
                    for dev in range(1, NDEV):
                        g = g + buf_ref[l, dev, r:r + 1, lane0:lane0 + n_]
                    at = (slice(l, l + 1), slice(c0, c0 + n_))
                    delta, m2, v2 = _adam_update(w_ref[at], g, m_ref[at], v_ref[at])
                    g_ref[at] = g
                    d_ref[at] = delta
                    m2_ref[at] = m2
                    v2_ref[at] = v2

    args = [buf]
    out_shape = []
    for w, m, v, _, _ in items:
        args += [w, m, v]
        out_shape += [jax.ShapeDtypeStruct(w.shape, F32)] * 4
    res = pl.pallas_call(body, name="adamw_vectors", in_specs=[_VMEM_WHOLE] * len(args),
                         out_specs=[_VMEM_WHOLE] * len(out_shape), out_shape=out_shape)(*args)
    return [tuple(res[4 * k:4 * k + 4]) for k in range(n)]


def _adamw_call(parts, w, m, v, name, pay=None, row0=0):
    nl, npart, _, c = parts.shape
    r = w.shape[1]
    rb = _row_block(r, (npart + 7) * c * 4)
    nb = r // rb
    assert row0 % rb == 0
    b0 = row0 // rb

    def body(p_ref, w_ref, m_ref, v_ref, g_ref, d_ref, m2_ref, v2_ref):
        g = p_ref[0].astype(F32)
        for k in range(1, npart):
            g = g + p_ref[k].astype(F32)
        delta, m2, v2 = _adam_update(w_ref[...], g, m_ref[...], v_ref[...])
        g_ref[...] = g
        d_ref[...] = delta
        m2_ref[...] = m2
        v2_ref[...] = v2

    blk = pl.BlockSpec((None, rb, c), lambda i: (i // nb, i % nb, 0))
    shp = jax.ShapeDtypeStruct((nl, r, c), F32)
    outs, _, alias = _pcall(body, name, nl * nb,
                            [pl.BlockSpec((None, npart, rb, c), lambda i: (i // nb, 0, b0 + i % nb, 0)), blk, blk, blk],
                            [blk] * 4, [shp] * 4, [], [parts, w, m, v], pay)
    return tuple(outs), alias


_ROW_OF = dict(ada_b=(0, 0), ln1_g=(6, 0), ln1_b=(7, 0), ln2_g=(8, 0), ln2_b=(9, 0), conv_b=(10, 0),
               pool_scale=(13, 0), sgu_ln_g=(13, DP), sgu_ln_b=(14, 0), sgu_b=(14, DP), conv_w=(15, 0))
_SMALL_ROWS = 24
_FF_PAD = 3 * 1024 - FF


def kernel(x, c, ada_w, ada_b, w_in, pool_w, pool_scale, sgu_ln_g, sgu_ln_b, sgu_w, sgu_b, w_out, ln1_g, ln1_b, w_up, conv_w, conv_b, w_down, ln2_g, ln2_b, loss_target, m_ada_w, m_ada_b, m_w_in, m_pool_w, m_pool_scale, m_sgu_ln_g, m_sgu_ln_b, m_sgu_w, m_sgu_b, m_w_out, m_ln1_g, m_ln1_b, m_w_up, m_conv_w, m_conv_b, m_w_down, m_ln2_g, m_ln2_b, v_ada_w, v_ada_b, v_w_in, v_pool_w, v_pool_scale, v_sgu_ln_g, v_sgu_ln_b, v_sgu_w, v_sgu_b, v_w_out, v_ln1_g, v_ln1_b, v_w_up, v_conv_w, v_conv_b, v_w_down, v_ln2_g, v_ln2_b):
    nl = ada_w.shape[0]
    alpha = (2.0 * nl) ** 0.25
    me = _me()
    x2 = x[0]
    tgt = loss_target[0]
    acols = ada_w.shape[2]
    icols = w_in.shape[2]
    ucols = w_up.shape[2]
    orows = w_out.shape[1]
    drows = w_down.shape[1]
    ccols = conv_w.shape[2]

    winT_sh = jnp.swapaxes(w_in, 1, 2).astype(BF16)
    wupT_sh = jnp.swapaxes(w_up, 1, 2).astype(BF16)
    wout_sh = w_out.astype(BF16)
    wd_sh = w_down.astype(BF16)

    pay = _Payload()
    pay.gather(jnp.broadcast_to(c, (8, D)))
    pay.gather(winT_sh, 0)
    pay.gather(wout_sh, 0)
    pay.gather(conv_w)
    (c_g, winT_g, wout_g, cw_g), _ = _comm_call(pay, "gather_first")
    c_all = c_g[:, 0, :]
    winT = winT_g.reshape(DIN, D)
    wout = wout_g.reshape(D, D)
    cw_full = jnp.transpose(cw_g, (1, 2, 0, 3)).reshape(nl, 3, FF)
    cb_full = conv_b[:, None, :]

    ada_b_my = lax.dynamic_slice(ada_b, (0, me * acols), (nl, acols))[:, None, :]
    mod_blk = _ada_fwd_call(c_all, ada_w, ada_b_my)
    pay = _Payload()
    pay.gather(mod_blk)
    (mod_g,), _ = _comm_call(pay, "gather_mod")
    mod_me = lax.dynamic_index_in_dim(mod_g, me, axis=2, keepdims=False)
    modv = jnp.swapaxes(mod_me, 0, 1).reshape(nl, 6, D)
    modv = jnp.concatenate([modv, jnp.zeros((nl, 2, D), F32)], axis=1)

    ln1 = jnp.stack([ln1_g, ln1_b], axis=1)
    ln2 = jnp.stack([ln2_g, ln2_b], axis=1)
    sln = jnp.stack([sgu_ln_g, sgu_ln_b], axis=1)
    sbf = jnp.broadcast_to(sgu_b[..., None], sgu_b.shape + (GW,))
    small = (pool_w, pool_scale[:, None, :], sln, sgu_w, sbf)

    r1s, r2s, f1s, f2s, ahs, ghs, wins, wouts, wups, wds, st1s, st2s = ([None] * nl for _ in range(12))
    wins[0], wouts[0] = winT, wout
    xin = x2
    for l in range(nl):
        pay = None
        if l == 0:
            pay = _Payload()
            pay.gather(wupT_sh, 0)
            pay.gather(wd_sh, 0)
        (r1, f1, *ah, st_prev), new, _ = _fa_call(l, l == 0, alpha, xin, modv, ln2, wins[l], wouts[l], small, pay)
        if l > 0:
            st2s[l - 1] = st_prev
        if l == 0:
            wups[0], wds[0] = new[0].reshape(4, FH, D), new[1].reshape(2, FH, D)
        pay = None
        if l + 1 < nl:
            pay = _Payload()
            pay.gather(winT_sh, l + 1)
            pay.gather(wout_sh, l + 1)
            pay.gather(wupT_sh, l + 1)
            pay.gather(wd_sh, l + 1)
        (r2, f2, *saved_b, st1s[l]), new, _ = _fb_call(l, alpha, r1, modv, ln1, wups[l], wds[l], cw_full, cb_full,
                                                       pay)
        if l + 1 < nl:
            wins[l + 1], wouts[l + 1] = new[0].reshape(DIN, D), new[1].reshape(D, D)
            wups[l + 1], wds[l + 1] = new[2].reshape(4, FH, D), new[3].reshape(2, FH, D)
        r1s[l], r2s[l], f1s[l], f2s[l], ahs[l], ghs[l] = r1, r2, f1, f2, ah, saved_b
        xin = r2

    buf_in = lax.empty((nl, NDEV, icols, D), BF16)
    buf_out = lax.empty((nl, NDEV, orows, D), BF16)
    buf_up = lax.empty((nl, NDEV, ucols, D), BF16)
    buf_down = lax.empty((nl, NDEV, drows, D), BF16)
    buf_small = lax.empty((nl, NDEV, _SMALL_ROWS, 1024), F32)
    buf_mat = lax.empty((nl, NDEV, 2 * NG * GW, GW), F32)

    def a_side_payload(l, dwin, dwout, flat, mats):
        p = _Payload()
        p.exchange_into([(dwin.reshape(NDEV, icols, D), 0)], buf_in, (l,))
        p.exchange_into([(dwout.reshape(NDEV, orows, D), 0)], buf_out, (l,))
        p.gather_into(flat, buf_small, (l,))
        p.gather_into(mats, buf_mat, (l,))
        return p

    dx = tgt
    loss_acc = None
    pending = None
    for l in reversed(range(nl)):
        last = l == nl - 1
        pay = None if pending is None else a_side_payload(*pending)
        h2, gs, vs, gcs, x1s = ghs[l]
        res0, _, al = _bb0_call(l, last, h2, r2s[l], st2s[l], dx, f2s[l], gs, vs, gcs, modv, ln2, wups[l], wds[l],
                                cw_full, pay)
        if pending is not None:
            buf_in, buf_out, buf_small, buf_mat = al
        dh2p, df2, dr2, dwg0, dwv0, dwd0, vb0, cacc0 = res0[:8]
        if last:
            loss_acc = res0[8]
        (dx1, dwg1, dwv1, dwd1, vb1, cacc1), _, _ = _bb1_call(
            l, alpha, h2, x1s, df2, dr2, dh2p, gs, vs, gcs, modv, wups[l], wds[l], cw_full)
        pay = _Payload()
        pay.exchange_into([(dwg0.reshape(2, ucols, D), 0), (dwg1.reshape(2, ucols, D), 2),
                           (dwv0.reshape(2, ucols, D), 4), (dwv1.reshape(2, ucols, D), 6)], buf_up, (l,))
        pay.exchange_into([(dwd0.reshape(4, drows, D), 0), (dwd1.reshape(4, drows, D), 4)], buf_down, (l,))
        (dx, dwin, dwout, dpw, dsw, dsb, va, va5), _, (buf_up, buf_down) = _ba_call(
            l, alpha, r1s[l], st1s[l], dx1, f1s[l], *ahs[l], modv, ln1, wins[l], wouts[l], small, pay)
        cacc = jnp.pad(jnp.concatenate([cacc0, cacc1], axis=1), ((0, 0), (0, _FF_PAD)))
        flat = jnp.concatenate([
            va[3], va[4], va[2], vb1[0], vb1[1], vb0[2],
            va[0], va[1], vb0[0], vb0[1],
            cacc[0],
            va5[0], va5[1], va5[2], dsb[:, :, 0].reshape(-1),
            cacc[1:4].reshape(-1),
        ]).reshape(_SMALL_ROWS, 1024)
        mats = jnp.concatenate([dpw.reshape(NG * GW, GW), dsw.reshape(NG * GW, GW)])
        pending = (l, dwin, dwout, flat, mats)
    grad_x = dx[None]
    loss = lax.psum((0.5 / D) * jnp.sum(loss_acc), ("x", "y", "c"))

    res = {}
    _, dwin0, dwout0, flat0, mats0 = pending
    t_ = lambda a: jnp.swapaxes(a, 1, 2)
    pay = _Payload()
    pay.exchange_into([(dwout0.reshape(NDEV, orows, D), 0)], buf_out, (0,))
    pay.gather_into(flat0, buf_small, (0,))
    pay.gather_into(mats0, buf_mat, (0,))
    outs, (buf_out, buf_small, buf_mat) = _adamw_call(buf_up, t_(w_up), t_(m_w_up), t_(v_w_up), "adamw_w_up", pay)
    res["w_up"] = tuple(t_(o) for o in outs)
    pay = _Payload()
    pay.exchange_into([(dwin0.reshape(NDEV, icols, D), 0)], buf_in, (0,))
    res["w_down"], (buf_in,) = _adamw_call(buf_down, w_down, m_w_down, v_w_down, "adamw_w_down", pay)
    outs, _ = _adamw_call(buf_in, t_(w_in), t_(m_w_in), t_(v_w_in), "adamw_w_in")
    res["w_in"] = tuple(t_(o) for o in outs)
    res["w_out"], _ = _adamw_call(buf_out, w_out, m_w_out, v_w_out, "adamw_w_out")

    weights = dict(ada_b=ada_b, pool_w=pool_w, pool_scale=pool_scale, sgu_ln_g=sgu_ln_g, sgu_ln_b=sgu_ln_b,
                   sgu_w=sgu_w, sgu_b=sgu_b, ln1_g=ln1_g, ln1_b=ln1_b, conv_w=conv_w, conv_b=conv_b, ln2_g=ln2_g,
                   ln2_b=ln2_b)
    m_of = dict(ada_b=m_ada_b, pool_w=m_pool_w, pool_scale=m_pool_scale, sgu_ln_g=m_sgu_ln_g, sgu_ln_b=m_sgu_ln_b,
                sgu_w=m_sgu_w, sgu_b=m_sgu_b, ln1_g=m_ln1_g, ln1_b=m_ln1_b, conv_w=m_conv_w, conv_b=m_conv_b,
                ln2_g=m_ln2_g, ln2_b=m_ln2_b)
    v_of = dict(ada_b=v_ada_b, pool_w=v_pool_w, pool_scale=v_pool_scale, sgu_ln_g=v_sgu_ln_g, sgu_ln_b=v_sgu_ln_b,
                sgu_w=v_sgu_w, sgu_b=v_sgu_b, ln1_g=v_ln1_g, ln1_b=v_ln1_b, conv_w=v_conv_w, conv_b=v_conv_b,
                ln2_g=v_ln2_g, ln2_b=v_ln2_b)
    vec_names = ("ada_b", "ln1_g", "ln1_b", "ln2_g", "ln2_b", "conv_b", "pool_scale", "sgu_ln_g", "sgu_ln_b")
    vec_res = _adamw_rows_call(buf_small, [(weights[n], m_of[n], v_of[n], *_ROW_OF[n]) for n in vec_names])
    res.update(zip(vec_names, vec_res))
    for k, name in enumerate(("pool_w", "sgu_w")):
        w3 = weights[name].reshape(nl, NG * GW, GW)
        outs, _ = _adamw_call(buf_mat, w3, m_of[name].reshape(w3.shape), v_of[name].reshape(w3.shape),
                              f"adamw_{name}", row0=k * NG * GW)
        res[name] = tuple(o.reshape(weights[name].shape) for o in outs)
    row, lane = _ROW_OF["sgu_b"]
    parts = buf_small[:, :, row, lane:lane + NG * GW].reshape(nl, NDEV, NG, GW)
    res["sgu_b"], _ = _adamw_call(parts, sgu_b, m_sgu_b, v_sgu_b, "adamw_sgu_b")
    row, _ = _ROW_OF["conv_w"]
    parts = buf_small[:, :, row:row + 9, :].reshape(nl, NDEV, 3, 3 * 1024)
    parts = lax.dynamic_slice_in_dim(parts, me * ccols, ccols, axis=3)
    res["conv_w"], _ = _adamw_call(parts, conv_w, m_conv_w, v_conv_w, "adamw_conv_w")

    dmod_my = lax.dynamic_slice_in_dim(buf_small[:, :, 0:6, :].reshape(nl, NDEV, 6 * D), me * acols, acols, axis=2)
    res["ada_w"] = _ada_bwd_call(jnp.swapaxes(c_all, 0, 1), dmod_my, ada_w, m_ada_w, v_ada_w)

    order = ["ada_w", "ada_b", "w_in", "pool_w", "pool_scale", "sgu_ln_g", "sgu_ln_b", "sgu_w", "sgu_b", "w_out",
             "ln1_g", "ln1_b", "w_up", "conv_w", "conv_b", "w_down", "ln2_g", "ln2_b"]
    out = [loss, grad_x]
    for k in range(4):
        out += [res[n][k] for n in order]
    return tuple(out)
```

```python
import jax
import jax.numpy as jnp
from jax import lax
from jax.experimental import pallas as pl
from jax.experimental.pallas import tpu as pltpu

F32 = jnp.float32
BF16 = jnp.bfloat16

NDEV = 8
D = 1024
DP = 512
DS = 512
DIN = DP + 2 * DS
FF = 2816
FH = FF // 2
FF_CHUNKS = ((0, 1536), (1536, FF))
NG = 4
GW = 128
WINDOWS = (2, 4, 8, 16)
AHALO = 16
GHALO = 8
LN_EPS = 1e-5
ADAM_LR, ADAM_B1, ADAM_B2, ADAM_EPS, ADAM_WD, ADAM_STEP = 0.001, 0.9, 0.999, 1e-08, 0.01, 10
TSA = 1024
TSF = 512
TSB = 256
_K0 = 0.7978845608028654
_K1 = 0.044715
MESH_ID = pl.DeviceIdType.MESH


def _mm(a, b):
    return jnp.dot(a, b, preferred_element_type=F32)


def _mm_nt(a, b):
    return lax.dot_general(a, b, (((1,), (1,)), ((), ())), preferred_element_type=F32)


def _mm_tn(a, b):
    return lax.dot_general(a, b, (((0,), (0,)), ((), ())), preferred_element_type=F32)


def _rowmean(x):
    return jnp.mean(x, axis=-1, keepdims=True)


def _ln_stats(x):
    mu = _rowmean(x)
    xc = x - mu
    rstd = lax.rsqrt(_rowmean(xc * xc) + LN_EPS)
    return xc * rstd, rstd


def _ln_stats_tile(x):
    mu = _rowmean(x)
    xc = x - mu
    rstd = lax.rsqrt(_rowmean(xc * xc) + LN_EPS)
    lane = lax.broadcasted_iota(jnp.int32, (x.shape[0], 128), 1)
    return xc * rstd, jnp.where(lane == 0, mu, jnp.where(lane == 1, rstd, 0.0))


def _ln_from_tile(x, st):
    rstd = st[:, 1:2]
    return (x - st[:, 0:1]) * rstd, rstd


def _ln_bwd(dy, gamma, xhat, rstd):
    dxh = dy * gamma
    return rstd * (dxh - _rowmean(dxh) - xhat * _rowmean(dxh * xhat))


def _gelu_t(x):
    t = jnp.tanh(x * (_K0 + (_K0 * _K1) * (x * x)))
    hx = 0.5 * x
    return hx + hx * t, t


def _dgelu(x, t):
    return (0.5 + 0.5 * t) + (0.5 * x) * (1.0 - t * t) * (_K0 + (3.0 * _K0 * _K1) * (x * x))


def _colsum8(x):
    t, n = x.shape
    return jnp.sum(x.reshape(t // 8, 8, n), axis=0)


def _tril_mask():
    r = lax.broadcasted_iota(jnp.int32, (GW, GW), 0)
    c = lax.broadcasted_iota(jnp.int32, (GW, GW), 1)
    return c <= r


def _full(shape):
    n = len(shape)
    return pl.BlockSpec(shape, lambda *_: (0,) * n)


def _resident(tail, lead=()):
    n = len(tail)
    return pl.BlockSpec((None,) * len(lead) + tuple(tail), lambda *_: tuple(lead) + (0,) * n,
                        pipeline_mode=pl.Buffered(1))


def _layer_vec(rows, width, l):
    return pl.BlockSpec((None, rows, width), lambda *_: (l, 0, 0))


_VMEM_WHOLE = pl.BlockSpec(memory_space=pltpu.VMEM)
_HBM = pl.BlockSpec(memory_space=pl.ANY)
_ARB = pltpu.CompilerParams(dimension_semantics=("arbitrary",))


def _me():
    return 4 * lax.axis_index("x") + 2 * lax.axis_index("y") + lax.axis_index("c")


def _coords(p):
    return (p >> 2, (p >> 1) & 1, p & 1)


class _Payload:
    def __init__(self):
        self.srcs, self.new, self.alias, self.transfers = [], [], [], []

    def _src(self, arr):
        self.srcs.append(arr)
        return len(self.srcs) - 1

    def _alias(self, buf):
        self.alias.append(buf)
        return len(self.alias) - 1

    def gather(self, arr, chunk=None):
        pos = self._src(arr)
        blk = arr.shape if chunk is None else arr.shape[1:]
        self.new.append(jax.ShapeDtypeStruct((NDEV,) + tuple(blk), arr.dtype))
        self.transfers.append(([(pos, chunk)] * NDEV, ("new", len(self.new) - 1), ()))
        return len(self.new) - 1

    def gather_into(self, arr, buf, lead):
        pos = self._src(arr)
        self.transfers.append(([(pos, None)] * NDEV, ("alias", self._alias(buf)), tuple(lead)))

    def exchange_into(self, parts, buf, lead):
        route = {}
        for arr, first in parts:
            pos = self._src(arr)
            for q in range(arr.shape[0]):
                route[first + q] = (pos, q)
        self.transfers.append(([route[p] for p in range(NDEV)], ("alias", self._alias(buf)), tuple(lead)))

    def _ends(self, t, io, dst_dev, src_dev):
        srcs, new_out, alias_out = io
        route, (kind, k), lead = self.transfers[t]
        pos, q = route[dst_dev]
        src = srcs[pos] if q is None else srcs[pos].at[q]
        buf = new_out[k] if kind == "new" else alias_out[k]
        return src, buf.at[lead + (src_dev,)]

    def _remote(self, t, io, sems, src_dev, dst_dev):
        src, dst = self._ends(t, io, dst_dev, src_dev)
        return pltpu.make_async_remote_copy(
            src_ref=src, dst_ref=dst, send_sem=sems[0].at[t, dst_dev], recv_sem=sems[1].at[t, src_dev],
            device_id=_coords(dst_dev), device_id_type=MESH_ID)

    def _local(self, t, io, sems, p):
        src, dst = self._ends(t, io, p, p)
        return pltpu.make_async_copy(src, dst, sems[2].at[t])

    def start(self, io, sems):
        me = _me()
        for p in range(NDEV):
            @pl.when(me == p)
            def _():
                for t in range(len(self.transfers)):
                    self._local(t, io, sems, p).start()

            @pl.when(me != p)
            def _():
                for t in range(len(self.transfers)):
                    self._remote(t, io, sems, me, p).start()

    def wait(self, io, sems):
        me = _me()
        for p in range(NDEV):
            @pl.when(me == p)
            def _():
                for t in range(len(self.transfers)):
                    self._local(t, io, sems, p).wait()

            @pl.when(me != p)
            def _():
                for t in range(len(self.transfers)):
                    self._remote(t, io, sems, p, p).wait_recv()
                    self._remote(t, io, sems, me, p).wait_send()


def _pcall(body, name, nsteps, in_specs, out_specs, out_shape, scratch, args, pay=None):
    n_in, n_out, n_scr = len(args), len(out_shape), len(scratch)
    if pay is None:
        res = pl.pallas_call(body, name=name, grid=(nsteps,), in_specs=list(in_specs), out_specs=list(out_specs),
                             out_shape=list(out_shape), scratch_shapes=list(scratch), compiler_params=_ARB)(*args)
        return list(res), [], []
    ns, nn, na, nt = len(pay.srcs), len(pay.new), len(pay.alias), len(pay.transfers)

    def full(*refs):
        cin = refs[:n_in]
        srcs = refs[n_in:n_in + ns]
        o0 = n_in + ns + na
        cout = refs[o0:o0 + n_out]
        new_out = refs[o0 + n_out:o0 + n_out + nn]
        alias_out = refs[o0 + n_out + nn:o0 + n_out + nn + na]
        s0 = o0 + n_out + nn + na
        cscr = refs[s0:s0 + n_scr]
        sems = refs[s0 + n_scr:]
        io = (srcs, new_out, alias_out)
        i = pl.program_id(0)

        @pl.when(i == 0)
        def _():
            pay.start(io, sems)

        body(*cin, *cout, *cscr)

        @pl.when(i == nsteps - 1)
        def _():
            pay.wait(io, sems)

    res = pl.pallas_call(
        full, name=name, grid=(nsteps,),
        in_specs=list(in_specs) + [_HBM] * (ns + na), out_specs=list(out_specs) + [_HBM] * (nn + na),
        out_shape=list(out_shape) + pay.new + [jax.ShapeDtypeStruct(b.shape, b.dtype) for b in pay.alias],
        input_output_aliases={n_in + ns + k: n_out + nn + k for k in range(na)},
        scratch_shapes=list(scratch) + [pltpu.SemaphoreType.DMA((nt, NDEV)), pltpu.SemaphoreType.DMA((nt, NDEV)),
                                        pltpu.SemaphoreType.DMA((nt,))],
        compiler_params=_ARB,
    )(*args, *pay.srcs, *pay.alias)
    return list(res[:n_out]), list(res[n_out:n_out + nn]), list(res[n_out + nn:])


def _comm_call(pay, name):
    def body():
        pass

    _, new, alias = _pcall(body, name, 1, [], [], [], [], [], pay)
    return new, alias


def _window_sums(x, halo, before):
    ts = x.shape[0]
    ext = jnp.concatenate([halo, x] if before else [x, halo], axis=0)
    n = ts + AHALO
    shift = (lambda k: k) if before else (lambda k: n - k)
    keep = slice(AHALO, n) if before else slice(0, ts)
    out = []
    s = ext
    for level in range(NG):
        s = s + pltpu.roll(s, shift(1 << level), 0)
        out.append(s[keep, 0:GW])
        if level + 1 < NG:
            s = s[:, GW:]
    return out


def _a_forward(a, u, v, halo, pw_ref, ps_ref, sln_ref, sw_ref, sbf_ref, zbuf, tile, ts, pooled=None):
    tglob = tile * ts + lax.broadcasted_iota(jnp.int32, (ts, 1), 0)
    sums = _window_sums(a, halo, True) if pooled is None else None
    pooled_b, mixed, inv_cnt, pwb = [], [], [], []
    for g, w in enumerate(WINDOWS):
        inv = 1.0 / jnp.minimum(tglob + 1, w).astype(F32)
        if pooled is None:
            pg = (sums[g] * inv - a[:, g * GW:(g + 1) * GW]).astype(BF16)
        else:
            pg = pooled[:, g * GW:(g + 1) * GW]
        wg = pw_ref[g].astype(BF16)
        pooled_b.append(pg)
        inv_cnt.append(inv)
        pwb.append(wg)
        mixed.append(_mm(pg, wg))
    mixed = jnp.concatenate(mixed, axis=1)
    ya = mixed * ps_ref[...]
    ug, tu = _gelu_t(u)
    vg, tv = _gelu_t(v)
    vhat, rstdv = _ln_stats(vg)
    vnb = (vhat * sln_ref[0:1, :] + sln_ref[1:2, :]).astype(BF16)
    tri = _tril_mask()
    wt = [jnp.where(tri, sw_ref[h], 0.0).astype(BF16) for h in range(NG)]
    for c in range(ts // GW):
        rs = slice(c * GW, (c + 1) * GW)
        for h in range(NG):
            cs = slice(h * GW, (h + 1) * GW)
            zbuf[rs, cs] = _mm(wt[h], vnb[rs, cs]) + sbf_ref[h]
    z = zbuf[...]
    yb = ug * z
    return dict(u=u, v=v, tu=tu, tv=tv, ug=ug, z=z, vhat=vhat, rstdv=rstdv, vnb=vnb,
                wt=wt, pooled_b=pooled_b, pwb=pwb, inv_cnt=inv_cnt, mixed=mixed, ya=ya, yb=yb)


def _small_specs(l):
    grp = pl.BlockSpec((None, NG, GW, GW), lambda i: (l, 0, 0, 0))
    return [grp, _layer_vec(1, DP, l), _layer_vec(2, DS, l), grp, grp]


def _fa_call(l, first, alpha, xin, modv, lnp, winT, wout, small, pay=None):
    s = xin.shape[0]
    ts = min(TSA, s)
    nt = s // ts

    def body(xin_ref, modv_ref, lnp_ref, winT_ref, wout_ref, pw_ref, ps_ref, sln_ref, sw_ref, sbf_ref,
             r1_ref, f1_ref, h_ref, proj_ref, x_ref, pooled_ref, st_ref, acarry, zbuf, mixbuf):
        i = pl.program_id(0)

        @pl.when(i == 0)
        def _():
            acarry[...] = jnp.zeros_like(acarry)

        x = xin_ref[...]
        if first:
            st_ref[...] = jnp.zeros_like(st_ref)
        else:
            xhat, st_ref[...] = _ln_stats_tile(x)
            x = xhat * lnp_ref[0:1, :] + lnp_ref[1:2, :]
        hb = (x * (1.0 + modv_ref[1:2, :]) + modv_ref[0:1, :]).astype(BF16)
        proj = _mm_nt(hb, winT_ref[...])
        a = proj[:, 0:DP]
        fw = _a_forward(a, proj[:, DP:DP + DS], proj[:, DP + DS:], acarry[...], pw_ref, ps_ref, sln_ref, sw_ref,
                        sbf_ref, zbuf, i, ts)
        acarry[...] = a[ts - AHALO:, :]
        mixbuf[:, 0:DP] = fw["ya"].astype(BF16)
        mixbuf[:, DP:] = fw["yb"].astype(BF16)
        f = _mm(mixbuf[...], wout_ref[...])
        r1_ref[...] = alpha * x + modv_ref[2:3, :] * f
        f1_ref[...] = f.astype(BF16)
        h_ref[...] = hb
        proj_ref[...] = proj.astype(BF16)
        x_ref[...] = x.astype(BF16)
        pooled_ref[...] = jnp.concatenate(fw["pooled_b"], axis=1)

    tile = pl.BlockSpec((ts, D), lambda i: (i, 0))
    return _pcall(
        body, f"fa{l}", nt,
        in_specs=[tile, _layer_vec(8, D, l), _layer_vec(2, D, max(l - 1, 0)), _resident((DIN, D)),
                  _resident((D, D))] + _small_specs(l),
        out_specs=[tile, tile, tile, pl.BlockSpec((ts, DIN), lambda i: (i, 0)), tile,
                   pl.BlockSpec((ts, DP), lambda i: (i, 0)), pl.BlockSpec((ts, 128), lambda i: (i, 0))],
        out_shape=[jax.ShapeDtypeStruct((s, D), F32), jax.ShapeDtypeStruct((s, D), BF16),
                   jax.ShapeDtypeStruct((s, D), BF16), jax.ShapeDtypeStruct((s, DIN), BF16),
                   jax.ShapeDtypeStruct((s, D), BF16), jax.ShapeDtypeStruct((s, DP), BF16),
                   jax.ShapeDtypeStruct((s, 128), F32)],
        scratch=[pltpu.VMEM((AHALO, DP), F32), pltpu.VMEM((ts, DS), F32), pltpu.VMEM((ts, D), BF16)],
        args=[xin, modv, lnp, winT, wout, *small], pay=pay)


def _ba_call(l, alpha, r1, st1, dx1, f1, hsave, proj, xsave, pooled, modv, ln1, winT, wout, small, pay=None):
    s = r1.shape[0]
    nt = s // TSB
    ts = TSB

    def body(r1_ref, st1_ref, dx1_ref, f1_ref, h_ref, proj_ref, xs_ref, pooled_ref, modv_ref, ln1_ref, winT_ref,
             wout_ref, pw_ref, ps_ref, sln_ref, sw_ref, sbf_ref,
             dx_ref, dwin_ref, dwout_ref, dpw_ref, dsw_ref, dsb_ref, va_ref, va5_ref,
             qnext, zbuf, dvnbuf, mixbuf, dpbuf, dbacc, vacc, vacc5, dwin_acc, dwout_acc):
        i = pl.program_id(0)
        j = nt - 1 - i

        @pl.when(i == 0)
        def _():
            dwin_acc[...] = jnp.zeros_like(dwin_acc)
            dwout_acc[...] = jnp.zeros_like(dwout_acc)
            dpw_ref[...] = jnp.zeros_like(dpw_ref)
            dsw_ref[...] = jnp.zeros_like(dsw_ref)
            dbacc[...] = jnp.zeros_like(dbacc)
            vacc[...] = jnp.zeros_like(vacc)
            vacc5[...] = jnp.zeros_like(vacc5)
            qnext[...] = jnp.zeros_like(qnext)

        hb = h_ref[...]
        sc1 = modv_ref[1:2, :]
        fw = _a_forward(None, proj_ref[:, DP:DP + DS].astype(F32), proj_ref[:, DP + DS:].astype(F32), None, pw_ref,
                        ps_ref, sln_ref, sw_ref, sbf_ref, zbuf, j, ts, pooled=pooled_ref[...])
        mixbuf[:, 0:DP] = fw["ya"].astype(BF16)
        mixbuf[:, DP:] = fw["yb"].astype(BF16)

        xhat1, rstd1 = _ln_from_tile(r1_ref[...], st1_ref[...])
        dy = dx1_ref[...]
        vacc[0] += _colsum8(dy * xhat1)
        vacc[1] += _colsum8(dy)
        dr1 = _ln_bwd(dy, ln1_ref[0:1, :], xhat1, rstd1)
        vacc[2] += _colsum8(dr1 * f1_ref[...].astype(F32))
        dfb = (dr1 * modv_ref[2:3, :]).astype(BF16)
        dwout_acc[...] += _mm_tn(mixbuf[...], dfb)
        dmix = _mm_nt(dfb, wout_ref[...])
        dya = dmix[:, 0:DP]
        dyb = dmix[:, DP:]

        vacc5[0] += _colsum8(dya * fw["mixed"])
        dmixed = (dya * ps_ref[...]).astype(BF16)
        dpooled, q = [], []
        for g in range(NG):
            cs = slice(g * GW, (g + 1) * GW)
            dpw_ref[g] += _mm_tn(fw["pooled_b"][g], dmixed[:, cs])
            dpg = _mm_nt(dmixed[:, cs], fw["pwb"][g])
            dpooled.append(dpg)
            q.append(dpg * fw["inv_cnt"][g])
        q = jnp.concatenate(q, axis=1)
        sums = _window_sums(q, qnext[...], False)
        qnext[...] = q[0:AHALO, :]
        for g in range(NG):
            dpbuf[:, g * GW:(g + 1) * GW] = (sums[g] - dpooled[g]).astype(BF16)

        dug = dyb * fw["z"]
        dz = dyb * fw["ug"]
        dzb = dz.astype(BF16)
        for c in range(ts // GW):
            rs = slice(c * GW, (c + 1) * GW)
            for h in range(NG):
                cs = slice(h * GW, (h + 1) * GW)
                dvnbuf[rs, cs] = _mm_tn(fw["wt"][h], dzb[rs, cs])
                dsw_ref[h] += _mm_nt(dzb[rs, cs], fw["vnb"][rs, cs])
            dbacc[...] += dz[rs, :]
        dvn = dvnbuf[...]
        vacc5[1] += _colsum8(dvn * fw["vhat"])
        vacc5[2] += _colsum8(dvn)
        dvg = _ln_bwd(dvn, sln_ref[0:1, :], fw["vhat"], fw["rstdv"])
        dpbuf[:, DP:DP + DS] = (dug * _dgelu(fw["u"], fw["tu"])).astype(BF16)
        dpbuf[:, DP + DS:] = (dvg * _dgelu(fw["v"], fw["tv"])).astype(BF16)

        dpb = dpbuf[...]
        dwin_acc[...] += _mm_tn(dpb, hb)
        dh = _mm(dpb, winT_ref[...])
        dx_ref[...] = dh * (1.0 + sc1) + alpha * dr1
        vacc[3] += _colsum8(dh)
        vacc[4] += _colsum8(dh * xs_ref[...].astype(F32))

        @pl.when(i == nt - 1)
        def _():
            dwin_ref[...] = dwin_acc[...].astype(BF16)
            dwout_ref[...] = dwout_acc[...].astype(BF16)
            tri = _tril_mask()
            for h in range(NG):
                dsw_ref[h] = jnp.where(tri, dsw_ref[h], 0.0)
                sb = jnp.sum(dbacc[:, h * GW:(h + 1) * GW], axis=1, keepdims=True)
                dsb_ref[h] = jnp.broadcast_to(sb, (GW, GW))
            for n in range(5):
                va_ref[n:n + 1, :] = jnp.sum(vacc[n], axis=0, keepdims=True)
            for n in range(3):
                va5_ref[n:n + 1, :] = jnp.sum(vacc5[n], axis=0, keepdims=True)

    rev = lambda i: (nt - 1 - i, 0)
    tile = pl.BlockSpec((ts, D), rev)
    return _pcall(
        body, f"ba{l}", nt,
        in_specs=[tile, pl.BlockSpec((ts, 128), rev), tile, tile, tile, pl.BlockSpec((ts, DIN), rev), tile,
                  pl.BlockSpec((ts, DP), rev),
                  _layer_vec(8, D, l), _layer_vec(2, D, l),
                  _resident((DIN, D)), _resident((D, D))] + _small_specs(l),
        out_specs=[tile] + [_VMEM_WHOLE] * 7,
        out_shape=[jax.ShapeDtypeStruct((s, D), F32), jax.ShapeDtypeStruct((DIN, D), BF16),
                   jax.ShapeDtypeStruct((D, D), BF16), jax.ShapeDtypeStruct((NG, GW, GW), F32),
                   jax.ShapeDtypeStruct((NG, GW, GW), F32), jax.ShapeDtypeStruct((NG, GW, GW), F32),
                   jax.ShapeDtypeStruct((5, D), F32), jax.ShapeDtypeStruct((3, DP), F32)],
        scratch=[pltpu.VMEM((AHALO, DP), F32),
                 pltpu.VMEM((ts, DS), F32), pltpu.VMEM((ts, DS), F32), pltpu.VMEM((ts, D), BF16),
                 pltpu.VMEM((ts, DIN), BF16), pltpu.VMEM((GW, DS), F32), pltpu.VMEM((5, 8, D), F32),
                 pltpu.VMEM((3, 8, DP), F32), pltpu.VMEM((DIN, D), F32), pltpu.VMEM((D, D), F32)],
        args=[r1, st1, dx1, f1, hsave, proj, xsave, pooled, modv, ln1, winT, wout, *small], pay=pay)


def _cast_call(arrs, name):
    r, c = arrs[0].shape
    rb = _row_block(r, 6 * c * len(arrs))

    def body(*refs):
        for src, dst in zip(refs[:len(arrs)], refs[len(arrs):]):
            dst[...] = src[...].astype(BF16)

    blk = pl.BlockSpec((rb, c), lambda i: (i, 0))
    return pl.pallas_call(
        body, name=name, grid=(r // rb,), in_specs=[blk] * len(arrs), out_specs=[blk] * len(arrs),
        out_shape=[jax.ShapeDtypeStruct((r, c), BF16)] * len(arrs), compiler_params=_ARB)(*arrs)


def _rows_before(halo, x):
    ext = jnp.concatenate([halo, x], axis=0)
    return pltpu.roll(ext, 1, 0)[GHALO:, :], pltpu.roll(ext, 2, 0)[GHALO:, :]


def _rows_after(x, halo):
    ts = x.shape[0]
    ext = jnp.concatenate([x, halo], axis=0)
    n = ts + GHALO
    return pltpu.roll(ext, n - 1, 0)[0:ts, :], pltpu.roll(ext, n - 2, 0)[0:ts, :]


def _fb_call(l, alpha, r1, modv, ln1, wup4, wd2, conv_w, conv_b, pay=None):
    s = r1.shape[0]
    nt = s // TSF

    def body(r1_ref, modv_ref, ln1_ref, wup_ref, wd_ref, cw_ref, cb_ref, r2_ref, f2_ref, h2_ref, gs_ref, vs_ref,
             gc_ref, x1_ref, st_ref, gbuf):
        i = pl.program_id(0)

        @pl.when(i == 0)
        def _():
            gbuf[...] = jnp.zeros_like(gbuf)

        xhat1, st_ref[...] = _ln_stats_tile(r1_ref[...])
        x1 = xhat1 * ln1_ref[0:1, :] + ln1_ref[1:2, :]
        h2b = (x1 * (1.0 + modv_ref[4:5, :]) + modv_ref[3:4, :]).astype(BF16)
        f2 = jnp.zeros((TSF, D), F32)
        for c0, c1 in FF_CHUNKS:
            cs = slice(c0, c1)
            g = _mm_nt(h2b, wup_ref[0, cs, :])
            val = _mm_nt(h2b, wup_ref[1, cs, :])
            gm1, gm2 = _rows_before(gbuf[:, cs], g)
            cw = cw_ref[:, cs]
            gc = cb_ref[:, cs] + cw[0:1, :] * gm2 + cw[1:2, :] * gm1 + cw[2:3, :] * g
            ge, _ = _gelu_t(gc)
            f2 = f2 + _mm((ge * val).astype(BF16), wd_ref[cs, :])
            gs_ref[:, cs] = g.astype(BF16)
            vs_ref[:, cs] = val.astype(BF16)
            gc_ref[:, cs] = gc.astype(BF16)
            gbuf[:, cs] = g[TSF - GHALO:, :]
        r2_ref[...] = alpha * x1 + modv_ref[5:6, :] * f2
        f2_ref[...] = f2.astype(BF16)
        h2_ref[...] = h2b
        x1_ref[...] = x1.astype(BF16)

    tile = pl.BlockSpec((TSF, D), lambda i: (i, 0))
    wide = pl.BlockSpec((TSF, FF), lambda i: (i, 0))
    return _pcall(
        body, f"fb{l}", nt,
        in_specs=[tile, _layer_vec(8, D, l), _layer_vec(2, D, l), _resident((2, FF, D)), _resident((FF, D)),
                  _layer_vec(3, FF, l), _layer_vec(1, FF, l)],
        out_specs=[tile, tile, tile, wide, wide, wide, tile, pl.BlockSpec((TSF, 128), lambda i: (i, 0))],
        out_shape=[jax.ShapeDtypeStruct((s, D), F32), jax.ShapeDtypeStruct((s, D), BF16),
                   jax.ShapeDtypeStruct((s, D), BF16), jax.ShapeDtypeStruct((s, FF), BF16),
                   jax.ShapeDtypeStruct((s, FF), BF16), jax.ShapeDtypeStruct((s, FF), BF16),
                   jax.ShapeDtypeStruct((s, D), BF16), jax.ShapeDtypeStruct((s, 128), F32)],
        scratch=[pltpu.VMEM((GHALO, FF), F32)],
        args=[r1, modv, ln1, wup4.reshape(2, FF, D), wd2.reshape(FF, D), conv_w, conv_b], pay=pay)


def _ff_backward(h2b, df2b, gs_ref, vs_ref, gc_ref, dgc_next, cw, wg_ref, wv_ref, wd_ref,
                 dwg_ref, dwv_ref, dwd_ref, cacc):
    g = gs_ref[...].astype(F32)
    val = vs_ref[...].astype(F32)
    gc = gc_ref[...].astype(F32)
    ge, tg = _gelu_t(gc)
    mb = (ge * val).astype(BF16)
    dm = _mm_nt(df2b, wd_ref[...])
    dwd_ref[...] += _mm_tn(mb, df2b)
    dval = dm * ge
    dgc = dm * val * _dgelu(gc, tg)
    dgp1, dgp2 = _rows_after(dgc, dgc_next[...])
    cacc[0] += _colsum8(dgc)
    cacc[1] += _colsum8(dgp2 * g)
    cacc[2] += _colsum8(dgp1 * g)
    cacc[3] += _colsum8(dgc * g)
    dg = cw[2:3, :] * dgc + cw[1:2, :] * dgp1 + cw[0:1, :] * dgp2
    dgc_next[...] = dgc[0:GHALO, :]
    dgb = dg.astype(BF16)
    dvb = dval.astype(BF16)
    dwg_ref[...] += _mm_tn(dgb, h2b)
    dwv_ref[...] += _mm_tn(dvb, h2b)
    return _mm(dgb, wg_ref[...]) + _mm(dvb, wv_ref[...])


def _bb_specs(l, hf, nt, ts):
    wide = pl.BlockSpec((ts, FH), lambda i: (nt - 1 - i, hf))
    ins = [wide, wide, wide, _resident((FH, D), (hf,)), _resident((FH, D), (2 + hf,)), _resident((FH, D), (hf,)),
           pl.BlockSpec((None, 3, FH), lambda i: (l, 0, hf))]
    acc_shapes = [jax.ShapeDtypeStruct((FH, D), BF16)] * 3
    return ins, acc_shapes, [pltpu.VMEM((FH, D), F32)] * 3


def _bb0_call(l, last, h2, r2, st2, dx2, f2, gs, vs, gcs, modv, ln2, wup4, wd2, conv_w, pay=None):
    s = r2.shape[0]
    nt = s // TSB
    ts = TSB

    def body(*refs):
        it = iter(refs)
        h2_ref, r2_ref, dx2_ref, f2_ref, gs_ref, vs_ref, gc_ref = (next(it) for _ in range(7))
        wg_ref, wv_ref, wd_ref, cw_ref, modv_ref, ln2_ref = (next(it) for _ in range(6))
        st2_ref = None if last else next(it)
        dh2_ref, df2_ref, dr2_ref, dwg_out, dwv_out, dwd_out, vb_ref, cacc_ref = (next(it) for _ in range(8))
        loss_ref = next(it) if last else None
        vacc, cacc, dgc_next, dwg_ref, dwv_ref, dwd_ref = (next(it) for _ in range(6))
        lacc = next(it) if last else None
        i = pl.program_id(0)

        @pl.when(i == 0)
        def _():
            dwg_ref[...] = jnp.zeros_like(dwg_ref)
            dwv_ref[...] = jnp.zeros_like(dwv_ref)
            dwd_ref[...] = jnp.zeros_like(dwd_ref)
            vacc[...] = jnp.zeros_like(vacc)
            cacc[...] = jnp.zeros_like(cacc)
            dgc_next[...] = jnp.zeros_like(dgc_next)
            if last:
                lacc[...] = jnp.zeros_like(lacc)

        if last:
            xhat2, rstd2 = _ln_stats(r2_ref[...])
        else:
            xhat2, rstd2 = _ln_from_tile(r2_ref[...], st2_ref[...])
        if last:
            diff = xhat2 * ln2_ref[0:1, :] + ln2_ref[1:2, :] - dx2_ref[...]
            dy = diff * (1.0 / D)
            lacc[...] += _colsum8(diff * diff)
        else:
            dy = dx2_ref[...]
        dr2 = _ln_bwd(dy, ln2_ref[0:1, :], xhat2, rstd2)
        vacc[0] += _colsum8(dy * xhat2)
        vacc[1] += _colsum8(dy)
        vacc[2] += _colsum8(dr2 * f2_ref[...].astype(F32))
        df2b = (dr2 * modv_ref[5:6, :]).astype(BF16)
        dr2_ref[...] = dr2
        df2_ref[...] = df2b
        dh2_ref[...] = _ff_backward(h2_ref[...], df2b, gs_ref, vs_ref, gc_ref, dgc_next, cw_ref[...],
                                    wg_ref, wv_ref, wd_ref, dwg_ref, dwv_ref, dwd_ref, cacc)

        @pl.when(i == nt - 1)
        def _():
            for n in range(3):
                vb_ref[n:n + 1, :] = jnp.sum(vacc[n], axis=0, keepdims=True)
            for n in range(4):
                cacc_ref[n:n + 1, :] = jnp.sum(cacc[n], axis=0, keepdims=True)
            dwg_out[...] = dwg_ref[...].astype(BF16)
            dwv_out[...] = dwv_ref[...].astype(BF16)
            dwd_out[...] = dwd_ref[...].astype(BF16)
            if last:
                loss_ref[...] = lacc[...]

    tile = pl.BlockSpec((ts, D), lambda i: (nt - 1 - i, 0))
    ff_ins, acc_shapes, acc_scratch = _bb_specs(l, 0, nt, ts)
    in_specs = [tile, tile, tile, tile] + ff_ins + [_layer_vec(8, D, l), _layer_vec(2, D, l)]
    args = [h2, r2, dx2, f2, gs, vs, gcs, wup4, wup4, wd2, conv_w, modv, ln2]
    if not last:
        in_specs.append(pl.BlockSpec((ts, 128), lambda i: (nt - 1 - i, 0)))
        args.append(st2)
    out_specs = [tile, tile, tile] + [_VMEM_WHOLE] * 5
    out_shape = [jax.ShapeDtypeStruct((s, D), F32), jax.ShapeDtypeStruct((s, D), BF16),
                 jax.ShapeDtypeStruct((s, D), F32)] + acc_shapes + [jax.ShapeDtypeStruct((3, D), F32),
                                                                    jax.ShapeDtypeStruct((4, FH), F32)]
    scratch = [pltpu.VMEM((3, 8, D), F32), pltpu.VMEM((4, 8, FH), F32), pltpu.VMEM((GHALO, FH), F32)] + acc_scratch
    if last:
        out_specs.append(_VMEM_WHOLE)
        out_shape.append(jax.ShapeDtypeStruct((8, D), F32))
        scratch.append(pltpu.VMEM((8, D), F32))
    return _pcall(body, f"bb{l}_0", nt, in_specs, out_specs, out_shape, scratch, args, pay=pay)


def _bb1_call(l, alpha, h2, x1s, df2, dr2, dh2_in, gs, vs, gcs, modv, wup4, wd2, conv_w, pay=None):
    s = h2.shape[0]
    nt = s // TSB
    ts = TSB

    def body(h2_ref, x1_ref, df2_ref, dr2_ref, dh2_ref, gs_ref, vs_ref, gc_ref, wg_ref, wv_ref, wd_ref, cw_ref,
             modv_ref, dx1_ref, dwg_out, dwv_out, dwd_out, vb_ref, cacc_ref, vacc, cacc, dgc_next,
             dwg_ref, dwv_ref, dwd_ref):
        i = pl.program_id(0)

        @pl.when(i == 0)
        def _():
            dwg_ref[...] = jnp.zeros_like(dwg_ref)
            dwv_ref[...] = jnp.zeros_like(dwv_ref)
            dwd_ref[...] = jnp.zeros_like(dwd_ref)
            vacc[...] = jnp.zeros_like(vacc)
            cacc[...] = jnp.zeros_like(cacc)
            dgc_next[...] = jnp.zeros_like(dgc_next)

        dh2 = dh2_ref[...] + _ff_backward(h2_ref[...], df2_ref[...], gs_ref, vs_ref, gc_ref, dgc_next,
                                          cw_ref[...], wg_ref, wv_ref, wd_ref, dwg_ref, dwv_ref, dwd_ref, cacc)
        dx1_ref[...] = dh2 * (1.0 + modv_ref[4:5, :]) + alpha * dr2_ref[...]
        vacc[0] += _colsum8(dh2)
        vacc[1] += _colsum8(dh2 * x1_ref[...].astype(F32))

        @pl.when(i == nt - 1)
        def _():
            dwg_out[...] = dwg_ref[...].astype(BF16)
            dwv_out[...] = dwv_ref[...].astype(BF16)
            dwd_out[...] = dwd_ref[...].astype(BF16)
            for n in range(2):
                vb_ref[n:n + 1, :] = jnp.sum(vacc[n], axis=0, keepdims=True)
            for n in range(4):
                cacc_ref[n:n + 1, :] = jnp.sum(cacc[n], axis=0, keepdims=True)

    tile = pl.BlockSpec((ts, D), lambda i: (nt - 1 - i, 0))
    ff_ins, acc_shapes, acc_scratch = _bb_specs(l, 1, nt, ts)
    out_shape = [jax.ShapeDtypeStruct((s, D), F32)] + acc_shapes + [jax.ShapeDtypeStruct((2, D), F32),
                                                                   jax.ShapeDtypeStruct((4, FH), F32)]
    scratch = [pltpu.VMEM((2, 8, D), F32), pltpu.VMEM((4, 8, FH), F32), pltpu.VMEM((GHALO, FH), F32)] + acc_scratch
    return _pcall(body, f"bb{l}_1", nt, [tile] * 5 + ff_ins + [_layer_vec(8, D, l)],
                  [tile] + [_VMEM_WHOLE] * 5, out_shape, scratch,
                  [h2, x1s, df2, dr2, dh2_in, gs, vs, gcs, wup4, wup4, wd2, conv_w, modv], pay=pay)


def _silu(c):
    return c * (1.0 / (1.0 + jnp.exp(-c)))


def _ada_fwd_call(c_all, ada_w, ada_b_my):
    nl, _, wcols = ada_w.shape

    def body(c_ref, w_ref, b_ref, o_ref):
        ca = _silu(c_ref[...])
        o_ref[...] = jnp.dot(ca, w_ref[...], preferred_element_type=F32,
                             precision=lax.Precision.HIGHEST) + b_ref[...]

    return pl.pallas_call(
        body, name="ada_fwd", grid=(nl,),
        in_specs=[_full((NDEV, D)), pl.BlockSpec((None, D, wcols), lambda l: (l, 0, 0)),
                  pl.BlockSpec((None, 1, wcols), lambda l: (l, 0, 0))],
        out_specs=pl.BlockSpec((None, NDEV, wcols), lambda l: (l, 0, 0)),
        out_shape=jax.ShapeDtypeStruct((nl, NDEV, wcols), F32),
        compiler_params=_ARB,
    )(c_all, ada_w, ada_b_my)


def _adam_update(w, g, m, v):
    m2 = ADAM_B1 * m + (1.0 - ADAM_B1) * g
    v2 = ADAM_B2 * v + (1.0 - ADAM_B2) * (g * g)
    m_hat = m2 / (1.0 - ADAM_B1 ** ADAM_STEP)
    v_hat = v2 / (1.0 - ADAM_B2 ** ADAM_STEP)
    delta = -ADAM_LR * (m_hat / (jnp.sqrt(v_hat) + ADAM_EPS) + ADAM_WD * w)
    return delta, m2, v2


def _ada_bwd_call(c_t, dmod_my, w, m, v):
    nl, _, wcols = w.shape
    rb = 256

    def body(ct_ref, dm_ref, w_ref, m_ref, v_ref, g_ref, d_ref, m2_ref, v2_ref):
        ca_t = _silu(ct_ref[...])
        dm = dm_ref[...]
        g = ca_t[:, 0:1] * dm[0:1, :]
        for b in range(1, NDEV):
            g = g + ca_t[:, b:b + 1] * dm[b:b + 1, :]
        delta, m2, v2 = _adam_update(w_ref[...], g, m_ref[...], v_ref[...])
        g_ref[...] = g
        d_ref[...] = delta
        m2_ref[...] = m2
        v2_ref[...] = v2

    blk = pl.BlockSpec((None, rb, wcols), lambda l, i: (l, i, 0))
    shp = jax.ShapeDtypeStruct(w.shape, F32)
    return pl.pallas_call(
        body, name="ada_bwd", grid=(nl, D // rb),
        in_specs=[pl.BlockSpec((rb, NDEV), lambda l, i: (i, 0)),
                  pl.BlockSpec((None, NDEV, wcols), lambda l, i: (l, 0, 0)), blk, blk, blk],
        out_specs=[blk, blk, blk, blk], out_shape=[shp, shp, shp, shp],
        compiler_params=pltpu.CompilerParams(dimension_semantics=("arbitrary", "arbitrary")),
    )(c_t, dmod_my, w, m, v)


def _row_block(r, row_bytes):
    budget = 6 * 1024 * 1024
    best = None
    for rb in range(16, min(r, 512) + 1, 16):
        if r % rb == 0 and rb * row_bytes <= budget:
            best = rb
    return best if best is not None else r


def _adamw_rows_call(buf, items):
    nl = buf.shape[0]
    n = len(items)

    def body(*refs):
        buf_ref, ins, outs = refs[0], refs[1:1 + 3 * n], refs[1 + 3 * n:]
        for k, (w, _, _, row, lane0) in enumerate(items):
            w_ref, m_ref, v_ref = ins[3 * k:3 * k + 3]
            g_ref, d_ref, m2_ref, v2_ref = outs[4 * k:4 * k + 4]
            width = w.shape[1]
            for l in range(nl):
                for c0 in range(0, width, 1024):
                    n_ = min(1024, width - c0)
                    r = row + c0 // 1024
                    g = buf_ref[l, 0, r:r + 1, lane0:lane0 + n_]
                    for dev in range(1, NDEV):
                        g = g + buf_ref[l, dev, r:r + 1, lane0:lane0 + n_]
                    at = (slice(l, l + 1), slice(c0, c0 + n_))
                    delta, m2, v2 = _adam_update(w_ref[at], g, m_ref[at], v_ref[at])
                    g_ref[at] = g
                    d_ref[at] = delta
                    m2_ref[at] = m2
                    v2_ref[at] = v2

    args = [buf]
    out_shape = []
    for w, m, v, _, _ in items:
        args += [w, m, v]
        out_shape += [jax.ShapeDtypeStruct(w.shape, F32)] * 4
    res = pl.pallas_call(body, name="adamw_vectors", in_specs=[_VMEM_WHOLE] * len(args),
                         out_specs=[_VMEM_WHOLE] * len(out_shape), out_shape=out_shape)(*args)
    return [tuple(res[4 * k:4 * k + 4]) for k in range(n)]


def _adamw_call(parts, w, m, v, name, pay=None, row0=0):
    nl, npart, _, c = parts.shape
    r = w.shape[1]
    rb = _row_block(r, (npart + 7) * c * 4)
    nb = r // rb
    assert row0 % rb == 0
    b0 = row0 // rb

    def body(p_ref, w_ref, m_ref, v_ref, g_ref, d_ref, m2_ref, v2_ref):
        g = p_ref[0].astype(F32)
        for k in range(1, npart):
            g = g + p_ref[k].astype(F32)
        delta, m2, v2 = _adam_update(w_ref[...], g, m_ref[...], v_ref[...])
        g_ref[...] = g
        d_ref[...] = delta
        m2_ref[...] = m2
        v2_ref[...] = v2

    blk = pl.BlockSpec((None, rb, c), lambda i: (i // nb, i % nb, 0))
    shp = jax.ShapeDtypeStruct((nl, r, c), F32)
    outs, _, alias = _pcall(body, name, nl * nb,
                            [pl.BlockSpec((None, npart, rb, c), lambda i: (i // nb, 0, b0 + i % nb, 0)), blk, blk, blk],
                            [blk] * 4, [shp] * 4, [], [parts, w, m, v], pay)
    return tuple(outs), alias


_ROW_OF = dict(ada_b=(0, 0), ln1_g=(6, 0), ln1_b=(7, 0), ln2_g=(8, 0), ln2_b=(9, 0), conv_b=(10, 0),
               pool_scale=(13, 0), sgu_ln_g=(13, DP), sgu_ln_b=(14, 0), sgu_b=(14, DP), conv_w=(15, 0))
_SMALL_ROWS = 24
_FF_PAD = 3 * 1024 - FF


def kernel(x, c, ada_w, ada_b, w_in, pool_w, pool_scale, sgu_ln_g, sgu_ln_b, sgu_w, sgu_b, w_out, ln1_g, ln1_b, w_up, conv_w, conv_b, w_down, ln2_g, ln2_b, loss_target, m_ada_w, m_ada_b, m_w_in, m_pool_w, m_pool_scale, m_sgu_ln_g, m_sgu_ln_b, m_sgu_w, m_sgu_b, m_w_out, m_ln1_g, m_ln1_b, m_w_up, m_conv_w, m_conv_b, m_w_down, m_ln2_g, m_ln2_b, v_ada_w, v_ada_b, v_w_in, v_pool_w, v_pool_scale, v_sgu_ln_g, v_sgu_ln_b, v_sgu_w, v_sgu_b, v_w_out, v_ln1_g, v_ln1_b, v_w_up, v_conv_w, v_conv_b, v_w_down, v_ln2_g, v_ln2_b):
    nl = ada_w.shape[0]
    alpha = (2.0 * nl) ** 0.25
    me = _me()
    x2 = x[0]
    tgt = loss_target[0]
    acols = ada_w.shape[2]
    icols = w_in.shape[2]
    ucols = w_up.shape[2]
    orows = w_out.shape[1]
    drows = w_down.shape[1]
    ccols = conv_w.shape[2]

    winT_sh = jnp.swapaxes(w_in, 1, 2).astype(BF16)
    wupT_sh = jnp.swapaxes(w_up, 1, 2).astype(BF16)
    wout_sh = w_out.astype(BF16)
    wd_sh = w_down.astype(BF16)

    pay = _Payload()
    pay.gather(jnp.broadcast_to(c, (8, D)))
    pay.gather(winT_sh, 0)
    pay.gather(wout_sh, 0)
    pay.gather(conv_w)
    (c_g, winT_g, wout_g, cw_g), _ = _comm_call(pay, "gather_first")
    c_all = c_g[:, 0, :]
    winT = winT_g.reshape(DIN, D)
    wout = wout_g.reshape(D, D)
    cw_full = jnp.transpose(cw_g, (1, 2, 0, 3)).reshape(nl, 3, FF)
    cb_full = conv_b[:, None, :]

    ada_b_my = lax.dynamic_slice(ada_b, (0, me * acols), (nl, acols))[:, None, :]
    mod_blk = _ada_fwd_call(c_all, ada_w, ada_b_my)
    pay = _Payload()
    pay.gather(mod_blk)
    (mod_g,), _ = _comm_call(pay, "gather_mod")
    mod_me = lax.dynamic_index_in_dim(mod_g, me, axis=2, keepdims=False)
    modv = jnp.swapaxes(mod_me, 0, 1).reshape(nl, 6, D)
    modv = jnp.concatenate([modv, jnp.zeros((nl, 2, D), F32)], axis=1)

    ln1 = jnp.stack([ln1_g, ln1_b], axis=1)
    ln2 = jnp.stack([ln2_g, ln2_b], axis=1)
    sln = jnp.stack([sgu_ln_g, sgu_ln_b], axis=1)
    sbf = jnp.broadcast_to(sgu_b[..., None], sgu_b.shape + (GW,))
    small = (pool_w, pool_scale[:, None, :], sln, sgu_w, sbf)

    r1s, r2s, f1s, f2s, ahs, ghs, wins, wouts, wups, wds, st1s, st2s = ([None] * nl for _ in range(12))
    wins[0], wouts[0] = winT, wout
    xin = x2
    for l in range(nl):
        pay = None
        if l == 0:
            pay = _Payload()
            pay.gather(wupT_sh, 0)
            pay.gather(wd_sh, 0)
        (r1, f1, *ah, st_prev), new, _ = _fa_call(l, l == 0, alpha, xin, modv, ln2, wins[l], wouts[l], small, pay)
        if l > 0:
            st2s[l - 1] = st_prev
        if l == 0:
            wups[0], wds[0] = new[0].reshape(4, FH, D), new[1].reshape(2, FH, D)
        pay = None
        if l + 1 < nl:
            pay = _Payload()
            pay.gather(winT_sh, l + 1)
            pay.gather(wout_sh, l + 1)
            pay.gather(wupT_sh, l + 1)
            pay.gather(wd_sh, l + 1)
        (r2, f2, *saved_b, st1s[l]), new, _ = _fb_call(l, alpha, r1, modv, ln1, wups[l], wds[l], cw_full, cb_full,
                                                       pay)
        if l + 1 < nl:
            wins[l + 1], wouts[l + 1] = new[0].reshape(DIN, D), new[1].reshape(D, D)
            wups[l + 1], wds[l + 1] = new[2].reshape(4, FH, D), new[3].reshape(2, FH, D)
        r1s[l], r2s[l], f1s[l], f2s[l], ahs[l], ghs[l] = r1, r2, f1, f2, ah, saved_b
        xin = r2

    buf_in = lax.empty((nl, NDEV, icols, D), BF16)
    buf_out = lax.empty((nl, NDEV, orows, D), BF16)
    buf_up = lax.empty((nl, NDEV, ucols, D), BF16)
    buf_down = lax.empty((nl, NDEV, drows, D), BF16)
    buf_small = lax.empty((nl, NDEV, _SMALL_ROWS, 1024), F32)
    buf_mat = lax.empty((nl, NDEV, 2 * NG * GW, GW), F32)

    def a_side_payload(l, dwin, dwout, flat, mats):
        p = _Payload()
        p.exchange_into([(dwin.reshape(NDEV, icols, D), 0)], buf_in, (l,))
        p.exchange_into([(dwout.reshape(NDEV, orows, D), 0)], buf_out, (l,))
        p.gather_into(flat, buf_small, (l,))
        p.gather_into(mats, buf_mat, (l,))
        return p

    dx = tgt
    loss_acc = None
    pending = None
    for l in reversed(range(nl)):
        last = l == nl - 1
        pay = None if pending is None else a_side_payload(*pending)
        h2, gs, vs, gcs, x1s = ghs[l]
        res0, _, al = _bb0_call(l, last, h2, r2s[l], st2s[l], dx, f2s[l], gs, vs, gcs, modv, ln2, wups[l], wds[l],
                                cw_full, pay)
        if pending is not None:
            buf_in, buf_out, buf_small, buf_mat = al
        dh2p, df2, dr2, dwg0, dwv0, dwd0, vb0, cacc0 = res0[:8]
        if last:
            loss_acc = res0[8]
        (dx1, dwg1, dwv1, dwd1, vb1, cacc1), _, _ = _bb1_call(
            l, alpha, h2, x1s, df2, dr2, dh2p, gs, vs, gcs, modv, wups[l], wds[l], cw_full)
        pay = _Payload()
        pay.exchange_into([(dwg0.reshape(2, ucols, D), 0), (dwg1.reshape(2, ucols, D), 2),
                           (dwv0.reshape(2, ucols, D), 4), (dwv1.reshape(2, ucols, D), 6)], buf_up, (l,))
        pay.exchange_into([(dwd0.reshape(4, drows, D), 0), (dwd1.reshape(4, drows, D), 4)], buf_down, (l,))
        (dx, dwin, dwout, dpw, dsw, dsb, va, va5), _, (buf_up, buf_down) = _ba_call(
            l, alpha, r1s[l], st1s[l], dx1, f1s[l], *ahs[l], modv, ln1, wins[l], wouts[l], small, pay)
        cacc = jnp.pad(jnp.concatenate([cacc0, cacc1], axis=1), ((0, 0), (0, _FF_PAD)))
        flat = jnp.concatenate([
            va[3], va[4], va[2], vb1[0], vb1[1], vb0[2],
            va[0], va[1], vb0[0], vb0[1],
            cacc[0],
            va5[0], va5[1], va5[2], dsb[:, :, 0].reshape(-1),
            cacc[1:4].reshape(-1),
        ]).reshape(_SMALL_ROWS, 1024)
        mats = jnp.concatenate([dpw.reshape(NG * GW, GW), dsw.reshape(NG * GW, GW)])
        pending = (l, dwin, dwout, flat, mats)
    grad_x = dx[None]
    loss = lax.psum((0.5 / D) * jnp.sum(loss_acc), ("x", "y", "c"))

    res = {}
    _, dwin0, dwout0, flat0, mats0 = pending
    t_ = lambda a: jnp.swapaxes(a, 1, 2)
    pay = _Payload()
    pay.exchange_into([(dwout0.reshape(NDEV, orows, D), 0)], buf_out, (0,))
    pay.gather_into(flat0, buf_small, (0,))
    pay.gather_into(mats0, buf_mat, (0,))
    outs, (buf_out, buf_small, buf_mat) = _adamw_call(buf_up, t_(w_up), t_(m_w_up), t_(v_w_up), "adamw_w_up", pay)
    res["w_up"] = tuple(t_(o) for o in outs)
    pay = _Payload()
    pay.exchange_into([(dwin0.reshape(NDEV, icols, D), 0)], buf_in, (0,))
    res["w_down"], (buf_in,) = _adamw_call(buf_down, w_down, m_w_down, v_w_down, "adamw_w_down", pay)
    outs, _ = _adamw_call(buf_in, t_(w_in), t_(m_w_in), t_(v_w_in), "adamw_w_in")
    res["w_in"] = tuple(t_(o) for o in outs)
    res["w_out"], _ = _adamw_call(buf_out, w_out, m_w_out, v_w_out, "adamw_w_out")

    weights = dict(ada_b=ada_b, pool_w=pool_w, pool_scale=pool_scale, sgu_ln_g=sgu_ln_g, sgu_ln_b=sgu_ln_b,
                   sgu_w=sgu_w, sgu_b=sgu_b, ln1_g=ln1_g, ln1_b=ln1_b, conv_w=conv_w, conv_b=conv_b, ln2_g=ln2_g,
                   ln2_b=ln2_b)
    m_of = dict(ada_b=m_ada_b, pool_w=m_pool_w, pool_scale=m_pool_scale, sgu_ln_g=m_sgu_ln_g, sgu_ln_b=m_sgu_ln_b,
                sgu_w=m_sgu_w, sgu_b=m_sgu_b, ln1_g=m_ln1_g, ln1_b=m_ln1_b, conv_w=m_conv_w, conv_b=m_conv_b,
                ln2_g=m_ln2_g, ln2_b=m_ln2_b)
    v_of = dict(ada_b=v_ada_b, pool_w=v_pool_w, pool_scale=v_pool_scale, sgu_ln_g=v_sgu_ln_g, sgu_ln_b=v_sgu_ln_b,
                sgu_w=v_sgu_w, sgu_b=v_sgu_b, ln1_g=v_ln1_g, ln1_b=v_ln1_b, conv_w=v_conv_w, conv_b=v_conv_b,
                ln2_g=v_ln2_g, ln2_b=v_ln2_b)
    vec_names = ("ada_b", "ln1_g", "ln1_b", "ln2_g", "ln2_b", "conv_b", "pool_scale", "sgu_ln_g", "sgu_ln_b")
    vec_res = _adamw_rows_call(buf_small, [(weights[n], m_of[n], v_of[n], *_ROW_OF[n]) for n in vec_names])
    res.update(zip(vec_names, vec_res))
    for k, name in enumerate(("pool_w", "sgu_w")):
        w3 = weights[name].reshape(nl, NG * GW, GW)
        outs, _ = _adamw_call(buf_mat, w3, m_of[name].reshape(w3.shape), v_of[name].reshape(w3.shape),
                              f"adamw_{name}", row0=k * NG * GW)
        res[name] = tuple(o.reshape(weights[name].shape) for o in outs)
    row, lane = _ROW_OF["sgu_b"]
    parts = buf_small[:, :, row, lane:lane + NG * GW].reshape(nl, NDEV, NG, GW)
    res["sgu_b"], _ = _adamw_call(parts, sgu_b, m_sgu_b, v_sgu_b, "adamw_sgu_b")
    row, _ = _ROW_OF["conv_w"]
    parts = buf_small[:, :, row:row + 9, :].reshape(nl, NDEV, 3, 3 * 1024)
    parts = lax.dynamic_slice_in_dim(parts, me * ccols, ccols, axis=3)
    res["conv_w"], _ = _adamw_call(parts, conv_w, m_conv_w, v_conv_w, "adamw_conv_w")

    dmod_my = lax.dynamic_slice_in_dim(buf_small[:, :, 0:6, :].reshape(nl, NDEV, 6 * D), me * acols, acols, axis=2)
    res["ada_w"] = _ada_bwd_call(jnp.swapaxes(c_all, 0, 1), dmod_my, ada_w, m_ada_w, v_ada_w)

    order = ["ada_w", "ada_b", "w_in", "pool_w", "pool_scale", "sgu_ln_g", "sgu_ln_b", "sgu_w", "sgu_b", "w_out",
             "ln1_g", "ln1_b", "w_up", "conv_w", "conv_b", "w_down", "ln2_g", "ln2_b"]
    out = [loss, grad_x]
    for k in range(4):
        out += [res[n][k] for n in order]
    return tuple(out)
```

```python
import jax
import jax.numpy as jnp
from jax import lax
from jax.experimental import pallas as pl
from jax.experimental.pallas import tpu as pltpu

F32 = jnp.float32
BF16 = jnp.bfloat16

NDEV = 8
D = 1024
DP = 512
DS = 512
DIN = DP + 2 * DS
FF = 2816
FH = FF // 2
FF_CHUNKS = ((0, 1536), (1536, FF))
NG = 4
GW = 128
WINDOWS = (2, 4, 8, 16)
AHALO = 16
GHALO = 8
LN_EPS = 1e-5
ADAM_LR, ADAM_B1, ADAM_B2, ADAM_EPS, ADAM_WD, ADAM_STEP = 0.001, 0.9, 0.999, 1e-08, 0.01, 10
TSA = 1024
TSF = 512
TSB = 256
TSBA = 512
_K0 = 0.7978845608028654
_K1 = 0.044715
MESH_ID = pl.DeviceIdType.MESH


def _mm(a, b):
    return jnp.dot(a, b, preferred_element_type=F32)


def _mm_nt(a, b):
    return lax.dot_general(a, b, (((1,), (1,)), ((), ())), preferred_element_type=F32)


def _mm_tn(a, b):
    return lax.dot_general(a, b, (((0,), (0,)), ((), ())), preferred_element_type=F32)


def _rowmean(x):
    return jnp.mean(x, axis=-1, keepdims=True)


def _ln_stats(x):
    mu = _rowmean(x)
    xc = x - mu
    rstd = lax.rsqrt(_rowmean(xc * xc) + LN_EPS)
    return xc * rstd, rstd


def _ln_stats_tile(x):
    mu = _rowmean(x)
    xc = x - mu
    rstd = lax.rsqrt(_rowmean(xc * xc) + LN_EPS)
    lane = lax.broadcasted_iota(jnp.int32, (x.shape[0], 128), 1)
    return xc * rstd, jnp.where(lane == 0, mu, jnp.where(lane == 1, rstd, 0.0))


def _ln_from_tile(x, st):
    rstd = st[:, 1:2]
    return (x - st[:, 0:1]) * rstd, rstd


def _ln_bwd(dy, gamma, xhat, rstd):
    dxh = dy * gamma
    return rstd * (dxh - _rowmean(dxh) - xhat * _rowmean(dxh * xhat))


def _gelu_t(x):
    t = jnp.tanh(x * (_K0 + (_K0 * _K1) * (x * x)))
    hx = 0.5 * x
    return hx + hx * t, t


def _dgelu(x, t):
    return (0.5 + 0.5 * t) + (0.5 * x) * (1.0 - t * t) * (_K0 + (3.0 * _K0 * _K1) * (x * x))


def _colsum8(x):
    t, n = x.shape
    return jnp.sum(x.reshape(t // 8, 8, n), axis=0)


def _tril_mask():
    r = lax.broadcasted_iota(jnp.int32, (GW, GW), 0)
    c = lax.broadcasted_iota(jnp.int32, (GW, GW), 1)
    return c <= r


def _full(shape):
    n = len(shape)
    return pl.BlockSpec(shape, lambda *_: (0,) * n)


def _resident(tail, lead=()):
    n = len(tail)
    return pl.BlockSpec((None,) * len(lead) + tuple(tail), lambda *_: tuple(lead) + (0,) * n,
                        pipeline_mode=pl.Buffered(1))


def _layer_vec(rows, width, l):
    return pl.BlockSpec((None, rows, width), lambda *_: (l, 0, 0))


_VMEM_WHOLE = pl.BlockSpec(memory_space=pltpu.VMEM)
_HBM = pl.BlockSpec(memory_space=pl.ANY)
_ARB = pltpu.CompilerParams(dimension_semantics=("arbitrary",))


def _me():
    return 4 * lax.axis_index("x") + 2 * lax.axis_index("y") + lax.axis_index("c")


def _coords(p):
    return (p >> 2, (p >> 1) & 1, p & 1)


class _Payload:
    def __init__(self):
        self.srcs, self.new, self.alias, self.transfers = [], [], [], []

    def _src(self, arr):
        self.srcs.append(arr)
        return len(self.srcs) - 1

    def _alias(self, buf):
        self.alias.append(buf)
        return len(self.alias) - 1

    def gather(self, arr, chunk=None):
        pos = self._src(arr)
        blk = arr.shape if chunk is None else arr.shape[1:]
        self.new.append(jax.ShapeDtypeStruct((NDEV,) + tuple(blk), arr.dtype))
        self.transfers.append(([(pos, chunk)] * NDEV, ("new", len(self.new) - 1), ()))
        return len(self.new) - 1

    def gather_into(self, arr, buf, lead):
        pos = self._src(arr)
        self.transfers.append(([(pos, None)] * NDEV, ("alias", self._alias(buf)), tuple(lead)))

    def exchange_into(self, parts, buf, lead):
        route = {}
        for arr, first in parts:
            pos = self._src(arr)
            for q in range(arr.shape[0]):
                route[first + q] = (pos, q)
        self.transfers.append(([route[p] for p in range(NDEV)], ("alias", self._alias(buf)), tuple(lead)))

    def _ends(self, t, io, dst_dev, src_dev):
        srcs, new_out, alias_out = io
        route, (kind, k), lead = self.transfers[t]
        pos, q = route[dst_dev]
        src = srcs[pos] if q is None else srcs[pos].at[q]
        buf = new_out[k] if kind == "new" else alias_out[k]
        return src, buf.at[lead + (src_dev,)]

    def _remote(self, t, io, sems, src_dev, dst_dev):
        src, dst = self._ends(t, io, dst_dev, src_dev)
        return pltpu.make_async_remote_copy(
            src_ref=src, dst_ref=dst, send_sem=sems[0].at[t, dst_dev], recv_sem=sems[1].at[t, src_dev],
            device_id=_coords(dst_dev), device_id_type=MESH_ID)

    def _local(self, t, io, sems, p):
        src, dst = self._ends(t, io, p, p)
        return pltpu.make_async_copy(src, dst, sems[2].at[t])

    def start(self, io, sems):
        me = _me()
        for p in range(NDEV):
            @pl.when(me == p)
            def _():
                for t in range(len(self.transfers)):
                    self._local(t, io, sems, p).start()

            @pl.when(me != p)
            def _():
                for t in range(len(self.transfers)):
                    self._remote(t, io, sems, me, p).start()

    def wait(self, io, sems):
        me = _me()
        for p in range(NDEV):
            @pl.when(me == p)
            def _():
                for t in range(len(self.transfers)):
                    self._local(t, io, sems, p).wait()

            @pl.when(me != p)
            def _():
                for t in range(len(self.transfers)):
                    self._remote(t, io, sems, p, p).wait_recv()
                    self._remote(t, io, sems, me, p).wait_send()


def _pcall(body, name, nsteps, in_specs, out_specs, out_shape, scratch, args, pay=None):
    n_in, n_out, n_scr = len(args), len(out_shape), len(scratch)
    if pay is None:
        res = pl.pallas_call(body, name=name, grid=(nsteps,), in_specs=list(in_specs), out_specs=list(out_specs),
                             out_shape=list(out_shape), scratch_shapes=list(scratch), compiler_params=_ARB)(*args)
        return list(res), [], []
    ns, nn, na, nt = len(pay.srcs), len(pay.new), len(pay.alias), len(pay.transfers)

    def full(*refs):
        cin = refs[:n_in]
        srcs = refs[n_in:n_in + ns]
        o0 = n_in + ns + na
        cout = refs[o0:o0 + n_out]
        new_out = refs[o0 + n_out:o0 + n_out + nn]
        alias_out = refs[o0 + n_out + nn:o0 + n_out + nn + na]
        s0 = o0 + n_out + nn + na
        cscr = refs[s0:s0 + n_scr]
        sems = refs[s0 + n_scr:]
        io = (srcs, new_out, alias_out)
        i = pl.program_id(0)

        @pl.when(i == 0)
        def _():
            pay.start(io, sems)

        body(*cin, *cout, *cscr)

        @pl.when(i == nsteps - 1)
        def _():
            pay.wait(io, sems)

    res = pl.pallas_call(
        full, name=name, grid=(nsteps,),
        in_specs=list(in_specs) + [_HBM] * (ns + na), out_specs=list(out_specs) + [_HBM] * (nn + na),
        out_shape=list(out_shape) + pay.new + [jax.ShapeDtypeStruct(b.shape, b.dtype) for b in pay.alias],
        input_output_aliases={n_in + ns + k: n_out + nn + k for k in range(na)},
        scratch_shapes=list(scratch) + [pltpu.SemaphoreType.DMA((nt, NDEV)), pltpu.SemaphoreType.DMA((nt, NDEV)),
                                        pltpu.SemaphoreType.DMA((nt,))],
        compiler_params=_ARB,
    )(*args, *pay.srcs, *pay.alias)
    return list(res[:n_out]), list(res[n_out:n_out + nn]), list(res[n_out + nn:])


def _comm_call(pay, name):
    def body():
        pass

    _, new, alias = _pcall(body, name, 1, [], [], [], [], [], pay)
    return new, alias


def _window_sums(x, halo, before):
    ts = x.shape[0]
    ext = jnp.concatenate([halo, x] if before else [x, halo], axis=0)
    n = ts + AHALO
    shift = (lambda k: k) if before else (lambda k: n - k)
    keep = slice(AHALO, n) if before else slice(0, ts)
    out = []
    s = ext
    for level in range(NG):
        s = s + pltpu.roll(s, shift(1 << level), 0)
        out.append(s[keep, 0:GW])
        if level + 1 < NG:
            s = s[:, GW:]
    return out


def _a_forward(a, u, v, halo, pw_ref, ps_ref, sln_ref, sw_ref, sbf_ref, zbuf, tile, ts, pooled=None):
    tglob = tile * ts + lax.broadcasted_iota(jnp.int32, (ts, 1), 0)
    sums = _window_sums(a, halo, True) if pooled is None else None
    pooled_b, mixed, inv_cnt, pwb = [], [], [], []
    for g, w in enumerate(WINDOWS):
        inv = 1.0 / jnp.minimum(tglob + 1, w).astype(F32)
        if pooled is None:
            pg = (sums[g] * inv - a[:, g * GW:(g + 1) * GW]).astype(BF16)
        else:
            pg = pooled[:, g * GW:(g + 1) * GW]
        wg = pw_ref[g].astype(BF16)
        pooled_b.append(pg)
        inv_cnt.append(inv)
        pwb.append(wg)
        mixed.append(_mm(pg, wg))
    mixed = jnp.concatenate(mixed, axis=1)
    ya = mixed * ps_ref[...]
    ug, tu = _gelu_t(u)
    vg, tv = _gelu_t(v)
    vhat, rstdv = _ln_stats(vg)
    vnb = (vhat * sln_ref[0:1, :] + sln_ref[1:2, :]).astype(BF16)
    tri = _tril_mask()
    wt = [jnp.where(tri, sw_ref[h], 0.0).astype(BF16) for h in range(NG)]
    for c in range(ts // GW):
        rs = slice(c * GW, (c + 1) * GW)
        for h in range(NG):
            cs = slice(h * GW, (h + 1) * GW)
            zbuf[rs, cs] = _mm(wt[h], vnb[rs, cs]) + sbf_ref[h]
    z = zbuf[...]
    yb = ug * z
    return dict(u=u, v=v, tu=tu, tv=tv, ug=ug, z=z, vhat=vhat, rstdv=rstdv, vnb=vnb,
                wt=wt, pooled_b=pooled_b, pwb=pwb, inv_cnt=inv_cnt, mixed=mixed, ya=ya, yb=yb)


def _small_specs(l):
    grp = pl.BlockSpec((None, NG, GW, GW), lambda i: (l, 0, 0, 0))
    return [grp, _layer_vec(1, DP, l), _layer_vec(2, DS, l), grp, grp]


def _fa_call(l, first, alpha, xin, modv, lnp, winT, wout, small, pay=None):
    s = xin.shape[0]
    ts = min(TSA, s)
    nt = s // ts

    def body(xin_ref, modv_ref, lnp_ref, winT_ref, wout_ref, pw_ref, ps_ref, sln_ref, sw_ref, sbf_ref,
             r1_ref, f1_ref, h_ref, proj_ref, x_ref, pooled_ref, st_ref, acarry, zbuf, mixbuf):
        i = pl.program_id(0)

        @pl.when(i == 0)
        def _():
            acarry[...] = jnp.zeros_like(acarry)

        x = xin_ref[...]
        if first:
            st_ref[...] = jnp.zeros_like(st_ref)
        else:
            xhat, st_ref[...] = _ln_stats_tile(x)
            x = xhat * lnp_ref[0:1, :] + lnp_ref[1:2, :]
        hb = (x * (1.0 + modv_ref[1:2, :]) + modv_ref[0:1, :]).astype(BF16)
        proj = _mm_nt(hb, winT_ref[...])
        a = proj[:, 0:DP]
        fw = _a_forward(a, proj[:, DP:DP + DS], proj[:, DP + DS:], acarry[...], pw_ref, ps_ref, sln_ref, sw_ref,
                        sbf_ref, zbuf, i, ts)
        acarry[...] = a[ts - AHALO:, :]
        mixbuf[:, 0:DP] = fw["ya"].astype(BF16)
        mixbuf[:, DP:] = fw["yb"].astype(BF16)
        f = _mm(mixbuf[...], wout_ref[...])
        r1_ref[...] = alpha * x + modv_ref[2:3, :] * f
        f1_ref[...] = f.astype(BF16)
        h_ref[...] = hb
        proj_ref[...] = proj.astype(BF16)
        x_ref[...] = x.astype(BF16)
        pooled_ref[...] = jnp.concatenate(fw["pooled_b"], axis=1)

    tile = pl.BlockSpec((ts, D), lambda i: (i, 0))
    return _pcall(
        body, f"fa{l}", nt,
        in_specs=[tile, _layer_vec(8, D, l), _layer_vec(2, D, max(l - 1, 0)), _resident((DIN, D)),
                  _resident((D, D))] + _small_specs(l),
        out_specs=[tile, tile, tile, pl.BlockSpec((ts, DIN), lambda i: (i, 0)), tile,
                   pl.BlockSpec((ts, DP), lambda i: (i, 0)), pl.BlockSpec((ts, 128), lambda i: (i, 0))],
        out_shape=[jax.ShapeDtypeStruct((s, D), F32), jax.ShapeDtypeStruct((s, D), BF16),
                   jax.ShapeDtypeStruct((s, D), BF16), jax.ShapeDtypeStruct((s, DIN), BF16),
                   jax.ShapeDtypeStruct((s, D), BF16), jax.ShapeDtypeStruct((s, DP), BF16),
                   jax.ShapeDtypeStruct((s, 128), F32)],
        scratch=[pltpu.VMEM((AHALO, DP), F32), pltpu.VMEM((ts, DS), F32), pltpu.VMEM((ts, D), BF16)],
        args=[xin, modv, lnp, winT, wout, *small], pay=pay)


def _ba_call(l, alpha, r1, st1, dx1, f1, hsave, proj, xsave, pooled, modv, ln1, winT, wout, small, pay=None):
    s = r1.shape[0]
    ts = min(TSBA, s)
    nt = s // ts

    def body(r1_ref, st1_ref, dx1_ref, f1_ref, h_ref, proj_ref, xs_ref, pooled_ref, modv_ref, ln1_ref, winT_ref,
             wout_ref, pw_ref, ps_ref, sln_ref, sw_ref, sbf_ref,
             dx_ref, dwin_ref, dwout_ref, dpw_ref, dsw_ref, dsb_ref, va_ref, va5_ref,
             qnext, zbuf, dvnbuf, mixbuf, dpbuf, dbacc, vacc, vacc5):
        i = pl.program_id(0)
        j = nt - 1 - i
        dwin_acc, dwout_acc = dwin_ref, dwout_ref

        @pl.when(i == 0)
        def _():
            dwin_acc[...] = jnp.zeros_like(dwin_acc)
            dwout_acc[...] = jnp.zeros_like(dwout_acc)
            dpw_ref[...] = jnp.zeros_like(dpw_ref)
            dsw_ref[...] = jnp.zeros_like(dsw_ref)
            dbacc[...] = jnp.zeros_like(dbacc)
            vacc[...] = jnp.zeros_like(vacc)
            vacc5[...] = jnp.zeros_like(vacc5)
            qnext[...] = jnp.zeros_like(qnext)

        hb = h_ref[...]
        sc1 = modv_ref[1:2, :]
        fw = _a_forward(None, proj_ref[:, DP:DP + DS].astype(F32), proj_ref[:, DP + DS:].astype(F32), None, pw_ref,
                        ps_ref, sln_ref, sw_ref, sbf_ref, zbuf, j, ts, pooled=pooled_ref[...])
        mixbuf[:, 0:DP] = fw["ya"].astype(BF16)
        mixbuf[:, DP:] = fw["yb"].astype(BF16)

        xhat1, rstd1 = _ln_from_tile(r1_ref[...], st1_ref[...])
        dy = dx1_ref[...]
        vacc[0] += _colsum8(dy * xhat1)
        vacc[1] += _colsum8(dy)
        dr1 = _ln_bwd(dy, ln1_ref[0:1, :], xhat1, rstd1)
        vacc[2] += _colsum8(dr1 * f1_ref[...].astype(F32))
        dfb = (dr1 * modv_ref[2:3, :]).astype(BF16)
        dwout_acc[...] += _mm_tn(mixbuf[...], dfb)
        dmix = _mm_nt(dfb, wout_ref[...])
        dya = dmix[:, 0:DP]
        dyb = dmix[:, DP:]

        vacc5[0] += _colsum8(dya * fw["mixed"])
        dmixed = (dya * ps_ref[...]).astype(BF16)
        dpooled, q = [], []
        for g in range(NG):
            cs = slice(g * GW, (g + 1) * GW)
            dpw_ref[g] += _mm_tn(fw["pooled_b"][g], dmixed[:, cs])
            dpg = _mm_nt(dmixed[:, cs], fw["pwb"][g])
            dpooled.append(dpg)
            q.append(dpg * fw["inv_cnt"][g])
        q = jnp.concatenate(q, axis=1)
        sums = _window_sums(q, qnext[...], False)
        qnext[...] = q[0:AHALO, :]
        for g in range(NG):
            dpbuf[:, g * GW:(g + 1) * GW] = (sums[g] - dpooled[g]).astype(BF16)

        dug = dyb * fw["z"]
        dz = dyb * fw["ug"]
        dzb = dz.astype(BF16)
        for c in range(ts // GW):
            rs = slice(c * GW, (c + 1) * GW)
            for h in range(NG):
                cs = slice(h * GW, (h + 1) * GW)
                dvnbuf[rs, cs] = _mm_tn(fw["wt"][h], dzb[rs, cs])
                dsw_ref[h] += _mm_nt(dzb[rs, cs], fw["vnb"][rs, cs])
            dbacc[...] += dz[rs, :]
        dvn = dvnbuf[...]
        vacc5[1] += _colsum8(dvn * fw["vhat"])
        vacc5[2] += _colsum8(dvn)
        dvg = _ln_bwd(dvn, sln_ref[0:1, :], fw["vhat"], fw["rstdv"])
        dpbuf[:, DP:DP + DS] = (dug * _dgelu(fw["u"], fw["tu"])).astype(BF16)
        dpbuf[:, DP + DS:] = (dvg * _dgelu(fw["v"], fw["tv"])).astype(BF16)

        dpb = dpbuf[...]
        dwin_acc[...] += _mm_tn(dpb, hb)
        dh = _mm(dpb, winT_ref[...])
        dx_ref[...] = dh * (1.0 + sc1) + alpha * dr1
        vacc[3] += _colsum8(dh)
        vacc[4] += _colsum8(dh * xs_ref[...].astype(F32))

        @pl.when(i == nt - 1)
        def _():
            tri = _tril_mask()
            for h in range(NG):
                dsw_ref[h] = jnp.where(tri, dsw_ref[h], 0.0)
                sb = jnp.sum(dbacc[:, h * GW:(h + 1) * GW], axis=1, keepdims=True)
                dsb_ref[h] = jnp.broadcast_to(sb, (GW, GW))
            for n in range(5):
                va_ref[n:n + 1, :] = jnp.sum(vacc[n], axis=0, keepdims=True)
            for n in range(3):
                va5_ref[n:n + 1, :] = jnp.sum(vacc5[n], axis=0, keepdims=True)

    rev = lambda i: (nt - 1 - i, 0)
    tile = pl.BlockSpec((ts, D), rev)
    return _pcall(
        body, f"ba{l}", nt,
        in_specs=[tile, pl.BlockSpec((ts, 128), rev), tile, tile, tile, pl.BlockSpec((ts, DIN), rev), tile,
                  pl.BlockSpec((ts, DP), rev),
                  _layer_vec(8, D, l), _layer_vec(2, D, l),
                  _resident((DIN, D)), _resident((D, D))] + _small_specs(l),
        out_specs=[tile] + [_VMEM_WHOLE] * 7,
        out_shape=[jax.ShapeDtypeStruct((s, D), F32), jax.ShapeDtypeStruct((DIN, D), F32),
                   jax.ShapeDtypeStruct((D, D), F32), jax.ShapeDtypeStruct((NG, GW, GW), F32),
                   jax.ShapeDtypeStruct((NG, GW, GW), F32), jax.ShapeDtypeStruct((NG, GW, GW), F32),
                   jax.ShapeDtypeStruct((5, D), F32), jax.ShapeDtypeStruct((3, DP), F32)],
        scratch=[pltpu.VMEM((AHALO, DP), F32),
                 pltpu.VMEM((ts, DS), F32), pltpu.VMEM((ts, DS), F32), pltpu.VMEM((ts, D), BF16),
                 pltpu.VMEM((ts, DIN), BF16), pltpu.VMEM((GW, DS), F32), pltpu.VMEM((5, 8, D), F32),
                 pltpu.VMEM((3, 8, DP), F32)],
        args=[r1, st1, dx1, f1, hsave, proj, xsave, pooled, modv, ln1, winT, wout, *small], pay=pay)


def _cast_call(arrs, name):
    r, c = arrs[0].shape
    rb = _row_block(r, 6 * c * len(arrs))

    def body(*refs):
        for src, dst in zip(refs[:len(arrs)], refs[len(arrs):]):
            dst[...] = src[...].astype(BF16)

    blk = pl.BlockSpec((rb, c), lambda i: (i, 0))
    return pl.pallas_call(
        body, name=name, grid=(r // rb,), in_specs=[blk] * len(arrs), out_specs=[blk] * len(arrs),
        out_shape=[jax.ShapeDtypeStruct((r, c), BF16)] * len(arrs), compiler_params=_ARB)(*arrs)


def _rows_before(halo, x):
    ext = jnp.concatenate([halo, x], axis=0)
    return pltpu.roll(ext, 1, 0)[GHALO:, :], pltpu.roll(ext, 2, 0)[GHALO:, :]


def _rows_after(x, halo):
    ts = x.shape[0]
    ext = jnp.concatenate([x, halo], axis=0)
    n = ts + GHALO
    return pltpu.roll(ext, n - 1, 0)[0:ts, :], pltpu.roll(ext, n - 2, 0)[0:ts, :]


def _fb_call(l, alpha, r1, modv, ln1, wup4, wd2, conv_w, conv_b, pay=None):
    s = r1.shape[0]
    nt = s // TSF

    def body(r1_ref, modv_ref, ln1_ref, wup_ref, wd_ref, cw_ref, cb_ref, r2_ref, f2_ref, h2_ref, gs_ref, vs_ref,
             gc_ref, x1_ref, st_ref, gbuf):
        i = pl.program_id(0)

        @pl.when(i == 0)
        def _():
            gbuf[...] = jnp.zeros_like(gbuf)

        xhat1, st_ref[...] = _ln_stats_tile(r1_ref[...])
        x1 = xhat1 * ln1_ref[0:1, :] + ln1_ref[1:2, :]
        h2b = (x1 * (1.0 + modv_ref[4:5, :]) + modv_ref[3:4, :]).astype(BF16)
        f2 = jnp.zeros((TSF, D), F32)
        for c0, c1 in FF_CHUNKS:
            cs = slice(c0, c1)
            g = _mm_nt(h2b, wup_ref[0, cs, :])
            val = _mm_nt(h2b, wup_ref[1, cs, :])
            gm1, gm2 = _rows_before(gbuf[:, cs], g)
            cw = cw_ref[:, cs]
            gc = cb_ref[:, cs] + cw[0:1, :] * gm2 + cw[1:2, :] * gm1 + cw[2:3, :] * g
            ge, _ = _gelu_t(gc)
            f2 = f2 + _mm((ge * val).astype(BF16), wd_ref[cs, :])
            gs_ref[:, cs] = g.astype(BF16)
            vs_ref[:, cs] = val.astype(BF16)
            gc_ref[:, cs] = gc.astype(BF16)
            gbuf[:, cs] = g[TSF - GHALO:, :]
        r2_ref[...] = alpha * x1 + modv_ref[5:6, :] * f2
        f2_ref[...] = f2.astype(BF16)
        h2_ref[...] = h2b
        x1_ref[...] = x1.astype(BF16)

    tile = pl.BlockSpec((TSF, D), lambda i: (i, 0))
    wide = pl.BlockSpec((TSF, FF), lambda i: (i, 0))
    return _pcall(
        body, f"fb{l}", nt,
        in_specs=[tile, _layer_vec(8, D, l), _layer_vec(2, D, l), _resident((2, FF, D)), _resident((FF, D)),
                  _layer_vec(3, FF, l), _layer_vec(1, FF, l)],
        out_specs=[tile, tile, tile, wide, wide, wide, tile, pl.BlockSpec((TSF, 128), lambda i: (i, 0))],
        out_shape=[jax.ShapeDtypeStruct((s, D), F32), jax.ShapeDtypeStruct((s, D), BF16),
                   jax.ShapeDtypeStruct((s, D), BF16), jax.ShapeDtypeStruct((s, FF), BF16),
                   jax.ShapeDtypeStruct((s, FF), BF16), jax.ShapeDtypeStruct((s, FF), BF16),
                   jax.ShapeDtypeStruct((s, D), BF16), jax.ShapeDtypeStruct((s, 128), F32)],
        scratch=[pltpu.VMEM((GHALO, FF), F32)],
        args=[r1, modv, ln1, wup4.reshape(2, FF, D), wd2.reshape(FF, D), conv_w, conv_b], pay=pay)


def _ff_backward(h2b, df2b, gs_ref, vs_ref, gc_ref, dgc_next, cw, wg_ref, wv_ref, wd_ref,
                 dwg_ref, dwv_ref, dwd_ref, cacc):
    g = gs_ref[...].astype(F32)
    val = vs_ref[...].astype(F32)
    gc = gc_ref[...].astype(F32)
    ge, tg = _gelu_t(gc)
    mb = (ge * val).astype(BF16)
    dm = _mm_nt(df2b, wd_ref[...])
    dwd_ref[...] += _mm_tn(mb, df2b)
    dval = dm * ge
    dgc = dm * val * _dgelu(gc, tg)
    dgp1, dgp2 = _rows_after(dgc, dgc_next[...])
    cacc[0] += _colsum8(dgc)
    cacc[1] += _colsum8(dgp2 * g)
    cacc[2] += _colsum8(dgp1 * g)
    cacc[3] += _colsum8(dgc * g)
    dg = cw[2:3, :] * dgc + cw[1:2, :] * dgp1 + cw[0:1, :] * dgp2
    dgc_next[...] = dgc[0:GHALO, :]
    dgb = dg.astype(BF16)
    dvb = dval.astype(BF16)
    dwg_ref[...] += _mm_tn(dgb, h2b)
    dwv_ref[...] += _mm_tn(dvb, h2b)
    return _mm(dgb, wg_ref[...]) + _mm(dvb, wv_ref[...])


def _bb_specs(l, hf, nt, ts):
    wide = pl.BlockSpec((ts, FH), lambda i: (nt - 1 - i, hf))
    ins = [wide, wide, wide, _resident((FH, D), (hf,)), _resident((FH, D), (2 + hf,)), _resident((FH, D), (hf,)),
           pl.BlockSpec((None, 3, FH), lambda i: (l, 0, hf))]
    acc_shapes = [jax.ShapeDtypeStruct((FH, D), BF16)] * 3
    return ins, acc_shapes, [pltpu.VMEM((FH, D), F32)] * 3


def _bb0_call(l, last, h2, r2, st2, dx2, f2, gs, vs, gcs, modv, ln2, wup4, wd2, conv_w, pay=None):
    s = r2.shape[0]
    nt = s // TSB
    ts = TSB

    def body(*refs):
        it = iter(refs)
        h2_ref, r2_ref, dx2_ref, f2_ref, gs_ref, vs_ref, gc_ref = (next(it) for _ in range(7))
        wg_ref, wv_ref, wd_ref, cw_ref, modv_ref, ln2_ref = (next(it) for _ in range(6))
        st2_ref = None if last else next(it)
        dh2_ref, df2_ref, dr2_ref, dwg_out, dwv_out, dwd_out, vb_ref, cacc_ref = (next(it) for _ in range(8))
        loss_ref = next(it) if last else None
        vacc, cacc, dgc_next, dwg_ref, dwv_ref, dwd_ref = (next(it) for _ in range(6))
        lacc = next(it) if last else None
        i = pl.program_id(0)

        @pl.when(i == 0)
        def _():
            dwg_ref[...] = jnp.zeros_like(dwg_ref)
            dwv_ref[...] = jnp.zeros_like(dwv_ref)
            dwd_ref[...] = jnp.zeros_like(dwd_ref)
            vacc[...] = jnp.zeros_like(vacc)
            cacc[...] = jnp.zeros_like(cacc)
            dgc_next[...] = jnp.zeros_like(dgc_next)
            if last:
                lacc[...] = jnp.zeros_like(lacc)

        if last:
            xhat2, rstd2 = _ln_stats(r2_ref[...])
        else:
            xhat2, rstd2 = _ln_from_tile(r2_ref[...], st2_ref[...])
        if last:
            diff = xhat2 * ln2_ref[0:1, :] + ln2_ref[1:2, :] - dx2_ref[...]
            dy = diff * (1.0 / D)
            lacc[...] += _colsum8(diff * diff)
        else:
            dy = dx2_ref[...]
        dr2 = _ln_bwd(dy, ln2_ref[0:1, :], xhat2, rstd2)
        vacc[0] += _colsum8(dy * xhat2)
        vacc[1] += _colsum8(dy)
        vacc[2] += _colsum8(dr2 * f2_ref[...].astype(F32))
        df2b = (dr2 * modv_ref[5:6, :]).astype(BF16)
        dr2_ref[...] = dr2
        df2_ref[...] = df2b
        dh2_ref[...] = _ff_backward(h2_ref[...], df2b, gs_ref, vs_ref, gc_ref, dgc_next, cw_ref[...],
                                    wg_ref, wv_ref, wd_ref, dwg_ref, dwv_ref, dwd_ref, cacc)

        @pl.when(i == nt - 1)
        def _():
            for n in range(3):
                vb_ref[n:n + 1, :] = jnp.sum(vacc[n], axis=0, keepdims=True)
            for n in range(4):
                cacc_ref[n:n + 1, :] = jnp.sum(cacc[n], axis=0, keepdims=True)
            dwg_out[...] = dwg_ref[...].astype(BF16)
            dwv_out[...] = dwv_ref[...].astype(BF16)
            dwd_out[...] = dwd_ref[...].astype(BF16)
            if last:
                loss_ref[...] = lacc[...]

    tile = pl.BlockSpec((ts, D), lambda i: (nt - 1 - i, 0))
    ff_ins, acc_shapes, acc_scratch = _bb_specs(l, 0, nt, ts)
    in_specs = [tile, tile, tile, tile] + ff_ins + [_layer_vec(8, D, l), _layer_vec(2, D, l)]
    args = [h2, r2, dx2, f2, gs, vs, gcs, wup4, wup4, wd2, conv_w, modv, ln2]
    if not last:
        in_specs.append(pl.BlockSpec((ts, 128), lambda i: (nt - 1 - i, 0)))
        args.append(st2)
    out_specs = [tile, tile, tile] + [_VMEM_WHOLE] * 5
    out_shape = [jax.ShapeDtypeStruct((s, D), F32), jax.ShapeDtypeStruct((s, D), BF16),
                 jax.ShapeDtypeStruct((s, D), F32)] + acc_shapes + [jax.ShapeDtypeStruct((3, D), F32),
                                                                    jax.ShapeDtypeStruct((4, FH), F32)]
    scratch = [pltpu.VMEM((3, 8, D), F32), pltpu.VMEM((4, 8, FH), F32), pltpu.VMEM((GHALO, FH), F32)] + acc_scratch
    if last:
        out_specs.append(_VMEM_WHOLE)
        out_shape.append(jax.ShapeDtypeStruct((8, D), F32))
        scratch.append(pltpu.VMEM((8, D), F32))
    return _pcall(body, f"bb{l}_0", nt, in_specs, out_specs, out_shape, scratch, args, pay=pay)


def _bb1_call(l, alpha, h2, x1s, df2, dr2, dh2_in, gs, vs, gcs, modv, wup4, wd2, conv_w, pay=None):
    s = h2.shape[0]
    nt = s // TSB
    ts = TSB

    def body(h2_ref, x1_ref, df2_ref, dr2_ref, dh2_ref, gs_ref, vs_ref, gc_ref, wg_ref, wv_ref, wd_ref, cw_ref,
             modv_ref, dx1_ref, dwg_out, dwv_out, dwd_out, vb_ref, cacc_ref, vacc, cacc, dgc_next,
             dwg_ref, dwv_ref, dwd_ref):
        i = pl.program_id(0)

        @pl.when(i == 0)
        def _():
            dwg_ref[...] = jnp.zeros_like(dwg_ref)
            dwv_ref[...] = jnp.zeros_like(dwv_ref)
            dwd_ref[...] = jnp.zeros_like(dwd_ref)
            vacc[...] = jnp.zeros_like(vacc)
            cacc[...] = jnp.zeros_like(cacc)
            dgc_next[...] = jnp.zeros_like(dgc_next)

        dh2 = dh2_ref[...] + _ff_backward(h2_ref[...], df2_ref[...], gs_ref, vs_ref, gc_ref, dgc_next,
                                          cw_ref[...], wg_ref, wv_ref, wd_ref, dwg_ref, dwv_ref, dwd_ref, cacc)
        dx1_ref[...] = dh2 * (1.0 + modv_ref[4:5, :]) + alpha * dr2_ref[...]
        vacc[0] += _colsum8(dh2)
        vacc[1] += _colsum8(dh2 * x1_ref[...].astype(F32))

        @pl.when(i == nt - 1)
        def _():
            dwg_out[...] = dwg_ref[...].astype(BF16)
            dwv_out[...] = dwv_ref[...].astype(BF16)
            dwd_out[...] = dwd_ref[...].astype(BF16)
            for n in range(2):
                vb_ref[n:n + 1, :] = jnp.sum(vacc[n], axis=0, keepdims=True)
            for n in range(4):
                cacc_ref[n:n + 1, :] = jnp.sum(cacc[n], axis=0, keepdims=True)

    tile = pl.BlockSpec((ts, D), lambda i: (nt - 1 - i, 0))
    ff_ins, acc_shapes, acc_scratch = _bb_specs(l, 1, nt, ts)
    out_shape = [jax.ShapeDtypeStruct((s, D), F32)] + acc_shapes + [jax.ShapeDtypeStruct((2, D), F32),
                                                                   jax.ShapeDtypeStruct((4, FH), F32)]
    scratch = [pltpu.VMEM((2, 8, D), F32), pltpu.VMEM((4, 8, FH), F32), pltpu.VMEM((GHALO, FH), F32)] + acc_scratch
    return _pcall(body, f"bb{l}_1", nt, [tile] * 5 + ff_ins + [_layer_vec(8, D, l)],
                  [tile] + [_VMEM_WHOLE] * 5, out_shape, scratch,
                  [h2, x1s, df2, dr2, dh2_in, gs, vs, gcs, wup4, wup4, wd2, conv_w, modv], pay=pay)


def _silu(c):
    return c * (1.0 / (1.0 + jnp.exp(-c)))


def _ada_fwd_call(c_all, ada_w, ada_b_my):
    nl, _, wcols = ada_w.shape

    def body(c_ref, w_ref, b_ref, o_ref):
        ca = _silu(c_ref[...])
        o_ref[...] = jnp.dot(ca, w_ref[...], preferred_element_type=F32,
                             precision=lax.Precision.HIGHEST) + b_ref[...]

    return pl.pallas_call(
        body, name="ada_fwd", grid=(nl,),
        in_specs=[_full((NDEV, D)), pl.BlockSpec((None, D, wcols), lambda l: (l, 0, 0)),
                  pl.BlockSpec((None, 1, wcols), lambda l: (l, 0, 0))],
        out_specs=pl.BlockSpec((None, NDEV, wcols), lambda l: (l, 0, 0)),
        out_shape=jax.ShapeDtypeStruct((nl, NDEV, wcols), F32),
        compiler_params=_ARB,
    )(c_all, ada_w, ada_b_my)


def _adam_update(w, g, m, v):
    m2 = ADAM_B1 * m + (1.0 - ADAM_B1) * g
    v2 = ADAM_B2 * v + (1.0 - ADAM_B2) * (g * g)
    m_hat = m2 / (1.0 - ADAM_B1 ** ADAM_STEP)
    v_hat = v2 / (1.0 - ADAM_B2 ** ADAM_STEP)
    delta = -ADAM_LR * (m_hat / (jnp.sqrt(v_hat) + ADAM_EPS) + ADAM_WD * w)
    return delta, m2, v2


def _ada_bwd_call(c_t, dmod_my, w, m, v):
    nl, _, wcols = w.shape
    rb = 256

    def body(ct_ref, dm_ref, w_ref, m_ref, v_ref, g_ref, d_ref, m2_ref, v2_ref):
        ca_t = _silu(ct_ref[...])
        dm = dm_ref[...]
        g = ca_t[:, 0:1] * dm[0:1, :]
        for b in range(1, NDEV):
            g = g + ca_t[:, b:b + 1] * dm[b:b + 1, :]
        delta, m2, v2 = _adam_update(w_ref[...], g, m_ref[...], v_ref[...])
        g_ref[...] = g
        d_ref[...] = delta
        m2_ref[...] = m2
        v2_ref[...] = v2

    blk = pl.BlockSpec((None, rb, wcols), lambda l, i: (l, i, 0))
    shp = jax.ShapeDtypeStruct(w.shape, F32)
    return pl.pallas_call(
        body, name="ada_bwd", grid=(nl, D // rb),
        in_specs=[pl.BlockSpec((rb, NDEV), lambda l, i: (i, 0)),
                  pl.BlockSpec((None, NDEV, wcols), lambda l, i: (l, 0, 0)), blk, blk, blk],
        out_specs=[blk, blk, blk, blk], out_shape=[shp, shp, shp, shp],
        compiler_params=pltpu.CompilerParams(dimension_semantics=("arbitrary", "arbitrary")),
    )(c_t, dmod_my, w, m, v)


def _row_block(r, row_bytes):
    budget = 6 * 1024 * 1024
    best = None
    for rb in range(16, min(r, 512) + 1, 16):
        if r % rb == 0 and rb * row_bytes <= budget:
            best = rb
    return best if best is not None else r


def _adamw_rows_call(buf, items):
    nl = buf.shape[0]
    n = len(items)

    def body(*refs):
        buf_ref, ins, outs = refs[0], refs[1:1 + 3 * n], refs[1 + 3 * n:]
        for k, (w, _, _, row, lane0) in enumerate(items):
            w_ref, m_ref, v_ref = ins[3 * k:3 * k + 3]
            g_ref, d_ref, m2_ref, v2_ref = outs[4 * k:4 * k + 4]
            width = w.shape[1]
            for l in range(nl):
                for c0 in range(0, width, 1024):
                    n_ = min(1024, width - c0)
                    r = row + c0 // 1024
                    g = buf_ref[l, 0, r:r + 1, lane0:lane0 + n_]
                    for dev in range(1, NDEV):
                        g = g + buf_ref[l, dev, r:r + 1, lane0:lane0 + n_]
                    at = (slice(l, l + 1), slice(c0, c0 + n_))
                    delta, m2, v2 = _adam_update(w_ref[at], g, m_ref[at], v_ref[at])
                    g_ref[at] = g
                    d_ref[at] = delta
                    m2_ref[at] = m2
                    v2_ref[at] = v2

    args = [buf]
    out_shape = []
    for w, m, v, _, _ in items:
        args += [w, m, v]
        out_shape += [jax.ShapeDtypeStruct(w.shape, F32)] * 4
    res = pl.pallas_call(body, name="adamw_vectors", in_specs=[_VMEM_WHOLE] * len(args),
                         out_specs=[_VMEM_WHOLE] * len(out_shape), out_shape=out_shape)(*args)
    return [tuple(res[4 * k:4 * k + 4]) for k in range(n)]


def _adamw_call(parts, w, m, v, name, pay=None, row0=0):
    nl, npart, _, c = parts.shape
    r = w.shape[1]
    rb = _row_block(r, (npart + 7) * c * 4)
    nb = r // rb
    assert row0 % rb == 0
    b0 = row0 // rb

    def body(p_ref, w_ref, m_ref, v_ref, g_ref, d_ref, m2_ref, v2_ref):
        g = p_ref[0].astype(F32)
        for k in range(1, npart):
            g = g + p_ref[k].astype(F32)
        delta, m2, v2 = _adam_update(w_ref[...], g, m_ref[...], v_ref[...])
        g_ref[...] = g
        d_ref[...] = delta
        m2_ref[...] = m2
        v2_ref[...] = v2

    blk = pl.BlockSpec((None, rb, c), lambda i: (i // nb, i % nb, 0))
    shp = jax.ShapeDtypeStruct((nl, r, c), F32)
    outs, _, alias = _pcall(body, name, nl * nb,
                            [pl.BlockSpec((None, npart, rb, c), lambda i: (i // nb, 0, b0 + i % nb, 0)), blk, blk, blk],
                            [blk] * 4, [shp] * 4, [], [parts, w, m, v], pay)
    return tuple(outs), alias


_ROW_OF = dict(ada_b=(0, 0), ln1_g=(6, 0), ln1_b=(7, 0), ln2_g=(8, 0), ln2_b=(9, 0), conv_b=(10, 0),
               pool_scale=(13, 0), sgu_ln_g=(13, DP), sgu_ln_b=(14, 0), sgu_b=(14, DP), conv_w=(15, 0))
_SMALL_ROWS = 24
_FF_PAD = 3 * 1024 - FF


def kernel(x, c, ada_w, ada_b, w_in, pool_w, pool_scale, sgu_ln_g, sgu_ln_b, sgu_w, sgu_b, w_out, ln1_g, ln1_b, w_up, conv_w, conv_b, w_down, ln2_g, ln2_b, loss_target, m_ada_w, m_ada_b, m_w_in, m_pool_w, m_pool_scale, m_sgu_ln_g, m_sgu_ln_b, m_sgu_w, m_sgu_b, m_w_out, m_ln1_g, m_ln1_b, m_w_up, m_conv_w, m_conv_b, m_w_down, m_ln2_g, m_ln2_b, v_ada_w, v_ada_b, v_w_in, v_pool_w, v_pool_scale, v_sgu_ln_g, v_sgu_ln_b, v_sgu_w, v_sgu_b, v_w_out, v_ln1_g, v_ln1_b, v_w_up, v_conv_w, v_conv_b, v_w_down, v_ln2_g, v_ln2_b):
    nl = ada_w.shape[0]
    alpha = (2.0 * nl) ** 0.25
    me = _me()
    x2 = x[0]
    tgt = loss_target[0]
    acols = ada_w.shape[2]
    icols = w_in.shape[2]
    ucols = w_up.shape[2]
    orows = w_out.shape[1]
    drows = w_down.shape[1]
    ccols = conv_w.shape[2]

    winT_sh = jnp.swapaxes(w_in, 1, 2).astype(BF16)
    wupT_sh = jnp.swapaxes(w_up, 1, 2).astype(BF16)
    wout_sh = w_out.astype(BF16)
    wd_sh = w_down.astype(BF16)

    pay = _Payload()
    pay.gather(jnp.broadcast_to(c, (8, D)))
    pay.gather(winT_sh, 0)
    pay.gather(wout_sh, 0)
    pay.gather(conv_w)
    (c_g, winT_g, wout_g, cw_g), _ = _comm_call(pay, "gather_first")
    c_all = c_g[:, 0, :]
    winT = winT_g.reshape(DIN, D)
    wout = wout_g.reshape(D, D)
    cw_full = jnp.transpose(cw_g, (1, 2, 0, 3)).reshape(nl, 3, FF)
    cb_full = conv_b[:, None, :]

    ada_b_my = lax.dynamic_slice(ada_b, (0, me * acols), (nl, acols))[:, None, :]
    mod_blk = _ada_fwd_call(c_all, ada_w, ada_b_my)
    pay = _Payload()
    pay.gather(mod_blk)
    (mod_g,), _ = _comm_call(pay, "gather_mod")
    mod_me = lax.dynamic_index_in_dim(mod_g, me, axis=2, keepdims=False)
    modv = jnp.swapaxes(mod_me, 0, 1).reshape(nl, 6, D)
    modv = jnp.concatenate([modv, jnp.zeros((nl, 2, D), F32)], axis=1)

    ln1 = jnp.stack([ln1_g, ln1_b], axis=1)
    ln2 = jnp.stack([ln2_g, ln2_b], axis=1)
    sln = jnp.stack([sgu_ln_g, sgu_ln_b], axis=1)
    sbf = jnp.broadcast_to(sgu_b[..., None], sgu_b.shape + (GW,))
    small = (pool_w, pool_scale[:, None, :], sln, sgu_w, sbf)

    r1s, r2s, f1s, f2s, ahs, ghs, wins, wouts, wups, wds, st1s, st2s = ([None] * nl for _ in range(12))
    wins[0], wouts[0] = winT, wout
    xin = x2
    for l in range(nl):
        pay = None
        if l == 0:
            pay = _Payload()
            pay.gather(wupT_sh, 0)
            pay.gather(wd_sh, 0)
        (r1, f1, *ah, st_prev), new, _ = _fa_call(l, l == 0, alpha, xin, modv, ln2, wins[l], wouts[l], small, pay)
        if l > 0:
            st2s[l - 1] = st_prev
        if l == 0:
            wups[0], wds[0] = new[0].reshape(4, FH, D), new[1].reshape(2, FH, D)
        pay = None
        if l + 1 < nl:
            pay = _Payload()
            pay.gather(winT_sh, l + 1)
            pay.gather(wout_sh, l + 1)
            pay.gather(wupT_sh, l + 1)
            pay.gather(wd_sh, l + 1)
        (r2, f2, *saved_b, st1s[l]), new, _ = _fb_call(l, alpha, r1, modv, ln1, wups[l], wds[l], cw_full, cb_full,
                                                       pay)
        if l + 1 < nl:
            wins[l + 1], wouts[l + 1] = new[0].reshape(DIN, D), new[1].reshape(D, D)
            wups[l + 1], wds[l + 1] = new[2].reshape(4, FH, D), new[3].reshape(2, FH, D)
        r1s[l], r2s[l], f1s[l], f2s[l], ahs[l], ghs[l] = r1, r2, f1, f2, ah, saved_b
        xin = r2

    buf_in = lax.empty((nl, NDEV, icols, D), BF16)
    buf_out = lax.empty((nl, NDEV, orows, D), BF16)
    buf_up = lax.empty((nl, NDEV, ucols, D), BF16)
    buf_down = lax.empty((nl, NDEV, drows, D), BF16)
    buf_small = lax.empty((nl, NDEV, _SMALL_ROWS, 1024), F32)
    buf_mat = lax.empty((nl, NDEV, 2 * NG * GW, GW), F32)

    def a_side_payload(l, dwin, dwout, flat, mats):
        p = _Payload()
        p.exchange_into([(dwin.reshape(NDEV, icols, D), 0)], buf_in, (l,))
        p.exchange_into([(dwout.reshape(NDEV, orows, D), 0)], buf_out, (l,))
        p.gather_into(flat, buf_small, (l,))
        p.gather_into(mats, buf_mat, (l,))
        return p

    dx = tgt
    loss_acc = None
    pending = None
    for l in reversed(range(nl)):
        last = l == nl - 1
        pay = None if pending is None else a_side_payload(*pending)
        h2, gs, vs, gcs, x1s = ghs[l]
        res0, _, al = _bb0_call(l, last, h2, r2s[l], st2s[l], dx, f2s[l], gs, vs, gcs, modv, ln2, wups[l], wds[l],
                                cw_full, pay)
        if pending is not None:
            buf_in, buf_out, buf_small, buf_mat = al
        dh2p, df2, dr2, dwg0, dwv0, dwd0, vb0, cacc0 = res0[:8]
        if last:
            loss_acc = res0[8]
        (dx1, dwg1, dwv1, dwd1, vb1, cacc1), _, _ = _bb1_call(
            l, alpha, h2, x1s, df2, dr2, dh2p, gs, vs, gcs, modv, wups[l], wds[l], cw_full)
        pay = _Payload()
        pay.exchange_into([(dwg0.reshape(2, ucols, D), 0), (dwg1.reshape(2, ucols, D), 2),
                           (dwv0.reshape(2, ucols, D), 4), (dwv1.reshape(2, ucols, D), 6)], buf_up, (l,))
        pay.exchange_into([(dwd0.reshape(4, drows, D), 0), (dwd1.reshape(4, drows, D), 4)], buf_down, (l,))
        (dx, dwin, dwout, dpw, dsw, dsb, va, va5), _, (buf_up, buf_down) = _ba_call(
            l, alpha, r1s[l], st1s[l], dx1, f1s[l], *ahs[l], modv, ln1, wins[l], wouts[l], small, pay)
        cacc = jnp.pad(jnp.concatenate([cacc0, cacc1], axis=1), ((0, 0), (0, _FF_PAD)))
        flat = jnp.concatenate([
            va[3], va[4], va[2], vb1[0], vb1[1], vb0[2],
            va[0], va[1], vb0[0], vb0[1],
            cacc[0],
            va5[0], va5[1], va5[2], dsb[:, :, 0].reshape(-1),
            cacc[1:4].reshape(-1),
        ]).reshape(_SMALL_ROWS, 1024)
        mats = jnp.concatenate([dpw.reshape(NG * GW, GW), dsw.reshape(NG * GW, GW)])
        (dwin,) = _cast_call([dwin], f"cast_in{l}")
        (dwout,) = _cast_call([dwout], f"cast_out{l}")
        pending = (l, dwin, dwout, flat, mats)
    grad_x = dx[None]
    loss = lax.psum((0.5 / D) * jnp.sum(loss_acc), ("x", "y", "c"))

    res = {}
    _, dwin0, dwout0, flat0, mats0 = pending
    t_ = lambda a: jnp.swapaxes(a, 1, 2)
    pay = _Payload()
    pay.exchange_into([(dwout0.reshape(NDEV, orows, D), 0)], buf_out, (0,))
    pay.gather_into(flat0, buf_small, (0,))
    pay.gather_into(mats0, buf_mat, (0,))
    outs, (buf_out, buf_small, buf_mat) = _adamw_call(buf_up, t_(w_up), t_(m_w_up), t_(v_w_up), "adamw_w_up", pay)
    res["w_up"] = tuple(t_(o) for o in outs)
    pay = _Payload()
    pay.exchange_into([(dwin0.reshape(NDEV, icols, D), 0)], buf_in, (0,))
    res["w_down"], (buf_in,) = _adamw_call(buf_down, w_down, m_w_down, v_w_down, "adamw_w_down", pay)
    outs, _ = _adamw_call(buf_in, t_(w_in), t_(m_w_in), t_(v_w_in), "adamw_w_in")
    res["w_in"] = tuple(t_(o) for o in outs)
    res["w_out"], _ = _adamw_call(buf_out, w_out, m_w_out, v_w_out, "adamw_w_out")

    weights = dict(ada_b=ada_b, pool_w=pool_w, pool_scale=pool_scale, sgu_ln_g=sgu_ln_g, sgu_ln_b=sgu_ln_b,
                   sgu_w=sgu_w, sgu_b=sgu_b, ln1_g=ln1_g, ln1_b=ln1_b, conv_w=conv_w, conv_b=conv_b, ln2_g=ln2_g,
                   ln2_b=ln2_b)
    m_of = dict(ada_b=m_ada_b, pool_w=m_pool_w, pool_scale=m_pool_scale, sgu_ln_g=m_sgu_ln_g, sgu_ln_b=m_sgu_ln_b,
                sgu_w=m_sgu_w, sgu_b=m_sgu_b, ln1_g=m_ln1_g, ln1_b=m_ln1_b, conv_w=m_conv_w, conv_b=m_conv_b,
                ln2_g=m_ln2_g, ln2_b=m_ln2_b)
    v_of = dict(ada_b=v_ada_b, pool_w=v_pool_w, pool_scale=v_pool_scale, sgu_ln_g=v_sgu_ln_g, sgu_ln_b=v_sgu_ln_b,
                sgu_w=v_sgu_w, sgu_b=v_sgu_b, ln1_g=v_ln1_g, ln1_b=v_ln1_b, conv_w=v_conv_w, conv_b=v_conv_b,
                ln2_g=v_ln2_g, ln2_b=v_ln2_b)
    vec_names = ("ada_b", "ln1_g", "ln1_b", "ln2_g", "ln2_b", "conv_b", "pool_scale", "sgu_ln_g", "sgu_ln_b")
    vec_res = _adamw_rows_call(buf_small, [(weights[n], m_of[n], v_of[n], *_ROW_OF[n]) for n in vec_names])
    res.update(zip(vec_names, vec_res))
    for k, name in enumerate(("pool_w", "sgu_w")):
        w3 = weights[name].reshape(nl, NG * GW, GW)
        outs, _ = _adamw_call(buf_mat, w3, m_of[name].reshape(w3.shape), v_of[name].reshape(w3.shape),
                              f"adamw_{name}", row0=k * NG * GW)
        res[name] = tuple(o.reshape(weights[name].shape) for o in outs)
    row, lane = _ROW_OF["sgu_b"]
    parts = buf_small[:, :, row, lane:lane + NG * GW].reshape(nl, NDEV, NG, GW)
    res["sgu_b"], _ = _adamw_call(parts, sgu_b, m_sgu_b, v_sgu_b, "adamw_sgu_b")
    row, _ = _ROW_OF["conv_w"]
    parts = buf_small[:, :, row:row + 9, :].reshape(nl, NDEV, 3, 3 * 1024)
    parts = lax.dynamic_slice_in_dim(parts, me * ccols, ccols, axis=3)
    res["conv_w"], _ = _adamw_call(parts, conv_w, m_conv_w, v_conv_w, "adamw_conv_w")

    dmod_my = lax.dynamic_slice_in_dim(buf_small[:, :, 0:6, :].reshape(nl, NDEV, 6 * D), me * acols, acols, axis=2)
    res["ada_w"] = _ada_bwd_call(jnp.swapaxes(c_all, 0, 1), dmod_my, ada_w, m_ada_w, v_ada_w)

    order = ["ada_w", "ada_b", "w_in", "pool_w", "pool_scale", "sgu_ln_g", "sgu_ln_b", "sgu_w", "sgu_b", "w_out",
             "ln1_g", "ln1_b", "w_up", "conv_w", "conv_b", "w_down", "ln2_g", "ln2_b"]
    out = [loss, grad_x]
    for k in range(4):
        out += [res[n][k] for n in order]
    return tuple(out)
```
